```python
import jax, jax.numpy as jnp
from jax import lax
import numpy as np

D_MODEL = 1024
BATCH = 16
SEQ = 256
DEPTH = 4
DEC_BATCH = 4
DEC_SEQ = 1024
PAST_LEN = 256

GRID_W = 64
N_MIXERS = 2
N_GLA = (DEPTH + 1) // 2
N_ATT = DEPTH // 2
GLA_HEADS = 4
GLA_DK = D_MODEL // 2 // GLA_HEADS
GLA_DV = D_MODEL // GLA_HEADS
GLA_RANK = 16
GLA_TAU = 16.0
GLA_CHUNK = 64
GLA_QD = GLA_HEADS * GLA_DK
GLA_VD = GLA_HEADS * GLA_DV
GLA_IN = 2 * GLA_QD + 2 * GLA_VD
HEAD_DIM = 128
ATT_HEADS = D_MODEL // HEAD_DIM
ATT_KV_HEADS = ATT_HEADS // 4
ATT_QD = ATT_HEADS * HEAD_DIM
ATT_KD = ATT_KV_HEADS * HEAD_DIM
ATT_IN = 2 * ATT_QD + 2 * ATT_KD
Q_BLOCK = 128
ROPE_THETA = 10000.0
EPS = 1e-6

kernel_name = "hybrid_gla_gqa_diffusion_step"


def rms_norm(x, g):
    xf = x.astype(jnp.float32)
    y = xf * lax.rsqrt(jnp.mean(xf * xf, axis=-1, keepdims=True) + EPS)
    return (y * g.astype(jnp.float32)).astype(x.dtype)


def rope_2d(x):
    T = x.shape[1]
    rows = T // GRID_W
    row = jnp.repeat(jnp.arange(rows), GRID_W)
    col = jnp.tile(jnp.arange(GRID_W), rows)
    half = HEAD_DIM // 2
    nf = half // 2
    freqs = ROPE_THETA ** (-jnp.arange(nf, dtype=jnp.float32) / nf)

    def rot(seg, pos):
        ang = pos.astype(jnp.float32)[:, None] * freqs[None, :]
        cos = jnp.cos(ang)[None, :, None, :]
        sin = jnp.sin(ang)[None, :, None, :]
        a, b = seg[..., :nf], seg[..., nf:]
        return jnp.concatenate([a * cos - b * sin, b * cos + a * sin], axis=-1)

    xf = x.astype(jnp.float32)
    out = jnp.concatenate([rot(xf[..., :half], row), rot(xf[..., half:], col)], axis=-1)
    return out.astype(x.dtype)


def to_chunks(a):
    B, T, H, d = a.shape
    return a.reshape(B, T // GLA_CHUNK, GLA_CHUNK, H, d).transpose(1, 0, 3, 2, 4)


def gla_chunked(q, k, v, logg, s0):
    B, T, H, _ = q.shape
    qc, kc, vc, gc = (to_chunks(a.astype(jnp.float32)) for a in (q, k, v, logg))
    mask = jnp.tril(jnp.ones((GLA_CHUNK, GLA_CHUNK), dtype=bool))

    def step(S, inp):
        qi, ki, vi, gi = inp
        b = jnp.cumsum(gi, axis=2)
        b_last = b[:, :, -1:, :]
        qe = qi * jnp.exp(b)
        ke = ki * jnp.exp(-b)
        a = jnp.where(mask, jnp.einsum("bhtk,bhsk->bhts", qe, ke), 0.0)
        o = jnp.einsum("bhts,bhsv->bhtv", a, vi) + jnp.einsum("bhtk,bhkv->bhtv", qe, S)
        S = jnp.exp(b_last[:, :, 0, :])[..., None] * S + jnp.einsum(
            "bhsk,bhsv->bhkv", ki * jnp.exp(b_last - b), vi)
        return S, o

    S, o = lax.scan(step, s0.astype(jnp.float32), (qc, kc, vc, gc))
    o = o.transpose(1, 0, 3, 2, 4).reshape(B, T, H, -1)
    return o, S


def gla_mixer(h, w_in, wa1, wa2, ba, onorm, w_out, s0):
    B, T, _ = h.shape
    proj = h @ w_in
    q, k, v, gate = jnp.split(proj, [GLA_QD, 2 * GLA_QD, 2 * GLA_QD + GLA_VD], axis=-1)
    q = q.reshape(B, T, GLA_HEADS, GLA_DK) * (GLA_DK ** -0.5)
    k = k.reshape(B, T, GLA_HEADS, GLA_DK)
    v = v.reshape(B, T, GLA_HEADS, GLA_DV)

    def decay(d):
        z = ((h @ wa1[d]) @ wa2[d] + ba[d]).astype(jnp.float32)
        return (jax.nn.log_sigmoid(z) / GLA_TAU).reshape(B, T, GLA_HEADS, GLA_DK)

    flip = lambda a: a[:, ::-1]
    o_f, s_f = gla_chunked(q, k, v, decay(0), s0[:, 0])
    o_b, s_b = gla_chunked(flip(q), flip(k), flip(v), flip(decay(1)), s0[:, 1])
    o = rms_norm(o_f + flip(o_b), onorm).reshape(B, T, GLA_VD).astype(h.dtype)
    y = (o * jax.nn.silu(gate)) @ w_out
    return y, jnp.stack([s_f, s_b], axis=1).astype(h.dtype)


def attn_project(h, w_in, qn, kn):
    B, T, _ = h.shape
    proj = h @ w_in
    q, k, v, gate = jnp.split(proj, [ATT_QD, ATT_QD + ATT_KD, ATT_QD + 2 * ATT_KD], axis=-1)
    q = rms_norm(q.reshape(B, T, ATT_HEADS, HEAD_DIM), qn)
    k = rms_norm(k.reshape(B, T, ATT_KV_HEADS, HEAD_DIM), kn)
    v = v.reshape(B, T, ATT_KV_HEADS, HEAD_DIM)
    return q, k, v, gate


def attend_blocks(q, keys, vals):
    B, T = q.shape[0], q.shape[1]
    G = ATT_HEADS // ATT_KV_HEADS
    nb = T // Q_BLOCK
    qb = q.reshape(B, nb, Q_BLOCK, ATT_KV_HEADS, G, HEAD_DIM).transpose(1, 0, 2, 3, 4, 5)
    scale = HEAD_DIM ** -0.5

    def one(qblk):
        s = jnp.einsum("bqhgd,bkhd->bhgqk", qblk, keys).astype(jnp.float32) * scale
        p = jax.nn.softmax(s, axis=-1).astype(vals.dtype)
        return jnp.einsum("bhgqk,bkhd->bqhgd", p, vals)

    o = lax.map(one, qb)
    return o.transpose(1, 0, 2, 3, 4, 5).reshape(B, T, ATT_QD)


def split_mod(mod):
    return jnp.split(mod, 3, axis=-1)


def setup_inputs(seed: int = 0) -> dict:
    key = jax.random.key(seed)
    ks = jax.random.split(key, 24)
    f32 = jnp.float32
    nrm = lambda k, shape, s: jax.random.normal(k, shape, f32) * s
    return {
        "x_prompt": nrm(ks[0], (BATCH, SEQ, D_MODEL), 1.0),
        "x_sample": nrm(ks[1], (DEC_BATCH, DEC_SEQ, D_MODEL), 1.0),
        "state_gla": nrm(ks[2], (DEC_BATCH, N_GLA, 2, GLA_HEADS, GLA_DK, GLA_DV), 1.0),
        "cache_k": nrm(ks[3], (DEC_BATCH, N_ATT, PAST_LEN, ATT_KV_HEADS, HEAD_DIM), 1.0),
        "cache_v": nrm(ks[4], (DEC_BATCH, N_ATT, PAST_LEN, ATT_KV_HEADS, HEAD_DIM), 1.0),
        "c": nrm(ks[5], (DEC_BATCH, D_MODEL), 1.0),
        "c_ctx": nrm(ks[6], (D_MODEL,), 1.0),
        "norm_g": 1.0 + nrm(ks[7], (DEPTH, D_MODEL), 0.1),
        "w_ada": nrm(ks[8], (DEPTH, D_MODEL, 3 * D_MODEL), 0.5 * D_MODEL ** -0.5),
        "b_ada": nrm(ks[9], (DEPTH, 3 * D_MODEL), 0.02),
        "gla_w_in": nrm(ks[10], (N_GLA, D_MODEL, GLA_IN), D_MODEL ** -0.5),
        "gla_wa1": nrm(ks[11], (N_GLA, 2, D_MODEL, GLA_RANK), D_MODEL ** -0.5),
        "gla_wa2": nrm(ks[12], (N_GLA, 2, GLA_RANK, GLA_QD), GLA_RANK ** -0.5),
        "gla_ba": nrm(ks[13], (N_GLA, 2, GLA_QD), 0.1),
        "gla_onorm": 1.0 + nrm(ks[14], (N_GLA, GLA_DV), 0.1),
        "gla_w_out": nrm(ks[15], (N_GLA, GLA_VD, D_MODEL), GLA_VD ** -0.5),
        "att_w_in": nrm(ks[16], (N_ATT, D_MODEL, ATT_IN), D_MODEL ** -0.5),
        "att_qnorm": 1.0 + nrm(ks[17], (N_ATT, HEAD_DIM), 0.1),
        "att_knorm": 1.0 + nrm(ks[18], (N_ATT, HEAD_DIM), 0.1),
        "att_w_out": nrm(ks[19], (N_ATT, ATT_QD, D_MODEL), ATT_QD ** -0.5),
    }


def reference(x_prompt, x_sample, state_gla, cache_k, cache_v, c, c_ctx, norm_g, w_ada, b_ada,
              gla_w_in, gla_wa1, gla_wa2, gla_ba, gla_onorm, gla_w_out,
              att_w_in, att_qnorm, att_knorm, att_w_out):
    xp, xs = x_prompt, x_sample
    Bp = xp.shape[0]
    gla_states, ctx_keys, ctx_vals = [], [], []
    for l in range(DEPTH):
        i = l // N_MIXERS
        mod_p = (jax.nn.silu(c_ctx) @ w_ada[l] + b_ada[l])[None, None, :]
        mod_s = (jax.nn.silu(c) @ w_ada[l] + b_ada[l])[:, None, :]
        sh_p, sc_p, gt_p = split_mod(mod_p)
        sh_s, sc_s, gt_s = split_mod(mod_s)
        hp = rms_norm(xp, norm_g[l]) * (1.0 + sc_p) + sh_p
        hs = rms_norm(xs, norm_g[l]) * (1.0 + sc_s) + sh_s
        if l % N_MIXERS == 0:
            s_zero = jnp.zeros((Bp, 2, GLA_HEADS, GLA_DK, GLA_DV), xp.dtype)
            out_p, st = gla_mixer(hp, gla_w_in[i], gla_wa1[i], gla_wa2[i], gla_ba[i],
                                  gla_onorm[i], gla_w_out[i], s_zero)
            out_s, _ = gla_mixer(hs, gla_w_in[i], gla_wa1[i], gla_wa2[i], gla_ba[i],
                                 gla_onorm[i], gla_w_out[i], state_gla[:, i])
            gla_states.append(st)
        else:
            qp, kp, vp, gp = attn_project(hp, att_w_in[i], att_qnorm[i], att_knorm[i])
            out_p = (attend_blocks(qp, kp, vp) * jax.nn.silu(gp)) @ att_w_out[i]
            ctx_keys.append(kp)
            ctx_vals.append(vp)
            qs, ks_, vs, gs = attn_project(hs, att_w_in[i], att_qnorm[i], att_knorm[i])
            qs = rope_2d(qs)
            ks_ = rope_2d(ks_)
            keys = jnp.concatenate([ks_, cache_k[:, i].astype(ks_.dtype)], axis=1)
            vals = jnp.concatenate([vs, cache_v[:, i].astype(vs.dtype)], axis=1)
            out_s = (attend_blocks(qs, keys, vals) * jax.nn.silu(gs)) @ att_w_out[i]
        xp = xp + gt_p * out_p
        xs = xs + gt_s * out_s
    state_gla_new = jnp.stack(gla_states, axis=1)
    cache_k_new = jnp.stack(ctx_keys, axis=1)
    cache_v_new = jnp.stack(ctx_vals, axis=1)
    return (xp, xs, state_gla_new, cache_k_new, cache_v_new)
```

```python
import functools
import math

import jax
import jax.numpy as jnp
import numpy as np
from jax import lax
from jax.experimental import pallas as pl
from jax.experimental.pallas import tpu as pltpu

D_MODEL = 1024
DEPTH = 4
GRID_W = 64
GLA_HEADS = 4
GLA_DK = 128
GLA_DV = 256
GLA_RANK = 16
GLA_TAU = 16.0
GLA_CHUNK = 64
GLA_QD = GLA_HEADS * GLA_DK
GLA_VD = GLA_HEADS * GLA_DV
HEAD_DIM = 128
ATT_HEADS = 8
ATT_KV_HEADS = 2
ATT_GROUP = ATT_HEADS // ATT_KV_HEADS
ATT_QD = ATT_HEADS * HEAD_DIM
ATT_KD = ATT_KV_HEADS * HEAD_DIM
ROPE_THETA = 10000.0
EPS = 1e-6

ROW_TILE = 256
MOD_ROWS = 8
RANK_PAD = 128
V7X_VMEM_LIMIT_BYTES = 60 * 1024 * 1024

F32 = jnp.float32
BF16 = jnp.bfloat16
_NT = (((1,), (1,)), ((), ()))
_TN = (((0,), (0,)), ((), ()))


def _dot(a, b):
    return jnp.dot(a, b, preferred_element_type=F32)


def _dot_nt(a, b):
    return lax.dot_general(a, b, _NT, preferred_element_type=F32)


def _dot_tn(a, b):
    return lax.dot_general(a, b, _TN, preferred_element_type=F32)


def _silu(x):
    return x * (1.0 / (1.0 + jnp.exp(-x)))


def _log_sigmoid(z):
    return jnp.minimum(z, 0.0) - jnp.log1p(jnp.exp(-jnp.abs(z)))


def _split_bf16(x):
    hi = x.astype(BF16)
    lo = (x - hi.astype(F32)).astype(BF16)
    return hi, lo


def _modulated_norm(x, ng, mod_ref):
    shift = mod_ref[0, 0, :, 0:D_MODEL]
    scale = mod_ref[0, 0, :, D_MODEL:2 * D_MODEL]
    ms = jnp.mean(x * x, axis=-1, keepdims=True)
    hn = (x * lax.rsqrt(ms + EPS)) * ng
    return (hn * (1.0 + scale) + shift).astype(BF16)


def _head_rms(x, g, width):
    outs = []
    for h in range(x.shape[-1] // width):
        xh = x[:, h * width:(h + 1) * width]
        ms = jnp.mean(xh * xh, axis=-1, keepdims=True)
        outs.append((xh * lax.rsqrt(ms + EPS)) * g)
    return jnp.concatenate(outs, axis=-1)


def _mod_kernel(c_ref, w_ref, b_ref, o_ref):
    s_hi, s_lo = _split_bf16(_silu(c_ref[...]))
    w_hi, w_lo = _split_bf16(w_ref[0])
    acc = _dot(s_hi, w_hi) + _dot(s_lo, w_hi) + _dot(s_hi, w_lo)
    o_ref[0] = acc + b_ref[0]


def _modulations(cvec, w_ada, b_ada):
    tn = D_MODEL
    n_tiles = 3 * D_MODEL // tn
    return pl.pallas_call(
        _mod_kernel,
        grid=(DEPTH, n_tiles),
        in_specs=[
            pl.BlockSpec((MOD_ROWS, D_MODEL), lambda l, j: (0, 0)),
            pl.BlockSpec((1, D_MODEL, tn), lambda l, j: (l, 0, j)),
            pl.BlockSpec((1, 1, tn), lambda l, j: (l, 0, j)),
        ],
        out_specs=pl.BlockSpec((1, MOD_ROWS, tn), lambda l, j: (l, 0, j)),
        out_shape=jax.ShapeDtypeStruct((DEPTH, MOD_ROWS, 3 * D_MODEL), F32),
        compiler_params=pltpu.CompilerParams(dimension_semantics=("arbitrary", "arbitrary")),
        name="adaln_modulation",
    )(cvec, w_ada, b_ada.reshape(DEPTH, 1, 3 * D_MODEL))


def _chunk_sum_matrices():
    r = lax.broadcasted_iota(jnp.int32, (ROW_TILE, ROW_TILE), 0)
    c = lax.broadcasted_iota(jnp.int32, (ROW_TILE, ROW_TILE), 1)
    same = (r // GLA_CHUNK) == (c // GLA_CHUNK)
    prefix = jnp.where(same & (c <= r), 1.0, 0.0).astype(BF16)
    suffix = jnp.where(same & (c >= r), 1.0, 0.0).astype(BF16)
    return prefix, suffix


def _gla_kernel(*refs, seq, has_s0, emit_state):
    n_tiles = seq // ROW_TILE
    n_chunks = seq // GLA_CHUNK
    chunks_per_tile = ROW_TILE // GLA_CHUNK
    it = iter(refs)
    x_ref, mod_ref, ng_ref, win_ref, wa1_ref, wa2_ref, ba_ref, on_ref, wout_ref = (next(it) for _ in range(9))
    s0_ref = next(it) if has_s0 else None
    y_ref = next(it)
    sout_ref = next(it) if emit_state else None
    qe_s, ke_s, kd_s, v_s, gate_s, dec_s, o_s, st_s = (next(it) for _ in range(8))

    ng = ng_ref[...]
    prefix_m, suffix_m = _chunk_sum_matrices()

    def project(t, carry):
        rows = pl.ds(pl.multiple_of(t * ROW_TILE, ROW_TILE), ROW_TILE)
        hb = _modulated_norm(x_ref[0, rows, :], ng, mod_ref)
        q = _dot(hb, win_ref[:, 0:GLA_QD]) * (GLA_DK ** -0.5)
        k = _dot(hb, win_ref[:, GLA_QD:2 * GLA_QD])
        v_s[rows, :] = _dot(hb, win_ref[:, 2 * GLA_QD:2 * GLA_QD + GLA_VD]).astype(BF16)
        gate_s[rows, :] = _dot(hb, win_ref[:, 2 * GLA_QD + GLA_VD:])
        low = _dot(hb, wa1_ref[...]).astype(BF16)
        z = _dot(low, wa2_ref[...]) + ba_ref[...]
        logg = _log_sigmoid(z) * (1.0 / GLA_TAU)
        for d, csum_m in enumerate((prefix_m, suffix_m)):
            g_hi, g_lo = _split_bf16(logg[:, d * GLA_QD:(d + 1) * GLA_QD])
            b = _dot(csum_m, g_hi) + _dot(csum_m, g_lo)
            edge = GLA_CHUNK - 1 if d == 0 else 0
            totals = [b[j * GLA_CHUNK + edge:j * GLA_CHUNK + edge + 1, :] for j in range(chunks_per_tile)]
            b_last = jnp.concatenate([jnp.broadcast_to(tt, (GLA_CHUNK, GLA_QD)) for tt in totals], axis=0)
            qe_s[d, rows, :] = (q * jnp.exp(b)).astype(BF16)
            ke_s[d, rows, :] = (k * jnp.exp(-b)).astype(BF16)
            kd_s[d, rows, :] = (k * jnp.exp(b_last - b)).astype(BF16)
            for j in range(chunks_per_tile):
                dec_s[d, t * chunks_per_tile + j] = jnp.exp(totals[j])
        return carry

    lax.fori_loop(0, n_tiles, project, 0)

    for d in range(2):
        for h in range(GLA_HEADS):
            if has_s0:
                st_s[d, h] = s0_ref[0, 0, d, h].T
            else:
                st_s[d, h] = jnp.zeros((GLA_DV, GLA_DK), F32)

    ri = lax.broadcasted_iota(jnp.int32, (GLA_CHUNK, GLA_CHUNK), 0)
    ci = lax.broadcasted_iota(jnp.int32, (GLA_CHUNK, GLA_CHUNK), 1)
    masks = (ci <= ri, ci >= ri)

    def scan_step(i, first_visit):
        for d in range(2):
            j = i if d == 0 else n_chunks - 1 - i
            rows = pl.ds(pl.multiple_of(j * GLA_CHUNK, GLA_CHUNK), GLA_CHUNK)
            dec = dec_s[d, j]
            for h in range(GLA_HEADS):
                kc = slice(h * GLA_DK, (h + 1) * GLA_DK)
                vc = slice(h * GLA_DV, (h + 1) * GLA_DV)
                qe = qe_s[d, rows, kc]
                v = v_s[rows, vc]
                a = jnp.where(masks[d], _dot_nt(qe, ke_s[d, rows, kc]), 0.0).astype(BF16)
                st = st_s[d, h]
                o = _dot(a, v) + _dot_nt(qe, st.astype(BF16))
                if first_visit:
                    o_s[rows, vc] = o
                else:
                    o_s[rows, vc] += o
                st_s[d, h] = dec[:, kc] * st + _dot_tn(v, kd_s[d, rows, kc])

    def first_half(i, carry):
        scan_step(i, True)
        return carry

    def second_half(i, carry):
        scan_step(i, False)
        return carry

    lax.fori_loop(0, n_chunks // 2, first_half, 0)
    lax.fori_loop(n_chunks // 2, n_chunks, second_half, 0)

    if emit_state:
        for d in range(2):
            for h in range(GLA_HEADS):
                sout_ref[0, d, h] = st_s[d, h].T

    on = on_ref[...]
    res_gate = mod_ref[0, 0, :, 2 * D_MODEL:]

    def finish(t, carry):
        rows = pl.ds(pl.multiple_of(t * ROW_TILE, ROW_TILE), ROW_TILE)
        o = _head_rms(o_s[rows, :], on, GLA_DV)
        y = _dot((o * _silu(gate_s[rows, :])).astype(BF16), wout_ref[...])
        y_ref[0, rows, :] = x_ref[0, rows, :] + res_gate * y
        return carry

    lax.fori_loop(0, n_tiles, finish, 0)


def _resident(shape):
    return pl.BlockSpec(shape, lambda b: (0,) * len(shape), pipeline_mode=pl.Buffered(1))


def _gla_layer(x, mods, layer, mod_row, ng, win, wa1, wa2, ba, on, wout, state_in, gla_idx):
    batch, seq, _ = x.shape
    has_s0 = state_in is not None
    emit_state = not has_s0
    n_chunks = seq // GLA_CHUNK
    in_specs = [
        pl.BlockSpec((1, seq, D_MODEL), lambda b: (b, 0, 0)),
        pl.BlockSpec((1, 1, 1, 3 * D_MODEL), lambda b: (layer, mod_row(b), 0, 0)),
        _resident((1, D_MODEL)),
        _resident((D_MODEL, 2 * GLA_QD + 2 * GLA_VD)),
        _resident((D_MODEL, RANK_PAD)),
        _resident((RANK_PAD, 2 * GLA_QD)),
        _resident((1, 2 * GLA_QD)),
        _resident((1, GLA_DV)),
        _resident((GLA_VD, D_MODEL)),
    ]
    args = [x, mods, ng, win, wa1, wa2, ba, on, wout]
    if has_s0:
        in_specs.append(pl.BlockSpec((1, 1, 2, GLA_HEADS, GLA_DK, GLA_DV),
                                     lambda b: (b, gla_idx, 0, 0, 0, 0)))
        args.append(state_in)
    out_specs = [pl.BlockSpec((1, seq, D_MODEL), lambda b: (b, 0, 0))]
    out_shape = [jax.ShapeDtypeStruct((batch, seq, D_MODEL), F32)]
    if emit_state:
        out_specs.append(pl.BlockSpec((1, 2, GLA_HEADS, GLA_DK, GLA_DV), lambda b: (b, 0, 0, 0, 0)))
        out_shape.append(jax.ShapeDtypeStruct((batch, 2, GLA_HEADS, GLA_DK, GLA_DV), F32))
    scratch = [
        pltpu.VMEM((2, seq, GLA_QD), BF16),
        pltpu.VMEM((2, seq, GLA_QD), BF16),
        pltpu.VMEM((2, seq, GLA_QD), BF16),
        pltpu.VMEM((seq, GLA_VD), BF16),
        pltpu.VMEM((seq, GLA_VD), F32),
        pltpu.VMEM((2, n_chunks, 1, GLA_QD), F32),
        pltpu.VMEM((seq, GLA_VD), F32),
        pltpu.VMEM((2, GLA_HEADS, GLA_DV, GLA_DK), F32),
    ]
    outs = pl.pallas_call(
        functools.partial(_gla_kernel, seq=seq, has_s0=has_s0, emit_state=emit_state),
        grid=(batch,),
        in_specs=in_specs,
        out_specs=out_specs,
        out_shape=out_shape,
        scratch_shapes=scratch,
        compiler_params=pltpu.CompilerParams(dimension_semantics=("arbitrary",),
                                             vmem_limit_bytes=V7X_VMEM_LIMIT_BYTES),
        name=f"gla_layer_seq{seq}",
    )(*args)
    return (outs[0], outs[1]) if emit_state else (outs[0], None)


def _rope_swap(x):
    lane = lax.broadcasted_iota(jnp.int32, x.shape, 1)
    quarter = HEAD_DIM // 4
    first = (lane % (2 * quarter)) < quarter
    return jnp.where(first, pltpu.roll(x, HEAD_DIM - quarter, 1), pltpu.roll(x, quarter, 1))


def _att_kernel(*refs, seq, latent):
    n_tiles = seq // ROW_TILE
    n_keys = seq + (refs[9].shape[2] if latent else 0)
    it = iter(refs)
    x_ref, mod_ref, ng_ref, win_ref, qn_ref, kn_ref, wout_ref = (next(it) for _ in range(7))
    if latent:
        cos_ref, sin_ref, ck_ref, cv_ref = (next(it) for _ in range(4))
    y_ref = next(it)
    if not latent:
        kout_ref, vout_ref = next(it), next(it)
    q_s, k_s, v_s, gate_s, ao_s = (next(it) for _ in range(5))

    ng = ng_ref[...]
    qn = qn_ref[...]
    kn = kn_ref[...]

    def project(t, carry):
        rows = pl.ds(pl.multiple_of(t * ROW_TILE, ROW_TILE), ROW_TILE)
        hb = _modulated_norm(x_ref[0, rows, :], ng, mod_ref)
        q = _head_rms(_dot(hb, win_ref[:, 0:ATT_QD]), qn, HEAD_DIM)
        k = _head_rms(_dot(hb, win_ref[:, ATT_QD:ATT_QD + ATT_KD]), kn, HEAD_DIM)
        v = _dot(hb, win_ref[:, ATT_QD + ATT_KD:ATT_QD + 2 * ATT_KD])
        gate_s[rows, :] = _dot(hb, win_ref[:, ATT_QD + 2 * ATT_KD:])
        if latent:
            cos = cos_ref[rows, :]
            sin = sin_ref[rows, :]
            q = jnp.concatenate(
                [q[:, h * HEAD_DIM:(h + 1) * HEAD_DIM] * cos + _rope_swap(q[:, h * HEAD_DIM:(h + 1) * HEAD_DIM]) * sin
                 for h in range(ATT_HEADS)], axis=-1)
            k = jnp.concatenate(
                [k[:, h * HEAD_DIM:(h + 1) * HEAD_DIM] * cos + _rope_swap(k[:, h * HEAD_DIM:(h + 1) * HEAD_DIM]) * sin
                 for h in range(ATT_KV_HEADS)], axis=-1)
        else:
            kout_ref[0, rows, :] = k
            vout_ref[0, rows, :] = v
        q_s[rows, :] = q.astype(BF16)
        k_s[rows, :] = k.astype(BF16)
        v_s[rows, :] = v.astype(BF16)
        return carry

    lax.fori_loop(0, n_tiles, project, 0)
    if latent:
        k_s[seq:n_keys, :] = ck_ref[0, 0].astype(BF16)
        v_s[seq:n_keys, :] = cv_ref[0, 0].astype(BF16)

    exp2_scale = (HEAD_DIM ** -0.5) * math.log2(math.e)

    def attend(t, carry):
        rows = pl.ds(pl.multiple_of(t * ROW_TILE, ROW_TILE), ROW_TILE)
        for kv in range(ATT_KV_HEADS):
            kc = slice(kv * HEAD_DIM, (kv + 1) * HEAD_DIM)
            keys = k_s[:, kc]
            vals = v_s[:, kc]
            for g in range(ATT_GROUP):
                qc = slice((kv * ATT_GROUP + g) * HEAD_DIM, (kv * ATT_GROUP + g + 1) * HEAD_DIM)
                s = _dot_nt(q_s[rows, qc], keys)
                m = jnp.max(s, axis=-1, keepdims=True)
                p = jnp.exp2((s - m) * exp2_scale)
                denom = jnp.sum(p, axis=-1, keepdims=True)
                ao_s[rows, qc] = _dot(p.astype(BF16), vals) * (1.0 / denom)
        return carry

    lax.fori_loop(0, n_tiles, attend, 0)

    res_gate = mod_ref[0, 0, :, 2 * D_MODEL:]

    def finish(t, carry):
        rows = pl.ds(pl.multiple_of(t * ROW_TILE, ROW_TILE), ROW_TILE)
        y = _dot((ao_s[rows, :] * _silu(gate_s[rows, :])).astype(BF16), wout_ref[...])
        y_ref[0, rows, :] = x_ref[0, rows, :] + res_gate * y
        return carry

    lax.fori_loop(0, n_tiles, finish, 0)


def _att_layer(x, mods, layer, mod_row, ng, win, qn, kn, wout, latent_inputs, att_idx):
    batch, seq, _ = x.shape
    latent = latent_inputs is not None
    in_specs = [
        pl.BlockSpec((1, seq, D_MODEL), lambda b: (b, 0, 0)),
        pl.BlockSpec((1, 1, 1, 3 * D_MODEL), lambda b: (layer, mod_row(b), 0, 0)),
        _resident((1, D_MODEL)),
        _resident((D_MODEL, 2 * ATT_QD + 2 * ATT_KD)),
        _resident((1, HEAD_DIM)),
        _resident((1, HEAD_DIM)),
        _resident((ATT_QD, D_MODEL)),
    ]
    args = [x, mods, ng, win, qn, kn, wout]
    n_keys = seq
    if latent:
        cos, sin, cache_k, cache_v = latent_inputs
        past = cache_k.shape[2]
        n_keys = seq + past
        in_specs += [
            _resident((seq, HEAD_DIM)),
            _resident((seq, HEAD_DIM)),
            pl.BlockSpec((1, 1, past, ATT_KD), lambda b: (b, att_idx, 0, 0)),
            pl.BlockSpec((1, 1, past, ATT_KD), lambda b: (b, att_idx, 0, 0)),
        ]
        args += [cos, sin, cache_k, cache_v]
    out_specs = [pl.BlockSpec((1, seq, D_MODEL), lambda b: (b, 0, 0))]
    out_shape = [jax.ShapeDtypeStruct((batch, seq, D_MODEL), F32)]
    if not latent:
        out_specs += [pl.BlockSpec((1, seq, ATT_KD), lambda b: (b, 0, 0))] * 2
        out_shape += [jax.ShapeDtypeStruct((batch, seq, ATT_KD), F32)] * 2
    scratch = [
        pltpu.VMEM((seq, ATT_QD), BF16),
        pltpu.VMEM((n_keys, ATT_KD), BF16),
        pltpu.VMEM((n_keys, ATT_KD), BF16),
        pltpu.VMEM((seq, ATT_QD), F32),
        pltpu.VMEM((seq, ATT_QD), F32),
    ]
    outs = pl.pallas_call(
        functools.partial(_att_kernel, seq=seq, latent=latent),
        grid=(batch,),
        in_specs=in_specs,
        out_specs=out_specs,
        out_shape=out_shape,
        scratch_shapes=scratch,
        compiler_params=pltpu.CompilerParams(dimension_semantics=("arbitrary",),
                                             vmem_limit_bytes=V7X_VMEM_LIMIT_BYTES),
        name=f"att_layer_seq{seq}",
    )(*args)
    return outs


def _rope_tables(seq):
    half = HEAD_DIM // 2
    nf = half // 2
    pos = np.arange(seq)
    freqs = jnp.asarray(ROPE_THETA, F32) ** (-jnp.arange(nf, dtype=F32) / nf)
    ang_row = jnp.asarray(pos // GRID_W, F32)[:, None] * freqs[None, :]
    ang_col = jnp.asarray(pos % GRID_W, F32)[:, None] * freqs[None, :]
    cos = jnp.concatenate([jnp.cos(ang_row)] * 2 + [jnp.cos(ang_col)] * 2, axis=-1)
    sin = jnp.concatenate([-jnp.sin(ang_row), jnp.sin(ang_row), -jnp.sin(ang_col), jnp.sin(ang_col)], axis=-1)
    return cos, sin


def kernel(x_prompt, x_sample, state_gla, cache_k, cache_v, c, c_ctx, norm_g, w_ada, b_ada,
           gla_w_in, gla_wa1, gla_wa2, gla_ba, gla_onorm, gla_w_out,
           att_w_in, att_qnorm, att_knorm, att_w_out):
    n_dec = x_sample.shape[0]
    assert 1 + n_dec <= MOD_ROWS
    assert x_prompt.shape[1] % ROW_TILE == 0 and x_sample.shape[1] % ROW_TILE == 0

    cvec = jnp.zeros((MOD_ROWS, D_MODEL), F32).at[0].set(c_ctx).at[1:1 + n_dec].set(c)
    mods = _modulations(cvec, w_ada, b_ada).reshape(DEPTH, MOD_ROWS, 1, 3 * D_MODEL)
    ctx_row = lambda b: 0
    dec_row = lambda b: b + 1

    cos, sin = _rope_tables(x_sample.shape[1])
    past = cache_k.shape[2]
    cache_k2 = cache_k.reshape(n_dec, -1, past, ATT_KD)
    cache_v2 = cache_v.reshape(n_dec, -1, past, ATT_KD)

    xp, xs = x_prompt, x_sample
    gla_states, ctx_keys, ctx_vals = [], [], []
    for l in range(DEPTH):
        i = l // 2
        ng = norm_g[l].reshape(1, D_MODEL)
        if l % 2 == 0:
            win = gla_w_in[i].astype(BF16)
            wout = gla_w_out[i].astype(BF16)
            wa1 = jnp.zeros((D_MODEL, RANK_PAD), F32)
            wa1 = wa1.at[:, 0:GLA_RANK].set(gla_wa1[i, 0]).at[:, GLA_RANK:2 * GLA_RANK].set(gla_wa1[i, 1])
            wa2 = jnp.zeros((RANK_PAD, 2 * GLA_QD), F32)
            wa2 = wa2.at[0:GLA_RANK, 0:GLA_QD].set(gla_wa2[i, 0])
            wa2 = wa2.at[GLA_RANK:2 * GLA_RANK, GLA_QD:].set(gla_wa2[i, 1])
            ba = gla_ba[i].reshape(1, 2 * GLA_QD)
            on = gla_onorm[i].reshape(1, GLA_DV)
            common = (ng, win, wa1.astype(BF16), wa2.astype(BF16), ba, on, wout)
            xp, st = _gla_layer(xp, mods, l, ctx_row, *common, None, i)
            xs, _ = _gla_layer(xs, mods, l, dec_row, *common, state_gla, i)
            gla_states.append(st)
        else:
            win = att_w_in[i].astype(BF16)
            wout = att_w_out[i].astype(BF16)
            qn = att_qnorm[i].reshape(1, HEAD_DIM)
            kn = att_knorm[i].reshape(1, HEAD_DIM)
            xp, kp, vp = _att_layer(xp, mods, l, ctx_row, ng, win, qn, kn, wout, None, i)
            (xs,) = _att_layer(xs, mods, l, dec_row, ng, win, qn, kn, wout, (cos, sin, cache_k2, cache_v2), i)
            ctx_keys.append(kp.reshape(kp.shape[0], kp.shape[1], ATT_KV_HEADS, HEAD_DIM))
            ctx_vals.append(vp.reshape(vp.shape[0], vp.shape[1], ATT_KV_HEADS, HEAD_DIM))
    return (xp, xs, jnp.stack(gla_states, axis=1), jnp.stack(ctx_keys, axis=1), jnp.stack(ctx_vals, axis=1))
```

```python
import functools
import math

import jax
import jax.numpy as jnp
import numpy as np
from jax import lax
from jax.experimental import pallas as pl
from jax.experimental.pallas import tpu as pltpu

D_MODEL = 1024
DEPTH = 4
GRID_W = 64
GLA_HEADS = 4
GLA_DK = 128
GLA_DV = 256
GLA_RANK = 16
GLA_TAU = 16.0
GLA_CHUNK = 64
GLA_QD = GLA_HEADS * GLA_DK
GLA_VD = GLA_HEADS * GLA_DV
HEAD_DIM = 128
ATT_HEADS = 8
ATT_KV_HEADS = 2
ATT_GROUP = ATT_HEADS // ATT_KV_HEADS
ATT_QD = ATT_HEADS * HEAD_DIM
ATT_KD = ATT_KV_HEADS * HEAD_DIM
ROPE_THETA = 10000.0
EPS = 1e-6

ROW_TILE = 256
MOD_ROWS = 8
RANK_PAD = 128
V7X_VMEM_LIMIT_BYTES = 60 * 1024 * 1024

F32 = jnp.float32
BF16 = jnp.bfloat16
_NT = (((1,), (1,)), ((), ()))
_TN = (((0,), (0,)), ((), ()))


def _dot(a, b):
    return jnp.dot(a, b, preferred_element_type=F32)


def _dot_nt(a, b):
    return lax.dot_general(a, b, _NT, preferred_element_type=F32)


def _dot_tn(a, b):
    return lax.dot_general(a, b, _TN, preferred_element_type=F32)


def _silu(x):
    return x * (1.0 / (1.0 + jnp.exp(-x)))


def _log_sigmoid(z):
    return jnp.minimum(z, 0.0) - jnp.log1p(jnp.exp(-jnp.abs(z)))


def _split_bf16(x):
    hi = x.astype(BF16)
    lo = (x - hi.astype(F32)).astype(BF16)
    return hi, lo


def _modulated_norm(x, ng, mod_ref):
    shift = mod_ref[0, 0, :, 0:D_MODEL]
    scale = mod_ref[0, 0, :, D_MODEL:2 * D_MODEL]
    ms = jnp.mean(x * x, axis=-1, keepdims=True)
    hn = (x * lax.rsqrt(ms + EPS)) * ng
    return (hn * (1.0 + scale) + shift).astype(BF16)


def _head_rms(x, g, width):
    outs = []
    for h in range(x.shape[-1] // width):
        xh = x[:, h * width:(h + 1) * width]
        ms = jnp.mean(xh * xh, axis=-1, keepdims=True)
        outs.append((xh * lax.rsqrt(ms + EPS)) * g)
    return jnp.concatenate(outs, axis=-1)


def _mod_kernel(c_ref, w_ref, b_ref, o_ref):
    s_hi, s_lo = _split_bf16(_silu(c_ref[...]))
    w_hi, w_lo = _split_bf16(w_ref[0])
    acc = _dot(s_hi, w_hi) + _dot(s_lo, w_hi) + _dot(s_hi, w_lo)
    o_ref[0] = acc + b_ref[0]


def _modulations(cvec, w_ada, b_ada):
    tn = D_MODEL
    n_tiles = 3 * D_MODEL // tn
    return pl.pallas_call(
        _mod_kernel,
        grid=(DEPTH, n_tiles),
        in_specs=[
            pl.BlockSpec((MOD_ROWS, D_MODEL), lambda l, j: (0, 0)),
            pl.BlockSpec((1, D_MODEL, tn), lambda l, j: (l, 0, j)),
            pl.BlockSpec((1, 1, tn), lambda l, j: (l, 0, j)),
        ],
        out_specs=pl.BlockSpec((1, MOD_ROWS, tn), lambda l, j: (l, 0, j)),
        out_shape=jax.ShapeDtypeStruct((DEPTH, MOD_ROWS, 3 * D_MODEL), F32),
        compiler_params=pltpu.CompilerParams(dimension_semantics=("arbitrary", "arbitrary")),
        name="adaln_modulation",
    )(cvec, w_ada, b_ada.reshape(DEPTH, 1, 3 * D_MODEL))


def _chunk_sum_matrices():
    r = lax.broadcasted_iota(jnp.int32, (ROW_TILE, ROW_TILE), 0)
    c = lax.broadcasted_iota(jnp.int32, (ROW_TILE, ROW_TILE), 1)
    same = (r // GLA_CHUNK) == (c // GLA_CHUNK)
    prefix = jnp.where(same & (c <= r), 1.0, 0.0).astype(BF16)
    suffix = jnp.where(same & (c >= r), 1.0, 0.0).astype(BF16)
    return prefix, suffix


def _gla_kernel(*refs, seq, has_s0, emit_state, state_alias):
    n_tiles = seq // ROW_TILE
    n_chunks = seq // GLA_CHUNK
    chunks_per_tile = ROW_TILE // GLA_CHUNK
    it = iter(refs)
    x_ref, mod_ref, ng_ref, win_ref, wa1_ref, wa2_ref, ba_ref, on_ref, wout_ref = (next(it) for _ in range(9))
    s0_ref = next(it) if has_s0 else None
    if state_alias:
        next(it)
    y_ref = next(it)
    sout_ref = next(it) if emit_state else None
    qe_s, ke_s, kd_s, v_s, gate_s, dec_s, o_s, st_s = (next(it) for _ in range(8))

    ng = ng_ref[...]
    prefix_m, suffix_m = _chunk_sum_matrices()

    def project(t, carry):
        rows = pl.ds(pl.multiple_of(t * ROW_TILE, ROW_TILE), ROW_TILE)
        hb = _modulated_norm(x_ref[0, rows, :], ng, mod_ref)
        q = _dot(hb, win_ref[:, 0:GLA_QD]) * (GLA_DK ** -0.5)
        k = _dot(hb, win_ref[:, GLA_QD:2 * GLA_QD])
        v_s[rows, :] = _dot(hb, win_ref[:, 2 * GLA_QD:2 * GLA_QD + GLA_VD]).astype(BF16)
        gate_s[rows, :] = _dot(hb, win_ref[:, 2 * GLA_QD + GLA_VD:])
        low = _dot(hb, wa1_ref[...]).astype(BF16)
        z = _dot(low, wa2_ref[...]) + ba_ref[...]
        logg = _log_sigmoid(z) * (1.0 / GLA_TAU)
        for d, csum_m in enumerate((prefix_m, suffix_m)):
            g_hi, g_lo = _split_bf16(logg[:, d * GLA_QD:(d + 1) * GLA_QD])
            b = _dot(csum_m, g_hi) + _dot(csum_m, g_lo)
            edge = GLA_CHUNK - 1 if d == 0 else 0
            totals = [b[j * GLA_CHUNK + edge:j * GLA_CHUNK + edge + 1, :] for j in range(chunks_per_tile)]
            b_last = jnp.concatenate([jnp.broadcast_to(tt, (GLA_CHUNK, GLA_QD)) for tt in totals], axis=0)
            qe_s[d, rows, :] = (q * jnp.exp(b)).astype(BF16)
            ke_s[d, rows, :] = (k * jnp.exp(-b)).astype(BF16)
            kd_s[d, rows, :] = (k * jnp.exp(b_last - b)).astype(BF16)
            for j in range(chunks_per_tile):
                dec_s[d, t * chunks_per_tile + j] = jnp.exp(totals[j])
        return carry

    lax.fori_loop(0, n_tiles, project, 0)

    for d in range(2):
        for h in range(GLA_HEADS):
            if has_s0:
                st_s[d, h] = s0_ref[0, 0, d, h].T
            else:
                st_s[d, h] = jnp.zeros((GLA_DV, GLA_DK), F32)

    ri = lax.broadcasted_iota(jnp.int32, (GLA_CHUNK, GLA_CHUNK), 0)
    ci = lax.broadcasted_iota(jnp.int32, (GLA_CHUNK, GLA_CHUNK), 1)
    masks = (ci <= ri, ci >= ri)

    def scan_step(i, first_visit):
        for d in range(2):
            j = i if d == 0 else n_chunks - 1 - i
            rows = pl.ds(pl.multiple_of(j * GLA_CHUNK, GLA_CHUNK), GLA_CHUNK)
            dec = dec_s[d, j]
            for h in range(GLA_HEADS):
                kc = slice(h * GLA_DK, (h + 1) * GLA_DK)
                vc = slice(h * GLA_DV, (h + 1) * GLA_DV)
                qe = qe_s[d, rows, kc]
                v = v_s[rows, vc]
                a = jnp.where(masks[d], _dot_nt(qe, ke_s[d, rows, kc]), 0.0).astype(BF16)
                st = st_s[d, h]
                o = _dot(a, v) + _dot_nt(qe, st.astype(BF16))
                if first_visit:
                    o_s[rows, vc] = o
                else:
                    o_s[rows, vc] += o
                st_s[d, h] = dec[:, kc] * st + _dot_tn(v, kd_s[d, rows, kc])

    def first_half(i, carry):
        scan_step(i, True)
        return carry

    def second_half(i, carry):
        scan_step(i, False)
        return carry

    lax.fori_loop(0, n_chunks // 2, first_half, 0)
    lax.fori_loop(n_chunks // 2, n_chunks, second_half, 0)

    if emit_state:
        for d in range(2):
            for h in range(GLA_HEADS):
                sout_ref[0, 0, d, h] = st_s[d, h].T

    on = on_ref[...]
    res_gate = mod_ref[0, 0, :, 2 * D_MODEL:]

    def finish(t, carry):
        rows = pl.ds(pl.multiple_of(t * ROW_TILE, ROW_TILE), ROW_TILE)
        o = _head_rms(o_s[rows, :], on, GLA_DV)
        y = _dot((o * _silu(gate_s[rows, :])).astype(BF16), wout_ref[...])
        y_ref[0, rows, :] = x_ref[0, rows, :] + res_gate * y
        return carry

    lax.fori_loop(0, n_tiles, finish, 0)


def _resident(shape):
    return pl.BlockSpec(shape, lambda b: (0,) * len(shape), pipeline_mode=pl.Buffered(1))


def _gla_layer(x, mods, layer, mod_row, ng, win, wa1, wa2, ba, on, wout, state_in, gla_idx, n_gla, states_so_far):
    batch, seq, _ = x.shape
    has_s0 = state_in is not None
    emit_state = not has_s0
    n_chunks = seq // GLA_CHUNK
    in_specs = [
        pl.BlockSpec((1, seq, D_MODEL), lambda b: (b, 0, 0)),
        pl.BlockSpec((1, 1, 1, 3 * D_MODEL), lambda b: (layer, mod_row(b), 0, 0)),
        _resident((1, D_MODEL)),
        _resident((D_MODEL, 2 * GLA_QD + 2 * GLA_VD)),
        _resident((D_MODEL, RANK_PAD)),
        _resident((RANK_PAD, 2 * GLA_QD)),
        _resident((1, 2 * GLA_QD)),
        _resident((1, GLA_DV)),
        _resident((GLA_VD, D_MODEL)),
    ]
    args = [x, mods, ng, win, wa1, wa2, ba, on, wout]
    if has_s0:
        in_specs.append(pl.BlockSpec((1, 1, 2, GLA_HEADS, GLA_DK, GLA_DV),
                                     lambda b: (b, gla_idx, 0, 0, 0, 0)))
        args.append(state_in)
    aliases = {}
    state_alias = emit_state and states_so_far is not None
    if state_alias:
        aliases[len(args)] = 1
        in_specs.append(pl.BlockSpec(memory_space=pl.ANY))
        args.append(states_so_far)
    out_specs = [pl.BlockSpec((1, seq, D_MODEL), lambda b: (b, 0, 0))]
    out_shape = [jax.ShapeDtypeStruct((batch, seq, D_MODEL), F32)]
    if emit_state:
        out_specs.append(pl.BlockSpec((1, 1, 2, GLA_HEADS, GLA_DK, GLA_DV), lambda b: (b, gla_idx, 0, 0, 0, 0)))
        out_shape.append(jax.ShapeDtypeStruct((batch, n_gla, 2, GLA_HEADS, GLA_DK, GLA_DV), F32))
    scratch = [
        pltpu.VMEM((2, seq, GLA_QD), BF16),
        pltpu.VMEM((2, seq, GLA_QD), BF16),
        pltpu.VMEM((2, seq, GLA_QD), BF16),
        pltpu.VMEM((seq, GLA_VD), BF16),
        pltpu.VMEM((seq, GLA_VD), F32),
        pltpu.VMEM((2, n_chunks, 1, GLA_QD), F32),
        pltpu.VMEM((seq, GLA_VD), F32),
        pltpu.VMEM((2, GLA_HEADS, GLA_DV, GLA_DK), F32),
    ]
    outs = pl.pallas_call(
        functools.partial(_gla_kernel, seq=seq, has_s0=has_s0, emit_state=emit_state, state_alias=state_alias),
        grid=(batch,),
        in_specs=in_specs,
        out_specs=out_specs,
        out_shape=out_shape,
        scratch_shapes=scratch,
        input_output_aliases=aliases,
        compiler_params=pltpu.CompilerParams(dimension_semantics=("arbitrary",),
                                             vmem_limit_bytes=V7X_VMEM_LIMIT_BYTES),
        name=f"gla_layer_seq{seq}",
    )(*args)
    return (outs[0], outs[1]) if emit_state else (outs[0], None)


def _rope_swap(x):
    lane = lax.broadcasted_iota(jnp.int32, x.shape, 1)
    quarter = HEAD_DIM // 4
    first = (lane % (2 * quarter)) < quarter
    return jnp.where(first, pltpu.roll(x, HEAD_DIM - quarter, 1), pltpu.roll(x, quarter, 1))


def _att_kernel(*refs, seq, latent, cache_alias):
    n_tiles = seq // ROW_TILE
    n_keys = seq + (refs[9].shape[2] if latent else 0)
    it = iter(refs)
    x_ref, mod_ref, ng_ref, win_ref, qn_ref, kn_ref, wout_ref = (next(it) for _ in range(7))
    if latent:
        cos_ref, sin_ref, ck_ref, cv_ref = (next(it) for _ in range(4))
    if cache_alias:
        next(it), next(it)
    y_ref = next(it)
    if not latent:
        kout_ref, vout_ref = next(it), next(it)
    q_s, k_s, v_s, gate_s, ao_s = (next(it) for _ in range(5))

    ng = ng_ref[...]
    qn = qn_ref[...]
    kn = kn_ref[...]

    def project(t, carry):
        rows = pl.ds(pl.multiple_of(t * ROW_TILE, ROW_TILE), ROW_TILE)
        hb = _modulated_norm(x_ref[0, rows, :], ng, mod_ref)
        q = _head_rms(_dot(hb, win_ref[:, 0:ATT_QD]), qn, HEAD_DIM)
        k = _head_rms(_dot(hb, win_ref[:, ATT_QD:ATT_QD + ATT_KD]), kn, HEAD_DIM)
        v = _dot(hb, win_ref[:, ATT_QD + ATT_KD:ATT_QD + 2 * ATT_KD])
        gate_s[rows, :] = _dot(hb, win_ref[:, ATT_QD + 2 * ATT_KD:])
        if latent:
            cos = cos_ref[rows, :]
            sin = sin_ref[rows, :]
            q = jnp.concatenate(
                [q[:, h * HEAD_DIM:(h + 1) * HEAD_DIM] * cos + _rope_swap(q[:, h * HEAD_DIM:(h + 1) * HEAD_DIM]) * sin
                 for h in range(ATT_HEADS)], axis=-1)
            k = jnp.concatenate(
                [k[:, h * HEAD_DIM:(h + 1) * HEAD_DIM] * cos + _rope_swap(k[:, h * HEAD_DIM:(h + 1) * HEAD_DIM]) * sin
                 for h in range(ATT_KV_HEADS)], axis=-1)
        else:
            for h in range(ATT_KV_HEADS):
                kout_ref[0, 0, rows, h, :] = k[:, h * HEAD_DIM:(h + 1) * HEAD_DIM]
                vout_ref[0, 0, rows, h, :] = v[:, h * HEAD_DIM:(h + 1) * HEAD_DIM]
        q_s[rows, :] = q.astype(BF16)
        k_s[rows, :] = k.astype(BF16)
        v_s[rows, :] = v.astype(BF16)
        return carry

    lax.fori_loop(0, n_tiles, project, 0)
    if latent:
        for h in range(ATT_KV_HEADS):
            k_s[seq:n_keys, h * HEAD_DIM:(h + 1) * HEAD_DIM] = ck_ref[0, 0, :, h, :].astype(BF16)
            v_s[seq:n_keys, h * HEAD_DIM:(h + 1) * HEAD_DIM] = cv_ref[0, 0, :, h, :].astype(BF16)

    exp2_scale = (HEAD_DIM ** -0.5) * math.log2(math.e)

    def attend(t, carry):
        rows = pl.ds(pl.multiple_of(t * ROW_TILE, ROW_TILE), ROW_TILE)
        for kv in range(ATT_KV_HEADS):
            kc = slice(kv * HEAD_DIM, (kv + 1) * HEAD_DIM)
            keys = k_s[:, kc]
            vals = v_s[:, kc]
            for g in range(ATT_GROUP):
                qc = slice((kv * ATT_GROUP + g) * HEAD_DIM, (kv * ATT_GROUP + g + 1) * HEAD_DIM)
                s = _dot_nt(q_s[rows, qc], keys)
                m = jnp.max(s, axis=-1, keepdims=True)
                p = jnp.exp2((s - m) * exp2_scale)
                denom = jnp.sum(p, axis=-1, keepdims=True)
                ao_s[rows, qc] = _dot(p.astype(BF16), vals) * (1.0 / denom)
        return carry

    lax.fori_loop(0, n_tiles, attend, 0)

    res_gate = mod_ref[0, 0, :, 2 * D_MODEL:]

    def finish(t, carry):
        rows = pl.ds(pl.multiple_of(t * ROW_TILE, ROW_TILE), ROW_TILE)
        y = _dot((ao_s[rows, :] * _silu(gate_s[rows, :])).astype(BF16), wout_ref[...])
        y_ref[0, rows, :] = x_ref[0, rows, :] + res_gate * y
        return carry

    lax.fori_loop(0, n_tiles, finish, 0)


def _att_layer(x, mods, layer, mod_row, ng, win, qn, kn, wout, latent_inputs, att_idx, n_att, caches_so_far):
    batch, seq, _ = x.shape
    latent = latent_inputs is not None
    in_specs = [
        pl.BlockSpec((1, seq, D_MODEL), lambda b: (b, 0, 0)),
        pl.BlockSpec((1, 1, 1, 3 * D_MODEL), lambda b: (layer, mod_row(b), 0, 0)),
        _resident((1, D_MODEL)),
        _resident((D_MODEL, 2 * ATT_QD + 2 * ATT_KD)),
        _resident((1, HEAD_DIM)),
        _resident((1, HEAD_DIM)),
        _resident((ATT_QD, D_MODEL)),
    ]
    args = [x, mods, ng, win, qn, kn, wout]
    n_keys = seq
    if latent:
        cos, sin, cache_k, cache_v = latent_inputs
        past = cache_k.shape[2]
        n_keys = seq + past
        in_specs += [
            _resident((seq, HEAD_DIM)),
            _resident((seq, HEAD_DIM)),
            pl.BlockSpec((1, 1, past, ATT_KV_HEADS, HEAD_DIM), lambda b: (b, att_idx, 0, 0, 0)),
            pl.BlockSpec((1, 1, past, ATT_KV_HEADS, HEAD_DIM), lambda b: (b, att_idx, 0, 0, 0)),
        ]
        args += [cos, sin, cache_k, cache_v]
    aliases = {}
    cache_alias = (not latent) and caches_so_far is not None
    if cache_alias:
        aliases = {len(args): 1, len(args) + 1: 2}
        in_specs += [pl.BlockSpec(memory_space=pl.ANY)] * 2
        args += list(caches_so_far)
    out_specs = [pl.BlockSpec((1, seq, D_MODEL), lambda b: (b, 0, 0))]
    out_shape = [jax.ShapeDtypeStruct((batch, seq, D_MODEL), F32)]
    if not latent:
        out_specs += [pl.BlockSpec((1, 1, seq, ATT_KV_HEADS, HEAD_DIM), lambda b: (b, att_idx, 0, 0, 0))] * 2
        out_shape += [jax.ShapeDtypeStruct((batch, n_att, seq, ATT_KV_HEADS, HEAD_DIM), F32)] * 2
    scratch = [
        pltpu.VMEM((seq, ATT_QD), BF16),
        pltpu.VMEM((n_keys, ATT_KD), BF16),
        pltpu.VMEM((n_keys, ATT_KD), BF16),
        pltpu.VMEM((seq, ATT_QD), F32),
        pltpu.VMEM((seq, ATT_QD), F32),
    ]
    outs = pl.pallas_call(
        functools.partial(_att_kernel, seq=seq, latent=latent, cache_alias=cache_alias),
        grid=(batch,),
        in_specs=in_specs,
        out_specs=out_specs,
        out_shape=out_shape,
        scratch_shapes=scratch,
        input_output_aliases=aliases,
        compiler_params=pltpu.CompilerParams(dimension_semantics=("arbitrary",),
                                             vmem_limit_bytes=V7X_VMEM_LIMIT_BYTES),
        name=f"att_layer_seq{seq}",
    )(*args)
    return outs


def _rope_tables(seq):
    half = HEAD_DIM // 2
    nf = half // 2
    pos = np.arange(seq)
    freqs = jnp.asarray(ROPE_THETA, F32) ** (-jnp.arange(nf, dtype=F32) / nf)
    ang_row = jnp.asarray(pos // GRID_W, F32)[:, None] * freqs[None, :]
    ang_col = jnp.asarray(pos % GRID_W, F32)[:, None] * freqs[None, :]
    cos = jnp.concatenate([jnp.cos(ang_row)] * 2 + [jnp.cos(ang_col)] * 2, axis=-1)
    sin = jnp.concatenate([-jnp.sin(ang_row), jnp.sin(ang_row), -jnp.sin(ang_col), jnp.sin(ang_col)], axis=-1)
    return cos, sin


def kernel(x_prompt, x_sample, state_gla, cache_k, cache_v, c, c_ctx, norm_g, w_ada, b_ada,
           gla_w_in, gla_wa1, gla_wa2, gla_ba, gla_onorm, gla_w_out,
           att_w_in, att_qnorm, att_knorm, att_w_out):
    n_dec = x_sample.shape[0]
    assert 1 + n_dec <= MOD_ROWS
    assert x_prompt.shape[1] % ROW_TILE == 0 and x_sample.shape[1] % ROW_TILE == 0

    cvec = jnp.zeros((MOD_ROWS, D_MODEL), F32).at[0].set(c_ctx).at[1:1 + n_dec].set(c)
    mods = _modulations(cvec, w_ada, b_ada).reshape(DEPTH, MOD_ROWS, 1, 3 * D_MODEL)
    ctx_row = lambda b: 0
    dec_row = lambda b: b + 1

    cos, sin = _rope_tables(x_sample.shape[1])
    n_gla, n_att = gla_w_in.shape[0], att_w_in.shape[0]

    xp, xs = x_prompt, x_sample
    states, caches = None, None
    for l in range(DEPTH):
        i = l // 2
        ng = norm_g[l].reshape(1, D_MODEL)
        if l % 2 == 0:
            win = gla_w_in[i].astype(BF16)
            wout = gla_w_out[i].astype(BF16)
            wa1 = jnp.zeros((D_MODEL, RANK_PAD), F32)
            wa1 = wa1.at[:, 0:GLA_RANK].set(gla_wa1[i, 0]).at[:, GLA_RANK:2 * GLA_RANK].set(gla_wa1[i, 1])
            wa2 = jnp.zeros((RANK_PAD, 2 * GLA_QD), F32)
            wa2 = wa2.at[0:GLA_RANK, 0:GLA_QD].set(gla_wa2[i, 0])
            wa2 = wa2.at[GLA_RANK:2 * GLA_RANK, GLA_QD:].set(gla_wa2[i, 1])
            ba = gla_ba[i].reshape(1, 2 * GLA_QD)
            on = gla_onorm[i].reshape(1, GLA_DV)
            common = (ng, win, wa1.astype(BF16), wa2.astype(BF16), ba, on, wout)
            xp, states = _gla_layer(xp, mods, l, ctx_row, *common, None, i, n_gla, states)
            xs, _ = _gla_layer(xs, mods, l, dec_row, *common, state_gla, i, n_gla, None)
        else:
            win = att_w_in[i].astype(BF16)
            wout = att_w_out[i].astype(BF16)
            qn = att_qnorm[i].reshape(1, HEAD_DIM)
            kn = att_knorm[i].reshape(1, HEAD_DIM)
            xp, *caches = _att_layer(xp, mods, l, ctx_row, ng, win, qn, kn, wout, None, i, n_att, caches)
            (xs,) = _att_layer(xs, mods, l, dec_row, ng, win, qn, kn, wout, (cos, sin, cache_k, cache_v), i, n_att, None)
    return (xp, xs, states, caches[0], caches[1])
```

```python
import functools
import math

import jax
import jax.numpy as jnp
import numpy as np
from jax import lax
from jax.experimental import pallas as pl
from jax.experimental.pallas import tpu as pltpu

D_MODEL = 1024
DEPTH = 4
GRID_W = 64
GLA_HEADS = 4
GLA_DK = 128
GLA_DV = 256
GLA_RANK = 16
GLA_TAU = 16.0
GLA_CHUNK = 64
GLA_QD = GLA_HEADS * GLA_DK
GLA_VD = GLA_HEADS * GLA_DV
HEAD_DIM = 128
ATT_HEADS = 8
ATT_KV_HEADS = 2
ATT_GROUP = ATT_HEADS // ATT_KV_HEADS
ATT_QD = ATT_HEADS * HEAD_DIM
ATT_KD = ATT_KV_HEADS * HEAD_DIM
ROPE_THETA = 10000.0
EPS = 1e-6

ROW_TILE = 256
GROUP_ROWS = 1024
MOD_ROWS = 8
RANK_PAD = 128
V7X_VMEM_LIMIT_BYTES = 60 * 1024 * 1024

F32 = jnp.float32
BF16 = jnp.bfloat16
_NT = (((1,), (1,)), ((), ()))
_TN = (((0,), (0,)), ((), ()))


def _dot(a, b):
    return jnp.dot(a, b, preferred_element_type=F32)


def _dot_nt(a, b):
    return lax.dot_general(a, b, _NT, preferred_element_type=F32)


def _dot_tn(a, b):
    return lax.dot_general(a, b, _TN, preferred_element_type=F32)


def _silu(x):
    return x * (1.0 / (1.0 + jnp.exp(-x)))


def _log_sigmoid(z):
    return jnp.minimum(z, 0.0) - jnp.log1p(jnp.exp(-jnp.abs(z)))


def _split_bf16(x):
    hi = x.astype(BF16)
    lo = (x - hi.astype(F32)).astype(BF16)
    return hi, lo


def _modulated_norm(x, ng, mod_ref):
    shift = mod_ref[0, 0, :, 0:D_MODEL]
    scale = mod_ref[0, 0, :, D_MODEL:2 * D_MODEL]
    ms = jnp.mean(x * x, axis=-1, keepdims=True)
    hn = (x * lax.rsqrt(ms + EPS)) * ng
    return (hn * (1.0 + scale) + shift).astype(BF16)


def _head_rms(x, g, width):
    outs = []
    for h in range(x.shape[-1] // width):
        xh = x[:, h * width:(h + 1) * width]
        ms = jnp.mean(xh * xh, axis=-1, keepdims=True)
        outs.append((xh * lax.rsqrt(ms + EPS)) * g)
    return jnp.concatenate(outs, axis=-1)


def _mod_kernel(c_ref, w_ref, b_ref, o_ref):
    s_hi, s_lo = _split_bf16(_silu(c_ref[...]))
    w_hi, w_lo = _split_bf16(w_ref[0])
    acc = _dot(s_hi, w_hi) + _dot(s_lo, w_hi) + _dot(s_hi, w_lo)
    o_ref[0] = acc + b_ref[0]


def _modulations(cvec, w_ada, b_ada):
    tn = D_MODEL
    n_tiles = 3 * D_MODEL // tn
    return pl.pallas_call(
        _mod_kernel,
        grid=(DEPTH, n_tiles),
        in_specs=[
            pl.BlockSpec((MOD_ROWS, D_MODEL), lambda l, j: (0, 0)),
            pl.BlockSpec((1, D_MODEL, tn), lambda l, j: (l, 0, j)),
            pl.BlockSpec((1, 1, tn), lambda l, j: (l, 0, j)),
        ],
        out_specs=pl.BlockSpec((1, MOD_ROWS, tn), lambda l, j: (l, 0, j)),
        out_shape=jax.ShapeDtypeStruct((DEPTH, MOD_ROWS, 3 * D_MODEL), F32),
        compiler_params=pltpu.CompilerParams(dimension_semantics=("arbitrary", "arbitrary")),
        name="adaln_modulation",
    )(cvec, w_ada, b_ada.reshape(DEPTH, 1, 3 * D_MODEL))


def _chunk_sum_matrices():
    r = lax.broadcasted_iota(jnp.int32, (ROW_TILE, ROW_TILE), 0)
    c = lax.broadcasted_iota(jnp.int32, (ROW_TILE, ROW_TILE), 1)
    same = (r // GLA_CHUNK) == (c // GLA_CHUNK)
    prefix = jnp.where(same & (c <= r), 1.0, 0.0).astype(BF16)
    suffix = jnp.where(same & (c >= r), 1.0, 0.0).astype(BF16)
    return prefix, suffix


def _gla_kernel(*refs, seq, has_s0, emit_state, state_alias):
    n_seq = GROUP_ROWS // seq
    n_tiles = GROUP_ROWS // ROW_TILE
    n_pos = seq // GLA_CHUNK
    chunks_per_tile = ROW_TILE // GLA_CHUNK
    it = iter(refs)
    x_ref, mod_ref, ng_ref, win_ref, wa1_ref, wa2_ref, ba_ref, on_ref, wout_ref = (next(it) for _ in range(9))
    s0_ref = next(it) if has_s0 else None
    if state_alias:
        next(it)
    y_ref = next(it)
    sout_ref = next(it) if emit_state else None
    qe_s, ke_s, kdt_s, v_s, gate_s, dec_s, o_s = (next(it) for _ in range(7))
    st_s = None if emit_state else next(it)

    def state_at(s, d, h):
        return sout_ref.at[s, 0, d, h] if emit_state else st_s.at[d, h]

    ng = ng_ref[...]
    prefix_m, suffix_m = _chunk_sum_matrices()

    def project(t):
        rows = slice(t * ROW_TILE, (t + 1) * ROW_TILE)
        hb = _modulated_norm(x_ref[0, rows, :], ng, mod_ref)
        q = _dot(hb, win_ref[:, 0:GLA_QD]) * (GLA_DK ** -0.5)
        k = _dot(hb, win_ref[:, GLA_QD:2 * GLA_QD])
        v_s[rows, :] = _dot(hb, win_ref[:, 2 * GLA_QD:2 * GLA_QD + GLA_VD]).astype(BF16)
        gate_s[rows, :] = _dot(hb, win_ref[:, 2 * GLA_QD + GLA_VD:])
        low = _dot(hb, wa1_ref[...]).astype(BF16)
        z = _dot(low, wa2_ref[...]) + ba_ref[...]
        logg = _log_sigmoid(z) * (1.0 / GLA_TAU)
        for d, csum_m in enumerate((prefix_m, suffix_m)):
            g_hi, g_lo = _split_bf16(logg[:, d * GLA_QD:(d + 1) * GLA_QD])
            b = _dot(csum_m, g_hi) + _dot(csum_m, g_lo)
            edge = GLA_CHUNK - 1 if d == 0 else 0
            totals = [b[j * GLA_CHUNK + edge:j * GLA_CHUNK + edge + 1, :] for j in range(chunks_per_tile)]
            b_last = jnp.concatenate([jnp.broadcast_to(tt, (GLA_CHUNK, GLA_QD)) for tt in totals], axis=0)
            qe_s[d, rows, :] = (q * jnp.exp(b)).astype(BF16)
            ke_s[d, rows, :] = (k * jnp.exp(-b)).astype(BF16)
            kd = k * jnp.exp(b_last - b)
            for j in range(chunks_per_tile):
                c = t * chunks_per_tile + j
                kdt_s[d, c] = kd[j * GLA_CHUNK:(j + 1) * GLA_CHUNK, :].T.astype(BF16)
                dec_s[d, c] = jnp.exp(totals[j])

    for t in range(n_tiles):
        project(t)

    for s in range(n_seq):
        for d in range(2):
            for h in range(GLA_HEADS):
                state_at(s, d, h)[...] = s0_ref[0, 0, d, h] if has_s0 else jnp.zeros((GLA_DK, GLA_DV), F32)

    ri = lax.broadcasted_iota(jnp.int32, (GLA_CHUNK, GLA_CHUNK), 0)
    ci = lax.broadcasted_iota(jnp.int32, (GLA_CHUNK, GLA_CHUNK), 1)
    masks = (ci <= ri, ci >= ri)

    def scan_step(i, first_visit):
        chains = []
        for s in range(n_seq):
            for d in range(2):
                c = s * n_pos + (i if d == 0 else n_pos - 1 - i)
                rows = pl.ds(pl.multiple_of(c * GLA_CHUNK, GLA_CHUNK), GLA_CHUNK)
                for h in range(GLA_HEADS):
                    chains.append((s, d, h, c, rows, slice(h * GLA_DK, (h + 1) * GLA_DK),
                                   slice(h * GLA_DV, (h + 1) * GLA_DV)))
        scores = [_dot_nt(qe_s[d, rows, kc], ke_s[d, rows, kc]) for (s, d, h, c, rows, kc, vc) in chains]
        updates = [_dot(kdt_s[d, c, kc, :], v_s[rows, vc]) for (s, d, h, c, rows, kc, vc) in chains]
        for (s, d, h, c, rows, kc, vc), sc in zip(chains, scores):
            a = jnp.where(masks[d], sc, 0.0).astype(BF16)
            o = _dot(jnp.concatenate([qe_s[d, rows, kc], a], axis=1),
                     jnp.concatenate([state_at(s, d, h)[...].astype(BF16), v_s[rows, vc]], axis=0))
            if first_visit:
                o_s[rows, vc] = o
            else:
                o_s[rows, vc] += o
        for (s, d, h, c, rows, kc, vc), upd in zip(chains, updates):
            dec_col = jnp.broadcast_to(dec_s[d, c][:, kc], (GLA_DK, GLA_DK)).T
            st_ref = state_at(s, d, h)
            st_ref[...] = jnp.concatenate([dec_col] * (GLA_DV // GLA_DK), axis=1) * st_ref[...] + upd

    def first_half(i, carry):
        scan_step(i, True)
        return carry

    def second_half(i, carry):
        scan_step(i, False)
        return carry

    lax.fori_loop(0, n_pos // 2, first_half, 0)
    lax.fori_loop(n_pos // 2, n_pos, second_half, 0)

    on = on_ref[...]
    res_gate = mod_ref[0, 0, :, 2 * D_MODEL:]

    for t in range(n_tiles):
        rows = slice(t * ROW_TILE, (t + 1) * ROW_TILE)
        o = _head_rms(o_s[rows, :], on, GLA_DV)
        y = _dot((o * _silu(gate_s[rows, :])).astype(BF16), wout_ref[...])
        y_ref[0, rows, :] = x_ref[0, rows, :] + res_gate * y


def _resident(shape):
    return pl.BlockSpec(shape, lambda b: (0,) * len(shape), pipeline_mode=pl.Buffered(1))


def _gla_layer(x, mods, layer, mod_row, ng, win, wa1, wa2, ba, on, wout, state_in, gla_idx, n_gla, states_so_far):
    batch, seq, _ = x.shape
    n_seq = GROUP_ROWS // seq
    n_groups = batch // n_seq
    has_s0 = state_in is not None
    emit_state = not has_s0
    assert not has_s0 or n_seq == 1
    n_chunks = GROUP_ROWS // GLA_CHUNK
    in_specs = [
        pl.BlockSpec((1, GROUP_ROWS, D_MODEL), lambda b: (b, 0, 0)),
        pl.BlockSpec((1, 1, 1, 3 * D_MODEL), lambda b: (layer, mod_row(b), 0, 0)),
        _resident((1, D_MODEL)),
        _resident((D_MODEL, 2 * GLA_QD + 2 * GLA_VD)),
        _resident((D_MODEL, RANK_PAD)),
        _resident((RANK_PAD, 2 * GLA_QD)),
        _resident((1, 2 * GLA_QD)),
        _resident((1, GLA_DV)),
        _resident((GLA_VD, D_MODEL)),
    ]
    args = [x.reshape(n_groups, GROUP_ROWS, D_MODEL), mods, ng, win, wa1, wa2, ba, on, wout]
    if has_s0:
        in_specs.append(pl.BlockSpec((1, 1, 2, GLA_HEADS, GLA_DK, GLA_DV),
                                     lambda b: (b, gla_idx, 0, 0, 0, 0)))
        args.append(state_in)
    aliases = {}
    state_alias = emit_state and states_so_far is not None
    if state_alias:
        aliases[len(args)] = 1
        in_specs.append(pl.BlockSpec(memory_space=pl.ANY))
        args.append(states_so_far)
    out_specs = [pl.BlockSpec((1, GROUP_ROWS, D_MODEL), lambda b: (b, 0, 0))]
    out_shape = [jax.ShapeDtypeStruct((n_groups, GROUP_ROWS, D_MODEL), F32)]
    if emit_state:
        out_specs.append(pl.BlockSpec((n_seq, 1, 2, GLA_HEADS, GLA_DK, GLA_DV), lambda b: (b, gla_idx, 0, 0, 0, 0)))
        out_shape.append(jax.ShapeDtypeStruct((batch, n_gla, 2, GLA_HEADS, GLA_DK, GLA_DV), F32))
    scratch = [
        pltpu.VMEM((2, GROUP_ROWS, GLA_QD), BF16),
        pltpu.VMEM((2, GROUP_ROWS, GLA_QD), BF16),
        pltpu.VMEM((2, n_chunks, GLA_QD, GLA_CHUNK), BF16),
        pltpu.VMEM((GROUP_ROWS, GLA_VD), BF16),
        pltpu.VMEM((GROUP_ROWS, GLA_VD), F32),
        pltpu.VMEM((2, n_chunks, 1, GLA_QD), F32),
        pltpu.VMEM((GROUP_ROWS, GLA_VD), F32),
    ]
    if not emit_state:
        scratch.append(pltpu.VMEM((2, GLA_HEADS, GLA_DK, GLA_DV), F32))
    outs = pl.pallas_call(
        functools.partial(_gla_kernel, seq=seq, has_s0=has_s0, emit_state=emit_state, state_alias=state_alias),
        grid=(n_groups,),
        in_specs=in_specs,
        out_specs=out_specs,
        out_shape=out_shape,
        scratch_shapes=scratch,
        input_output_aliases=aliases,
        compiler_params=pltpu.CompilerParams(dimension_semantics=("arbitrary",),
                                             vmem_limit_bytes=V7X_VMEM_LIMIT_BYTES),
        name=f"gla_layer_seq{seq}",
    )(*args)
    y = outs[0].reshape(batch, seq, D_MODEL)
    return (y, outs[1]) if emit_state else (y, None)


def _rope_swap(x):
    lane = lax.broadcasted_iota(jnp.int32, x.shape, 1)
    quarter = HEAD_DIM // 4
    first = (lane % (2 * quarter)) < quarter
    return jnp.where(first, pltpu.roll(x, HEAD_DIM - quarter, 1), pltpu.roll(x, quarter, 1))


def _att_kernel(*refs, seq, latent, cache_alias):
    n_tiles = seq // ROW_TILE
    n_keys = seq + (refs[9].shape[2] if latent else 0)
    it = iter(refs)
    x_ref, mod_ref, ng_ref, win_ref, qn_ref, kn_ref, wout_ref = (next(it) for _ in range(7))
    if latent:
        cos_ref, sin_ref, ck_ref, cv_ref = (next(it) for _ in range(4))
    if cache_alias:
        next(it), next(it)
    y_ref = next(it)
    if not latent:
        kout_ref, vout_ref = next(it), next(it)
    q_s, k_s, v_s, gate_s, ao_s = (next(it) for _ in range(5))

    ng = ng_ref[...]
    qn = qn_ref[...]
    kn = kn_ref[...]

    def project(t, carry):
        rows = pl.ds(pl.multiple_of(t * ROW_TILE, ROW_TILE), ROW_TILE)
        hb = _modulated_norm(x_ref[0, rows, :], ng, mod_ref)
        q = _head_rms(_dot(hb, win_ref[:, 0:ATT_QD]), qn, HEAD_DIM)
        k = _head_rms(_dot(hb, win_ref[:, ATT_QD:ATT_QD + ATT_KD]), kn, HEAD_DIM)
        v = _dot(hb, win_ref[:, ATT_QD + ATT_KD:ATT_QD + 2 * ATT_KD])
        gate_s[rows, :] = _dot(hb, win_ref[:, ATT_QD + 2 * ATT_KD:])
        if latent:
            cos = cos_ref[rows, :]
            sin = sin_ref[rows, :]
            q = jnp.concatenate(
                [q[:, h * HEAD_DIM:(h + 1) * HEAD_DIM] * cos + _rope_swap(q[:, h * HEAD_DIM:(h + 1) * HEAD_DIM]) * sin
                 for h in range(ATT_HEADS)], axis=-1)
            k = jnp.concatenate(
                [k[:, h * HEAD_DIM:(h + 1) * HEAD_DIM] * cos + _rope_swap(k[:, h * HEAD_DIM:(h + 1) * HEAD_DIM]) * sin
                 for h in range(ATT_KV_HEADS)], axis=-1)
        else:
            for h in range(ATT_KV_HEADS):
                kout_ref[0, 0, rows, h, :] = k[:, h * HEAD_DIM:(h + 1) * HEAD_DIM]
                vout_ref[0, 0, rows, h, :] = v[:, h * HEAD_DIM:(h + 1) * HEAD_DIM]
        q_s[rows, :] = q.astype(BF16)
        k_s[rows, :] = k.astype(BF16)
        v_s[rows, :] = v.astype(BF16)
        return carry

    lax.fori_loop(0, n_tiles, project, 0)
    if latent:
        for h in range(ATT_KV_HEADS):
            k_s[seq:n_keys, h * HEAD_DIM:(h + 1) * HEAD_DIM] = ck_ref[0, 0, :, h, :].astype(BF16)
            v_s[seq:n_keys, h * HEAD_DIM:(h + 1) * HEAD_DIM] = cv_ref[0, 0, :, h, :].astype(BF16)

    exp2_scale = (HEAD_DIM ** -0.5) * math.log2(math.e)

    def attend(t, carry):
        rows = pl.ds(pl.multiple_of(t * ROW_TILE, ROW_TILE), ROW_TILE)
        for kv in range(ATT_KV_HEADS):
            kc = slice(kv * HEAD_DIM, (kv + 1) * HEAD_DIM)
            keys = k_s[:, kc]
            vals = v_s[:, kc]
            for g in range(ATT_GROUP):
                qc = slice((kv * ATT_GROUP + g) * HEAD_DIM, (kv * ATT_GROUP + g + 1) * HEAD_DIM)
                s = _dot_nt(q_s[rows, qc], keys)
                m = jnp.max(s, axis=-1, keepdims=True)
                p = jnp.exp2((s - m) * exp2_scale)
                denom = jnp.sum(p, axis=-1, keepdims=True)
                ao_s[rows, qc] = _dot(p.astype(BF16), vals) * (1.0 / denom)
        return carry

    lax.fori_loop(0, n_tiles, attend, 0)

    res_gate = mod_ref[0, 0, :, 2 * D_MODEL:]

    def finish(t, carry):
        rows = pl.ds(pl.multiple_of(t * ROW_TILE, ROW_TILE), ROW_TILE)
        y = _dot((ao_s[rows, :] * _silu(gate_s[rows, :])).astype(BF16), wout_ref[...])
        y_ref[0, rows, :] = x_ref[0, rows, :] + res_gate * y
        return carry

    lax.fori_loop(0, n_tiles, finish, 0)


def _att_layer(x, mods, layer, mod_row, ng, win, qn, kn, wout, latent_inputs, att_idx, n_att, caches_so_far):
    batch, seq, _ = x.shape
    latent = latent_inputs is not None
    in_specs = [
        pl.BlockSpec((1, seq, D_MODEL), lambda b: (b, 0, 0)),
        pl.BlockSpec((1, 1, 1, 3 * D_MODEL), lambda b: (layer, mod_row(b), 0, 0)),
        _resident((1, D_MODEL)),
        _resident((D_MODEL, 2 * ATT_QD + 2 * ATT_KD)),
        _resident((1, HEAD_DIM)),
        _resident((1, HEAD_DIM)),
        _resident((ATT_QD, D_MODEL)),
    ]
    args = [x, mods, ng, win, qn, kn, wout]
    n_keys = seq
    if latent:
        cos, sin, cache_k, cache_v = latent_inputs
        past = cache_k.shape[2]
        n_keys = seq + past
        in_specs += [
            _resident((seq, HEAD_DIM)),
            _resident((seq, HEAD_DIM)),
            pl.BlockSpec((1, 1, past, ATT_KV_HEADS, HEAD_DIM), lambda b: (b, att_idx, 0, 0, 0)),
            pl.BlockSpec((1, 1, past, ATT_KV_HEADS, HEAD_DIM), lambda b: (b, att_idx, 0, 0, 0)),
        ]
        args += [cos, sin, cache_k, cache_v]
    aliases = {}
    cache_alias = (not latent) and caches_so_far is not None
    if cache_alias:
        aliases = {len(args): 1, len(args) + 1: 2}
        in_specs += [pl.BlockSpec(memory_space=pl.ANY)] * 2
        args += list(caches_so_far)
    out_specs = [pl.BlockSpec((1, seq, D_MODEL), lambda b: (b, 0, 0))]
    out_shape = [jax.ShapeDtypeStruct((batch, seq, D_MODEL), F32)]
    if not latent:
        out_specs += [pl.BlockSpec((1, 1, seq, ATT_KV_HEADS, HEAD_DIM), lambda b: (b, att_idx, 0, 0, 0))] * 2
        out_shape += [jax.ShapeDtypeStruct((batch, n_att, seq, ATT_KV_HEADS, HEAD_DIM), F32)] * 2
    scratch = [
        pltpu.VMEM((seq, ATT_QD), BF16),
        pltpu.VMEM((n_keys, ATT_KD), BF16),
        pltpu.VMEM((n_keys, ATT_KD), BF16),
        pltpu.VMEM((seq, ATT_QD), F32),
        pltpu.VMEM((seq, ATT_QD), F32),
    ]
    outs = pl.pallas_call(
        functools.partial(_att_kernel, seq=seq, latent=latent, cache_alias=cache_alias),
        grid=(batch,),
        in_specs=in_specs,
        out_specs=out_specs,
        out_shape=out_shape,
        scratch_shapes=scratch,
        input_output_aliases=aliases,
        compiler_params=pltpu.CompilerParams(dimension_semantics=("arbitrary",),
                                             vmem_limit_bytes=V7X_VMEM_LIMIT_BYTES),
        name=f"att_layer_seq{seq}",
    )(*args)
    return outs


def _rope_tables(seq):
    half = HEAD_DIM // 2
    nf = half // 2
    pos = np.arange(seq)
    freqs = jnp.asarray(ROPE_THETA, F32) ** (-jnp.arange(nf, dtype=F32) / nf)
    ang_row = jnp.asarray(pos // GRID_W, F32)[:, None] * freqs[None, :]
    ang_col = jnp.asarray(pos % GRID_W, F32)[:, None] * freqs[None, :]
    cos = jnp.concatenate([jnp.cos(ang_row)] * 2 + [jnp.cos(ang_col)] * 2, axis=-1)
    sin = jnp.concatenate([-jnp.sin(ang_row), jnp.sin(ang_row), -jnp.sin(ang_col), jnp.sin(ang_col)], axis=-1)
    return cos, sin


def kernel(x_prompt, x_sample, state_gla, cache_k, cache_v, c, c_ctx, norm_g, w_ada, b_ada,
           gla_w_in, gla_wa1, gla_wa2, gla_ba, gla_onorm, gla_w_out,
           att_w_in, att_qnorm, att_knorm, att_w_out):
    n_dec = x_sample.shape[0]
    assert 1 + n_dec <= MOD_ROWS
    assert x_prompt.shape[1] % ROW_TILE == 0 and x_sample.shape[1] % ROW_TILE == 0

    cvec = jnp.zeros((MOD_ROWS, D_MODEL), F32).at[0].set(c_ctx).at[1:1 + n_dec].set(c)
    mods = _modulations(cvec, w_ada, b_ada).reshape(DEPTH, MOD_ROWS, 1, 3 * D_MODEL)
    ctx_row = lambda b: 0
    dec_row = lambda b: b + 1

    cos, sin = _rope_tables(x_sample.shape[1])
    n_gla, n_att = gla_w_in.shape[0], att_w_in.shape[0]

    xp, xs = x_prompt, x_sample
    states, caches = None, None
    for l in range(DEPTH):
        i = l // 2
        ng = norm_g[l].reshape(1, D_MODEL)
        if l % 2 == 0:
            win = gla_w_in[i].astype(BF16)
            wout = gla_w_out[i].astype(BF16)
            wa1 = jnp.zeros((D_MODEL, RANK_PAD), F32)
            wa1 = wa1.at[:, 0:GLA_RANK].set(gla_wa1[i, 0]).at[:, GLA_RANK:2 * GLA_RANK].set(gla_wa1[i, 1])
            wa2 = jnp.zeros((RANK_PAD, 2 * GLA_QD), F32)
            wa2 = wa2.at[0:GLA_RANK, 0:GLA_QD].set(gla_wa2[i, 0])
            wa2 = wa2.at[GLA_RANK:2 * GLA_RANK, GLA_QD:].set(gla_wa2[i, 1])
            ba = gla_ba[i].reshape(1, 2 * GLA_QD)
            on = gla_onorm[i].reshape(1, GLA_DV)
            common = (ng, win, wa1.astype(BF16), wa2.astype(BF16), ba, on, wout)
            xp, states = _gla_layer(xp, mods, l, ctx_row, *common, None, i, n_gla, states)
            xs, _ = _gla_layer(xs, mods, l, dec_row, *common, state_gla, i, n_gla, None)
        else:
            win = att_w_in[i].astype(BF16)
            wout = att_w_out[i].astype(BF16)
            qn = att_qnorm[i].reshape(1, HEAD_DIM)
            kn = att_knorm[i].reshape(1, HEAD_DIM)
            xp, *caches = _att_layer(xp, mods, l, ctx_row, ng, win, qn, kn, wout, None, i, n_att, caches)
            (xs,) = _att_layer(xs, mods, l, dec_row, ng, win, qn, kn, wout, (cos, sin, cache_k, cache_v), i, n_att, None)
    return (xp, xs, states, caches[0], caches[1])
```

```python
import functools
import math

import jax
import jax.numpy as jnp
import numpy as np
from jax import lax
from jax.experimental import pallas as pl
from jax.experimental.pallas import tpu as pltpu

D_MODEL = 1024
DEPTH = 4
GRID_W = 64
GLA_HEADS = 4
GLA_DK = 128
GLA_DV = 256
GLA_RANK = 16
GLA_TAU = 16.0
GLA_CHUNK = 64
GLA_QD = GLA_HEADS * GLA_DK
GLA_VD = GLA_HEADS * GLA_DV
HEAD_DIM = 128
ATT_HEADS = 8
ATT_KV_HEADS = 2
ATT_GROUP = ATT_HEADS // ATT_KV_HEADS
ATT_QD = ATT_HEADS * HEAD_DIM
ATT_KD = ATT_KV_HEADS * HEAD_DIM
ROPE_THETA = 10000.0
EPS = 1e-6

ROW_TILE = 256
GROUP_ROWS = 1024
MOD_ROWS = 8
RANK_PAD = 128
V7X_VMEM_LIMIT_BYTES = 60 * 1024 * 1024

F32 = jnp.float32
BF16 = jnp.bfloat16
_NT = (((1,), (1,)), ((), ()))
_TN = (((0,), (0,)), ((), ()))


def _dot(a, b):
    return jnp.dot(a, b, preferred_element_type=F32)


def _dot_nt(a, b):
    return lax.dot_general(a, b, _NT, preferred_element_type=F32)


def _dot_tn(a, b):
    return lax.dot_general(a, b, _TN, preferred_element_type=F32)


def _silu(x):
    return x * (1.0 / (1.0 + jnp.exp(-x)))


def _log_sigmoid(z):
    return jnp.minimum(z, 0.0) - jnp.log(1.0 + jnp.exp(-jnp.abs(z)))


def _split_bf16(x):
    hi = x.astype(BF16)
    lo = (x - hi.astype(F32)).astype(BF16)
    return hi, lo


def _modulated_norm(x, ng, mod_ref):
    shift = mod_ref[0, 0, :, 0:D_MODEL]
    scale = mod_ref[0, 0, :, D_MODEL:2 * D_MODEL]
    ms = jnp.mean(x * x, axis=-1, keepdims=True)
    hn = (x * lax.rsqrt(ms + EPS)) * ng
    return (hn * (1.0 + scale) + shift).astype(BF16)


def _head_rms(x, g, width):
    outs = []
    for h in range(x.shape[-1] // width):
        xh = x[:, h * width:(h + 1) * width]
        ms = jnp.mean(xh * xh, axis=-1, keepdims=True)
        outs.append((xh * lax.rsqrt(ms + EPS)) * g)
    return jnp.concatenate(outs, axis=-1)


def _mod_kernel(c_ref, w_ref, b_ref, o_ref):
    s_hi, s_lo = _split_bf16(_silu(c_ref[...]))
    w_hi, w_lo = _split_bf16(w_ref[0])
    acc = _dot(s_hi, w_hi) + _dot(s_lo, w_hi) + _dot(s_hi, w_lo)
    o_ref[0] = acc + b_ref[0]


def _modulations(cvec, w_ada, b_ada):
    tn = D_MODEL
    n_tiles = 3 * D_MODEL // tn
    return pl.pallas_call(
        _mod_kernel,
        grid=(DEPTH, n_tiles),
        in_specs=[
            pl.BlockSpec((MOD_ROWS, D_MODEL), lambda l, j: (0, 0)),
            pl.BlockSpec((1, D_MODEL, tn), lambda l, j: (l, 0, j)),
            pl.BlockSpec((1, 1, tn), lambda l, j: (l, 0, j)),
        ],
        out_specs=pl.BlockSpec((1, MOD_ROWS, tn), lambda l, j: (l, 0, j)),
        out_shape=jax.ShapeDtypeStruct((DEPTH, MOD_ROWS, 3 * D_MODEL), F32),
        compiler_params=pltpu.CompilerParams(dimension_semantics=("arbitrary", "arbitrary")),
        name="adaln_modulation",
    )(cvec, w_ada, b_ada.reshape(DEPTH, 1, 3 * D_MODEL))


def _chunk_sum_matrices():
    r = lax.broadcasted_iota(jnp.int32, (ROW_TILE, ROW_TILE), 0)
    c = lax.broadcasted_iota(jnp.int32, (ROW_TILE, ROW_TILE), 1)
    same = (r // GLA_CHUNK) == (c // GLA_CHUNK)
    prefix = jnp.where(same & (c <= r), 1.0, 0.0).astype(BF16)
    suffix = jnp.where(same & (c >= r), 1.0, 0.0).astype(BF16)
    return prefix, suffix


def _gla_kernel(*refs, seq, has_s0, emit_state, state_alias):
    n_seq = GROUP_ROWS // seq
    n_tiles = GROUP_ROWS // ROW_TILE
    n_pos = seq // GLA_CHUNK
    chunks_per_tile = ROW_TILE // GLA_CHUNK
    it = iter(refs)
    x_ref, mod_ref, ng_ref, win_ref, wa1_ref, wa2_ref, ba_ref, on_ref, wout_ref = (next(it) for _ in range(9))
    s0_ref = next(it) if has_s0 else None
    if state_alias:
        next(it)
    y_ref = next(it)
    sout_ref = next(it) if emit_state else None
    qe_s, ke_s, kdt_s, v_s, gate_s, dec_s, o_s = (next(it) for _ in range(7))
    st_s = None if emit_state else next(it)

    def state_at(s, d, h):
        return sout_ref.at[s, 0, d, h] if emit_state else st_s.at[d, h]

    ng = ng_ref[...]
    prefix_m, suffix_m = _chunk_sum_matrices()

    def project_wide(t):
        rows = slice(t * ROW_TILE, (t + 1) * ROW_TILE)
        hb = _modulated_norm(x_ref[0, rows, :], ng, mod_ref)
        q = _dot(hb, win_ref[:, 0:GLA_QD]) * (GLA_DK ** -0.5)
        k = _dot(hb, win_ref[:, GLA_QD:2 * GLA_QD])
        v_s[rows, :] = _dot(hb, win_ref[:, 2 * GLA_QD:2 * GLA_QD + GLA_VD]).astype(BF16)
        gate_s[rows, :] = _dot(hb, win_ref[:, 2 * GLA_QD + GLA_VD:])
        low = _dot(hb, wa1_ref[...]).astype(BF16)
        z = _dot(low, wa2_ref[...]) + ba_ref[...]
        return q, k, z

    def decay_sums(z):
        logg2 = _log_sigmoid(z) * (math.log2(math.e) / GLA_TAU)
        sums = []
        for d, csum_m in enumerate((prefix_m, suffix_m)):
            g_hi, g_lo = _split_bf16(logg2[:, d * GLA_QD:(d + 1) * GLA_QD])
            sums.append(_dot(csum_m, g_hi) + _dot(csum_m, g_lo))
        return sums

    def decay_factors(t, q, k, sums):
        rows = slice(t * ROW_TILE, (t + 1) * ROW_TILE)
        for d, b in enumerate(sums):
            edge = GLA_CHUNK - 1 if d == 0 else 0
            totals = [b[j * GLA_CHUNK + edge:j * GLA_CHUNK + edge + 1, :] for j in range(chunks_per_tile)]
            b_last = jnp.concatenate([jnp.broadcast_to(tt, (GLA_CHUNK, GLA_QD)) for tt in totals], axis=0)
            qe_s[d, rows, :] = (q * jnp.exp2(b)).astype(BF16)
            ke_s[d, rows, :] = (k * jnp.exp2(-b)).astype(BF16)
            kd = k * jnp.exp2(b_last - b)
            for j in range(chunks_per_tile):
                c = t * chunks_per_tile + j
                kdt_s[d, c] = kd[j * GLA_CHUNK:(j + 1) * GLA_CHUNK, :].T.astype(BF16)
                dec_s[d, c] = jnp.exp2(totals[j])

    wide = project_wide(0)
    for t in range(n_tiles):
        nxt = project_wide(t + 1) if t + 1 < n_tiles else None
        q, k, z = wide
        decay_factors(t, q, k, decay_sums(z))
        wide = nxt

    for s in range(n_seq):
        for d in range(2):
            for h in range(GLA_HEADS):
                state_at(s, d, h)[...] = s0_ref[0, 0, d, h] if has_s0 else jnp.zeros((GLA_DK, GLA_DV), F32)

    ri = lax.broadcasted_iota(jnp.int32, (GLA_CHUNK, GLA_CHUNK), 0)
    ci = lax.broadcasted_iota(jnp.int32, (GLA_CHUNK, GLA_CHUNK), 1)
    masks = (ci <= ri, ci >= ri)

    def scan_step(i, first_visit):
        chains = []
        for s in range(n_seq):
            for d in range(2):
                c = s * n_pos + (i if d == 0 else n_pos - 1 - i)
                rows = pl.ds(pl.multiple_of(c * GLA_CHUNK, GLA_CHUNK), GLA_CHUNK)
                for h in range(GLA_HEADS):
                    chains.append((s, d, h, c, rows, slice(h * GLA_DK, (h + 1) * GLA_DK),
                                   slice(h * GLA_DV, (h + 1) * GLA_DV)))
        scores = [_dot_nt(qe_s[d, rows, kc], ke_s[d, rows, kc]) for (s, d, h, c, rows, kc, vc) in chains]
        updates = [_dot(kdt_s[d, c, kc, :], v_s[rows, vc]) for (s, d, h, c, rows, kc, vc) in chains]
        for (s, d, h, c, rows, kc, vc), sc in zip(chains, scores):
            a = jnp.where(masks[d], sc, 0.0).astype(BF16)
            o = _dot(jnp.concatenate([qe_s[d, rows, kc], a], axis=1),
                     jnp.concatenate([state_at(s, d, h)[...].astype(BF16), v_s[rows, vc]], axis=0))
            if first_visit:
                o_s[rows, vc] = o
            else:
                o_s[rows, vc] += o
        for (s, d, h, c, rows, kc, vc), upd in zip(chains, updates):
            dec_col = jnp.broadcast_to(dec_s[d, c][:, kc], (GLA_DK, GLA_DK)).T
            st_ref = state_at(s, d, h)
            st_ref[...] = jnp.concatenate([dec_col] * (GLA_DV // GLA_DK), axis=1) * st_ref[...] + upd

    def first_half(i, carry):
        scan_step(i, True)
        return carry

    def second_half(i, carry):
        scan_step(i, False)
        return carry

    lax.fori_loop(0, n_pos // 2, first_half, 0)
    lax.fori_loop(n_pos // 2, n_pos, second_half, 0)

    on = on_ref[...]
    res_gate = mod_ref[0, 0, :, 2 * D_MODEL:]

    def gated(t):
        rows = slice(t * ROW_TILE, (t + 1) * ROW_TILE)
        return (_head_rms(o_s[rows, :], on, GLA_DV) * _silu(gate_s[rows, :])).astype(BF16)

    og = gated(0)
    for t in range(n_tiles):
        rows = slice(t * ROW_TILE, (t + 1) * ROW_TILE)
        og_next = gated(t + 1) if t + 1 < n_tiles else None
        y_ref[0, rows, :] = x_ref[0, rows, :] + res_gate * _dot(og, wout_ref[...])
        og = og_next


def _resident(shape):
    return pl.BlockSpec(shape, lambda b: (0,) * len(shape), pipeline_mode=pl.Buffered(1))


def _gla_layer(x, mods, layer, mod_row, ng, win, wa1, wa2, ba, on, wout, state_in, gla_idx, n_gla, states_so_far):
    batch, seq, _ = x.shape
    n_seq = GROUP_ROWS // seq
    n_groups = batch // n_seq
    has_s0 = state_in is not None
    emit_state = not has_s0
    assert not has_s0 or n_seq == 1
    n_chunks = GROUP_ROWS // GLA_CHUNK
    in_specs = [
        pl.BlockSpec((1, GROUP_ROWS, D_MODEL), lambda b: (b, 0, 0)),
        pl.BlockSpec((1, 1, 1, 3 * D_MODEL), lambda b: (layer, mod_row(b), 0, 0)),
        _resident((1, D_MODEL)),
        _resident((D_MODEL, 2 * GLA_QD + 2 * GLA_VD)),
        _resident((D_MODEL, RANK_PAD)),
        _resident((RANK_PAD, 2 * GLA_QD)),
        _resident((1, 2 * GLA_QD)),
        _resident((1, GLA_DV)),
        _resident((GLA_VD, D_MODEL)),
    ]
    args = [x.reshape(n_groups, GROUP_ROWS, D_MODEL), mods, ng, win, wa1, wa2, ba, on, wout]
    if has_s0:
        in_specs.append(pl.BlockSpec((1, 1, 2, GLA_HEADS, GLA_DK, GLA_DV),
                                     lambda b: (b, gla_idx, 0, 0, 0, 0)))
        args.append(state_in)
    aliases = {}
    state_alias = emit_state and states_so_far is not None
    if state_alias:
        aliases[len(args)] = 1
        in_specs.append(pl.BlockSpec(memory_space=pl.ANY))
        args.append(states_so_far)
    out_specs = [pl.BlockSpec((1, GROUP_ROWS, D_MODEL), lambda b: (b, 0, 0))]
    out_shape = [jax.ShapeDtypeStruct((n_groups, GROUP_ROWS, D_MODEL), F32)]
    if emit_state:
        out_specs.append(pl.BlockSpec((n_seq, 1, 2, GLA_HEADS, GLA_DK, GLA_DV), lambda b: (b, gla_idx, 0, 0, 0, 0)))
        out_shape.append(jax.ShapeDtypeStruct((batch, n_gla, 2, GLA_HEADS, GLA_DK, GLA_DV), F32))
    scratch = [
        pltpu.VMEM((2, GROUP_ROWS, GLA_QD), BF16),
        pltpu.VMEM((2, GROUP_ROWS, GLA_QD), BF16),
        pltpu.VMEM((2, n_chunks, GLA_QD, GLA_CHUNK), BF16),
        pltpu.VMEM((GROUP_ROWS, GLA_VD), BF16),
        pltpu.VMEM((GROUP_ROWS, GLA_VD), F32),
        pltpu.VMEM((2, n_chunks, 1, GLA_QD), F32),
        pltpu.VMEM((GROUP_ROWS, GLA_VD), F32),
    ]
    if not emit_state:
        scratch.append(pltpu.VMEM((2, GLA_HEADS, GLA_DK, GLA_DV), F32))
    outs = pl.pallas_call(
        functools.partial(_gla_kernel, seq=seq, has_s0=has_s0, emit_state=emit_state, state_alias=state_alias),
        grid=(n_groups,),
        in_specs=in_specs,
        out_specs=out_specs,
        out_shape=out_shape,
        scratch_shapes=scratch,
        input_output_aliases=aliases,
        compiler_params=pltpu.CompilerParams(dimension_semantics=("arbitrary",),
                                             vmem_limit_bytes=V7X_VMEM_LIMIT_BYTES),
        name=f"gla_layer_seq{seq}",
    )(*args)
    y = outs[0].reshape(batch, seq, D_MODEL)
    return (y, outs[1]) if emit_state else (y, None)


def _rope_swap(x):
    lane = lax.broadcasted_iota(jnp.int32, x.shape, 1)
    quarter = HEAD_DIM // 4
    first = (lane % (2 * quarter)) < quarter
    return jnp.where(first, pltpu.roll(x, HEAD_DIM - quarter, 1), pltpu.roll(x, quarter, 1))


def _att_kernel(*refs, seq, latent, cache_alias):
    n_tiles = seq // ROW_TILE
    n_keys = seq + (refs[9].shape[2] if latent else 0)
    it = iter(refs)
    x_ref, mod_ref, ng_ref, win_ref, qn_ref, kn_ref, wout_ref = (next(it) for _ in range(7))
    if latent:
        cos_ref, sin_ref, ck_ref, cv_ref = (next(it) for _ in range(4))
    if cache_alias:
        next(it), next(it)
    y_ref = next(it)
    if not latent:
        kout_ref, vout_ref = next(it), next(it)
    q_s, k_s, v_s, gate_s, ao_s = (next(it) for _ in range(5))

    ng = ng_ref[...]
    qn = qn_ref[...]
    kn = kn_ref[...]

    def project(t, carry):
        rows = pl.ds(pl.multiple_of(t * ROW_TILE, ROW_TILE), ROW_TILE)
        hb = _modulated_norm(x_ref[0, rows, :], ng, mod_ref)
        q = _head_rms(_dot(hb, win_ref[:, 0:ATT_QD]), qn, HEAD_DIM)
        k = _head_rms(_dot(hb, win_ref[:, ATT_QD:ATT_QD + ATT_KD]), kn, HEAD_DIM)
        v = _dot(hb, win_ref[:, ATT_QD + ATT_KD:ATT_QD + 2 * ATT_KD])
        gate_s[rows, :] = _dot(hb, win_ref[:, ATT_QD + 2 * ATT_KD:])
        if latent:
            cos = cos_ref[rows, :]
            sin = sin_ref[rows, :]
            q = jnp.concatenate(
                [q[:, h * HEAD_DIM:(h + 1) * HEAD_DIM] * cos + _rope_swap(q[:, h * HEAD_DIM:(h + 1) * HEAD_DIM]) * sin
                 for h in range(ATT_HEADS)], axis=-1)
            k = jnp.concatenate(
                [k[:, h * HEAD_DIM:(h + 1) * HEAD_DIM] * cos + _rope_swap(k[:, h * HEAD_DIM:(h + 1) * HEAD_DIM]) * sin
                 for h in range(ATT_KV_HEADS)], axis=-1)
        else:
            for h in range(ATT_KV_HEADS):
                kout_ref[0, 0, rows, h, :] = k[:, h * HEAD_DIM:(h + 1) * HEAD_DIM]
                vout_ref[0, 0, rows, h, :] = v[:, h * HEAD_DIM:(h + 1) * HEAD_DIM]
        q_s[rows, :] = q.astype(BF16)
        k_s[rows, :] = k.astype(BF16)
        v_s[rows, :] = v.astype(BF16)
        return carry

    lax.fori_loop(0, n_tiles, project, 0)
    if latent:
        for h in range(ATT_KV_HEADS):
            k_s[seq:n_keys, h * HEAD_DIM:(h + 1) * HEAD_DIM] = ck_ref[0, 0, :, h, :].astype(BF16)
            v_s[seq:n_keys, h * HEAD_DIM:(h + 1) * HEAD_DIM] = cv_ref[0, 0, :, h, :].astype(BF16)

    exp2_scale = (HEAD_DIM ** -0.5) * math.log2(math.e)

    def attend(t, carry):
        rows = pl.ds(pl.multiple_of(t * ROW_TILE, ROW_TILE), ROW_TILE)

        def scores(h):
            kc = slice((h // ATT_GROUP) * HEAD_DIM, (h // ATT_GROUP + 1) * HEAD_DIM)
            return _dot_nt(q_s[rows, h * HEAD_DIM:(h + 1) * HEAD_DIM], k_s[:, kc])

        s = scores(0)
        for h in range(ATT_HEADS):
            s_next = scores(h + 1) if h + 1 < ATT_HEADS else None
            kc = slice((h // ATT_GROUP) * HEAD_DIM, (h // ATT_GROUP + 1) * HEAD_DIM)
            m = jnp.max(s, axis=-1, keepdims=True)
            p = jnp.exp2((s - m) * exp2_scale)
            denom = jnp.sum(p, axis=-1, keepdims=True)
            ao_s[rows, h * HEAD_DIM:(h + 1) * HEAD_DIM] = _dot(p.astype(BF16), v_s[:, kc]) * (1.0 / denom)
            s = s_next
        return carry

    lax.fori_loop(0, n_tiles, attend, 0)

    res_gate = mod_ref[0, 0, :, 2 * D_MODEL:]

    def finish(t, carry):
        rows = pl.ds(pl.multiple_of(t * ROW_TILE, ROW_TILE), ROW_TILE)
        y = _dot((ao_s[rows, :] * _silu(gate_s[rows, :])).astype(BF16), wout_ref[...])
        y_ref[0, rows, :] = x_ref[0, rows, :] + res_gate * y
        return carry

    lax.fori_loop(0, n_tiles, finish, 0)


def _att_layer(x, mods, layer, mod_row, ng, win, qn, kn, wout, latent_inputs, att_idx, n_att, caches_so_far):
    batch, seq, _ = x.shape
    latent = latent_inputs is not None
    in_specs = [
        pl.BlockSpec((1, seq, D_MODEL), lambda b: (b, 0, 0)),
        pl.BlockSpec((1, 1, 1, 3 * D_MODEL), lambda b: (layer, mod_row(b), 0, 0)),
        _resident((1, D_MODEL)),
        _resident((D_MODEL, 2 * ATT_QD + 2 * ATT_KD)),
        _resident((1, HEAD_DIM)),
        _resident((1, HEAD_DIM)),
        _resident((ATT_QD, D_MODEL)),
    ]
    args = [x, mods, ng, win, qn, kn, wout]
    n_keys = seq
    if latent:
        cos, sin, cache_k, cache_v = latent_inputs
        past = cache_k.shape[2]
        n_keys = seq + past
        in_specs += [
            _resident((seq, HEAD_DIM)),
            _resident((seq, HEAD_DIM)),
            pl.BlockSpec((1, 1, past, ATT_KV_HEADS, HEAD_DIM), lambda b: (b, att_idx, 0, 0, 0)),
            pl.BlockSpec((1, 1, past, ATT_KV_HEADS, HEAD_DIM), lambda b: (b, att_idx, 0, 0, 0)),
        ]
        args += [cos, sin, cache_k, cache_v]
    aliases = {}
    cache_alias = (not latent) and caches_so_far is not None
    if cache_alias:
        aliases = {len(args): 1, len(args) + 1: 2}
        in_specs += [pl.BlockSpec(memory_space=pl.ANY)] * 2
        args += list(caches_so_far)
    out_specs = [pl.BlockSpec((1, seq, D_MODEL), lambda b: (b, 0, 0))]
    out_shape = [jax.ShapeDtypeStruct((batch, seq, D_MODEL), F32)]
    if not latent:
        out_specs += [pl.BlockSpec((1, 1, seq, ATT_KV_HEADS, HEAD_DIM), lambda b: (b, att_idx, 0, 0, 0))] * 2
        out_shape += [jax.ShapeDtypeStruct((batch, n_att, seq, ATT_KV_HEADS, HEAD_DIM), F32)] * 2
    scratch = [
        pltpu.VMEM((seq, ATT_QD), BF16),
        pltpu.VMEM((n_keys, ATT_KD), BF16),
        pltpu.VMEM((n_keys, ATT_KD), BF16),
        pltpu.VMEM((seq, ATT_QD), F32),
        pltpu.VMEM((seq, ATT_QD), F32),
    ]
    outs = pl.pallas_call(
        functools.partial(_att_kernel, seq=seq, latent=latent, cache_alias=cache_alias),
        grid=(batch,),
        in_specs=in_specs,
        out_specs=out_specs,
        out_shape=out_shape,
        scratch_shapes=scratch,
        input_output_aliases=aliases,
        compiler_params=pltpu.CompilerParams(dimension_semantics=("arbitrary",),
                                             vmem_limit_bytes=V7X_VMEM_LIMIT_BYTES),
        name=f"att_layer_seq{seq}",
    )(*args)
    return outs


def _rope_tables(seq):
    half = HEAD_DIM // 2
    nf = half // 2
    pos = np.arange(seq)
    freqs = jnp.asarray(ROPE_THETA, F32) ** (-jnp.arange(nf, dtype=F32) / nf)
    ang_row = jnp.asarray(pos // GRID_W, F32)[:, None] * freqs[None, :]
    ang_col = jnp.asarray(pos % GRID_W, F32)[:, None] * freqs[None, :]
    cos = jnp.concatenate([jnp.cos(ang_row)] * 2 + [jnp.cos(ang_col)] * 2, axis=-1)
    sin = jnp.concatenate([-jnp.sin(ang_row), jnp.sin(ang_row), -jnp.sin(ang_col), jnp.sin(ang_col)], axis=-1)
    return cos, sin


def kernel(x_prompt, x_sample, state_gla, cache_k, cache_v, c, c_ctx, norm_g, w_ada, b_ada,
           gla_w_in, gla_wa1, gla_wa2, gla_ba, gla_onorm, gla_w_out,
           att_w_in, att_qnorm, att_knorm, att_w_out):
    n_dec = x_sample.shape[0]
    assert 1 + n_dec <= MOD_ROWS
    assert x_prompt.shape[1] % ROW_TILE == 0 and x_sample.shape[1] % ROW_TILE == 0

    cvec = jnp.zeros((MOD_ROWS, D_MODEL), F32).at[0].set(c_ctx).at[1:1 + n_dec].set(c)
    mods = _modulations(cvec, w_ada, b_ada).reshape(DEPTH, MOD_ROWS, 1, 3 * D_MODEL)
    ctx_row = lambda b: 0
    dec_row = lambda b: b + 1

    cos, sin = _rope_tables(x_sample.shape[1])
    n_gla, n_att = gla_w_in.shape[0], att_w_in.shape[0]

    xp, xs = x_prompt, x_sample
    states, caches = None, None
    for l in range(DEPTH):
        i = l // 2
        ng = norm_g[l].reshape(1, D_MODEL)
        if l % 2 == 0:
            win = gla_w_in[i].astype(BF16)
            wout = gla_w_out[i].astype(BF16)
            wa1 = jnp.zeros((D_MODEL, RANK_PAD), F32)
            wa1 = wa1.at[:, 0:GLA_RANK].set(gla_wa1[i, 0]).at[:, GLA_RANK:2 * GLA_RANK].set(gla_wa1[i, 1])
            wa2 = jnp.zeros((RANK_PAD, 2 * GLA_QD), F32)
            wa2 = wa2.at[0:GLA_RANK, 0:GLA_QD].set(gla_wa2[i, 0])
            wa2 = wa2.at[GLA_RANK:2 * GLA_RANK, GLA_QD:].set(gla_wa2[i, 1])
            ba = gla_ba[i].reshape(1, 2 * GLA_QD)
            on = gla_onorm[i].reshape(1, GLA_DV)
            common = (ng, win, wa1.astype(BF16), wa2.astype(BF16), ba, on, wout)
            xp, states = _gla_layer(xp, mods, l, ctx_row, *common, None, i, n_gla, states)
            xs, _ = _gla_layer(xs, mods, l, dec_row, *common, state_gla, i, n_gla, None)
        else:
            win = att_w_in[i].astype(BF16)
            wout = att_w_out[i].astype(BF16)
            qn = att_qnorm[i].reshape(1, HEAD_DIM)
            kn = att_knorm[i].reshape(1, HEAD_DIM)
            xp, *caches = _att_layer(xp, mods, l, ctx_row, ng, win, qn, kn, wout, None, i, n_att, caches)
            (xs,) = _att_layer(xs, mods, l, dec_row, ng, win, qn, kn, wout, (cos, sin, cache_k, cache_v), i, n_att, None)
    return (xp, xs, states, caches[0], caches[1])
```

```python
import functools
import math

import jax
import jax.numpy as jnp
import numpy as np
from jax import lax
from jax.experimental import pallas as pl
from jax.experimental.pallas import tpu as pltpu

D_MODEL = 1024
DEPTH = 4
GRID_W = 64
GLA_HEADS = 4
GLA_DK = 128
GLA_DV = 256
GLA_RANK = 16
GLA_TAU = 16.0
GLA_CHUNK = 64
GLA_QD = GLA_HEADS * GLA_DK
GLA_VD = GLA_HEADS * GLA_DV
HEAD_DIM = 128
ATT_HEADS = 8
ATT_KV_HEADS = 2
ATT_GROUP = ATT_HEADS // ATT_KV_HEADS
ATT_QD = ATT_HEADS * HEAD_DIM
ATT_KD = ATT_KV_HEADS * HEAD_DIM
ROPE_THETA = 10000.0
EPS = 1e-6

ROW_TILE = 256
GROUP_ROWS = 1024
MOD_ROWS = 8
RANK_PAD = 128
V7X_VMEM_LIMIT_BYTES = 60 * 1024 * 1024

F32 = jnp.float32
BF16 = jnp.bfloat16
_NT = (((1,), (1,)), ((), ()))
_TN = (((0,), (0,)), ((), ()))


def _dot(a, b):
    return jnp.dot(a, b, preferred_element_type=F32)


def _dot_nt(a, b):
    return lax.dot_general(a, b, _NT, preferred_element_type=F32)


def _dot_tn(a, b):
    return lax.dot_general(a, b, _TN, preferred_element_type=F32)


def _silu(x):
    return x * (1.0 / (1.0 + jnp.exp(-x)))


def _log_sigmoid(z):
    return jnp.minimum(z, 0.0) - jnp.log(1.0 + jnp.exp(-jnp.abs(z)))


def _split_top(x):
    top = pltpu.bitcast(pltpu.bitcast(x, jnp.uint32) & jnp.uint32(0xFFFF0000), F32)
    return top, x - top


def _modulated_norm(x, ng, mod_ref):
    shift = mod_ref[0, 0, :, 0:D_MODEL]
    scale = mod_ref[0, 0, :, D_MODEL:2 * D_MODEL]
    ms = jnp.mean(x * x, axis=-1, keepdims=True)
    return ((x * lax.rsqrt(ms + EPS)) * (ng * (1.0 + scale)) + shift).astype(BF16)


def _head_rms(x, g, width):
    outs = []
    for h in range(x.shape[-1] // width):
        xh = x[:, h * width:(h + 1) * width]
        ms = jnp.mean(xh * xh, axis=-1, keepdims=True)
        outs.append((xh * lax.rsqrt(ms + EPS)) * g)
    return jnp.concatenate(outs, axis=-1)


def _mod_kernel(c_ref, w_ref, b_ref, o_ref):
    acc = _dot(jnp.concatenate(_split_top(_silu(c_ref[...])), axis=0).astype(BF16), w_ref[0].astype(BF16))
    o_ref[0] = acc[0:MOD_ROWS] + acc[MOD_ROWS:] + b_ref[0]


def _modulations(cvec, w_ada, b_ada):
    tn = D_MODEL
    n_tiles = 3 * D_MODEL // tn
    return pl.pallas_call(
        _mod_kernel,
        grid=(DEPTH, n_tiles),
        in_specs=[
            pl.BlockSpec((MOD_ROWS, D_MODEL), lambda l, j: (0, 0)),
            pl.BlockSpec((1, D_MODEL, tn), lambda l, j: (l, 0, j)),
            pl.BlockSpec((1, 1, tn), lambda l, j: (l, 0, j)),
        ],
        out_specs=pl.BlockSpec((1, MOD_ROWS, tn), lambda l, j: (l, 0, j)),
        out_shape=jax.ShapeDtypeStruct((DEPTH, MOD_ROWS, 3 * D_MODEL), F32),
        compiler_params=pltpu.CompilerParams(dimension_semantics=("arbitrary", "arbitrary")),
        name="adaln_modulation",
    )(cvec, w_ada, b_ada.reshape(DEPTH, 1, 3 * D_MODEL))


def _chunk_sum_matrices():
    r = lax.broadcasted_iota(jnp.int32, (ROW_TILE, ROW_TILE), 0)
    c = lax.broadcasted_iota(jnp.int32, (ROW_TILE, ROW_TILE), 1)
    same = (r // GLA_CHUNK) == (c // GLA_CHUNK)
    prefix = jnp.where(same & (c <= r), 1.0, 0.0).astype(BF16)
    suffix = jnp.where(same & (c >= r), 1.0, 0.0).astype(BF16)
    return prefix, suffix


def _gla_kernel(*refs, seq, has_s0, emit_state, state_alias):
    n_seq = GROUP_ROWS // seq
    n_tiles = GROUP_ROWS // ROW_TILE
    n_pos = seq // GLA_CHUNK
    chunks_per_tile = ROW_TILE // GLA_CHUNK
    it = iter(refs)
    x_ref, mod_ref, ng_ref, win_ref, wa1_ref, wa2_ref, ba_ref, on_ref, wout_ref = (next(it) for _ in range(9))
    s0_ref = next(it) if has_s0 else None
    if state_alias:
        next(it)
    y_ref = next(it)
    sout_ref = next(it) if emit_state else None
    qe_s, ke_s, kdt_s, v_s, gate_s, dec_s, o_s = (next(it) for _ in range(7))
    st_s = None if emit_state else next(it)

    def state_at(s, d, h):
        return sout_ref.at[s, 0, d, h] if emit_state else st_s.at[d, h]

    ng = ng_ref[...]
    prefix_m, suffix_m = _chunk_sum_matrices()

    def project_wide(t):
        rows = slice(t * ROW_TILE, (t + 1) * ROW_TILE)
        hb = _modulated_norm(x_ref[0, rows, :], ng, mod_ref)
        q = _dot(hb, win_ref[0, :, 0:GLA_QD]) * (GLA_DK ** -0.5)
        k = _dot(hb, win_ref[0, :, GLA_QD:2 * GLA_QD])
        v_s[rows, :] = _dot(hb, win_ref[0, :, 2 * GLA_QD:2 * GLA_QD + GLA_VD]).astype(BF16)
        gate_s[rows, :] = _dot(hb, win_ref[0, :, 2 * GLA_QD + GLA_VD:])
        low = _dot(hb, wa1_ref[...]).astype(BF16)
        z = _dot(low, wa2_ref[...]) + ba_ref[...]
        return q, k, z

    def decay_sums(z):
        logg2 = _log_sigmoid(z) * (math.log2(math.e) / GLA_TAU)
        sums = []
        for d, csum_m in enumerate((prefix_m, suffix_m)):
            parts = jnp.concatenate(_split_top(logg2[:, d * GLA_QD:(d + 1) * GLA_QD]), axis=0).astype(BF16)
            sums.append(_dot(jnp.concatenate([csum_m, csum_m], axis=1), parts))
        return sums

    def decay_factors(t, q, k, sums):
        rows = slice(t * ROW_TILE, (t + 1) * ROW_TILE)
        for d, b in enumerate(sums):
            edge = GLA_CHUNK - 1 if d == 0 else 0
            totals = [b[j * GLA_CHUNK + edge:j * GLA_CHUNK + edge + 1, :] for j in range(chunks_per_tile)]
            b_last = jnp.concatenate([jnp.broadcast_to(tt, (GLA_CHUNK, GLA_QD)) for tt in totals], axis=0)
            qe_s[d, rows, :] = (q * jnp.exp2(b)).astype(BF16)
            ke_s[d, rows, :] = (k * jnp.exp2(-b)).astype(BF16)
            kd = k * jnp.exp2(b_last - b)
            for j in range(chunks_per_tile):
                c = t * chunks_per_tile + j
                kdt_s[d, c] = kd[j * GLA_CHUNK:(j + 1) * GLA_CHUNK, :].T.astype(BF16)
                dec_s[d, c] = jnp.exp2(totals[j])

    wide = project_wide(0)
    for t in range(n_tiles):
        nxt = project_wide(t + 1) if t + 1 < n_tiles else None
        q, k, z = wide
        decay_factors(t, q, k, decay_sums(z))
        wide = nxt

    for s in range(n_seq):
        for d in range(2):
            for h in range(GLA_HEADS):
                state_at(s, d, h)[...] = s0_ref[0, 0, d, h] if has_s0 else jnp.zeros((GLA_DK, GLA_DV), F32)

    ri = lax.broadcasted_iota(jnp.int32, (GLA_CHUNK, GLA_CHUNK), 0)
    ci = lax.broadcasted_iota(jnp.int32, (GLA_CHUNK, GLA_CHUNK), 1)
    masks = (ci <= ri, ci >= ri)

    def scan_step(i, first_visit):
        chains = []
        for s in range(n_seq):
            for d in range(2):
                c = s * n_pos + (i if d == 0 else n_pos - 1 - i)
                rows = pl.ds(pl.multiple_of(c * GLA_CHUNK, GLA_CHUNK), GLA_CHUNK)
                for h in range(GLA_HEADS):
                    chains.append((s, d, h, c, rows, slice(h * GLA_DK, (h + 1) * GLA_DK),
                                   slice(h * GLA_DV, (h + 1) * GLA_DV)))
        scores = [_dot_nt(qe_s[d, rows, kc], ke_s[d, rows, kc]) for (s, d, h, c, rows, kc, vc) in chains]
        updates = [_dot(kdt_s[d, c, kc, :], v_s[rows, vc]) for (s, d, h, c, rows, kc, vc) in chains]
        for (s, d, h, c, rows, kc, vc), sc in zip(chains, scores):
            a = jnp.where(masks[d], sc, 0.0).astype(BF16)
            o = _dot(jnp.concatenate([qe_s[d, rows, kc], a], axis=1),
                     jnp.concatenate([state_at(s, d, h)[...].astype(BF16), v_s[rows, vc]], axis=0))
            if first_visit:
                o_s[rows, vc] = o
            else:
                o_s[rows, vc] += o
        for (s, d, h, c, rows, kc, vc), upd in zip(chains, updates):
            dec_col = jnp.broadcast_to(dec_s[d, c][:, kc], (GLA_DK, GLA_DK)).T
            st_ref = state_at(s, d, h)
            st_ref[...] = jnp.concatenate([dec_col] * (GLA_DV // GLA_DK), axis=1) * st_ref[...] + upd

    def first_half(i, carry):
        scan_step(i, True)
        return carry

    def second_half(i, carry):
        scan_step(i, False)
        return carry

    lax.fori_loop(0, n_pos // 2, first_half, 0)
    lax.fori_loop(n_pos // 2, n_pos, second_half, 0)

    on = on_ref[...]
    res_gate = mod_ref[0, 0, :, 2 * D_MODEL:]

    def gated(t):
        rows = slice(t * ROW_TILE, (t + 1) * ROW_TILE)
        return (_head_rms(o_s[rows, :], on, GLA_DV) * _silu(gate_s[rows, :])).astype(BF16)

    og = gated(0)
    for t in range(n_tiles):
        rows = slice(t * ROW_TILE, (t + 1) * ROW_TILE)
        og_next = gated(t + 1) if t + 1 < n_tiles else None
        y_ref[0, rows, :] = x_ref[0, rows, :] + res_gate * _dot(og, wout_ref[0])
        og = og_next


def _resident(shape, layer=None):
    if layer is None:
        return pl.BlockSpec(shape, lambda b: (0,) * len(shape), pipeline_mode=pl.Buffered(1))
    return pl.BlockSpec((1,) + shape, lambda b: (layer,) + (0,) * len(shape), pipeline_mode=pl.Buffered(1))


def _gla_layer(x, mods, layer, mod_row, ng, win, wa1, wa2, ba, on, wout, state_in, gla_idx, n_gla, states_so_far):
    batch, seq, _ = x.shape
    n_seq = GROUP_ROWS // seq
    n_groups = batch // n_seq
    has_s0 = state_in is not None
    emit_state = not has_s0
    assert not has_s0 or n_seq == 1
    n_chunks = GROUP_ROWS // GLA_CHUNK
    in_specs = [
        pl.BlockSpec((1, GROUP_ROWS, D_MODEL), lambda b: (b, 0, 0)),
        pl.BlockSpec((1, 1, 1, 3 * D_MODEL), lambda b: (layer, mod_row(b), 0, 0)),
        _resident((1, D_MODEL)),
        _resident((D_MODEL, 2 * GLA_QD + 2 * GLA_VD), gla_idx),
        _resident((D_MODEL, RANK_PAD)),
        _resident((RANK_PAD, 2 * GLA_QD)),
        _resident((1, 2 * GLA_QD)),
        _resident((1, GLA_DV)),
        _resident((GLA_VD, D_MODEL), gla_idx),
    ]
    args = [x.reshape(n_groups, GROUP_ROWS, D_MODEL), mods, ng, win, wa1, wa2, ba, on, wout]
    if has_s0:
        in_specs.append(pl.BlockSpec((1, 1, 2, GLA_HEADS, GLA_DK, GLA_DV),
                                     lambda b: (b, gla_idx, 0, 0, 0, 0)))
        args.append(state_in)
    aliases = {}
    state_alias = emit_state and states_so_far is not None
    if state_alias:
        aliases[len(args)] = 1
        in_specs.append(pl.BlockSpec(memory_space=pl.ANY))
        args.append(states_so_far)
    out_specs = [pl.BlockSpec((1, GROUP_ROWS, D_MODEL), lambda b: (b, 0, 0))]
    out_shape = [jax.ShapeDtypeStruct((n_groups, GROUP_ROWS, D_MODEL), F32)]
    if emit_state:
        out_specs.append(pl.BlockSpec((n_seq, 1, 2, GLA_HEADS, GLA_DK, GLA_DV), lambda b: (b, gla_idx, 0, 0, 0, 0)))
        out_shape.append(jax.ShapeDtypeStruct((batch, n_gla, 2, GLA_HEADS, GLA_DK, GLA_DV), F32))
    scratch = [
        pltpu.VMEM((2, GROUP_ROWS, GLA_QD), BF16),
        pltpu.VMEM((2, GROUP_ROWS, GLA_QD), BF16),
        pltpu.VMEM((2, n_chunks, GLA_QD, GLA_CHUNK), BF16),
        pltpu.VMEM((GROUP_ROWS, GLA_VD), BF16),
        pltpu.VMEM((GROUP_ROWS, GLA_VD), F32),
        pltpu.VMEM((2, n_chunks, 1, GLA_QD), F32),
        pltpu.VMEM((GROUP_ROWS, GLA_VD), F32),
    ]
    if not emit_state:
        scratch.append(pltpu.VMEM((2, GLA_HEADS, GLA_DK, GLA_DV), F32))
    outs = pl.pallas_call(
        functools.partial(_gla_kernel, seq=seq, has_s0=has_s0, emit_state=emit_state, state_alias=state_alias),
        grid=(n_groups,),
        in_specs=in_specs,
        out_specs=out_specs,
        out_shape=out_shape,
        scratch_shapes=scratch,
        input_output_aliases=aliases,
        compiler_params=pltpu.CompilerParams(dimension_semantics=("arbitrary",),
                                             vmem_limit_bytes=V7X_VMEM_LIMIT_BYTES),
        name=f"gla_layer_seq{seq}",
    )(*args)
    y = outs[0].reshape(batch, seq, D_MODEL)
    return (y, outs[1]) if emit_state else (y, None)


def _rope_swap(x):
    lane = lax.broadcasted_iota(jnp.int32, x.shape, 1)
    quarter = HEAD_DIM // 4
    first = (lane % (2 * quarter)) < quarter
    return jnp.where(first, pltpu.roll(x, HEAD_DIM - quarter, 1), pltpu.roll(x, quarter, 1))


def _att_kernel(*refs, seq, latent, cache_alias):
    n_tiles = seq // ROW_TILE
    n_keys = seq + (refs[9].shape[2] if latent else 0)
    it = iter(refs)
    x_ref, mod_ref, ng_ref, win_ref, qn_ref, kn_ref, wout_ref = (next(it) for _ in range(7))
    if latent:
        cos_ref, sin_ref, ck_ref, cv_ref = (next(it) for _ in range(4))
    if cache_alias:
        next(it), next(it)
    y_ref = next(it)
    if not latent:
        kout_ref, vout_ref = next(it), next(it)
    q_s, k_s, v_s, gate_s, ao_s = (next(it) for _ in range(5))

    ng = ng_ref[...]
    qn = qn_ref[...]
    kn = kn_ref[...]

    def project(t, carry):
        rows = pl.ds(pl.multiple_of(t * ROW_TILE, ROW_TILE), ROW_TILE)
        hb = _modulated_norm(x_ref[0, rows, :], ng, mod_ref)
        q = _head_rms(_dot(hb, win_ref[0, :, 0:ATT_QD]), qn, HEAD_DIM)
        k = _head_rms(_dot(hb, win_ref[0, :, ATT_QD:ATT_QD + ATT_KD]), kn, HEAD_DIM)
        v = _dot(hb, win_ref[0, :, ATT_QD + ATT_KD:ATT_QD + 2 * ATT_KD])
        gate_s[rows, :] = _dot(hb, win_ref[0, :, ATT_QD + 2 * ATT_KD:])
        if latent:
            cos = cos_ref[rows, :]
            sin = sin_ref[rows, :]
            q = jnp.concatenate(
                [q[:, h * HEAD_DIM:(h + 1) * HEAD_DIM] * cos + _rope_swap(q[:, h * HEAD_DIM:(h + 1) * HEAD_DIM]) * sin
                 for h in range(ATT_HEADS)], axis=-1)
            k = jnp.concatenate(
                [k[:, h * HEAD_DIM:(h + 1) * HEAD_DIM] * cos + _rope_swap(k[:, h * HEAD_DIM:(h + 1) * HEAD_DIM]) * sin
                 for h in range(ATT_KV_HEADS)], axis=-1)
        else:
            for h in range(ATT_KV_HEADS):
                kout_ref[0, 0, rows, h, :] = k[:, h * HEAD_DIM:(h + 1) * HEAD_DIM]
                vout_ref[0, 0, rows, h, :] = v[:, h * HEAD_DIM:(h + 1) * HEAD_DIM]
        q_s[rows, :] = q.astype(BF16)
        k_s[rows, :] = k.astype(BF16)
        v_s[rows, :] = v.astype(BF16)
        return carry

    lax.fori_loop(0, n_tiles, project, 0)
    if latent:
        for h in range(ATT_KV_HEADS):
            k_s[seq:n_keys, h * HEAD_DIM:(h + 1) * HEAD_DIM] = ck_ref[0, 0, :, h, :].astype(BF16)
            v_s[seq:n_keys, h * HEAD_DIM:(h + 1) * HEAD_DIM] = cv_ref[0, 0, :, h, :].astype(BF16)

    exp2_scale = (HEAD_DIM ** -0.5) * math.log2(math.e)

    def attend(t, carry):
        rows = pl.ds(pl.multiple_of(t * ROW_TILE, ROW_TILE), ROW_TILE)

        def scores(h):
            kc = slice((h // ATT_GROUP) * HEAD_DIM, (h // ATT_GROUP + 1) * HEAD_DIM)
            return _dot_nt(q_s[rows, h * HEAD_DIM:(h + 1) * HEAD_DIM], k_s[:, kc])

        s = scores(0)
        for h in range(ATT_HEADS):
            s_next = scores(h + 1) if h + 1 < ATT_HEADS else None
            kc = slice((h // ATT_GROUP) * HEAD_DIM, (h // ATT_GROUP + 1) * HEAD_DIM)
            m = jnp.max(s, axis=-1, keepdims=True)
            p = jnp.exp2((s - m) * exp2_scale)
            denom = jnp.sum(p, axis=-1, keepdims=True)
            ao_s[rows, h * HEAD_DIM:(h + 1) * HEAD_DIM] = _dot(p.astype(BF16), v_s[:, kc]) * (1.0 / denom)
            s = s_next
        return carry

    lax.fori_loop(0, n_tiles, attend, 0)

    res_gate = mod_ref[0, 0, :, 2 * D_MODEL:]

    def finish(t, carry):
        rows = pl.ds(pl.multiple_of(t * ROW_TILE, ROW_TILE), ROW_TILE)
        y = _dot((ao_s[rows, :] * _silu(gate_s[rows, :])).astype(BF16), wout_ref[0])
        y_ref[0, rows, :] = x_ref[0, rows, :] + res_gate * y
        return carry

    lax.fori_loop(0, n_tiles, finish, 0)


def _att_layer(x, mods, layer, mod_row, ng, win, qn, kn, wout, latent_inputs, att_idx, n_att, caches_so_far):
    batch, seq, _ = x.shape
    latent = latent_inputs is not None
    in_specs = [
        pl.BlockSpec((1, seq, D_MODEL), lambda b: (b, 0, 0)),
        pl.BlockSpec((1, 1, 1, 3 * D_MODEL), lambda b: (layer, mod_row(b), 0, 0)),
        _resident((1, D_MODEL)),
        _resident((D_MODEL, 2 * ATT_QD + 2 * ATT_KD), att_idx),
        _resident((1, HEAD_DIM)),
        _resident((1, HEAD_DIM)),
        _resident((ATT_QD, D_MODEL), att_idx),
    ]
    args = [x, mods, ng, win, qn, kn, wout]
    n_keys = seq
    if latent:
        cos, sin, cache_k, cache_v = latent_inputs
        past = cache_k.shape[2]
        n_keys = seq + past
        in_specs += [
            _resident((seq, HEAD_DIM)),
            _resident((seq, HEAD_DIM)),
            pl.BlockSpec((1, 1, past, ATT_KV_HEADS, HEAD_DIM), lambda b: (b, att_idx, 0, 0, 0)),
            pl.BlockSpec((1, 1, past, ATT_KV_HEADS, HEAD_DIM), lambda b: (b, att_idx, 0, 0, 0)),
        ]
        args += [cos, sin, cache_k, cache_v]
    aliases = {}
    cache_alias = (not latent) and caches_so_far is not None
    if cache_alias:
        aliases = {len(args): 1, len(args) + 1: 2}
        in_specs += [pl.BlockSpec(memory_space=pl.ANY)] * 2
        args += list(caches_so_far)
    out_specs = [pl.BlockSpec((1, seq, D_MODEL), lambda b: (b, 0, 0))]
    out_shape = [jax.ShapeDtypeStruct((batch, seq, D_MODEL), F32)]
    if not latent:
        out_specs += [pl.BlockSpec((1, 1, seq, ATT_KV_HEADS, HEAD_DIM), lambda b: (b, att_idx, 0, 0, 0))] * 2
        out_shape += [jax.ShapeDtypeStruct((batch, n_att, seq, ATT_KV_HEADS, HEAD_DIM), F32)] * 2
    scratch = [
        pltpu.VMEM((seq, ATT_QD), BF16),
        pltpu.VMEM((n_keys, ATT_KD), BF16),
        pltpu.VMEM((n_keys, ATT_KD), BF16),
        pltpu.VMEM((seq, ATT_QD), F32),
        pltpu.VMEM((seq, ATT_QD), F32),
    ]
    outs = pl.pallas_call(
        functools.partial(_att_kernel, seq=seq, latent=latent, cache_alias=cache_alias),
        grid=(batch,),
        in_specs=in_specs,
        out_specs=out_specs,
        out_shape=out_shape,
        scratch_shapes=scratch,
        input_output_aliases=aliases,
        compiler_params=pltpu.CompilerParams(dimension_semantics=("arbitrary",),
                                             vmem_limit_bytes=V7X_VMEM_LIMIT_BYTES),
        name=f"att_layer_seq{seq}",
    )(*args)
    return outs


def _rope_tables(seq):
    half = HEAD_DIM // 2
    nf = half // 2
    pos = np.arange(seq)
    freqs = jnp.asarray(ROPE_THETA, F32) ** (-jnp.arange(nf, dtype=F32) / nf)
    ang_row = jnp.asarray(pos // GRID_W, F32)[:, None] * freqs[None, :]
    ang_col = jnp.asarray(pos % GRID_W, F32)[:, None] * freqs[None, :]
    cos = jnp.concatenate([jnp.cos(ang_row)] * 2 + [jnp.cos(ang_col)] * 2, axis=-1)
    sin = jnp.concatenate([-jnp.sin(ang_row), jnp.sin(ang_row), -jnp.sin(ang_col), jnp.sin(ang_col)], axis=-1)
    return cos, sin


def kernel(x_prompt, x_sample, state_gla, cache_k, cache_v, c, c_ctx, norm_g, w_ada, b_ada,
           gla_w_in, gla_wa1, gla_wa2, gla_ba, gla_onorm, gla_w_out,
           att_w_in, att_qnorm, att_knorm, att_w_out):
    n_dec = x_sample.shape[0]
    assert 1 + n_dec <= MOD_ROWS
    assert x_prompt.shape[1] % ROW_TILE == 0 and x_sample.shape[1] % ROW_TILE == 0

    cvec = jnp.zeros((MOD_ROWS, D_MODEL), F32).at[0].set(c_ctx).at[1:1 + n_dec].set(c)
    mods = _modulations(cvec, w_ada, b_ada).reshape(DEPTH, MOD_ROWS, 1, 3 * D_MODEL)
    ctx_row = lambda b: 0
    dec_row = lambda b: b + 1

    cos, sin = _rope_tables(x_sample.shape[1])
    n_gla, n_att = gla_w_in.shape[0], att_w_in.shape[0]

    gla_win, gla_wout = gla_w_in.astype(BF16), gla_w_out.astype(BF16)
    att_win, att_wout = att_w_in.astype(BF16), att_w_out.astype(BF16)

    xp, xs = x_prompt, x_sample
    states, caches = None, None
    for l in range(DEPTH):
        i = l // 2
        ng = norm_g[l].reshape(1, D_MODEL)
        if l % 2 == 0:
            wa1 = jnp.zeros((D_MODEL, RANK_PAD), F32)
            wa1 = wa1.at[:, 0:GLA_RANK].set(gla_wa1[i, 0]).at[:, GLA_RANK:2 * GLA_RANK].set(gla_wa1[i, 1])
            wa2 = jnp.zeros((RANK_PAD, 2 * GLA_QD), F32)
            wa2 = wa2.at[0:GLA_RANK, 0:GLA_QD].set(gla_wa2[i, 0])
            wa2 = wa2.at[GLA_RANK:2 * GLA_RANK, GLA_QD:].set(gla_wa2[i, 1])
            ba = gla_ba[i].reshape(1, 2 * GLA_QD)
            on = gla_onorm[i].reshape(1, GLA_DV)
            common = (ng, gla_win, wa1.astype(BF16), wa2.astype(BF16), ba, on, gla_wout)
            xp, states = _gla_layer(xp, mods, l, ctx_row, *common, None, i, n_gla, states)
            xs, _ = _gla_layer(xs, mods, l, dec_row, *common, state_gla, i, n_gla, None)
        else:
            qn = att_qnorm[i].reshape(1, HEAD_DIM)
            kn = att_knorm[i].reshape(1, HEAD_DIM)
            xp, *caches = _att_layer(xp, mods, l, ctx_row, ng, att_win, qn, kn, att_wout, None, i, n_att, caches)
            (xs,) = _att_layer(xs, mods, l, dec_row, ng, att_win, qn, kn, att_wout, (cos, sin, cache_k, cache_v), i,
                               n_att, None)
    return (xp, xs, states, caches[0], caches[1])
```

```python
import functools
import math

import jax
import jax.numpy as jnp
import numpy as np
from jax import lax
from jax.experimental import pallas as pl
from jax.experimental.pallas import tpu as pltpu

D_MODEL = 1024
DEPTH = 4
GRID_W = 64
GLA_HEADS = 4
GLA_DK = 128
GLA_DV = 256
GLA_RANK = 16
GLA_TAU = 16.0
GLA_CHUNK = 64
GLA_QD = GLA_HEADS * GLA_DK
GLA_VD = GLA_HEADS * GLA_DV
HEAD_DIM = 128
ATT_HEADS = 8
ATT_KV_HEADS = 2
ATT_GROUP = ATT_HEADS // ATT_KV_HEADS
ATT_QD = ATT_HEADS * HEAD_DIM
ATT_KD = ATT_KV_HEADS * HEAD_DIM
ROPE_THETA = 10000.0
EPS = 1e-6

ROW_TILE = 256
GROUP_ROWS = 1024
MOD_ROWS = 8
RANK_PAD = 128
V7X_VMEM_LIMIT_BYTES = 60 * 1024 * 1024

F32 = jnp.float32
BF16 = jnp.bfloat16
_NT = (((1,), (1,)), ((), ()))
_TN = (((0,), (0,)), ((), ()))


def _dot(a, b):
    return jnp.dot(a, b, preferred_element_type=F32)


def _dot_nt(a, b):
    return lax.dot_general(a, b, _NT, preferred_element_type=F32)


def _dot_tn(a, b):
    return lax.dot_general(a, b, _TN, preferred_element_type=F32)


def _silu(x):
    return x * (1.0 / (1.0 + jnp.exp(-x)))


def _log_sigmoid(z):
    return jnp.minimum(z, 0.0) - jnp.log(1.0 + jnp.exp(-jnp.abs(z)))


def _split_top(x):
    top = pltpu.bitcast(pltpu.bitcast(x, jnp.uint32) & jnp.uint32(0xFFFF0000), F32)
    return top, x - top


def _modulated_norm(x, ng, mod_ref):
    shift = mod_ref[0, 0, :, 0:D_MODEL]
    scale = mod_ref[0, 0, :, D_MODEL:2 * D_MODEL]
    ms = jnp.mean(x * x, axis=-1, keepdims=True)
    return ((x * lax.rsqrt(ms + EPS)) * (ng * (1.0 + scale)) + shift).astype(BF16)


def _head_rms(x, g, width):
    outs = []
    for h in range(x.shape[-1] // width):
        xh = x[:, h * width:(h + 1) * width]
        ms = jnp.mean(xh * xh, axis=-1, keepdims=True)
        outs.append((xh * lax.rsqrt(ms + EPS)) * g)
    return jnp.concatenate(outs, axis=-1)


def _mod_kernel(c_ref, w_ref, b_ref, o_ref):
    acc = _dot(jnp.concatenate(_split_top(_silu(c_ref[...])), axis=0).astype(BF16), w_ref[0].astype(BF16))
    o_ref[0] = acc[0:MOD_ROWS] + acc[MOD_ROWS:] + b_ref[0]


def _modulations(cvec, w_ada, b_ada):
    tn = D_MODEL
    n_tiles = 3 * D_MODEL // tn
    return pl.pallas_call(
        _mod_kernel,
        grid=(DEPTH, n_tiles),
        in_specs=[
            pl.BlockSpec((MOD_ROWS, D_MODEL), lambda l, j: (0, 0)),
            pl.BlockSpec((1, D_MODEL, tn), lambda l, j: (l, 0, j)),
            pl.BlockSpec((1, 1, tn), lambda l, j: (l, 0, j)),
        ],
        out_specs=pl.BlockSpec((1, MOD_ROWS, tn), lambda l, j: (l, 0, j)),
        out_shape=jax.ShapeDtypeStruct((DEPTH, MOD_ROWS, 3 * D_MODEL), F32),
        compiler_params=pltpu.CompilerParams(dimension_semantics=("arbitrary", "arbitrary")),
        name="adaln_modulation",
    )(cvec, w_ada, b_ada.reshape(DEPTH, 1, 3 * D_MODEL))


def _chunk_sum_matrices():
    r = lax.broadcasted_iota(jnp.int32, (ROW_TILE, ROW_TILE), 0)
    c = lax.broadcasted_iota(jnp.int32, (ROW_TILE, ROW_TILE), 1)
    same = (r // GLA_CHUNK) == (c // GLA_CHUNK)
    prefix = jnp.where(same & (c <= r), 1.0, 0.0).astype(BF16)
    suffix = jnp.where(same & (c >= r), 1.0, 0.0).astype(BF16)
    return prefix, suffix


def _gla_kernel(*refs, seq, has_s0, emit_state, state_alias):
    n_seq = GROUP_ROWS // seq
    n_tiles = GROUP_ROWS // ROW_TILE
    n_pos = seq // GLA_CHUNK
    chunks_per_tile = ROW_TILE // GLA_CHUNK
    it = iter(refs)
    x_ref, mod_ref, ng_ref, win_ref, wa1_ref, wa2_ref, ba_ref, on_ref, wout_ref = (next(it) for _ in range(9))
    s0_ref = next(it) if has_s0 else None
    if state_alias:
        next(it)
    y_ref = next(it)
    sout_ref = next(it) if emit_state else None
    qe_s, ke_s, kdt_s, v_s, gate_s, dec_s, o_s = (next(it) for _ in range(7))
    st_s = None if emit_state else next(it)

    def state_at(s, d, h):
        return sout_ref.at[s, 0, d, h] if emit_state else st_s.at[d, h]

    ng = ng_ref[...]
    prefix_m, suffix_m = _chunk_sum_matrices()

    def project_wide(t):
        rows = slice(t * ROW_TILE, (t + 1) * ROW_TILE)
        hb = _modulated_norm(x_ref[0, rows, :], ng, mod_ref)
        q = _dot(hb, win_ref[0, :, 0:GLA_QD]) * (GLA_DK ** -0.5)
        k = _dot(hb, win_ref[0, :, GLA_QD:2 * GLA_QD])
        v_s[rows, :] = _dot(hb, win_ref[0, :, 2 * GLA_QD:2 * GLA_QD + GLA_VD]).astype(BF16)
        gate_s[rows, :] = _dot(hb, win_ref[0, :, 2 * GLA_QD + GLA_VD:])
        low = _dot(hb, wa1_ref[...]).astype(BF16)
        z = _dot(low, wa2_ref[...]) + ba_ref[...]
        return q, k, z

    def decay_sums(z):
        logg2 = _log_sigmoid(z) * (math.log2(math.e) / GLA_TAU)
        sums = []
        for d, csum_m in enumerate((prefix_m, suffix_m)):
            parts = jnp.concatenate(_split_top(logg2[:, d * GLA_QD:(d + 1) * GLA_QD]), axis=0).astype(BF16)
            sums.append(_dot(jnp.concatenate([csum_m, csum_m], axis=1), parts))
        return sums

    def decay_factors(t, q, k, sums):
        rows = slice(t * ROW_TILE, (t + 1) * ROW_TILE)
        for d, b in enumerate(sums):
            edge = GLA_CHUNK - 1 if d == 0 else 0
            totals = [b[j * GLA_CHUNK + edge:j * GLA_CHUNK + edge + 1, :] for j in range(chunks_per_tile)]
            b_last = jnp.concatenate([jnp.broadcast_to(tt, (GLA_CHUNK, GLA_QD)) for tt in totals], axis=0)
            qe_s[d, rows, :] = (q * jnp.exp2(b)).astype(BF16)
            ke_s[d, rows, :] = (k * jnp.exp2(-b)).astype(BF16)
            kd = k * jnp.exp2(b_last - b)
            for j in range(chunks_per_tile):
                c = t * chunks_per_tile + j
                kdt_s[d, c] = kd[j * GLA_CHUNK:(j + 1) * GLA_CHUNK, :].T.astype(BF16)
                dec_s[d, c] = jnp.exp2(totals[j])

    wide = project_wide(0)
    for t in range(n_tiles):
        nxt = project_wide(t + 1) if t + 1 < n_tiles else None
        q, k, z = wide
        decay_factors(t, q, k, decay_sums(z))
        wide = nxt

    for s in range(n_seq):
        for d in range(2):
            for h in range(GLA_HEADS):
                state_at(s, d, h)[...] = s0_ref[0, 0, d, h] if has_s0 else jnp.zeros((GLA_DK, GLA_DV), F32)

    ri = lax.broadcasted_iota(jnp.int32, (GLA_CHUNK, GLA_CHUNK), 0)
    ci = lax.broadcasted_iota(jnp.int32, (GLA_CHUNK, GLA_CHUNK), 1)
    masks = (ci <= ri, ci >= ri)

    def scan_step(i, first_visit):
        chains = []
        for s in range(n_seq):
            for d in range(2):
                c = s * n_pos + (i if d == 0 else n_pos - 1 - i)
                rows = pl.ds(pl.multiple_of(c * GLA_CHUNK, GLA_CHUNK), GLA_CHUNK)
                for h in range(GLA_HEADS):
                    chains.append((s, d, h, c, rows, slice(h * GLA_DK, (h + 1) * GLA_DK),
                                   slice(h * GLA_DV, (h + 1) * GLA_DV)))
        scores = [_dot_nt(qe_s[d, rows, kc], ke_s[d, rows, kc]) for (s, d, h, c, rows, kc, vc) in chains]
        updates = [_dot(kdt_s[d, c, kc, :], v_s[rows, vc]) for (s, d, h, c, rows, kc, vc) in chains]
        for (s, d, h, c, rows, kc, vc), sc in zip(chains, scores):
            a = jnp.where(masks[d], sc, 0.0).astype(BF16)
            o = _dot(jnp.concatenate([qe_s[d, rows, kc], a], axis=1),
                     jnp.concatenate([state_at(s, d, h)[...].astype(BF16), v_s[rows, vc]], axis=0))
            if first_visit:
                o_s[rows, vc] = o
            else:
                o_s[rows, vc] += o
        for (s, d, h, c, rows, kc, vc), upd in zip(chains, updates):
            dec_col = jnp.broadcast_to(dec_s[d, c][:, kc], (GLA_DK, GLA_DK)).T
            st_ref = state_at(s, d, h)
            st_ref[...] = jnp.concatenate([dec_col] * (GLA_DV // GLA_DK), axis=1) * st_ref[...] + upd

    def first_half(i, carry):
        scan_step(i, True)
        return carry

    def second_half(i, carry):
        scan_step(i, False)
        return carry

    lax.fori_loop(0, n_pos // 2, first_half, 0)
    lax.fori_loop(n_pos // 2, n_pos, second_half, 0)

    on = on_ref[...]
    res_gate = mod_ref[0, 0, :, 2 * D_MODEL:]

    def gated(t):
        rows = slice(t * ROW_TILE, (t + 1) * ROW_TILE)
        return (_head_rms(o_s[rows, :], on, GLA_DV) * _silu(gate_s[rows, :])).astype(BF16)

    og = gated(0)
    for t in range(n_tiles):
        rows = slice(t * ROW_TILE, (t + 1) * ROW_TILE)
        og_next = gated(t + 1) if t + 1 < n_tiles else None
        y_ref[0, rows, :] = x_ref[0, rows, :] + res_gate * _dot(og, wout_ref[0])
        og = og_next


def _resident(shape, layer=None):
    if layer is None:
        return pl.BlockSpec(shape, lambda b: (0,) * len(shape), pipeline_mode=pl.Buffered(1))
    return pl.BlockSpec((1,) + shape, lambda b: (layer,) + (0,) * len(shape), pipeline_mode=pl.Buffered(1))


def _gla_layer(x, mods, layer, mod_row, ng, win, wa1, wa2, ba, on, wout, state_in, gla_idx, n_gla, states_so_far):
    batch, seq, _ = x.shape
    n_seq = GROUP_ROWS // seq
    n_groups = batch // n_seq
    has_s0 = state_in is not None
    emit_state = not has_s0
    assert not has_s0 or n_seq == 1
    n_chunks = GROUP_ROWS // GLA_CHUNK
    in_specs = [
        pl.BlockSpec((1, GROUP_ROWS, D_MODEL), lambda b: (b, 0, 0)),
        pl.BlockSpec((1, 1, 1, 3 * D_MODEL), lambda b: (layer, mod_row(b), 0, 0)),
        _resident((1, D_MODEL)),
        _resident((D_MODEL, 2 * GLA_QD + 2 * GLA_VD), gla_idx),
        _resident((D_MODEL, RANK_PAD)),
        _resident((RANK_PAD, 2 * GLA_QD)),
        _resident((1, 2 * GLA_QD)),
        _resident((1, GLA_DV)),
        _resident((GLA_VD, D_MODEL), gla_idx),
    ]
    args = [x.reshape(n_groups, GROUP_ROWS, D_MODEL), mods, ng, win, wa1, wa2, ba, on, wout]
    if has_s0:
        in_specs.append(pl.BlockSpec((1, 1, 2, GLA_HEADS, GLA_DK, GLA_DV),
                                     lambda b: (b, gla_idx, 0, 0, 0, 0)))
        args.append(state_in)
    aliases = {}
    state_alias = emit_state and states_so_far is not None
    if state_alias:
        aliases[len(args)] = 1
        in_specs.append(pl.BlockSpec(memory_space=pl.ANY))
        args.append(states_so_far)
    out_specs = [pl.BlockSpec((1, GROUP_ROWS, D_MODEL), lambda b: (b, 0, 0))]
    out_shape = [jax.ShapeDtypeStruct((n_groups, GROUP_ROWS, D_MODEL), F32)]
    if emit_state:
        out_specs.append(pl.BlockSpec((n_seq, 1, 2, GLA_HEADS, GLA_DK, GLA_DV), lambda b: (b, gla_idx, 0, 0, 0, 0)))
        out_shape.append(jax.ShapeDtypeStruct((batch, n_gla, 2, GLA_HEADS, GLA_DK, GLA_DV), F32))
    scratch = [
        pltpu.VMEM((2, GROUP_ROWS, GLA_QD), BF16),
        pltpu.VMEM((2, GROUP_ROWS, GLA_QD), BF16),
        pltpu.VMEM((2, n_chunks, GLA_QD, GLA_CHUNK), BF16),
        pltpu.VMEM((GROUP_ROWS, GLA_VD), BF16),
        pltpu.VMEM((GROUP_ROWS, GLA_VD), F32),
        pltpu.VMEM((2, n_chunks, 1, GLA_QD), F32),
        pltpu.VMEM((GROUP_ROWS, GLA_VD), F32),
    ]
    if not emit_state:
        scratch.append(pltpu.VMEM((2, GLA_HEADS, GLA_DK, GLA_DV), F32))
    outs = pl.pallas_call(
        functools.partial(_gla_kernel, seq=seq, has_s0=has_s0, emit_state=emit_state, state_alias=state_alias),
        grid=(n_groups,),
        in_specs=in_specs,
        out_specs=out_specs,
        out_shape=out_shape,
        scratch_shapes=scratch,
        input_output_aliases=aliases,
        compiler_params=pltpu.CompilerParams(dimension_semantics=("arbitrary",),
                                             vmem_limit_bytes=V7X_VMEM_LIMIT_BYTES),
        name=f"gla_layer_seq{seq}",
    )(*args)
    y = outs[0].reshape(batch, seq, D_MODEL)
    return (y, outs[1]) if emit_state else (y, None)


def _rope_swap(x):
    lane = lax.broadcasted_iota(jnp.int32, x.shape, 1)
    quarter = HEAD_DIM // 4
    first = (lane % (2 * quarter)) < quarter
    return jnp.where(first, pltpu.roll(x, HEAD_DIM - quarter, 1), pltpu.roll(x, quarter, 1))


def _att_kernel(*refs, seq, latent, cache_alias):
    n_tiles = seq // ROW_TILE
    n_keys = seq + (refs[9].shape[2] if latent else 0)
    it = iter(refs)
    x_ref, mod_ref, ng_ref, win_ref, qn_ref, kn_ref, wout_ref = (next(it) for _ in range(7))
    if latent:
        cos_ref, sin_ref, ck_ref, cv_ref = (next(it) for _ in range(4))
    if cache_alias:
        next(it), next(it)
    y_ref = next(it)
    if not latent:
        kout_ref, vout_ref = next(it), next(it)
    q_s, k_s, v_s, gate_s, ao_s = (next(it) for _ in range(5))

    ng = ng_ref[...]
    qn = qn_ref[...]
    kn = kn_ref[...]
    exp2_scale = (HEAD_DIM ** -0.5) * math.log2(math.e)

    def project(t, carry):
        rows = pl.ds(pl.multiple_of(t * ROW_TILE, ROW_TILE), ROW_TILE)
        hb = _modulated_norm(x_ref[0, rows, :], ng, mod_ref)
        q = _head_rms(_dot(hb, win_ref[0, :, 0:ATT_QD]), qn, HEAD_DIM)
        k = _head_rms(_dot(hb, win_ref[0, :, ATT_QD:ATT_QD + ATT_KD]), kn, HEAD_DIM)
        v = _dot(hb, win_ref[0, :, ATT_QD + ATT_KD:ATT_QD + 2 * ATT_KD])
        gate_s[rows, :] = _dot(hb, win_ref[0, :, ATT_QD + 2 * ATT_KD:])
        if latent:
            cos = cos_ref[rows, :]
            sin = sin_ref[rows, :]
            q = jnp.concatenate(
                [q[:, h * HEAD_DIM:(h + 1) * HEAD_DIM] * cos + _rope_swap(q[:, h * HEAD_DIM:(h + 1) * HEAD_DIM]) * sin
                 for h in range(ATT_HEADS)], axis=-1)
            k = jnp.concatenate(
                [k[:, h * HEAD_DIM:(h + 1) * HEAD_DIM] * cos + _rope_swap(k[:, h * HEAD_DIM:(h + 1) * HEAD_DIM]) * sin
                 for h in range(ATT_KV_HEADS)], axis=-1)
        else:
            for h in range(ATT_KV_HEADS):
                kout_ref[0, 0, rows, h, :] = k[:, h * HEAD_DIM:(h + 1) * HEAD_DIM]
                vout_ref[0, 0, rows, h, :] = v[:, h * HEAD_DIM:(h + 1) * HEAD_DIM]
        q_s[rows, :] = (q * exp2_scale).astype(BF16)
        k_s[rows, :] = k.astype(BF16)
        for h in range(ATT_KV_HEADS):
            v_s[rows, 2 * h * HEAD_DIM:(2 * h + 1) * HEAD_DIM] = v[:, h * HEAD_DIM:(h + 1) * HEAD_DIM].astype(BF16)
        return carry

    for h in range(ATT_KV_HEADS):
        v_s[:, (2 * h + 1) * HEAD_DIM:(2 * h + 2) * HEAD_DIM] = jnp.ones((n_keys, HEAD_DIM), BF16)
    lax.fori_loop(0, n_tiles, project, 0)
    if latent:
        for h in range(ATT_KV_HEADS):
            k_s[seq:n_keys, h * HEAD_DIM:(h + 1) * HEAD_DIM] = ck_ref[0, 0, :, h, :].astype(BF16)
            v_s[seq:n_keys, 2 * h * HEAD_DIM:(2 * h + 1) * HEAD_DIM] = cv_ref[0, 0, :, h, :].astype(BF16)

    def attend(t, carry):
        rows = pl.ds(pl.multiple_of(t * ROW_TILE, ROW_TILE), ROW_TILE)

        def scores(h):
            kc = slice((h // ATT_GROUP) * HEAD_DIM, (h // ATT_GROUP + 1) * HEAD_DIM)
            return _dot_nt(q_s[rows, h * HEAD_DIM:(h + 1) * HEAD_DIM], k_s[:, kc])

        s = scores(0)
        for h in range(ATT_HEADS):
            s_next = scores(h + 1) if h + 1 < ATT_HEADS else None
            vc = slice((h // ATT_GROUP) * 2 * HEAD_DIM, (h // ATT_GROUP + 1) * 2 * HEAD_DIM)
            p = jnp.exp2(s - jnp.max(s, axis=-1, keepdims=True))
            o = _dot(p.astype(BF16), v_s[:, vc])
            ao_s[rows, h * HEAD_DIM:(h + 1) * HEAD_DIM] = o[:, 0:HEAD_DIM] / o[:, HEAD_DIM:]
            s = s_next
        return carry

    lax.fori_loop(0, n_tiles, attend, 0)

    res_gate = mod_ref[0, 0, :, 2 * D_MODEL:]

    def finish(t, carry):
        rows = pl.ds(pl.multiple_of(t * ROW_TILE, ROW_TILE), ROW_TILE)
        y = _dot((ao_s[rows, :] * _silu(gate_s[rows, :])).astype(BF16), wout_ref[0])
        y_ref[0, rows, :] = x_ref[0, rows, :] + res_gate * y
        return carry

    lax.fori_loop(0, n_tiles, finish, 0)


def _att_layer(x, mods, layer, mod_row, ng, win, qn, kn, wout, latent_inputs, att_idx, n_att, caches_so_far):
    batch, seq, _ = x.shape
    latent = latent_inputs is not None
    in_specs = [
        pl.BlockSpec((1, seq, D_MODEL), lambda b: (b, 0, 0)),
        pl.BlockSpec((1, 1, 1, 3 * D_MODEL), lambda b: (layer, mod_row(b), 0, 0)),
        _resident((1, D_MODEL)),
        _resident((D_MODEL, 2 * ATT_QD + 2 * ATT_KD), att_idx),
        _resident((1, HEAD_DIM)),
        _resident((1, HEAD_DIM)),
        _resident((ATT_QD, D_MODEL), att_idx),
    ]
    args = [x, mods, ng, win, qn, kn, wout]
    n_keys = seq
    if latent:
        cos, sin, cache_k, cache_v = latent_inputs
        past = cache_k.shape[2]
        n_keys = seq + past
        in_specs += [
            _resident((seq, HEAD_DIM)),
            _resident((seq, HEAD_DIM)),
            pl.BlockSpec((1, 1, past, ATT_KV_HEADS, HEAD_DIM), lambda b: (b, att_idx, 0, 0, 0)),
            pl.BlockSpec((1, 1, past, ATT_KV_HEADS, HEAD_DIM), lambda b: (b, att_idx, 0, 0, 0)),
        ]
        args += [cos, sin, cache_k, cache_v]
    aliases = {}
    cache_alias = (not latent) and caches_so_far is not None
    if cache_alias:
        aliases = {len(args): 1, len(args) + 1: 2}
        in_specs += [pl.BlockSpec(memory_space=pl.ANY)] * 2
        args += list(caches_so_far)
    out_specs = [pl.BlockSpec((1, seq, D_MODEL), lambda b: (b, 0, 0))]
    out_shape = [jax.ShapeDtypeStruct((batch, seq, D_MODEL), F32)]
    if not latent:
        out_specs += [pl.BlockSpec((1, 1, seq, ATT_KV_HEADS, HEAD_DIM), lambda b: (b, att_idx, 0, 0, 0))] * 2
        out_shape += [jax.ShapeDtypeStruct((batch, n_att, seq, ATT_KV_HEADS, HEAD_DIM), F32)] * 2
    scratch = [
        pltpu.VMEM((seq, ATT_QD), BF16),
        pltpu.VMEM((n_keys, ATT_KD), BF16),
        pltpu.VMEM((n_keys, 2 * ATT_KD), BF16),
        pltpu.VMEM((seq, ATT_QD), F32),
        pltpu.VMEM((seq, ATT_QD), F32),
    ]
    outs = pl.pallas_call(
        functools.partial(_att_kernel, seq=seq, latent=latent, cache_alias=cache_alias),
        grid=(batch,),
        in_specs=in_specs,
        out_specs=out_specs,
        out_shape=out_shape,
        scratch_shapes=scratch,
        input_output_aliases=aliases,
        compiler_params=pltpu.CompilerParams(dimension_semantics=("arbitrary",),
                                             vmem_limit_bytes=V7X_VMEM_LIMIT_BYTES),
        name=f"att_layer_seq{seq}",
    )(*args)
    return outs


def _rope_tables(seq):
    half = HEAD_DIM // 2
    nf = half // 2
    pos = np.arange(seq)
    freqs = jnp.asarray(ROPE_THETA, F32) ** (-jnp.arange(nf, dtype=F32) / nf)
    ang_row = jnp.asarray(pos // GRID_W, F32)[:, None] * freqs[None, :]
    ang_col = jnp.asarray(pos % GRID_W, F32)[:, None] * freqs[None, :]
    cos = jnp.concatenate([jnp.cos(ang_row)] * 2 + [jnp.cos(ang_col)] * 2, axis=-1)
    sin = jnp.concatenate([-jnp.sin(ang_row), jnp.sin(ang_row), -jnp.sin(ang_col), jnp.sin(ang_col)], axis=-1)
    return cos, sin


def kernel(x_prompt, x_sample, state_gla, cache_k, cache_v, c, c_ctx, norm_g, w_ada, b_ada,
           gla_w_in, gla_wa1, gla_wa2, gla_ba, gla_onorm, gla_w_out,
           att_w_in, att_qnorm, att_knorm, att_w_out):
    n_dec = x_sample.shape[0]
    assert 1 + n_dec <= MOD_ROWS
    assert x_prompt.shape[1] % ROW_TILE == 0 and x_sample.shape[1] % ROW_TILE == 0

    cvec = jnp.zeros((MOD_ROWS, D_MODEL), F32).at[0].set(c_ctx).at[1:1 + n_dec].set(c)
    mods = _modulations(cvec, w_ada, b_ada).reshape(DEPTH, MOD_ROWS, 1, 3 * D_MODEL)
    ctx_row = lambda b: 0
    dec_row = lambda b: b + 1

    cos, sin = _rope_tables(x_sample.shape[1])
    n_gla, n_att = gla_w_in.shape[0], att_w_in.shape[0]

    gla_win, gla_wout = gla_w_in.astype(BF16), gla_w_out.astype(BF16)
    att_win, att_wout = att_w_in.astype(BF16), att_w_out.astype(BF16)

    xp, xs = x_prompt, x_sample
    states, caches = None, None
    for l in range(DEPTH):
        i = l // 2
        ng = norm_g[l].reshape(1, D_MODEL)
        if l % 2 == 0:
            wa1 = jnp.zeros((D_MODEL, RANK_PAD), F32)
            wa1 = wa1.at[:, 0:GLA_RANK].set(gla_wa1[i, 0]).at[:, GLA_RANK:2 * GLA_RANK].set(gla_wa1[i, 1])
            wa2 = jnp.zeros((RANK_PAD, 2 * GLA_QD), F32)
            wa2 = wa2.at[0:GLA_RANK, 0:GLA_QD].set(gla_wa2[i, 0])
            wa2 = wa2.at[GLA_RANK:2 * GLA_RANK, GLA_QD:].set(gla_wa2[i, 1])
            ba = gla_ba[i].reshape(1, 2 * GLA_QD)
            on = gla_onorm[i].reshape(1, GLA_DV)
            common = (ng, gla_win, wa1.astype(BF16), wa2.astype(BF16), ba, on, gla_wout)
            xp, states = _gla_layer(xp, mods, l, ctx_row, *common, None, i, n_gla, states)
            xs, _ = _gla_layer(xs, mods, l, dec_row, *common, state_gla, i, n_gla, None)
        else:
            qn = att_qnorm[i].reshape(1, HEAD_DIM)
            kn = att_knorm[i].reshape(1, HEAD_DIM)
            xp, *caches = _att_layer(xp, mods, l, ctx_row, ng, att_win, qn, kn, att_wout, None, i, n_att, caches)
            (xs,) = _att_layer(xs, mods, l, dec_row, ng, att_win, qn, kn, att_wout, (cos, sin, cache_k, cache_v), i,
                               n_att, None)
    return (xp, xs, states, caches[0], caches[1])
```

```python
import functools
import math

import jax
import jax.numpy as jnp
import numpy as np
from jax import lax
from jax.experimental import pallas as pl
from jax.experimental.pallas import tpu as pltpu

D_MODEL = 1024
DEPTH = 4
GRID_W = 64
GLA_HEADS = 4
GLA_DK = 128
GLA_DV = 256
GLA_RANK = 16
GLA_TAU = 16.0
GLA_CHUNK = 64
GLA_QD = GLA_HEADS * GLA_DK
GLA_VD = GLA_HEADS * GLA_DV
HEAD_DIM = 128
ATT_HEADS = 8
ATT_KV_HEADS = 2
ATT_GROUP = ATT_HEADS // ATT_KV_HEADS
ATT_QD = ATT_HEADS * HEAD_DIM
ATT_KD = ATT_KV_HEADS * HEAD_DIM
ROPE_THETA = 10000.0
EPS = 1e-6

ROW_TILE = 256
GROUP_ROWS = 1024
MOD_ROWS = 8
RANK_PAD = 128
V7X_VMEM_LIMIT_BYTES = 60 * 1024 * 1024

F32 = jnp.float32
BF16 = jnp.bfloat16
_NT = (((1,), (1,)), ((), ()))
_TN = (((0,), (0,)), ((), ()))


def _dot(a, b):
    return jnp.dot(a, b, preferred_element_type=F32)


def _dot_nt(a, b):
    return lax.dot_general(a, b, _NT, preferred_element_type=F32)


def _dot_tn(a, b):
    return lax.dot_general(a, b, _TN, preferred_element_type=F32)


def _silu(x):
    return x * (1.0 / (1.0 + jnp.exp(-x)))


def _log_sigmoid(z):
    return jnp.minimum(z, 0.0) - jnp.log(1.0 + jnp.exp(-jnp.abs(z)))


def _split_top(x):
    top = pltpu.bitcast(pltpu.bitcast(x, jnp.uint32) & jnp.uint32(0xFFFF0000), F32)
    return top, x - top


def _modulated_norm(x, ng, mod_ref):
    shift = mod_ref[0, 0, :, 0:D_MODEL]
    scale = mod_ref[0, 0, :, D_MODEL:2 * D_MODEL]
    ms = jnp.mean(x * x, axis=-1, keepdims=True)
    return ((x * lax.rsqrt(ms + EPS)) * (ng * (1.0 + scale)) + shift).astype(BF16)


def _head_rms(x, g, width):
    outs = []
    for h in range(x.shape[-1] // width):
        xh = x[:, h * width:(h + 1) * width]
        ms = jnp.mean(xh * xh, axis=-1, keepdims=True)
        outs.append((xh * lax.rsqrt(ms + EPS)) * g)
    return jnp.concatenate(outs, axis=-1)


def _mod_kernel(c_ref, w_ref, b_ref, o_ref):
    acc = _dot(jnp.concatenate(_split_top(_silu(c_ref[...])), axis=0).astype(BF16), w_ref[0].astype(BF16))
    o_ref[0] = acc[0:MOD_ROWS] + acc[MOD_ROWS:] + b_ref[0]


def _modulations(cvec, w_ada, b_ada):
    tn = D_MODEL
    n_tiles = 3 * D_MODEL // tn
    return pl.pallas_call(
        _mod_kernel,
        grid=(DEPTH, n_tiles),
        in_specs=[
            pl.BlockSpec((MOD_ROWS, D_MODEL), lambda l, j: (0, 0)),
            pl.BlockSpec((1, D_MODEL, tn), lambda l, j: (l, 0, j)),
            pl.BlockSpec((1, 1, tn), lambda l, j: (l, 0, j)),
        ],
        out_specs=pl.BlockSpec((1, MOD_ROWS, tn), lambda l, j: (l, 0, j)),
        out_shape=jax.ShapeDtypeStruct((DEPTH, MOD_ROWS, 3 * D_MODEL), F32),
        compiler_params=pltpu.CompilerParams(dimension_semantics=("arbitrary", "arbitrary")),
        name="adaln_modulation",
    )(cvec, w_ada, b_ada.reshape(DEPTH, 1, 3 * D_MODEL))


def _chunk_sum_matrices():
    r = lax.broadcasted_iota(jnp.int32, (ROW_TILE, ROW_TILE), 0)
    c = lax.broadcasted_iota(jnp.int32, (ROW_TILE, ROW_TILE), 1)
    same = (r // GLA_CHUNK) == (c // GLA_CHUNK)
    prefix = jnp.where(same & (c <= r), 1.0, 0.0).astype(BF16)
    suffix = jnp.where(same & (c >= r), 1.0, 0.0).astype(BF16)
    return prefix, suffix


def _gla_kernel(*refs, seq, has_s0, emit_state, state_alias):
    n_seq = GROUP_ROWS // seq
    n_tiles = GROUP_ROWS // ROW_TILE
    n_pos = seq // GLA_CHUNK
    it = iter(refs)
    x_ref, mod_ref, ng_ref, win_ref, wa1_ref, wa2_ref, ba_ref, on_ref, wout_ref = (next(it) for _ in range(9))
    s0_ref = next(it) if has_s0 else None
    if state_alias:
        next(it)
    y_ref = next(it)
    sout_ref = next(it) if emit_state else None
    q_s, k_s, b_s, v_s, gate_s, o_s = (next(it) for _ in range(6))
    st_s = None if emit_state else next(it)

    def state_at(s, d, h):
        return sout_ref.at[s, 0, d, h] if emit_state else st_s.at[d, h]

    ng = ng_ref[...]
    prefix_m, suffix_m = _chunk_sum_matrices()

    def project_wide(t):
        rows = slice(t * ROW_TILE, (t + 1) * ROW_TILE)
        hb = _modulated_norm(x_ref[0, rows, :], ng, mod_ref)
        q_s[rows, :] = _dot(hb, win_ref[0, :, 0:GLA_QD]) * (GLA_DK ** -0.5)
        k_s[rows, :] = _dot(hb, win_ref[0, :, GLA_QD:2 * GLA_QD])
        v_s[rows, :] = _dot(hb, win_ref[0, :, 2 * GLA_QD:2 * GLA_QD + GLA_VD]).astype(BF16)
        gate_s[rows, :] = _dot(hb, win_ref[0, :, 2 * GLA_QD + GLA_VD:])
        low = _dot(hb, wa1_ref[...]).astype(BF16)
        return _dot(low, wa2_ref[...]) + ba_ref[...]

    def decay_sums(t, z):
        rows = slice(t * ROW_TILE, (t + 1) * ROW_TILE)
        logg2 = _log_sigmoid(z) * (math.log2(math.e) / GLA_TAU)
        for d, csum_m in enumerate((prefix_m, suffix_m)):
            parts = jnp.concatenate(_split_top(logg2[:, d * GLA_QD:(d + 1) * GLA_QD]), axis=0).astype(BF16)
            b_s[d, rows, :] = _dot(jnp.concatenate([csum_m, csum_m], axis=1), parts)

    z = project_wide(0)
    for t in range(n_tiles):
        z_next = project_wide(t + 1) if t + 1 < n_tiles else None
        decay_sums(t, z)
        z = z_next

    for s in range(n_seq):
        for d in range(2):
            for h in range(GLA_HEADS):
                state_at(s, d, h)[...] = s0_ref[0, 0, d, h] if has_s0 else jnp.zeros((GLA_DK, GLA_DV), F32)

    ri = lax.broadcasted_iota(jnp.int32, (GLA_CHUNK, GLA_CHUNK), 0)
    ci = lax.broadcasted_iota(jnp.int32, (GLA_CHUNK, GLA_CHUNK), 1)
    masks = (ci <= ri, ci >= ri)

    def scan_step(i, first_visit):
        chains = []
        for s in range(n_seq):
            for d in range(2):
                c = s * n_pos + (i if d == 0 else n_pos - 1 - i)
                rows = pl.ds(pl.multiple_of(c * GLA_CHUNK, GLA_CHUNK), GLA_CHUNK)
                b = b_s[d, rows, :]
                edge = GLA_CHUNK - 1 if d == 0 else 0
                total = b[edge:edge + 1, :]
                k = k_s[rows, :]
                qe = (q_s[rows, :] * jnp.exp2(b)).astype(BF16)
                ke = (k * jnp.exp2(-b)).astype(BF16)
                kdt = (k * jnp.exp2(total - b)).T.astype(BF16)
                dec = jnp.exp2(total)
                for h in range(GLA_HEADS):
                    kc = slice(h * GLA_DK, (h + 1) * GLA_DK)
                    chains.append((s, d, h, rows, slice(h * GLA_DV, (h + 1) * GLA_DV),
                                   qe[:, kc], ke[:, kc], kdt[kc, :], dec[:, kc]))
        scores = [_dot_nt(qe, ke) for (s, d, h, rows, vc, qe, ke, kdt, dec) in chains]
        updates = [_dot(kdt, v_s[rows, vc]) for (s, d, h, rows, vc, qe, ke, kdt, dec) in chains]
        for (s, d, h, rows, vc, qe, ke, kdt, dec), sc in zip(chains, scores):
            a = jnp.where(masks[d], sc, 0.0).astype(BF16)
            o = _dot(jnp.concatenate([qe, a], axis=1),
                     jnp.concatenate([state_at(s, d, h)[...].astype(BF16), v_s[rows, vc]], axis=0))
            if first_visit:
                o_s[rows, vc] = o
            else:
                o_s[rows, vc] += o
        for (s, d, h, rows, vc, qe, ke, kdt, dec), upd in zip(chains, updates):
            dec_col = jnp.broadcast_to(dec, (GLA_DK, GLA_DK)).T
            st_ref = state_at(s, d, h)
            st_ref[...] = jnp.concatenate([dec_col] * (GLA_DV // GLA_DK), axis=1) * st_ref[...] + upd

    def first_half(i, carry):
        scan_step(i, True)
        return carry

    def second_half(i, carry):
        scan_step(i, False)
        return carry

    lax.fori_loop(0, n_pos // 2, first_half, 0)
    lax.fori_loop(n_pos // 2, n_pos, second_half, 0)

    on = on_ref[...]
    res_gate = mod_ref[0, 0, :, 2 * D_MODEL:]

    def gated(t):
        rows = slice(t * ROW_TILE, (t + 1) * ROW_TILE)
        return (_head_rms(o_s[rows, :], on, GLA_DV) * _silu(gate_s[rows, :])).astype(BF16)

    og = gated(0)
    for t in range(n_tiles):
        rows = slice(t * ROW_TILE, (t + 1) * ROW_TILE)
        og_next = gated(t + 1) if t + 1 < n_tiles else None
        y_ref[0, rows, :] = x_ref[0, rows, :] + res_gate * _dot(og, wout_ref[0])
        og = og_next


def _resident(shape, layer=None):
    if layer is None:
        return pl.BlockSpec(shape, lambda b: (0,) * len(shape), pipeline_mode=pl.Buffered(1))
    return pl.BlockSpec((1,) + shape, lambda b: (layer,) + (0,) * len(shape), pipeline_mode=pl.Buffered(1))


def _gla_layer(x, mods, layer, mod_row, ng, win, wa1, wa2, ba, on, wout, state_in, gla_idx, n_gla, states_so_far):
    batch, seq, _ = x.shape
    n_seq = GROUP_ROWS // seq
    n_groups = batch // n_seq
    has_s0 = state_in is not None
    emit_state = not has_s0
    assert not has_s0 or n_seq == 1
    in_specs = [
        pl.BlockSpec((1, GROUP_ROWS, D_MODEL), lambda b: (b, 0, 0)),
        pl.BlockSpec((1, 1, 1, 3 * D_MODEL), lambda b: (layer, mod_row(b), 0, 0)),
        _resident((1, D_MODEL)),
        _resident((D_MODEL, 2 * GLA_QD + 2 * GLA_VD), gla_idx),
        _resident((D_MODEL, RANK_PAD)),
        _resident((RANK_PAD, 2 * GLA_QD)),
        _resident((1, 2 * GLA_QD)),
        _resident((1, GLA_DV)),
        _resident((GLA_VD, D_MODEL), gla_idx),
    ]
    args = [x.reshape(n_groups, GROUP_ROWS, D_MODEL), mods, ng, win, wa1, wa2, ba, on, wout]
    if has_s0:
        in_specs.append(pl.BlockSpec((1, 1, 2, GLA_HEADS, GLA_DK, GLA_DV),
                                     lambda b: (b, gla_idx, 0, 0, 0, 0)))
        args.append(state_in)
    aliases = {}
    state_alias = emit_state and states_so_far is not None
    if state_alias:
        aliases[len(args)] = 1
        in_specs.append(pl.BlockSpec(memory_space=pl.ANY))
        args.append(states_so_far)
    out_specs = [pl.BlockSpec((1, GROUP_ROWS, D_MODEL), lambda b: (b, 0, 0))]
    out_shape = [jax.ShapeDtypeStruct((n_groups, GROUP_ROWS, D_MODEL), F32)]
    if emit_state:
        out_specs.append(pl.BlockSpec((n_seq, 1, 2, GLA_HEADS, GLA_DK, GLA_DV), lambda b: (b, gla_idx, 0, 0, 0, 0)))
        out_shape.append(jax.ShapeDtypeStruct((batch, n_gla, 2, GLA_HEADS, GLA_DK, GLA_DV), F32))
    scratch = [
        pltpu.VMEM((GROUP_ROWS, GLA_QD), F32),
        pltpu.VMEM((GROUP_ROWS, GLA_QD), F32),
        pltpu.VMEM((2, GROUP_ROWS, GLA_QD), F32),
        pltpu.VMEM((GROUP_ROWS, GLA_VD), BF16),
        pltpu.VMEM((GROUP_ROWS, GLA_VD), F32),
        pltpu.VMEM((GROUP_ROWS, GLA_VD), F32),
    ]
    if not emit_state:
        scratch.append(pltpu.VMEM((2, GLA_HEADS, GLA_DK, GLA_DV), F32))
    outs = pl.pallas_call(
        functools.partial(_gla_kernel, seq=seq, has_s0=has_s0, emit_state=emit_state, state_alias=state_alias),
        grid=(n_groups,),
        in_specs=in_specs,
        out_specs=out_specs,
        out_shape=out_shape,
        scratch_shapes=scratch,
        input_output_aliases=aliases,
        compiler_params=pltpu.CompilerParams(dimension_semantics=("arbitrary",),
                                             vmem_limit_bytes=V7X_VMEM_LIMIT_BYTES),
        name=f"gla_layer_seq{seq}",
    )(*args)
    y = outs[0].reshape(batch, seq, D_MODEL)
    return (y, outs[1]) if emit_state else (y, None)


def _rope_swap(x):
    lane = lax.broadcasted_iota(jnp.int32, x.shape, 1)
    quarter = HEAD_DIM // 4
    first = (lane % (2 * quarter)) < quarter
    return jnp.where(first, pltpu.roll(x, HEAD_DIM - quarter, 1), pltpu.roll(x, quarter, 1))


def _att_kernel(*refs, seq, latent, cache_alias):
    n_tiles = seq // ROW_TILE
    n_keys = seq + (refs[9].shape[2] if latent else 0)
    it = iter(refs)
    x_ref, mod_ref, ng_ref, win_ref, qn_ref, kn_ref, wout_ref = (next(it) for _ in range(7))
    if latent:
        cos_ref, sin_ref, ck_ref, cv_ref = (next(it) for _ in range(4))
    if cache_alias:
        next(it), next(it)
    y_ref = next(it)
    if not latent:
        kout_ref, vout_ref = next(it), next(it)
    q_s, k_s, v_s, gate_s, ao_s = (next(it) for _ in range(5))

    ng = ng_ref[...]
    qn = qn_ref[...]
    kn = kn_ref[...]
    exp2_scale = (HEAD_DIM ** -0.5) * math.log2(math.e)

    def project(t, carry):
        rows = pl.ds(pl.multiple_of(t * ROW_TILE, ROW_TILE), ROW_TILE)
        hb = _modulated_norm(x_ref[0, rows, :], ng, mod_ref)
        q = _head_rms(_dot(hb, win_ref[0, :, 0:ATT_QD]), qn, HEAD_DIM)
        k = _head_rms(_dot(hb, win_ref[0, :, ATT_QD:ATT_QD + ATT_KD]), kn, HEAD_DIM)
        v = _dot(hb, win_ref[0, :, ATT_QD + ATT_KD:ATT_QD + 2 * ATT_KD])
        gate_s[rows, :] = _dot(hb, win_ref[0, :, ATT_QD + 2 * ATT_KD:])
        if latent:
            cos = cos_ref[rows, :]
            sin = sin_ref[rows, :]
            q = jnp.concatenate(
                [q[:, h * HEAD_DIM:(h + 1) * HEAD_DIM] * cos + _rope_swap(q[:, h * HEAD_DIM:(h + 1) * HEAD_DIM]) * sin
                 for h in range(ATT_HEADS)], axis=-1)
            k = jnp.concatenate(
                [k[:, h * HEAD_DIM:(h + 1) * HEAD_DIM] * cos + _rope_swap(k[:, h * HEAD_DIM:(h + 1) * HEAD_DIM]) * sin
                 for h in range(ATT_KV_HEADS)], axis=-1)
        else:
            for h in range(ATT_KV_HEADS):
                kout_ref[0, 0, rows, h, :] = k[:, h * HEAD_DIM:(h + 1) * HEAD_DIM]
                vout_ref[0, 0, rows, h, :] = v[:, h * HEAD_DIM:(h + 1) * HEAD_DIM]
        q_s[rows, :] = (q * exp2_scale).astype(BF16)
        k_s[rows, :] = k.astype(BF16)
        for h in range(ATT_KV_HEADS):
            v_s[rows, 2 * h * HEAD_DIM:(2 * h + 1) * HEAD_DIM] = v[:, h * HEAD_DIM:(h + 1) * HEAD_DIM].astype(BF16)
        return carry

    for h in range(ATT_KV_HEADS):
        v_s[:, (2 * h + 1) * HEAD_DIM:(2 * h + 2) * HEAD_DIM] = jnp.ones((n_keys, HEAD_DIM), BF16)
    lax.fori_loop(0, n_tiles, project, 0)
    if latent:
        for h in range(ATT_KV_HEADS):
            k_s[seq:n_keys, h * HEAD_DIM:(h + 1) * HEAD_DIM] = ck_ref[0, 0, :, h, :].astype(BF16)
            v_s[seq:n_keys, 2 * h * HEAD_DIM:(2 * h + 1) * HEAD_DIM] = cv_ref[0, 0, :, h, :].astype(BF16)

    def attend(t, carry):
        rows = pl.ds(pl.multiple_of(t * ROW_TILE, ROW_TILE), ROW_TILE)

        def scores(h):
            kc = slice((h // ATT_GROUP) * HEAD_DIM, (h // ATT_GROUP + 1) * HEAD_DIM)
            return _dot_nt(q_s[rows, h * HEAD_DIM:(h + 1) * HEAD_DIM], k_s[:, kc])

        s = scores(0)
        for h in range(ATT_HEADS):
            s_next = scores(h + 1) if h + 1 < ATT_HEADS else None
            vc = slice((h // ATT_GROUP) * 2 * HEAD_DIM, (h // ATT_GROUP + 1) * 2 * HEAD_DIM)
            p = jnp.exp2(s - jnp.max(s, axis=-1, keepdims=True))
            o = _dot(p.astype(BF16), v_s[:, vc])
            ao_s[rows, h * HEAD_DIM:(h + 1) * HEAD_DIM] = o[:, 0:HEAD_DIM] / o[:, HEAD_DIM:]
            s = s_next
        return carry

    lax.fori_loop(0, n_tiles, attend, 0)

    res_gate = mod_ref[0, 0, :, 2 * D_MODEL:]

    def finish(t, carry):
        rows = pl.ds(pl.multiple_of(t * ROW_TILE, ROW_TILE), ROW_TILE)
        y = _dot((ao_s[rows, :] * _silu(gate_s[rows, :])).astype(BF16), wout_ref[0])
        y_ref[0, rows, :] = x_ref[0, rows, :] + res_gate * y
        return carry

    lax.fori_loop(0, n_tiles, finish, 0)


def _att_layer(x, mods, layer, mod_row, ng, win, qn, kn, wout, latent_inputs, att_idx, n_att, caches_so_far):
    batch, seq, _ = x.shape
    latent = latent_inputs is not None
    in_specs = [
        pl.BlockSpec((1, seq, D_MODEL), lambda b: (b, 0, 0)),
        pl.BlockSpec((1, 1, 1, 3 * D_MODEL), lambda b: (layer, mod_row(b), 0, 0)),
        _resident((1, D_MODEL)),
        _resident((D_MODEL, 2 * ATT_QD + 2 * ATT_KD), att_idx),
        _resident((1, HEAD_DIM)),
        _resident((1, HEAD_DIM)),
        _resident((ATT_QD, D_MODEL), att_idx),
    ]
    args = [x, mods, ng, win, qn, kn, wout]
    n_keys = seq
    if latent:
        cos, sin, cache_k, cache_v = latent_inputs
        past = cache_k.shape[2]
        n_keys = seq + past
        in_specs += [
            _resident((seq, HEAD_DIM)),
            _resident((seq, HEAD_DIM)),
            pl.BlockSpec((1, 1, past, ATT_KV_HEADS, HEAD_DIM), lambda b: (b, att_idx, 0, 0, 0)),
            pl.BlockSpec((1, 1, past, ATT_KV_HEADS, HEAD_DIM), lambda b: (b, att_idx, 0, 0, 0)),
        ]
        args += [cos, sin, cache_k, cache_v]
    aliases = {}
    cache_alias = (not latent) and caches_so_far is not None
    if cache_alias:
        aliases = {len(args): 1, len(args) + 1: 2}
        in_specs += [pl.BlockSpec(memory_space=pl.ANY)] * 2
        args += list(caches_so_far)
    out_specs = [pl.BlockSpec((1, seq, D_MODEL), lambda b: (b, 0, 0))]
    out_shape = [jax.ShapeDtypeStruct((batch, seq, D_MODEL), F32)]
    if not latent:
        out_specs += [pl.BlockSpec((1, 1, seq, ATT_KV_HEADS, HEAD_DIM), lambda b: (b, att_idx, 0, 0, 0))] * 2
        out_shape += [jax.ShapeDtypeStruct((batch, n_att, seq, ATT_KV_HEADS, HEAD_DIM), F32)] * 2
    scratch = [
        pltpu.VMEM((seq, ATT_QD), BF16),
        pltpu.VMEM((n_keys, ATT_KD), BF16),
        pltpu.VMEM((n_keys, 2 * ATT_KD), BF16),
        pltpu.VMEM((seq, ATT_QD), F32),
        pltpu.VMEM((seq, ATT_QD), F32),
    ]
    outs = pl.pallas_call(
        functools.partial(_att_kernel, seq=seq, latent=latent, cache_alias=cache_alias),
        grid=(batch,),
        in_specs=in_specs,
        out_specs=out_specs,
        out_shape=out_shape,
        scratch_shapes=scratch,
        input_output_aliases=aliases,
        compiler_params=pltpu.CompilerParams(dimension_semantics=("arbitrary",),
                                             vmem_limit_bytes=V7X_VMEM_LIMIT_BYTES),
        name=f"att_layer_seq{seq}",
    )(*args)
    return outs


def _rope_tables(seq):
    half = HEAD_DIM // 2
    nf = half // 2
    pos = np.arange(seq)
    freqs = jnp.asarray(ROPE_THETA, F32) ** (-jnp.arange(nf, dtype=F32) / nf)
    ang_row = jnp.asarray(pos // GRID_W, F32)[:, None] * freqs[None, :]
    ang_col = jnp.asarray(pos % GRID_W, F32)[:, None] * freqs[None, :]
    cos = jnp.concatenate([jnp.cos(ang_row)] * 2 + [jnp.cos(ang_col)] * 2, axis=-1)
    sin = jnp.concatenate([-jnp.sin(ang_row), jnp.sin(ang_row), -jnp.sin(ang_col), jnp.sin(ang_col)], axis=-1)
    return cos, sin


def kernel(x_prompt, x_sample, state_gla, cache_k, cache_v, c, c_ctx, norm_g, w_ada, b_ada,
           gla_w_in, gla_wa1, gla_wa2, gla_ba, gla_onorm, gla_w_out,
           att_w_in, att_qnorm, att_knorm, att_w_out):
    n_dec = x_sample.shape[0]
    assert 1 + n_dec <= MOD_ROWS
    assert x_prompt.shape[1] % ROW_TILE == 0 and x_sample.shape[1] % ROW_TILE == 0

    cvec = jnp.zeros((MOD_ROWS, D_MODEL), F32).at[0].set(c_ctx).at[1:1 + n_dec].set(c)
    mods = _modulations(cvec, w_ada, b_ada).reshape(DEPTH, MOD_ROWS, 1, 3 * D_MODEL)
    ctx_row = lambda b: 0
    dec_row = lambda b: b + 1

    cos, sin = _rope_tables(x_sample.shape[1])
    n_gla, n_att = gla_w_in.shape[0], att_w_in.shape[0]

    gla_win, gla_wout = gla_w_in.astype(BF16), gla_w_out.astype(BF16)
    att_win, att_wout = att_w_in.astype(BF16), att_w_out.astype(BF16)

    xp, xs = x_prompt, x_sample
    states, caches = None, None
    for l in range(DEPTH):
        i = l // 2
        ng = norm_g[l].reshape(1, D_MODEL)
        if l % 2 == 0:
            wa1 = jnp.zeros((D_MODEL, RANK_PAD), F32)
            wa1 = wa1.at[:, 0:GLA_RANK].set(gla_wa1[i, 0]).at[:, GLA_RANK:2 * GLA_RANK].set(gla_wa1[i, 1])
            wa2 = jnp.zeros((RANK_PAD, 2 * GLA_QD), F32)
            wa2 = wa2.at[0:GLA_RANK, 0:GLA_QD].set(gla_wa2[i, 0])
            wa2 = wa2.at[GLA_RANK:2 * GLA_RANK, GLA_QD:].set(gla_wa2[i, 1])
            ba = gla_ba[i].reshape(1, 2 * GLA_QD)
            on = gla_onorm[i].reshape(1, GLA_DV)
            common = (ng, gla_win, wa1.astype(BF16), wa2.astype(BF16), ba, on, gla_wout)
            xp, states = _gla_layer(xp, mods, l, ctx_row, *common, None, i, n_gla, states)
            xs, _ = _gla_layer(xs, mods, l, dec_row, *common, state_gla, i, n_gla, None)
        else:
            qn = att_qnorm[i].reshape(1, HEAD_DIM)
            kn = att_knorm[i].reshape(1, HEAD_DIM)
            xp, *caches = _att_layer(xp, mods, l, ctx_row, ng, att_win, qn, kn, att_wout, None, i, n_att, caches)
            (xs,) = _att_layer(xs, mods, l, dec_row, ng, att_win, qn, kn, att_wout, (cos, sin, cache_k, cache_v), i,
                               n_att, None)
    return (xp, xs, states, caches[0], caches[1])
```

```python
import functools
import math

import jax
import jax.numpy as jnp
import numpy as np
from jax import lax
from jax.experimental import pallas as pl
from jax.experimental.pallas import tpu as pltpu

D_MODEL = 1024
DEPTH = 4
GRID_W = 64
GLA_HEADS = 4
GLA_DK = 128
GLA_DV = 256
GLA_RANK = 16
GLA_TAU = 16.0
GLA_CHUNK = 64
GLA_QD = GLA_HEADS * GLA_DK
GLA_VD = GLA_HEADS * GLA_DV
HEAD_DIM = 128
ATT_HEADS = 8
ATT_KV_HEADS = 2
ATT_GROUP = ATT_HEADS // ATT_KV_HEADS
ATT_QD = ATT_HEADS * HEAD_DIM
ATT_KD = ATT_KV_HEADS * HEAD_DIM
ROPE_THETA = 10000.0
EPS = 1e-6

ROW_TILE = 256
GROUP_ROWS = 1024
MOD_ROWS = 8
RANK_PAD = 128
V7X_VMEM_LIMIT_BYTES = 60 * 1024 * 1024

F32 = jnp.float32
BF16 = jnp.bfloat16
_NT = (((1,), (1,)), ((), ()))
_TN = (((0,), (0,)), ((), ()))


def _dot(a, b):
    return jnp.dot(a, b, preferred_element_type=F32)


def _dot_nt(a, b):
    return lax.dot_general(a, b, _NT, preferred_element_type=F32)


def _dot_tn(a, b):
    return lax.dot_general(a, b, _TN, preferred_element_type=F32)


def _silu(x):
    return x * (1.0 / (1.0 + jnp.exp(-x)))


def _log_sigmoid(z):
    return jnp.minimum(z, 0.0) - jnp.log(1.0 + jnp.exp(-jnp.abs(z)))


def _split_top(x):
    top = pltpu.bitcast(pltpu.bitcast(x, jnp.uint32) & jnp.uint32(0xFFFF0000), F32)
    return top, x - top


def _modulated_norm(x, ng, mod_ref):
    shift = mod_ref[0, 0, :, 0:D_MODEL]
    scale = mod_ref[0, 0, :, D_MODEL:2 * D_MODEL]
    ms = jnp.mean(x * x, axis=-1, keepdims=True)
    return ((x * lax.rsqrt(ms + EPS)) * (ng * (1.0 + scale)) + shift).astype(BF16)


def _head_rms(x, g, width):
    outs = []
    for h in range(x.shape[-1] // width):
        xh = x[:, h * width:(h + 1) * width]
        ms = jnp.mean(xh * xh, axis=-1, keepdims=True)
        outs.append((xh * lax.rsqrt(ms + EPS)) * g)
    return jnp.concatenate(outs, axis=-1)


def _mod_kernel(c_ref, w_ref, b_ref, o_ref):
    acc = _dot(jnp.concatenate(_split_top(_silu(c_ref[...])), axis=0).astype(BF16), w_ref[0].astype(BF16))
    o_ref[0] = acc[0:MOD_ROWS] + acc[MOD_ROWS:] + b_ref[0]


def _modulations(cvec, w_ada, b_ada):
    tn = D_MODEL
    n_tiles = 3 * D_MODEL // tn
    return pl.pallas_call(
        _mod_kernel,
        grid=(DEPTH, n_tiles),
        in_specs=[
            pl.BlockSpec((MOD_ROWS, D_MODEL), lambda l, j: (0, 0)),
            pl.BlockSpec((1, D_MODEL, tn), lambda l, j: (l, 0, j)),
            pl.BlockSpec((1, 1, tn), lambda l, j: (l, 0, j)),
        ],
        out_specs=pl.BlockSpec((1, MOD_ROWS, tn), lambda l, j: (l, 0, j)),
        out_shape=jax.ShapeDtypeStruct((DEPTH, MOD_ROWS, 3 * D_MODEL), F32),
        compiler_params=pltpu.CompilerParams(dimension_semantics=("arbitrary", "arbitrary")),
        name="adaln_modulation",
    )(cvec, w_ada, b_ada.reshape(DEPTH, 1, 3 * D_MODEL))


def _chunk_sum_matrices():
    r = lax.broadcasted_iota(jnp.int32, (ROW_TILE, ROW_TILE), 0)
    c = lax.broadcasted_iota(jnp.int32, (ROW_TILE, ROW_TILE), 1)
    same = (r // GLA_CHUNK) == (c // GLA_CHUNK)
    prefix = jnp.where(same & (c <= r), 1.0, 0.0).astype(BF16)
    suffix = jnp.where(same & (c >= r), 1.0, 0.0).astype(BF16)
    return prefix, suffix


def _gla_kernel(*refs, seq, has_s0, emit_state, state_alias):
    n_seq = GROUP_ROWS // seq
    n_tiles = GROUP_ROWS // ROW_TILE
    n_pos = seq // GLA_CHUNK
    it = iter(refs)
    x_ref, mod_ref, ng_ref, win_ref, wa1_ref, wa2_ref, ba_ref, on_ref, wout_ref = (next(it) for _ in range(9))
    s0_ref = next(it) if has_s0 else None
    if state_alias:
        next(it)
    y_ref = next(it)
    sout_ref = next(it) if emit_state else None
    q_s, k_s, b_s, v_s, gate_s, o_s = (next(it) for _ in range(6))
    st_s = None if emit_state else next(it)

    def state_at(s, d, h):
        return sout_ref.at[s, 0, d, h] if emit_state else st_s.at[d, h]

    ng = ng_ref[...]
    prefix_m, suffix_m = _chunk_sum_matrices()

    def project_wide(t):
        rows = slice(t * ROW_TILE, (t + 1) * ROW_TILE)
        hb = _modulated_norm(x_ref[0, rows, :], ng, mod_ref)
        q_s[rows, :] = _dot(hb, win_ref[0, :, 0:GLA_QD]) * (GLA_DK ** -0.5)
        k_s[rows, :] = _dot(hb, win_ref[0, :, GLA_QD:2 * GLA_QD])
        v_s[rows, :] = _dot(hb, win_ref[0, :, 2 * GLA_QD:2 * GLA_QD + GLA_VD]).astype(BF16)
        gate_s[rows, :] = _dot(hb, win_ref[0, :, 2 * GLA_QD + GLA_VD:])
        low = _dot(hb, wa1_ref[...]).astype(BF16)
        return _dot(low, wa2_ref[...]) + ba_ref[...]

    def decay_sums(t, z):
        rows = slice(t * ROW_TILE, (t + 1) * ROW_TILE)
        logg2 = _log_sigmoid(z) * (math.log2(math.e) / GLA_TAU)
        for d, csum_m in enumerate((prefix_m, suffix_m)):
            parts = jnp.concatenate(_split_top(logg2[:, d * GLA_QD:(d + 1) * GLA_QD]), axis=0).astype(BF16)
            b_s[d, rows, :] = _dot(jnp.concatenate([csum_m, csum_m], axis=1), parts)

    z = project_wide(0)
    for t in range(n_tiles):
        z_next = project_wide(t + 1) if t + 1 < n_tiles else None
        decay_sums(t, z)
        z = z_next

    for s in range(n_seq):
        for d in range(2):
            for h in range(GLA_HEADS):
                state_at(s, d, h)[...] = s0_ref[0, 0, d, h] if has_s0 else jnp.zeros((GLA_DK, GLA_DV), F32)

    ri = lax.broadcasted_iota(jnp.int32, (GLA_CHUNK, GLA_CHUNK), 0)
    ci = lax.broadcasted_iota(jnp.int32, (GLA_CHUNK, GLA_CHUNK), 1)
    masks = (ci <= ri, ci >= ri)

    def scan_step(i, first_visit):
        chains = []
        for s in range(n_seq):
            for d in range(2):
                c = s * n_pos + (i if d == 0 else n_pos - 1 - i)
                rows = pl.ds(pl.multiple_of(c * GLA_CHUNK, GLA_CHUNK), GLA_CHUNK)
                b = b_s[d, rows, :]
                edge = GLA_CHUNK - 1 if d == 0 else 0
                total = b[edge:edge + 1, :]
                k = k_s[rows, :]
                qe = (q_s[rows, :] * jnp.exp2(b)).astype(BF16)
                ke = (k * jnp.exp2(-b)).astype(BF16)
                kdt = (k * jnp.exp2(total - b)).T.astype(BF16)
                dec = jnp.exp2(total)
                for h in range(GLA_HEADS):
                    kc = slice(h * GLA_DK, (h + 1) * GLA_DK)
                    chains.append((s, d, h, rows, slice(h * GLA_DV, (h + 1) * GLA_DV),
                                   qe[:, kc], ke[:, kc], kdt[kc, :], dec[:, kc]))
        scores = [_dot_nt(qe, ke) for (s, d, h, rows, vc, qe, ke, kdt, dec) in chains]
        updates = [_dot(kdt, v_s[rows, vc]) for (s, d, h, rows, vc, qe, ke, kdt, dec) in chains]
        for (s, d, h, rows, vc, qe, ke, kdt, dec), sc in zip(chains, scores):
            a = jnp.where(masks[d], sc, 0.0).astype(BF16)
            o = _dot(jnp.concatenate([qe, a], axis=1),
                     jnp.concatenate([state_at(s, d, h)[...].astype(BF16), v_s[rows, vc]], axis=0))
            if first_visit:
                o_s[rows, vc] = o
            else:
                o_s[rows, vc] += o
        for (s, d, h, rows, vc, qe, ke, kdt, dec), upd in zip(chains, updates):
            dec_col = jnp.broadcast_to(dec, (GLA_DK, GLA_DK)).T
            st_ref = state_at(s, d, h)
            st_ref[...] = jnp.concatenate([dec_col] * (GLA_DV // GLA_DK), axis=1) * st_ref[...] + upd

    def first_half(i, carry):
        scan_step(i, True)
        return carry

    def second_half(i, carry):
        scan_step(i, False)
        return carry

    lax.fori_loop(0, n_pos // 2, first_half, 0)
    lax.fori_loop(n_pos // 2, n_pos, second_half, 0)

    on = on_ref[...]
    res_gate = mod_ref[0, 0, :, 2 * D_MODEL:]

    def gated(t):
        rows = slice(t * ROW_TILE, (t + 1) * ROW_TILE)
        return (_head_rms(o_s[rows, :], on, GLA_DV) * _silu(gate_s[rows, :])).astype(BF16)

    og = gated(0)
    for t in range(n_tiles):
        rows = slice(t * ROW_TILE, (t + 1) * ROW_TILE)
        og_next = gated(t + 1) if t + 1 < n_tiles else None
        y_ref[0, rows, :] = x_ref[0, rows, :] + res_gate * _dot(og, wout_ref[0])
        og = og_next


def _resident(shape, layer=None):
    if layer is None:
        return pl.BlockSpec(shape, lambda b: (0,) * len(shape), pipeline_mode=pl.Buffered(1))
    return pl.BlockSpec((1,) + shape, lambda b: (layer,) + (0,) * len(shape), pipeline_mode=pl.Buffered(1))


def _gla_layer(x, mods, layer, mod_row, ng, win, wa1, wa2, ba, on, wout, state_in, gla_idx, n_gla, states_so_far):
    batch, seq, _ = x.shape
    n_seq = GROUP_ROWS // seq
    n_groups = batch // n_seq
    has_s0 = state_in is not None
    emit_state = not has_s0
    assert not has_s0 or n_seq == 1
    in_specs = [
        pl.BlockSpec((1, GROUP_ROWS, D_MODEL), lambda b: (b, 0, 0)),
        pl.BlockSpec((1, 1, 1, 3 * D_MODEL), lambda b: (layer, mod_row(b), 0, 0)),
        _resident((1, D_MODEL)),
        _resident((D_MODEL, 2 * GLA_QD + 2 * GLA_VD), gla_idx),
        _resident((D_MODEL, RANK_PAD)),
        _resident((RANK_PAD, 2 * GLA_QD)),
        _resident((1, 2 * GLA_QD)),
        _resident((1, GLA_DV)),
        _resident((GLA_VD, D_MODEL), gla_idx),
    ]
    args = [x.reshape(n_groups, GROUP_ROWS, D_MODEL), mods, ng, win, wa1, wa2, ba, on, wout]
    if has_s0:
        in_specs.append(pl.BlockSpec((1, 1, 2, GLA_HEADS, GLA_DK, GLA_DV),
                                     lambda b: (b, gla_idx, 0, 0, 0, 0)))
        args.append(state_in)
    aliases = {}
    state_alias = emit_state and states_so_far is not None
    if state_alias:
        aliases[len(args)] = 1
        in_specs.append(pl.BlockSpec(memory_space=pl.ANY))
        args.append(states_so_far)
    out_specs = [pl.BlockSpec((1, GROUP_ROWS, D_MODEL), lambda b: (b, 0, 0))]
    out_shape = [jax.ShapeDtypeStruct((n_groups, GROUP_ROWS, D_MODEL), F32)]
    if emit_state:
        out_specs.append(pl.BlockSpec((n_seq, 1, 2, GLA_HEADS, GLA_DK, GLA_DV), lambda b: (b, gla_idx, 0, 0, 0, 0)))
        out_shape.append(jax.ShapeDtypeStruct((batch, n_gla, 2, GLA_HEADS, GLA_DK, GLA_DV), F32))
    scratch = [
        pltpu.VMEM((GROUP_ROWS, GLA_QD), F32),
        pltpu.VMEM((GROUP_ROWS, GLA_QD), F32),
        pltpu.VMEM((2, GROUP_ROWS, GLA_QD), F32),
        pltpu.VMEM((GROUP_ROWS, GLA_VD), BF16),
        pltpu.VMEM((GROUP_ROWS, GLA_VD), F32),
        pltpu.VMEM((GROUP_ROWS, GLA_VD), F32),
    ]
    if not emit_state:
        scratch.append(pltpu.VMEM((2, GLA_HEADS, GLA_DK, GLA_DV), F32))
    outs = pl.pallas_call(
        functools.partial(_gla_kernel, seq=seq, has_s0=has_s0, emit_state=emit_state, state_alias=state_alias),
        grid=(n_groups,),
        in_specs=in_specs,
        out_specs=out_specs,
        out_shape=out_shape,
        scratch_shapes=scratch,
        input_output_aliases=aliases,
        compiler_params=pltpu.CompilerParams(dimension_semantics=("arbitrary",),
                                             vmem_limit_bytes=V7X_VMEM_LIMIT_BYTES),
        name=f"gla_layer_seq{seq}",
    )(*args)
    y = outs[0].reshape(batch, seq, D_MODEL)
    return (y, outs[1]) if emit_state else (y, None)


def _rope_swap(x):
    lane = lax.broadcasted_iota(jnp.int32, x.shape, 1)
    quarter = HEAD_DIM // 4
    first = (lane % (2 * quarter)) < quarter
    return jnp.where(first, pltpu.roll(x, HEAD_DIM - quarter, 1), pltpu.roll(x, quarter, 1))


def _att_kernel(*refs, seq, latent, cache_alias):
    group_rows = refs[0].shape[1]
    n_tiles = group_rows // ROW_TILE
    n_keys = seq + (refs[9].shape[2] if latent else 0)
    it = iter(refs)
    x_ref, mod_ref, ng_ref, win_ref, qn_ref, kn_ref, wout_ref = (next(it) for _ in range(7))
    if latent:
        cos_ref, sin_ref, ck_ref, cv_ref = (next(it) for _ in range(4))
    if cache_alias:
        next(it), next(it)
    y_ref = next(it)
    if not latent:
        kout_ref, vout_ref = next(it), next(it)
    q_s, k_s, v_s, gate_s, ao_s = (next(it) for _ in range(5))

    ng = ng_ref[...]
    qn = qn_ref[...]
    kn = kn_ref[...]
    res_gate = mod_ref[0, 0, :, 2 * D_MODEL:]
    exp2_scale = (HEAD_DIM ** -0.5) * math.log2(math.e)

    def project(rows, seq_in_group):
        hb = _modulated_norm(x_ref[0, rows, :], ng, mod_ref)
        q = _head_rms(_dot(hb, win_ref[0, :, 0:ATT_QD]), qn, HEAD_DIM)
        k = _head_rms(_dot(hb, win_ref[0, :, ATT_QD:ATT_QD + ATT_KD]), kn, HEAD_DIM)
        v = _dot(hb, win_ref[0, :, ATT_QD + ATT_KD:ATT_QD + 2 * ATT_KD])
        gate_s[rows, :] = _dot(hb, win_ref[0, :, ATT_QD + 2 * ATT_KD:])
        if latent:
            cos = cos_ref[rows, :]
            sin = sin_ref[rows, :]
            q = jnp.concatenate(
                [q[:, h * HEAD_DIM:(h + 1) * HEAD_DIM] * cos + _rope_swap(q[:, h * HEAD_DIM:(h + 1) * HEAD_DIM]) * sin
                 for h in range(ATT_HEADS)], axis=-1)
            k = jnp.concatenate(
                [k[:, h * HEAD_DIM:(h + 1) * HEAD_DIM] * cos + _rope_swap(k[:, h * HEAD_DIM:(h + 1) * HEAD_DIM]) * sin
                 for h in range(ATT_KV_HEADS)], axis=-1)
        else:
            for h in range(ATT_KV_HEADS):
                kout_ref[seq_in_group, 0, :, h, :] = k[:, h * HEAD_DIM:(h + 1) * HEAD_DIM]
                vout_ref[seq_in_group, 0, :, h, :] = v[:, h * HEAD_DIM:(h + 1) * HEAD_DIM]
        q_s[rows, :] = (q * exp2_scale).astype(BF16)
        k_s[rows, :] = k.astype(BF16)
        for h in range(ATT_KV_HEADS):
            v_s[rows, 2 * h * HEAD_DIM:(2 * h + 1) * HEAD_DIM] = v[:, h * HEAD_DIM:(h + 1) * HEAD_DIM].astype(BF16)

    def attend(rows, keys):
        def scores(h):
            kc = slice((h // ATT_GROUP) * HEAD_DIM, (h // ATT_GROUP + 1) * HEAD_DIM)
            return _dot_nt(q_s[rows, h * HEAD_DIM:(h + 1) * HEAD_DIM], k_s[keys, kc])

        s = scores(0)
        for h in range(ATT_HEADS):
            s_next = scores(h + 1) if h + 1 < ATT_HEADS else None
            vc = slice((h // ATT_GROUP) * 2 * HEAD_DIM, (h // ATT_GROUP + 1) * 2 * HEAD_DIM)
            p = jnp.exp2(s - jnp.max(s, axis=-1, keepdims=True))
            o = _dot(p.astype(BF16), v_s[keys, vc])
            ao_s[rows, h * HEAD_DIM:(h + 1) * HEAD_DIM] = o[:, 0:HEAD_DIM] / o[:, HEAD_DIM:]
            s = s_next

    def finish(rows):
        y = _dot((ao_s[rows, :] * _silu(gate_s[rows, :])).astype(BF16), wout_ref[0])
        y_ref[0, rows, :] = x_ref[0, rows, :] + res_gate * y

    for h in range(ATT_KV_HEADS):
        v_s[:, (2 * h + 1) * HEAD_DIM:(2 * h + 2) * HEAD_DIM] = jnp.ones((v_s.shape[0], HEAD_DIM), BF16)

    if latent:
        def over_tiles(stage):
            def body(t, carry):
                stage(pl.ds(pl.multiple_of(t * ROW_TILE, ROW_TILE), ROW_TILE))
                return carry
            lax.fori_loop(0, n_tiles, body, 0)

        over_tiles(lambda rows: project(rows, 0))
        for h in range(ATT_KV_HEADS):
            k_s[seq:n_keys, h * HEAD_DIM:(h + 1) * HEAD_DIM] = ck_ref[0, 0, :, h, :].astype(BF16)
            v_s[seq:n_keys, 2 * h * HEAD_DIM:(2 * h + 1) * HEAD_DIM] = cv_ref[0, 0, :, h, :].astype(BF16)
        over_tiles(lambda rows: attend(rows, slice(0, n_keys)))
        over_tiles(finish)
    else:
        tiles = [slice(t * ROW_TILE, (t + 1) * ROW_TILE) for t in range(n_tiles)]
        for u in range(n_tiles + 2):
            if u < n_tiles:
                project(tiles[u], u)
            if 0 <= u - 1 < n_tiles:
                attend(tiles[u - 1], tiles[u - 1])
            if 0 <= u - 2 < n_tiles:
                finish(tiles[u - 2])


def _att_layer(x, mods, layer, mod_row, ng, win, qn, kn, wout, latent_inputs, att_idx, n_att, caches_so_far):
    batch, seq, _ = x.shape
    latent = latent_inputs is not None
    assert latent or seq == ROW_TILE
    group_rows = seq if latent else GROUP_ROWS
    n_seq = group_rows // seq
    n_groups = batch // n_seq
    in_specs = [
        pl.BlockSpec((1, group_rows, D_MODEL), lambda b: (b, 0, 0)),
        pl.BlockSpec((1, 1, 1, 3 * D_MODEL), lambda b: (layer, mod_row(b), 0, 0)),
        _resident((1, D_MODEL)),
        _resident((D_MODEL, 2 * ATT_QD + 2 * ATT_KD), att_idx),
        _resident((1, HEAD_DIM)),
        _resident((1, HEAD_DIM)),
        _resident((ATT_QD, D_MODEL), att_idx),
    ]
    args = [x.reshape(n_groups, group_rows, D_MODEL), mods, ng, win, qn, kn, wout]
    n_key_rows = group_rows
    if latent:
        cos, sin, cache_k, cache_v = latent_inputs
        past = cache_k.shape[2]
        n_key_rows = seq + past
        in_specs += [
            _resident((seq, HEAD_DIM)),
            _resident((seq, HEAD_DIM)),
            pl.BlockSpec((1, 1, past, ATT_KV_HEADS, HEAD_DIM), lambda b: (b, att_idx, 0, 0, 0)),
            pl.BlockSpec((1, 1, past, ATT_KV_HEADS, HEAD_DIM), lambda b: (b, att_idx, 0, 0, 0)),
        ]
        args += [cos, sin, cache_k, cache_v]
    aliases = {}
    cache_alias = (not latent) and caches_so_far is not None
    if cache_alias:
        aliases = {len(args): 1, len(args) + 1: 2}
        in_specs += [pl.BlockSpec(memory_space=pl.ANY)] * 2
        args += list(caches_so_far)
    out_specs = [pl.BlockSpec((1, group_rows, D_MODEL), lambda b: (b, 0, 0))]
    out_shape = [jax.ShapeDtypeStruct((n_groups, group_rows, D_MODEL), F32)]
    if not latent:
        out_specs += [pl.BlockSpec((n_seq, 1, seq, ATT_KV_HEADS, HEAD_DIM), lambda b: (b, att_idx, 0, 0, 0))] * 2
        out_shape += [jax.ShapeDtypeStruct((batch, n_att, seq, ATT_KV_HEADS, HEAD_DIM), F32)] * 2
    scratch = [
        pltpu.VMEM((group_rows, ATT_QD), BF16),
        pltpu.VMEM((n_key_rows, ATT_KD), BF16),
        pltpu.VMEM((n_key_rows, 2 * ATT_KD), BF16),
        pltpu.VMEM((group_rows, ATT_QD), F32),
        pltpu.VMEM((group_rows, ATT_QD), F32),
    ]
    outs = pl.pallas_call(
        functools.partial(_att_kernel, seq=seq, latent=latent, cache_alias=cache_alias),
        grid=(n_groups,),
        in_specs=in_specs,
        out_specs=out_specs,
        out_shape=out_shape,
        scratch_shapes=scratch,
        input_output_aliases=aliases,
        compiler_params=pltpu.CompilerParams(dimension_semantics=("arbitrary",),
                                             vmem_limit_bytes=V7X_VMEM_LIMIT_BYTES),
        name=f"att_layer_seq{seq}",
    )(*args)
    return [outs[0].reshape(batch, seq, D_MODEL)] + list(outs[1:])


def _rope_tables(seq):
    half = HEAD_DIM // 2
    nf = half // 2
    pos = np.arange(seq)
    freqs = jnp.asarray(ROPE_THETA, F32) ** (-jnp.arange(nf, dtype=F32) / nf)
    ang_row = jnp.asarray(pos // GRID_W, F32)[:, None] * freqs[None, :]
    ang_col = jnp.asarray(pos % GRID_W, F32)[:, None] * freqs[None, :]
    cos = jnp.concatenate([jnp.cos(ang_row)] * 2 + [jnp.cos(ang_col)] * 2, axis=-1)
    sin = jnp.concatenate([-jnp.sin(ang_row), jnp.sin(ang_row), -jnp.sin(ang_col), jnp.sin(ang_col)], axis=-1)
    return cos, sin


def kernel(x_prompt, x_sample, state_gla, cache_k, cache_v, c, c_ctx, norm_g, w_ada, b_ada,
           gla_w_in, gla_wa1, gla_wa2, gla_ba, gla_onorm, gla_w_out,
           att_w_in, att_qnorm, att_knorm, att_w_out):
    n_dec = x_sample.shape[0]
    assert 1 + n_dec <= MOD_ROWS
    assert x_prompt.shape[1] % ROW_TILE == 0 and x_sample.shape[1] % ROW_TILE == 0

    cvec = jnp.zeros((MOD_ROWS, D_MODEL), F32).at[0].set(c_ctx).at[1:1 + n_dec].set(c)
    mods = _modulations(cvec, w_ada, b_ada).reshape(DEPTH, MOD_ROWS, 1, 3 * D_MODEL)
    ctx_row = lambda b: 0
    dec_row = lambda b: b + 1

    cos, sin = _rope_tables(x_sample.shape[1])
    n_gla, n_att = gla_w_in.shape[0], att_w_in.shape[0]

    gla_win, gla_wout = gla_w_in.astype(BF16), gla_w_out.astype(BF16)
    att_win, att_wout = att_w_in.astype(BF16), att_w_out.astype(BF16)

    xp, xs = x_prompt, x_sample
    states, caches = None, None
    for l in range(DEPTH):
        i = l // 2
        ng = norm_g[l].reshape(1, D_MODEL)
        if l % 2 == 0:
            wa1 = jnp.zeros((D_MODEL, RANK_PAD), F32)
            wa1 = wa1.at[:, 0:GLA_RANK].set(gla_wa1[i, 0]).at[:, GLA_RANK:2 * GLA_RANK].set(gla_wa1[i, 1])
            wa2 = jnp.zeros((RANK_PAD, 2 * GLA_QD), F32)
            wa2 = wa2.at[0:GLA_RANK, 0:GLA_QD].set(gla_wa2[i, 0])
            wa2 = wa2.at[GLA_RANK:2 * GLA_RANK, GLA_QD:].set(gla_wa2[i, 1])
            ba = gla_ba[i].reshape(1, 2 * GLA_QD)
            on = gla_onorm[i].reshape(1, GLA_DV)
            common = (ng, gla_win, wa1.astype(BF16), wa2.astype(BF16), ba, on, gla_wout)
            xp, states = _gla_layer(xp, mods, l, ctx_row, *common, None, i, n_gla, states)
            xs, _ = _gla_layer(xs, mods, l, dec_row, *common, state_gla, i, n_gla, None)
        else:
            qn = att_qnorm[i].reshape(1, HEAD_DIM)
            kn = att_knorm[i].reshape(1, HEAD_DIM)
            xp, *caches = _att_layer(xp, mods, l, ctx_row, ng, att_win, qn, kn, att_wout, None, i, n_att, caches)
            (xs,) = _att_layer(xs, mods, l, dec_row, ng, att_win, qn, kn, att_wout, (cos, sin, cache_k, cache_v), i,
                               n_att, None)
    return (xp, xs, states, caches[0], caches[1])
```

```python
import functools
import math

import jax
import jax.numpy as jnp
import numpy as np
from jax import lax
from jax.experimental import pallas as pl
from jax.experimental.pallas import tpu as pltpu

D_MODEL = 1024
DEPTH = 4
GRID_W = 64
GLA_HEADS = 4
GLA_DK = 128
GLA_DV = 256
GLA_RANK = 16
GLA_TAU = 16.0
GLA_CHUNK = 64
GLA_QD = GLA_HEADS * GLA_DK
GLA_VD = GLA_HEADS * GLA_DV
HEAD_DIM = 128
ATT_HEADS = 8
ATT_KV_HEADS = 2
ATT_GROUP = ATT_HEADS // ATT_KV_HEADS
ATT_QD = ATT_HEADS * HEAD_DIM
ATT_KD = ATT_KV_HEADS * HEAD_DIM
ROPE_THETA = 10000.0
EPS = 1e-6

ROW_TILE = 256
GROUP_ROWS = 1024
MOD_ROWS = 8
RANK_PAD = 128
V7X_VMEM_LIMIT_BYTES = 60 * 1024 * 1024

F32 = jnp.float32
BF16 = jnp.bfloat16
_NT = (((1,), (1,)), ((), ()))
_TN = (((0,), (0,)), ((), ()))


def _dot(a, b):
    return jnp.dot(a, b, preferred_element_type=F32)


def _dot_nt(a, b):
    return lax.dot_general(a, b, _NT, preferred_element_type=F32)


def _dot_tn(a, b):
    return lax.dot_general(a, b, _TN, preferred_element_type=F32)


def _silu(x):
    return x * (1.0 / (1.0 + jnp.exp(-x)))


def _log_sigmoid(z):
    return jnp.minimum(z, 0.0) - jnp.log(1.0 + jnp.exp(-jnp.abs(z)))


def _split_top(x):
    top = pltpu.bitcast(pltpu.bitcast(x, jnp.uint32) & jnp.uint32(0xFFFF0000), F32)
    return top, x - top


def _modulated_norm(x, ng, mod_ref):
    shift = mod_ref[0, 0, :, 0:D_MODEL]
    scale = mod_ref[0, 0, :, D_MODEL:2 * D_MODEL]
    ms = jnp.mean(x * x, axis=-1, keepdims=True)
    return ((x * lax.rsqrt(ms + EPS)) * (ng * (1.0 + scale)) + shift).astype(BF16)


def _head_rms(x, g, width):
    outs = []
    for h in range(x.shape[-1] // width):
        xh = x[:, h * width:(h + 1) * width]
        ms = jnp.mean(xh * xh, axis=-1, keepdims=True)
        outs.append((xh * lax.rsqrt(ms + EPS)) * g)
    return jnp.concatenate(outs, axis=-1)


def _mod_kernel(c_ref, w_ref, b_ref, o_ref):
    acc = _dot(jnp.concatenate(_split_top(_silu(c_ref[...])), axis=0).astype(BF16), w_ref[0].astype(BF16))
    o_ref[0] = acc[0:MOD_ROWS] + acc[MOD_ROWS:] + b_ref[0]


def _modulations(cvec, w_ada, b_ada):
    tn = D_MODEL
    n_tiles = 3 * D_MODEL // tn
    return pl.pallas_call(
        _mod_kernel,
        grid=(DEPTH, n_tiles),
        in_specs=[
            pl.BlockSpec((MOD_ROWS, D_MODEL), lambda l, j: (0, 0)),
            pl.BlockSpec((1, D_MODEL, tn), lambda l, j: (l, 0, j)),
            pl.BlockSpec((1, 1, tn), lambda l, j: (l, 0, j)),
        ],
        out_specs=pl.BlockSpec((1, MOD_ROWS, tn), lambda l, j: (l, 0, j)),
        out_shape=jax.ShapeDtypeStruct((DEPTH, MOD_ROWS, 3 * D_MODEL), F32),
        compiler_params=pltpu.CompilerParams(dimension_semantics=("arbitrary", "arbitrary")),
        name="adaln_modulation",
    )(cvec, w_ada, b_ada.reshape(DEPTH, 1, 3 * D_MODEL))


def _chunk_sum_matrices():
    r = lax.broadcasted_iota(jnp.int32, (ROW_TILE, ROW_TILE), 0)
    c = lax.broadcasted_iota(jnp.int32, (ROW_TILE, ROW_TILE), 1)
    same = (r // GLA_CHUNK) == (c // GLA_CHUNK)
    prefix = jnp.where(same & (c <= r), 1.0, 0.0).astype(BF16)
    suffix = jnp.where(same & (c >= r), 1.0, 0.0).astype(BF16)
    return prefix, suffix


def _gla_kernel(*refs, seq, has_s0, emit_state, state_alias):
    n_seq = GROUP_ROWS // seq
    n_tiles = GROUP_ROWS // ROW_TILE
    n_pos = seq // GLA_CHUNK
    it = iter(refs)
    x_ref, mod_ref, ng_ref, win_ref, wa1_ref, wa2_ref, ba_ref, on_ref, wout_ref = (next(it) for _ in range(9))
    s0_ref = next(it) if has_s0 else None
    if state_alias:
        next(it)
    y_ref = next(it)
    sout_ref = next(it) if emit_state else None
    q_s, k_s, b_s, v_s, gate_s, o_s = (next(it) for _ in range(6))
    st_s = None if emit_state else next(it)

    def state_at(s, d, h):
        return sout_ref.at[s, 0, d, h] if emit_state else st_s.at[d, h]

    ng = ng_ref[...]
    prefix_m, suffix_m = _chunk_sum_matrices()

    def project_wide(t):
        rows = slice(t * ROW_TILE, (t + 1) * ROW_TILE)
        hb = _modulated_norm(x_ref[0, rows, :], ng, mod_ref)
        q_s[rows, :] = _dot(hb, win_ref[0, :, 0:GLA_QD]) * (GLA_DK ** -0.5)
        k_s[rows, :] = _dot(hb, win_ref[0, :, GLA_QD:2 * GLA_QD])
        v_s[rows, :] = _dot(hb, win_ref[0, :, 2 * GLA_QD:2 * GLA_QD + GLA_VD]).astype(BF16)
        gate_s[rows, :] = _dot(hb, win_ref[0, :, 2 * GLA_QD + GLA_VD:])
        low = _dot(hb, wa1_ref[...]).astype(BF16)
        return _dot(low, wa2_ref[...]) + ba_ref[...]

    def decay_sums(t, z):
        rows = slice(t * ROW_TILE, (t + 1) * ROW_TILE)
        logg2 = _log_sigmoid(z) * (math.log2(math.e) / GLA_TAU)
        for d, csum_m in enumerate((prefix_m, suffix_m)):
            parts = jnp.concatenate(_split_top(logg2[:, d * GLA_QD:(d + 1) * GLA_QD]), axis=0).astype(BF16)
            b_s[d, rows, :] = _dot(jnp.concatenate([csum_m, csum_m], axis=1), parts)

    z = project_wide(0)
    for t in range(n_tiles):
        z_next = project_wide(t + 1) if t + 1 < n_tiles else None
        decay_sums(t, z)
        z = z_next

    for s in range(n_seq):
        for d in range(2):
            for h in range(GLA_HEADS):
                state_at(s, d, h)[...] = s0_ref[0, 0, d, h] if has_s0 else jnp.zeros((GLA_DK, GLA_DV), F32)

    ri = lax.broadcasted_iota(jnp.int32, (GLA_CHUNK, GLA_CHUNK), 0)
    ci = lax.broadcasted_iota(jnp.int32, (GLA_CHUNK, GLA_CHUNK), 1)
    masks = (ci <= ri, ci >= ri)

    def scan_step(i, first_visit):
        chains = []
        for s in range(n_seq):
            for d in range(2):
                c = s * n_pos + (i if d == 0 else n_pos - 1 - i)
                rows = pl.ds(pl.multiple_of(c * GLA_CHUNK, GLA_CHUNK), GLA_CHUNK)
                b = b_s[d, rows, :]
                edge = GLA_CHUNK - 1 if d == 0 else 0
                total = b[edge:edge + 1, :]
                k = k_s[rows, :]
                qe = (q_s[rows, :] * jnp.exp2(b)).astype(BF16)
                ke = (k * jnp.exp2(-b)).astype(BF16)
                kdt = (k * jnp.exp2(total - b)).T.astype(BF16)
                dec = jnp.exp2(total)
                for h in range(GLA_HEADS):
                    kc = slice(h * GLA_DK, (h + 1) * GLA_DK)
                    chains.append((s, d, h, rows, slice(h * GLA_DV, (h + 1) * GLA_DV),
                                   qe[:, kc], ke[:, kc], kdt[kc, :], dec[:, kc]))
        scores = [_dot_nt(qe, ke) for (s, d, h, rows, vc, qe, ke, kdt, dec) in chains]
        no_state = jnp.zeros((GLA_DK, GLA_DK), BF16)
        for (s, d, h, rows, vc, qe, ke, kdt, dec), sc in zip(chains, scores):
            a = jnp.where(masks[d], sc, 0.0).astype(BF16)
            st_ref = state_at(s, d, h)
            st = st_ref[...]
            both = _dot(jnp.concatenate([jnp.concatenate([qe, a], axis=1),
                                         jnp.concatenate([no_state, kdt], axis=1)], axis=0),
                        jnp.concatenate([st.astype(BF16), v_s[rows, vc]], axis=0))
            if first_visit:
                o_s[rows, vc] = both[0:GLA_CHUNK]
            else:
                o_s[rows, vc] += both[0:GLA_CHUNK]
            dec_col = jnp.broadcast_to(dec, (GLA_DK, GLA_DK)).T
            st_ref[...] = jnp.concatenate([dec_col] * (GLA_DV // GLA_DK), axis=1) * st + both[GLA_CHUNK:]

    def first_half(i, carry):
        scan_step(i, True)
        return carry

    def second_half(i, carry):
        scan_step(i, False)
        return carry

    lax.fori_loop(0, n_pos // 2, first_half, 0)
    lax.fori_loop(n_pos // 2, n_pos, second_half, 0)

    on = on_ref[...]
    res_gate = mod_ref[0, 0, :, 2 * D_MODEL:]

    def gated(t):
        rows = slice(t * ROW_TILE, (t + 1) * ROW_TILE)
        return (_head_rms(o_s[rows, :], on, GLA_DV) * _silu(gate_s[rows, :])).astype(BF16)

    og = gated(0)
    for t in range(n_tiles):
        rows = slice(t * ROW_TILE, (t + 1) * ROW_TILE)
        og_next = gated(t + 1) if t + 1 < n_tiles else None
        y_ref[0, rows, :] = x_ref[0, rows, :] + res_gate * _dot(og, wout_ref[0])
        og = og_next


def _resident(shape, layer=None):
    if layer is None:
        return pl.BlockSpec(shape, lambda b: (0,) * len(shape), pipeline_mode=pl.Buffered(1))
    return pl.BlockSpec((1,) + shape, lambda b: (layer,) + (0,) * len(shape), pipeline_mode=pl.Buffered(1))


def _gla_layer(x, mods, layer, mod_row, ng, win, wa1, wa2, ba, on, wout, state_in, gla_idx, n_gla, states_so_far):
    batch, seq, _ = x.shape
    n_seq = GROUP_ROWS // seq
    n_groups = batch // n_seq
    has_s0 = state_in is not None
    emit_state = not has_s0
    assert not has_s0 or n_seq == 1
    in_specs = [
        pl.BlockSpec((1, GROUP_ROWS, D_MODEL), lambda b: (b, 0, 0)),
        pl.BlockSpec((1, 1, 1, 3 * D_MODEL), lambda b: (layer, mod_row(b), 0, 0)),
        _resident((1, D_MODEL)),
        _resident((D_MODEL, 2 * GLA_QD + 2 * GLA_VD), gla_idx),
        _resident((D_MODEL, RANK_PAD)),
        _resident((RANK_PAD, 2 * GLA_QD)),
        _resident((1, 2 * GLA_QD)),
        _resident((1, GLA_DV)),
        _resident((GLA_VD, D_MODEL), gla_idx),
    ]
    args = [x.reshape(n_groups, GROUP_ROWS, D_MODEL), mods, ng, win, wa1, wa2, ba, on, wout]
    if has_s0:
        in_specs.append(pl.BlockSpec((1, 1, 2, GLA_HEADS, GLA_DK, GLA_DV),
                                     lambda b: (b, gla_idx, 0, 0, 0, 0)))
        args.append(state_in)
    aliases = {}
    state_alias = emit_state and states_so_far is not None
    if state_alias:
        aliases[len(args)] = 1
        in_specs.append(pl.BlockSpec(memory_space=pl.ANY))
        args.append(states_so_far)
    out_specs = [pl.BlockSpec((1, GROUP_ROWS, D_MODEL), lambda b: (b, 0, 0))]
    out_shape = [jax.ShapeDtypeStruct((n_groups, GROUP_ROWS, D_MODEL), F32)]
    if emit_state:
        out_specs.append(pl.BlockSpec((n_seq, 1, 2, GLA_HEADS, GLA_DK, GLA_DV), lambda b: (b, gla_idx, 0, 0, 0, 0)))
        out_shape.append(jax.ShapeDtypeStruct((batch, n_gla, 2, GLA_HEADS, GLA_DK, GLA_DV), F32))
    scratch = [
        pltpu.VMEM((GROUP_ROWS, GLA_QD), F32),
        pltpu.VMEM((GROUP_ROWS, GLA_QD), F32),
        pltpu.VMEM((2, GROUP_ROWS, GLA_QD), F32),
        pltpu.VMEM((GROUP_ROWS, GLA_VD), BF16),
        pltpu.VMEM((GROUP_ROWS, GLA_VD), F32),
        pltpu.VMEM((GROUP_ROWS, GLA_VD), F32),
    ]
    if not emit_state:
        scratch.append(pltpu.VMEM((2, GLA_HEADS, GLA_DK, GLA_DV), F32))
    outs = pl.pallas_call(
        functools.partial(_gla_kernel, seq=seq, has_s0=has_s0, emit_state=emit_state, state_alias=state_alias),
        grid=(n_groups,),
        in_specs=in_specs,
        out_specs=out_specs,
        out_shape=out_shape,
        scratch_shapes=scratch,
        input_output_aliases=aliases,
        compiler_params=pltpu.CompilerParams(dimension_semantics=("arbitrary",),
                                             vmem_limit_bytes=V7X_VMEM_LIMIT_BYTES),
        name=f"gla_layer_seq{seq}",
    )(*args)
    y = outs[0].reshape(batch, seq, D_MODEL)
    return (y, outs[1]) if emit_state else (y, None)


def _rope_swap(x):
    lane = lax.broadcasted_iota(jnp.int32, x.shape, 1)
    quarter = HEAD_DIM // 4
    first = (lane % (2 * quarter)) < quarter
    return jnp.where(first, pltpu.roll(x, HEAD_DIM - quarter, 1), pltpu.roll(x, quarter, 1))


def _att_kernel(*refs, seq, latent, cache_alias):
    n_tiles = seq // ROW_TILE
    n_keys = seq + (refs[9].shape[2] if latent else 0)
    it = iter(refs)
    x_ref, mod_ref, ng_ref, win_ref, qn_ref, kn_ref, wout_ref = (next(it) for _ in range(7))
    if latent:
        cos_ref, sin_ref, ck_ref, cv_ref = (next(it) for _ in range(4))
    if cache_alias:
        next(it), next(it)
    y_ref = next(it)
    if not latent:
        kout_ref, vout_ref = next(it), next(it)
    q_s, k_s, v_s, gate_s, ao_s = (next(it) for _ in range(5))

    ng = ng_ref[...]
    qn = qn_ref[...]
    kn = kn_ref[...]
    exp2_scale = (HEAD_DIM ** -0.5) * math.log2(math.e)

    def project(t, carry):
        rows = pl.ds(pl.multiple_of(t * ROW_TILE, ROW_TILE), ROW_TILE)
        hb = _modulated_norm(x_ref[0, rows, :], ng, mod_ref)
        q = _head_rms(_dot(hb, win_ref[0, :, 0:ATT_QD]), qn, HEAD_DIM)
        k = _head_rms(_dot(hb, win_ref[0, :, ATT_QD:ATT_QD + ATT_KD]), kn, HEAD_DIM)
        v = _dot(hb, win_ref[0, :, ATT_QD + ATT_KD:ATT_QD + 2 * ATT_KD])
        gate_s[rows, :] = _dot(hb, win_ref[0, :, ATT_QD + 2 * ATT_KD:])
        if latent:
            cos = cos_ref[rows, :]
            sin = sin_ref[rows, :]
            q = jnp.concatenate(
                [q[:, h * HEAD_DIM:(h + 1) * HEAD_DIM] * cos + _rope_swap(q[:, h * HEAD_DIM:(h + 1) * HEAD_DIM]) * sin
                 for h in range(ATT_HEADS)], axis=-1)
            k = jnp.concatenate(
                [k[:, h * HEAD_DIM:(h + 1) * HEAD_DIM] * cos + _rope_swap(k[:, h * HEAD_DIM:(h + 1) * HEAD_DIM]) * sin
                 for h in range(ATT_KV_HEADS)], axis=-1)
        else:
            for h in range(ATT_KV_HEADS):
                kout_ref[0, 0, rows, h, :] = k[:, h * HEAD_DIM:(h + 1) * HEAD_DIM]
                vout_ref[0, 0, rows, h, :] = v[:, h * HEAD_DIM:(h + 1) * HEAD_DIM]
        q_s[rows, :] = (q * exp2_scale).astype(BF16)
        k_s[rows, :] = k.astype(BF16)
        for h in range(ATT_KV_HEADS):
            v_s[rows, 2 * h * HEAD_DIM:(2 * h + 1) * HEAD_DIM] = v[:, h * HEAD_DIM:(h + 1) * HEAD_DIM].astype(BF16)
        return carry

    for h in range(ATT_KV_HEADS):
        v_s[:, (2 * h + 1) * HEAD_DIM:(2 * h + 2) * HEAD_DIM] = jnp.ones((n_keys, HEAD_DIM), BF16)
    lax.fori_loop(0, n_tiles, project, 0)
    if latent:
        for h in range(ATT_KV_HEADS):
            k_s[seq:n_keys, h * HEAD_DIM:(h + 1) * HEAD_DIM] = ck_ref[0, 0, :, h, :].astype(BF16)
            v_s[seq:n_keys, 2 * h * HEAD_DIM:(2 * h + 1) * HEAD_DIM] = cv_ref[0, 0, :, h, :].astype(BF16)

    def attend(t, carry):
        rows = pl.ds(pl.multiple_of(t * ROW_TILE, ROW_TILE), ROW_TILE)

        def scores(h):
            kc = slice((h // ATT_GROUP) * HEAD_DIM, (h // ATT_GROUP + 1) * HEAD_DIM)
            return _dot_nt(q_s[rows, h * HEAD_DIM:(h + 1) * HEAD_DIM], k_s[:, kc])

        s = scores(0)
        for h in range(ATT_HEADS):
            s_next = scores(h + 1) if h + 1 < ATT_HEADS else None
            vc = slice((h // ATT_GROUP) * 2 * HEAD_DIM, (h // ATT_GROUP + 1) * 2 * HEAD_DIM)
            p = jnp.exp2(s - jnp.max(s, axis=-1, keepdims=True))
            o = _dot(p.astype(BF16), v_s[:, vc])
            ao_s[rows, h * HEAD_DIM:(h + 1) * HEAD_DIM] = o[:, 0:HEAD_DIM] / o[:, HEAD_DIM:]
            s = s_next
        return carry

    lax.fori_loop(0, n_tiles, attend, 0)

    res_gate = mod_ref[0, 0, :, 2 * D_MODEL:]

    def finish(t, carry):
        rows = pl.ds(pl.multiple_of(t * ROW_TILE, ROW_TILE), ROW_TILE)
        y = _dot((ao_s[rows, :] * _silu(gate_s[rows, :])).astype(BF16), wout_ref[0])
        y_ref[0, rows, :] = x_ref[0, rows, :] + res_gate * y
        return carry

    lax.fori_loop(0, n_tiles, finish, 0)


def _att_layer(x, mods, layer, mod_row, ng, win, qn, kn, wout, latent_inputs, att_idx, n_att, caches_so_far):
    batch, seq, _ = x.shape
    latent = latent_inputs is not None
    in_specs = [
        pl.BlockSpec((1, seq, D_MODEL), lambda b: (b, 0, 0)),
        pl.BlockSpec((1, 1, 1, 3 * D_MODEL), lambda b: (layer, mod_row(b), 0, 0)),
        _resident((1, D_MODEL)),
        _resident((D_MODEL, 2 * ATT_QD + 2 * ATT_KD), att_idx),
        _resident((1, HEAD_DIM)),
        _resident((1, HEAD_DIM)),
        _resident((ATT_QD, D_MODEL), att_idx),
    ]
    args = [x, mods, ng, win, qn, kn, wout]
    n_keys = seq
    if latent:
        cos, sin, cache_k, cache_v = latent_inputs
        past = cache_k.shape[2]
        n_keys = seq + past
        in_specs += [
            _resident((seq, HEAD_DIM)),
            _resident((seq, HEAD_DIM)),
            pl.BlockSpec((1, 1, past, ATT_KV_HEADS, HEAD_DIM), lambda b: (b, att_idx, 0, 0, 0)),
            pl.BlockSpec((1, 1, past, ATT_KV_HEADS, HEAD_DIM), lambda b: (b, att_idx, 0, 0, 0)),
        ]
        args += [cos, sin, cache_k, cache_v]
    aliases = {}
    cache_alias = (not latent) and caches_so_far is not None
    if cache_alias:
        aliases = {len(args): 1, len(args) + 1: 2}
        in_specs += [pl.BlockSpec(memory_space=pl.ANY)] * 2
        args += list(caches_so_far)
    out_specs = [pl.BlockSpec((1, seq, D_MODEL), lambda b: (b, 0, 0))]
    out_shape = [jax.ShapeDtypeStruct((batch, seq, D_MODEL), F32)]
    if not latent:
        out_specs += [pl.BlockSpec((1, 1, seq, ATT_KV_HEADS, HEAD_DIM), lambda b: (b, att_idx, 0, 0, 0))] * 2
        out_shape += [jax.ShapeDtypeStruct((batch, n_att, seq, ATT_KV_HEADS, HEAD_DIM), F32)] * 2
    scratch = [
        pltpu.VMEM((seq, ATT_QD), BF16),
        pltpu.VMEM((n_keys, ATT_KD), BF16),
        pltpu.VMEM((n_keys, 2 * ATT_KD), BF16),
        pltpu.VMEM((seq, ATT_QD), F32),
        pltpu.VMEM((seq, ATT_QD), F32),
    ]
    outs = pl.pallas_call(
        functools.partial(_att_kernel, seq=seq, latent=latent, cache_alias=cache_alias),
        grid=(batch,),
        in_specs=in_specs,
        out_specs=out_specs,
        out_shape=out_shape,
        scratch_shapes=scratch,
        input_output_aliases=aliases,
        compiler_params=pltpu.CompilerParams(dimension_semantics=("arbitrary",),
                                             vmem_limit_bytes=V7X_VMEM_LIMIT_BYTES),
        name=f"att_layer_seq{seq}",
    )(*args)
    return outs


def _rope_tables(seq):
    half = HEAD_DIM // 2
    nf = half // 2
    pos = np.arange(seq)
    freqs = jnp.asarray(ROPE_THETA, F32) ** (-jnp.arange(nf, dtype=F32) / nf)
    ang_row = jnp.asarray(pos // GRID_W, F32)[:, None] * freqs[None, :]
    ang_col = jnp.asarray(pos % GRID_W, F32)[:, None] * freqs[None, :]
    cos = jnp.concatenate([jnp.cos(ang_row)] * 2 + [jnp.cos(ang_col)] * 2, axis=-1)
    sin = jnp.concatenate([-jnp.sin(ang_row), jnp.sin(ang_row), -jnp.sin(ang_col), jnp.sin(ang_col)], axis=-1)
    return cos, sin


def kernel(x_prompt, x_sample, state_gla, cache_k, cache_v, c, c_ctx, norm_g, w_ada, b_ada,
           gla_w_in, gla_wa1, gla_wa2, gla_ba, gla_onorm, gla_w_out,
           att_w_in, att_qnorm, att_knorm, att_w_out):
    n_dec = x_sample.shape[0]
    assert 1 + n_dec <= MOD_ROWS
    assert x_prompt.shape[1] % ROW_TILE == 0 and x_sample.shape[1] % ROW_TILE == 0

    cvec = jnp.zeros((MOD_ROWS, D_MODEL), F32).at[0].set(c_ctx).at[1:1 + n_dec].set(c)
    mods = _modulations(cvec, w_ada, b_ada).reshape(DEPTH, MOD_ROWS, 1, 3 * D_MODEL)
    ctx_row = lambda b: 0
    dec_row = lambda b: b + 1

    cos, sin = _rope_tables(x_sample.shape[1])
    n_gla, n_att = gla_w_in.shape[0], att_w_in.shape[0]

    gla_win, gla_wout = gla_w_in.astype(BF16), gla_w_out.astype(BF16)
    att_win, att_wout = att_w_in.astype(BF16), att_w_out.astype(BF16)

    xp, xs = x_prompt, x_sample
    states, caches = None, None
    for l in range(DEPTH):
        i = l // 2
        ng = norm_g[l].reshape(1, D_MODEL)
        if l % 2 == 0:
            wa1 = jnp.zeros((D_MODEL, RANK_PAD), F32)
            wa1 = wa1.at[:, 0:GLA_RANK].set(gla_wa1[i, 0]).at[:, GLA_RANK:2 * GLA_RANK].set(gla_wa1[i, 1])
            wa2 = jnp.zeros((RANK_PAD, 2 * GLA_QD), F32)
            wa2 = wa2.at[0:GLA_RANK, 0:GLA_QD].set(gla_wa2[i, 0])
            wa2 = wa2.at[GLA_RANK:2 * GLA_RANK, GLA_QD:].set(gla_wa2[i, 1])
            ba = gla_ba[i].reshape(1, 2 * GLA_QD)
            on = gla_onorm[i].reshape(1, GLA_DV)
            common = (ng, gla_win, wa1.astype(BF16), wa2.astype(BF16), ba, on, gla_wout)
            xp, states = _gla_layer(xp, mods, l, ctx_row, *common, None, i, n_gla, states)
            xs, _ = _gla_layer(xs, mods, l, dec_row, *common, state_gla, i, n_gla, None)
        else:
            qn = att_qnorm[i].reshape(1, HEAD_DIM)
            kn = att_knorm[i].reshape(1, HEAD_DIM)
            xp, *caches = _att_layer(xp, mods, l, ctx_row, ng, att_win, qn, kn, att_wout, None, i, n_att, caches)
            (xs,) = _att_layer(xs, mods, l, dec_row, ng, att_win, qn, kn, att_wout, (cos, sin, cache_k, cache_v), i,
                               n_att, None)
    return (xp, xs, states, caches[0], caches[1])
```

```python
import functools
import math

import jax
import jax.numpy as jnp
import numpy as np
from jax import lax
from jax.experimental import pallas as pl
from jax.experimental.pallas import tpu as pltpu

D_MODEL = 1024
DEPTH = 4
GRID_W = 64
GLA_HEADS = 4
GLA_DK = 128
GLA_DV = 256
GLA_RANK = 16
GLA_TAU = 16.0
GLA_CHUNK = 64
GLA_QD = GLA_HEADS * GLA_DK
GLA_VD = GLA_HEADS * GLA_DV
HEAD_DIM = 128
ATT_HEADS = 8
ATT_KV_HEADS = 2
ATT_GROUP = ATT_HEADS // ATT_KV_HEADS
ATT_QD = ATT_HEADS * HEAD_DIM
ATT_KD = ATT_KV_HEADS * HEAD_DIM
ROPE_THETA = 10000.0
EPS = 1e-6

ROW_TILE = 256
GROUP_ROWS = 1024
MOD_ROWS = 8
RANK_PAD = 128
V7X_VMEM_LIMIT_BYTES = 60 * 1024 * 1024

F32 = jnp.float32
BF16 = jnp.bfloat16
_NT = (((1,), (1,)), ((), ()))
_TN = (((0,), (0,)), ((), ()))


def _dot(a, b):
    return jnp.dot(a, b, preferred_element_type=F32)


def _dot_nt(a, b):
    return lax.dot_general(a, b, _NT, preferred_element_type=F32)


def _dot_tn(a, b):
    return lax.dot_general(a, b, _TN, preferred_element_type=F32)


def _silu(x):
    return x * (1.0 / (1.0 + jnp.exp(-x)))


def _log_sigmoid(z):
    return jnp.minimum(z, 0.0) - jnp.log(1.0 + jnp.exp(-jnp.abs(z)))


def _split_top(x):
    top = pltpu.bitcast(pltpu.bitcast(x, jnp.uint32) & jnp.uint32(0xFFFF0000), F32)
    return top, x - top


def _modulated_norm(x, ng, mod_ref):
    shift = mod_ref[0, 0, :, 0:D_MODEL]
    scale = mod_ref[0, 0, :, D_MODEL:2 * D_MODEL]
    ms = jnp.mean(x * x, axis=-1, keepdims=True)
    return ((x * lax.rsqrt(ms + EPS)) * (ng * (1.0 + scale)) + shift).astype(BF16)


def _head_rms(x, g, width):
    outs = []
    for h in range(x.shape[-1] // width):
        xh = x[:, h * width:(h + 1) * width]
        ms = jnp.mean(xh * xh, axis=-1, keepdims=True)
        outs.append((xh * lax.rsqrt(ms + EPS)) * g)
    return jnp.concatenate(outs, axis=-1)


def _mod_kernel(c_ref, w_ref, b_ref, o_ref):
    acc = _dot(jnp.concatenate(_split_top(_silu(c_ref[...])), axis=0).astype(BF16), w_ref[0].astype(BF16))
    o_ref[0] = acc[0:MOD_ROWS] + acc[MOD_ROWS:] + b_ref[0]


def _modulations(cvec, w_ada, b_ada):
    tn = D_MODEL
    n_tiles = 3 * D_MODEL // tn
    return pl.pallas_call(
        _mod_kernel,
        grid=(DEPTH, n_tiles),
        in_specs=[
            pl.BlockSpec((MOD_ROWS, D_MODEL), lambda l, j: (0, 0)),
            pl.BlockSpec((1, D_MODEL, tn), lambda l, j: (l, 0, j)),
            pl.BlockSpec((1, 1, tn), lambda l, j: (l, 0, j)),
        ],
        out_specs=pl.BlockSpec((1, MOD_ROWS, tn), lambda l, j: (l, 0, j)),
        out_shape=jax.ShapeDtypeStruct((DEPTH, MOD_ROWS, 3 * D_MODEL), F32),
        compiler_params=pltpu.CompilerParams(dimension_semantics=("arbitrary", "arbitrary")),
        name="adaln_modulation",
    )(cvec, w_ada, b_ada.reshape(DEPTH, 1, 3 * D_MODEL))


def _chunk_sum_matrices():
    r = lax.broadcasted_iota(jnp.int32, (ROW_TILE, ROW_TILE), 0)
    c = lax.broadcasted_iota(jnp.int32, (ROW_TILE, ROW_TILE), 1)
    same = (r // GLA_CHUNK) == (c // GLA_CHUNK)
    prefix = jnp.where(same & (c <= r), 1.0, 0.0).astype(BF16)
    suffix = jnp.where(same & (c >= r), 1.0, 0.0).astype(BF16)
    return prefix, suffix


def _gla_kernel(*refs, seq, has_s0, emit_state, state_alias):
    n_seq = GROUP_ROWS // seq
    n_tiles = GROUP_ROWS // ROW_TILE
    n_pos = seq // GLA_CHUNK
    merge_products = n_seq > 1
    it = iter(refs)
    x_ref, mod_ref, ng_ref, win_ref, wa1_ref, wa2_ref, ba_ref, on_ref, wout_ref = (next(it) for _ in range(9))
    s0_ref = next(it) if has_s0 else None
    if state_alias:
        next(it)
    y_ref = next(it)
    sout_ref = next(it) if emit_state else None
    q_s, k_s, b_s, v_s, gate_s, o_s = (next(it) for _ in range(6))
    st_s = None if emit_state else next(it)

    def state_at(s, d, h):
        return sout_ref.at[s, 0, d, h] if emit_state else st_s.at[d, h]

    ng = ng_ref[...]
    prefix_m, suffix_m = _chunk_sum_matrices()

    def project_wide(t):
        rows = slice(t * ROW_TILE, (t + 1) * ROW_TILE)
        hb = _modulated_norm(x_ref[0, rows, :], ng, mod_ref)
        q_s[rows, :] = _dot(hb, win_ref[0, :, 0:GLA_QD]) * (GLA_DK ** -0.5)
        k_s[rows, :] = _dot(hb, win_ref[0, :, GLA_QD:2 * GLA_QD])
        v_s[rows, :] = _dot(hb, win_ref[0, :, 2 * GLA_QD:2 * GLA_QD + GLA_VD]).astype(BF16)
        gate_s[rows, :] = _dot(hb, win_ref[0, :, 2 * GLA_QD + GLA_VD:])
        low = _dot(hb, wa1_ref[...]).astype(BF16)
        return _dot(low, wa2_ref[...]) + ba_ref[...]

    def decay_sums(t, z):
        rows = slice(t * ROW_TILE, (t + 1) * ROW_TILE)
        logg2 = _log_sigmoid(z) * (math.log2(math.e) / GLA_TAU)
        for d, csum_m in enumerate((prefix_m, suffix_m)):
            parts = jnp.concatenate(_split_top(logg2[:, d * GLA_QD:(d + 1) * GLA_QD]), axis=0).astype(BF16)
            b_s[d, rows, :] = _dot(jnp.concatenate([csum_m, csum_m], axis=1), parts)

    z = project_wide(0)
    for t in range(n_tiles):
        z_next = project_wide(t + 1) if t + 1 < n_tiles else None
        decay_sums(t, z)
        z = z_next

    for s in range(n_seq):
        for d in range(2):
            for h in range(GLA_HEADS):
                state_at(s, d, h)[...] = s0_ref[0, 0, d, h] if has_s0 else jnp.zeros((GLA_DK, GLA_DV), F32)

    ri = lax.broadcasted_iota(jnp.int32, (GLA_CHUNK, GLA_CHUNK), 0)
    ci = lax.broadcasted_iota(jnp.int32, (GLA_CHUNK, GLA_CHUNK), 1)
    masks = (ci <= ri, ci >= ri)

    def scan_step(i, first_visit):
        chains = []
        for s in range(n_seq):
            for d in range(2):
                c = s * n_pos + (i if d == 0 else n_pos - 1 - i)
                rows = pl.ds(pl.multiple_of(c * GLA_CHUNK, GLA_CHUNK), GLA_CHUNK)
                b = b_s[d, rows, :]
                edge = GLA_CHUNK - 1 if d == 0 else 0
                total = b[edge:edge + 1, :]
                k = k_s[rows, :]
                qe = (q_s[rows, :] * jnp.exp2(b)).astype(BF16)
                ke = (k * jnp.exp2(-b)).astype(BF16)
                kdt = (k * jnp.exp2(total - b)).T.astype(BF16)
                dec = jnp.exp2(total)
                for h in range(GLA_HEADS):
                    kc = slice(h * GLA_DK, (h + 1) * GLA_DK)
                    chains.append((s, d, h, rows, slice(h * GLA_DV, (h + 1) * GLA_DV),
                                   qe[:, kc], ke[:, kc], kdt[kc, :], dec[:, kc]))
        scores = [_dot_nt(qe, ke) for (s, d, h, rows, vc, qe, ke, kdt, dec) in chains]
        if not merge_products:
            updates = [_dot(kdt, v_s[rows, vc]) for (s, d, h, rows, vc, qe, ke, kdt, dec) in chains]
        no_state = jnp.zeros((GLA_DK, GLA_DK), BF16)

        def decayed(st, dec):
            dec_col = jnp.broadcast_to(dec, (GLA_DK, GLA_DK)).T
            return jnp.concatenate([dec_col] * (GLA_DV // GLA_DK), axis=1) * st

        for (s, d, h, rows, vc, qe, ke, kdt, dec), sc in zip(chains, scores):
            a = jnp.where(masks[d], sc, 0.0).astype(BF16)
            st_ref = state_at(s, d, h)
            st = st_ref[...]
            lhs = jnp.concatenate([qe, a], axis=1)
            if merge_products:
                lhs = jnp.concatenate([lhs, jnp.concatenate([no_state, kdt], axis=1)], axis=0)
            both = _dot(lhs, jnp.concatenate([st.astype(BF16), v_s[rows, vc]], axis=0))
            if first_visit:
                o_s[rows, vc] = both[0:GLA_CHUNK]
            else:
                o_s[rows, vc] += both[0:GLA_CHUNK]
            if merge_products:
                st_ref[...] = decayed(st, dec) + both[GLA_CHUNK:]
        if not merge_products:
            for (s, d, h, rows, vc, qe, ke, kdt, dec), upd in zip(chains, updates):
                st_ref = state_at(s, d, h)
                st_ref[...] = decayed(st_ref[...], dec) + upd

    def first_half(i, carry):
        scan_step(i, True)
        return carry

    def second_half(i, carry):
        scan_step(i, False)
        return carry

    lax.fori_loop(0, n_pos // 2, first_half, 0)
    lax.fori_loop(n_pos // 2, n_pos, second_half, 0)

    on = on_ref[...]
    res_gate = mod_ref[0, 0, :, 2 * D_MODEL:]

    def gated(t):
        rows = slice(t * ROW_TILE, (t + 1) * ROW_TILE)
        return (_head_rms(o_s[rows, :], on, GLA_DV) * _silu(gate_s[rows, :])).astype(BF16)

    og = gated(0)
    for t in range(n_tiles):
        rows = slice(t * ROW_TILE, (t + 1) * ROW_TILE)
        og_next = gated(t + 1) if t + 1 < n_tiles else None
        y_ref[0, rows, :] = x_ref[0, rows, :] + res_gate * _dot(og, wout_ref[0])
        og = og_next


def _resident(shape, layer=None):
    if layer is None:
        return pl.BlockSpec(shape, lambda b: (0,) * len(shape), pipeline_mode=pl.Buffered(1))
    return pl.BlockSpec((1,) + shape, lambda b: (layer,) + (0,) * len(shape), pipeline_mode=pl.Buffered(1))


def _gla_layer(x, mods, layer, mod_row, ng, win, wa1, wa2, ba, on, wout, state_in, gla_idx, n_gla, states_so_far):
    batch, seq, _ = x.shape
    n_seq = GROUP_ROWS // seq
    n_groups = batch // n_seq
    has_s0 = state_in is not None
    emit_state = not has_s0
    assert not has_s0 or n_seq == 1
    in_specs = [
        pl.BlockSpec((1, GROUP_ROWS, D_MODEL), lambda b: (b, 0, 0)),
        pl.BlockSpec((1, 1, 1, 3 * D_MODEL), lambda b: (layer, mod_row(b), 0, 0)),
        _resident((1, D_MODEL)),
        _resident((D_MODEL, 2 * GLA_QD + 2 * GLA_VD), gla_idx),
        _resident((D_MODEL, RANK_PAD)),
        _resident((RANK_PAD, 2 * GLA_QD)),
        _resident((1, 2 * GLA_QD)),
        _resident((1, GLA_DV)),
        _resident((GLA_VD, D_MODEL), gla_idx),
    ]
    args = [x.reshape(n_groups, GROUP_ROWS, D_MODEL), mods, ng, win, wa1, wa2, ba, on, wout]
    if has_s0:
        in_specs.append(pl.BlockSpec((1, 1, 2, GLA_HEADS, GLA_DK, GLA_DV),
                                     lambda b: (b, gla_idx, 0, 0, 0, 0)))
        args.append(state_in)
    aliases = {}
    state_alias = emit_state and states_so_far is not None
    if state_alias:
        aliases[len(args)] = 1
        in_specs.append(pl.BlockSpec(memory_space=pl.ANY))
        args.append(states_so_far)
    out_specs = [pl.BlockSpec((1, GROUP_ROWS, D_MODEL), lambda b: (b, 0, 0))]
    out_shape = [jax.ShapeDtypeStruct((n_groups, GROUP_ROWS, D_MODEL), F32)]
    if emit_state:
        out_specs.append(pl.BlockSpec((n_seq, 1, 2, GLA_HEADS, GLA_DK, GLA_DV), lambda b: (b, gla_idx, 0, 0, 0, 0)))
        out_shape.append(jax.ShapeDtypeStruct((batch, n_gla, 2, GLA_HEADS, GLA_DK, GLA_DV), F32))
    scratch = [
        pltpu.VMEM((GROUP_ROWS, GLA_QD), F32),
        pltpu.VMEM((GROUP_ROWS, GLA_QD), F32),
        pltpu.VMEM((2, GROUP_ROWS, GLA_QD), F32),
        pltpu.VMEM((GROUP_ROWS, GLA_VD), BF16),
        pltpu.VMEM((GROUP_ROWS, GLA_VD), F32),
        pltpu.VMEM((GROUP_ROWS, GLA_VD), F32),
    ]
    if not emit_state:
        scratch.append(pltpu.VMEM((2, GLA_HEADS, GLA_DK, GLA_DV), F32))
    outs = pl.pallas_call(
        functools.partial(_gla_kernel, seq=seq, has_s0=has_s0, emit_state=emit_state, state_alias=state_alias),
        grid=(n_groups,),
        in_specs=in_specs,
        out_specs=out_specs,
        out_shape=out_shape,
        scratch_shapes=scratch,
        input_output_aliases=aliases,
        compiler_params=pltpu.CompilerParams(dimension_semantics=("arbitrary",),
                                             vmem_limit_bytes=V7X_VMEM_LIMIT_BYTES),
        name=f"gla_layer_seq{seq}",
    )(*args)
    y = outs[0].reshape(batch, seq, D_MODEL)
    return (y, outs[1]) if emit_state else (y, None)


def _rope_swap(x):
    lane = lax.broadcasted_iota(jnp.int32, x.shape, 1)
    quarter = HEAD_DIM // 4
    first = (lane % (2 * quarter)) < quarter
    return jnp.where(first, pltpu.roll(x, HEAD_DIM - quarter, 1), pltpu.roll(x, quarter, 1))


def _att_kernel(*refs, seq, latent, cache_alias):
    n_tiles = seq // ROW_TILE
    n_keys = seq + (refs[9].shape[2] if latent else 0)
    it = iter(refs)
    x_ref, mod_ref, ng_ref, win_ref, qn_ref, kn_ref, wout_ref = (next(it) for _ in range(7))
    if latent:
        cos_ref, sin_ref, ck_ref, cv_ref = (next(it) for _ in range(4))
    if cache_alias:
        next(it), next(it)
    y_ref = next(it)
    if not latent:
        kout_ref, vout_ref = next(it), next(it)
    q_s, k_s, v_s, gate_s, ao_s = (next(it) for _ in range(5))

    ng = ng_ref[...]
    qn = qn_ref[...]
    kn = kn_ref[...]
    exp2_scale = (HEAD_DIM ** -0.5) * math.log2(math.e)

    def project(t, carry):
        rows = pl.ds(pl.multiple_of(t * ROW_TILE, ROW_TILE), ROW_TILE)
        hb = _modulated_norm(x_ref[0, rows, :], ng, mod_ref)
        q = _head_rms(_dot(hb, win_ref[0, :, 0:ATT_QD]), qn, HEAD_DIM)
        k = _head_rms(_dot(hb, win_ref[0, :, ATT_QD:ATT_QD + ATT_KD]), kn, HEAD_DIM)
        v = _dot(hb, win_ref[0, :, ATT_QD + ATT_KD:ATT_QD + 2 * ATT_KD])
        gate_s[rows, :] = _dot(hb, win_ref[0, :, ATT_QD + 2 * ATT_KD:])
        if latent:
            cos = cos_ref[rows, :]
            sin = sin_ref[rows, :]
            q = jnp.concatenate(
                [q[:, h * HEAD_DIM:(h + 1) * HEAD_DIM] * cos + _rope_swap(q[:, h * HEAD_DIM:(h + 1) * HEAD_DIM]) * sin
                 for h in range(ATT_HEADS)], axis=-1)
            k = jnp.concatenate(
                [k[:, h * HEAD_DIM:(h + 1) * HEAD_DIM] * cos + _rope_swap(k[:, h * HEAD_DIM:(h + 1) * HEAD_DIM]) * sin
                 for h in range(ATT_KV_HEADS)], axis=-1)
        else:
            for h in range(ATT_KV_HEADS):
                kout_ref[0, 0, rows, h, :] = k[:, h * HEAD_DIM:(h + 1) * HEAD_DIM]
                vout_ref[0, 0, rows, h, :] = v[:, h * HEAD_DIM:(h + 1) * HEAD_DIM]
        q_s[rows, :] = (q * exp2_scale).astype(BF16)
        k_s[rows, :] = k.astype(BF16)
        for h in range(ATT_KV_HEADS):
            v_s[rows, 2 * h * HEAD_DIM:(2 * h + 1) * HEAD_DIM] = v[:, h * HEAD_DIM:(h + 1) * HEAD_DIM].astype(BF16)
        return carry

    for h in range(ATT_KV_HEADS):
        v_s[:, (2 * h + 1) * HEAD_DIM:(2 * h + 2) * HEAD_DIM] = jnp.ones((n_keys, HEAD_DIM), BF16)
    lax.fori_loop(0, n_tiles, project, 0)
    if latent:
        for h in range(ATT_KV_HEADS):
            k_s[seq:n_keys, h * HEAD_DIM:(h + 1) * HEAD_DIM] = ck_ref[0, 0, :, h, :].astype(BF16)
            v_s[seq:n_keys, 2 * h * HEAD_DIM:(2 * h + 1) * HEAD_DIM] = cv_ref[0, 0, :, h, :].astype(BF16)

    def attend(t, carry):
        rows = pl.ds(pl.multiple_of(t * ROW_TILE, ROW_TILE), ROW_TILE)

        def scores(h):
            kc = slice((h // ATT_GROUP) * HEAD_DIM, (h // ATT_GROUP + 1) * HEAD_DIM)
            return _dot_nt(q_s[rows, h * HEAD_DIM:(h + 1) * HEAD_DIM], k_s[:, kc])

        s = scores(0)
        for h in range(ATT_HEADS):
            s_next = scores(h + 1) if h + 1 < ATT_HEADS else None
            vc = slice((h // ATT_GROUP) * 2 * HEAD_DIM, (h // ATT_GROUP + 1) * 2 * HEAD_DIM)
            p = jnp.exp2(s - jnp.max(s, axis=-1, keepdims=True))
            o = _dot(p.astype(BF16), v_s[:, vc])
            ao_s[rows, h * HEAD_DIM:(h + 1) * HEAD_DIM] = o[:, 0:HEAD_DIM] / o[:, HEAD_DIM:]
            s = s_next
        return carry

    lax.fori_loop(0, n_tiles, attend, 0)

    res_gate = mod_ref[0, 0, :, 2 * D_MODEL:]

    def finish(t, carry):
        rows = pl.ds(pl.multiple_of(t * ROW_TILE, ROW_TILE), ROW_TILE)
        y = _dot((ao_s[rows, :] * _silu(gate_s[rows, :])).astype(BF16), wout_ref[0])
        y_ref[0, rows, :] = x_ref[0, rows, :] + res_gate * y
        return carry

    lax.fori_loop(0, n_tiles, finish, 0)


def _att_layer(x, mods, layer, mod_row, ng, win, qn, kn, wout, latent_inputs, att_idx, n_att, caches_so_far):
    batch, seq, _ = x.shape
    latent = latent_inputs is not None
    in_specs = [
        pl.BlockSpec((1, seq, D_MODEL), lambda b: (b, 0, 0)),
        pl.BlockSpec((1, 1, 1, 3 * D_MODEL), lambda b: (layer, mod_row(b), 0, 0)),
        _resident((1, D_MODEL)),
        _resident((D_MODEL, 2 * ATT_QD + 2 * ATT_KD), att_idx),
        _resident((1, HEAD_DIM)),
        _resident((1, HEAD_DIM)),
        _resident((ATT_QD, D_MODEL), att_idx),
    ]
    args = [x, mods, ng, win, qn, kn, wout]
    n_keys = seq
    if latent:
        cos, sin, cache_k, cache_v = latent_inputs
        past = cache_k.shape[2]
        n_keys = seq + past
        in_specs += [
            _resident((seq, HEAD_DIM)),
            _resident((seq, HEAD_DIM)),
            pl.BlockSpec((1, 1, past, ATT_KV_HEADS, HEAD_DIM), lambda b: (b, att_idx, 0, 0, 0)),
            pl.BlockSpec((1, 1, past, ATT_KV_HEADS, HEAD_DIM), lambda b: (b, att_idx, 0, 0, 0)),
        ]
        args += [cos, sin, cache_k, cache_v]
    aliases = {}
    cache_alias = (not latent) and caches_so_far is not None
    if cache_alias:
        aliases = {len(args): 1, len(args) + 1: 2}
        in_specs += [pl.BlockSpec(memory_space=pl.ANY)] * 2
        args += list(caches_so_far)
    out_specs = [pl.BlockSpec((1, seq, D_MODEL), lambda b: (b, 0, 0))]
    out_shape = [jax.ShapeDtypeStruct((batch, seq, D_MODEL), F32)]
    if not latent:
        out_specs += [pl.BlockSpec((1, 1, seq, ATT_KV_HEADS, HEAD_DIM), lambda b: (b, att_idx, 0, 0, 0))] * 2
        out_shape += [jax.ShapeDtypeStruct((batch, n_att, seq, ATT_KV_HEADS, HEAD_DIM), F32)] * 2
    scratch = [
        pltpu.VMEM((seq, ATT_QD), BF16),
        pltpu.VMEM((n_keys, ATT_KD), BF16),
        pltpu.VMEM((n_keys, 2 * ATT_KD), BF16),
        pltpu.VMEM((seq, ATT_QD), F32),
        pltpu.VMEM((seq, ATT_QD), F32),
    ]
    outs = pl.pallas_call(
        functools.partial(_att_kernel, seq=seq, latent=latent, cache_alias=cache_alias),
        grid=(batch,),
        in_specs=in_specs,
        out_specs=out_specs,
        out_shape=out_shape,
        scratch_shapes=scratch,
        input_output_aliases=aliases,
        compiler_params=pltpu.CompilerParams(dimension_semantics=("arbitrary",),
                                             vmem_limit_bytes=V7X_VMEM_LIMIT_BYTES),
        name=f"att_layer_seq{seq}",
    )(*args)
    return outs


def _rope_tables(seq):
    half = HEAD_DIM // 2
    nf = half // 2
    pos = np.arange(seq)
    freqs = jnp.asarray(ROPE_THETA, F32) ** (-jnp.arange(nf, dtype=F32) / nf)
    ang_row = jnp.asarray(pos // GRID_W, F32)[:, None] * freqs[None, :]
    ang_col = jnp.asarray(pos % GRID_W, F32)[:, None] * freqs[None, :]
    cos = jnp.concatenate([jnp.cos(ang_row)] * 2 + [jnp.cos(ang_col)] * 2, axis=-1)
    sin = jnp.concatenate([-jnp.sin(ang_row), jnp.sin(ang_row), -jnp.sin(ang_col), jnp.sin(ang_col)], axis=-1)
    return cos, sin


def kernel(x_prompt, x_sample, state_gla, cache_k, cache_v, c, c_ctx, norm_g, w_ada, b_ada,
           gla_w_in, gla_wa1, gla_wa2, gla_ba, gla_onorm, gla_w_out,
           att_w_in, att_qnorm, att_knorm, att_w_out):
    n_dec = x_sample.shape[0]
    assert 1 + n_dec <= MOD_ROWS
    assert x_prompt.shape[1] % ROW_TILE == 0 and x_sample.shape[1] % ROW_TILE == 0

    cvec = jnp.zeros((MOD_ROWS, D_MODEL), F32).at[0].set(c_ctx).at[1:1 + n_dec].set(c)
    mods = _modulations(cvec, w_ada, b_ada).reshape(DEPTH, MOD_ROWS, 1, 3 * D_MODEL)
    ctx_row = lambda b: 0
    dec_row = lambda b: b + 1

    cos, sin = _rope_tables(x_sample.shape[1])
    n_gla, n_att = gla_w_in.shape[0], att_w_in.shape[0]

    gla_win, gla_wout = gla_w_in.astype(BF16), gla_w_out.astype(BF16)
    att_win, att_wout = att_w_in.astype(BF16), att_w_out.astype(BF16)

    xp, xs = x_prompt, x_sample
    states, caches = None, None
    for l in range(DEPTH):
        i = l // 2
        ng = norm_g[l].reshape(1, D_MODEL)
        if l % 2 == 0:
            wa1 = jnp.zeros((D_MODEL, RANK_PAD), F32)
            wa1 = wa1.at[:, 0:GLA_RANK].set(gla_wa1[i, 0]).at[:, GLA_RANK:2 * GLA_RANK].set(gla_wa1[i, 1])
            wa2 = jnp.zeros((RANK_PAD, 2 * GLA_QD), F32)
            wa2 = wa2.at[0:GLA_RANK, 0:GLA_QD].set(gla_wa2[i, 0])
            wa2 = wa2.at[GLA_RANK:2 * GLA_RANK, GLA_QD:].set(gla_wa2[i, 1])
            ba = gla_ba[i].reshape(1, 2 * GLA_QD)
            on = gla_onorm[i].reshape(1, GLA_DV)
            common = (ng, gla_win, wa1.astype(BF16), wa2.astype(BF16), ba, on, gla_wout)
            xp, states = _gla_layer(xp, mods, l, ctx_row, *common, None, i, n_gla, states)
            xs, _ = _gla_layer(xs, mods, l, dec_row, *common, state_gla, i, n_gla, None)
        else:
            qn = att_qnorm[i].reshape(1, HEAD_DIM)
            kn = att_knorm[i].reshape(1, HEAD_DIM)
            xp, *caches = _att_layer(xp, mods, l, ctx_row, ng, att_win, qn, kn, att_wout, None, i, n_att, caches)
            (xs,) = _att_layer(xs, mods, l, dec_row, ng, att_win, qn, kn, att_wout, (cos, sin, cache_k, cache_v), i,
                               n_att, None)
    return (xp, xs, states, caches[0], caches[1])
```

```python
import functools
import math

import jax
import jax.numpy as jnp
import numpy as np
from jax import lax
from jax.experimental import pallas as pl
from jax.experimental.pallas import tpu as pltpu

D_MODEL = 1024
DEPTH = 4
GRID_W = 64
GLA_HEADS = 4
GLA_DK = 128
GLA_DV = 256
GLA_RANK = 16
GLA_TAU = 16.0
GLA_CHUNK = 64
GLA_QD = GLA_HEADS * GLA_DK
GLA_VD = GLA_HEADS * GLA_DV
HEAD_DIM = 128
ATT_HEADS = 8
ATT_KV_HEADS = 2
ATT_GROUP = ATT_HEADS // ATT_KV_HEADS
ATT_QD = ATT_HEADS * HEAD_DIM
ATT_KD = ATT_KV_HEADS * HEAD_DIM
ROPE_THETA = 10000.0
EPS = 1e-6

ROW_TILE = 256
GROUP_ROWS = 1024
MOD_ROWS = 8
RANK_PAD = 128
V7X_VMEM_LIMIT_BYTES = 60 * 1024 * 1024

F32 = jnp.float32
BF16 = jnp.bfloat16
_NT = (((1,), (1,)), ((), ()))
_TN = (((0,), (0,)), ((), ()))


def _dot(a, b):
    return jnp.dot(a, b, preferred_element_type=F32)


def _dot_nt(a, b):
    return lax.dot_general(a, b, _NT, preferred_element_type=F32)


def _dot_tn(a, b):
    return lax.dot_general(a, b, _TN, preferred_element_type=F32)


def _silu(x):
    return x * (1.0 / (1.0 + jnp.exp(-x)))


def _log_sigmoid(z):
    return jnp.minimum(z, 0.0) - jnp.log(1.0 + jnp.exp(-jnp.abs(z)))


def _split_top(x):
    top = pltpu.bitcast(pltpu.bitcast(x, jnp.uint32) & jnp.uint32(0xFFFF0000), F32)
    return top, x - top


def _modulated_norm(x, ng, mod_ref):
    shift = mod_ref[0, 0, :, 0:D_MODEL]
    scale = mod_ref[0, 0, :, D_MODEL:2 * D_MODEL]
    ms = jnp.mean(x * x, axis=-1, keepdims=True)
    return ((x * lax.rsqrt(ms + EPS)) * (ng * (1.0 + scale)) + shift).astype(BF16)


def _head_rms(x, g, width):
    outs = []
    for h in range(x.shape[-1] // width):
        xh = x[:, h * width:(h + 1) * width]
        ms = jnp.mean(xh * xh, axis=-1, keepdims=True)
        outs.append((xh * lax.rsqrt(ms + EPS)) * g)
    return jnp.concatenate(outs, axis=-1)


def _mod_kernel(c_ref, w_ref, b_ref, o_ref):
    acc = _dot(jnp.concatenate(_split_top(_silu(c_ref[...])), axis=0).astype(BF16), w_ref[0].astype(BF16))
    o_ref[0, :, 0, :] = acc[0:MOD_ROWS] + acc[MOD_ROWS:] + b_ref[pl.ds(pl.program_id(0), 1), :]


def _modulations(cvec, w_ada, b_ada):
    tn = D_MODEL
    n_tiles = 3 * D_MODEL // tn
    return pl.pallas_call(
        _mod_kernel,
        grid=(DEPTH, n_tiles),
        in_specs=[
            pl.BlockSpec((MOD_ROWS, D_MODEL), lambda l, j: (0, 0)),
            pl.BlockSpec((1, D_MODEL, tn), lambda l, j: (l, 0, j)),
            pl.BlockSpec((DEPTH, tn), lambda l, j: (0, j)),
        ],
        out_specs=pl.BlockSpec((1, MOD_ROWS, 1, tn), lambda l, j: (l, 0, 0, j)),
        out_shape=jax.ShapeDtypeStruct((DEPTH, MOD_ROWS, 1, 3 * D_MODEL), F32),
        compiler_params=pltpu.CompilerParams(dimension_semantics=("arbitrary", "arbitrary")),
        name="adaln_modulation",
    )(cvec, w_ada, b_ada)


def _chunk_sum_matrices():
    r = lax.broadcasted_iota(jnp.int32, (ROW_TILE, ROW_TILE), 0)
    c = lax.broadcasted_iota(jnp.int32, (ROW_TILE, ROW_TILE), 1)
    same = (r // GLA_CHUNK) == (c // GLA_CHUNK)
    prefix = jnp.where(same & (c <= r), 1.0, 0.0).astype(BF16)
    suffix = jnp.where(same & (c >= r), 1.0, 0.0).astype(BF16)
    return prefix, suffix


def _gla_kernel(*refs, seq, has_s0, emit_state, state_alias):
    n_seq = GROUP_ROWS // seq
    n_tiles = GROUP_ROWS // ROW_TILE
    n_pos = seq // GLA_CHUNK
    merge_products = n_seq > 1
    it = iter(refs)
    x_ref, mod_ref, ng_ref, win_ref, wa1_ref, wa2_ref, ba_ref, on_ref, wout_ref = (next(it) for _ in range(9))
    s0_ref = next(it) if has_s0 else None
    if state_alias:
        next(it)
    y_ref = next(it)
    sout_ref = next(it) if emit_state else None
    q_s, k_s, b_s, v_s, gate_s, o_s = (next(it) for _ in range(6))
    st_s = None if emit_state else next(it)

    def state_at(s, d, h):
        return sout_ref.at[s, 0, d, h] if emit_state else st_s.at[d, h]

    ng = ng_ref[...]
    prefix_m, suffix_m = _chunk_sum_matrices()

    def project_wide(t):
        rows = slice(t * ROW_TILE, (t + 1) * ROW_TILE)
        hb = _modulated_norm(x_ref[0, rows, :], ng, mod_ref)
        q_s[rows, :] = _dot(hb, win_ref[0, :, 0:GLA_QD]) * (GLA_DK ** -0.5)
        k_s[rows, :] = _dot(hb, win_ref[0, :, GLA_QD:2 * GLA_QD])
        v_s[rows, :] = _dot(hb, win_ref[0, :, 2 * GLA_QD:2 * GLA_QD + GLA_VD]).astype(BF16)
        gate_s[rows, :] = _dot(hb, win_ref[0, :, 2 * GLA_QD + GLA_VD:])
        low = _dot(hb, wa1_ref[0]).astype(BF16)
        return _dot(low, wa2_ref[0]) + ba_ref[...]

    def decay_sums(t, z):
        rows = slice(t * ROW_TILE, (t + 1) * ROW_TILE)
        logg2 = _log_sigmoid(z) * (math.log2(math.e) / GLA_TAU)
        for d, csum_m in enumerate((prefix_m, suffix_m)):
            parts = jnp.concatenate(_split_top(logg2[:, d * GLA_QD:(d + 1) * GLA_QD]), axis=0).astype(BF16)
            b_s[d, rows, :] = _dot(jnp.concatenate([csum_m, csum_m], axis=1), parts)

    z = project_wide(0)
    for t in range(n_tiles):
        z_next = project_wide(t + 1) if t + 1 < n_tiles else None
        decay_sums(t, z)
        z = z_next

    for s in range(n_seq):
        for d in range(2):
            for h in range(GLA_HEADS):
                state_at(s, d, h)[...] = s0_ref[0, 0, d, h] if has_s0 else jnp.zeros((GLA_DK, GLA_DV), F32)

    ri = lax.broadcasted_iota(jnp.int32, (GLA_CHUNK, GLA_CHUNK), 0)
    ci = lax.broadcasted_iota(jnp.int32, (GLA_CHUNK, GLA_CHUNK), 1)
    masks = (ci <= ri, ci >= ri)

    def scan_step(i, first_visit):
        chains = []
        for s in range(n_seq):
            for d in range(2):
                c = s * n_pos + (i if d == 0 else n_pos - 1 - i)
                rows = pl.ds(pl.multiple_of(c * GLA_CHUNK, GLA_CHUNK), GLA_CHUNK)
                b = b_s[d, rows, :]
                edge = GLA_CHUNK - 1 if d == 0 else 0
                total = b[edge:edge + 1, :]
                k = k_s[rows, :]
                qe = (q_s[rows, :] * jnp.exp2(b)).astype(BF16)
                ke = (k * jnp.exp2(-b)).astype(BF16)
                kdt = (k * jnp.exp2(total - b)).T.astype(BF16)
                dec = jnp.exp2(total)
                for h in range(GLA_HEADS):
                    kc = slice(h * GLA_DK, (h + 1) * GLA_DK)
                    chains.append((s, d, h, rows, slice(h * GLA_DV, (h + 1) * GLA_DV),
                                   qe[:, kc], ke[:, kc], kdt[kc, :], dec[:, kc]))
        scores = [_dot_nt(qe, ke) for (s, d, h, rows, vc, qe, ke, kdt, dec) in chains]
        if not merge_products:
            updates = [_dot(kdt, v_s[rows, vc]) for (s, d, h, rows, vc, qe, ke, kdt, dec) in chains]
        no_state = jnp.zeros((GLA_DK, GLA_DK), BF16)

        def decayed(st, dec):
            dec_col = jnp.broadcast_to(dec, (GLA_DK, GLA_DK)).T
            return jnp.concatenate([dec_col] * (GLA_DV // GLA_DK), axis=1) * st

        for (s, d, h, rows, vc, qe, ke, kdt, dec), sc in zip(chains, scores):
            a = jnp.where(masks[d], sc, 0.0).astype(BF16)
            st_ref = state_at(s, d, h)
            st = st_ref[...]
            lhs = jnp.concatenate([qe, a], axis=1)
            if merge_products:
                lhs = jnp.concatenate([lhs, jnp.concatenate([no_state, kdt], axis=1)], axis=0)
            both = _dot(lhs, jnp.concatenate([st.astype(BF16), v_s[rows, vc]], axis=0))
            if first_visit:
                o_s[rows, vc] = both[0:GLA_CHUNK]
            else:
                o_s[rows, vc] += both[0:GLA_CHUNK]
            if merge_products:
                st_ref[...] = decayed(st, dec) + both[GLA_CHUNK:]
        if not merge_products:
            for (s, d, h, rows, vc, qe, ke, kdt, dec), upd in zip(chains, updates):
                st_ref = state_at(s, d, h)
                st_ref[...] = decayed(st_ref[...], dec) + upd

    def first_half(i, carry):
        scan_step(i, True)
        return carry

    def second_half(i, carry):
        scan_step(i, False)
        return carry

    lax.fori_loop(0, n_pos // 2, first_half, 0)
    lax.fori_loop(n_pos // 2, n_pos, second_half, 0)

    on = on_ref[...]
    res_gate = mod_ref[0, 0, :, 2 * D_MODEL:]

    def gated(t):
        rows = slice(t * ROW_TILE, (t + 1) * ROW_TILE)
        return (_head_rms(o_s[rows, :], on, GLA_DV) * _silu(gate_s[rows, :])).astype(BF16)

    og = gated(0)
    for t in range(n_tiles):
        rows = slice(t * ROW_TILE, (t + 1) * ROW_TILE)
        og_next = gated(t + 1) if t + 1 < n_tiles else None
        y_ref[0, rows, :] = x_ref[0, rows, :] + res_gate * _dot(og, wout_ref[0])
        og = og_next


def _resident(shape, layer=None):
    if layer is None:
        return pl.BlockSpec(shape, lambda b: (0,) * len(shape), pipeline_mode=pl.Buffered(1))
    return pl.BlockSpec((1,) + shape, lambda b: (layer,) + (0,) * len(shape), pipeline_mode=pl.Buffered(1))


def _gla_layer(x, mods, layer, mod_row, ng, win, wa1, wa2, ba, on, wout, state_in, gla_idx, n_gla, states_so_far):
    batch, seq, _ = x.shape
    n_seq = GROUP_ROWS // seq
    n_groups = batch // n_seq
    has_s0 = state_in is not None
    emit_state = not has_s0
    assert not has_s0 or n_seq == 1
    in_specs = [
        pl.BlockSpec((1, GROUP_ROWS, D_MODEL), lambda b: (b, 0, 0)),
        pl.BlockSpec((1, 1, 1, 3 * D_MODEL), lambda b: (layer, mod_row(b), 0, 0)),
        _resident((1, D_MODEL)),
        _resident((D_MODEL, 2 * GLA_QD + 2 * GLA_VD), gla_idx),
        _resident((D_MODEL, RANK_PAD), gla_idx),
        _resident((RANK_PAD, 2 * GLA_QD), gla_idx),
        _resident((1, 2 * GLA_QD)),
        _resident((1, GLA_DV)),
        _resident((GLA_VD, D_MODEL), gla_idx),
    ]
    args = [x.reshape(n_groups, GROUP_ROWS, D_MODEL), mods, ng, win, wa1, wa2, ba, on, wout]
    if has_s0:
        in_specs.append(pl.BlockSpec((1, 1, 2, GLA_HEADS, GLA_DK, GLA_DV),
                                     lambda b: (b, gla_idx, 0, 0, 0, 0)))
        args.append(state_in)
    aliases = {}
    state_alias = emit_state and states_so_far is not None
    if state_alias:
        aliases[len(args)] = 1
        in_specs.append(pl.BlockSpec(memory_space=pl.ANY))
        args.append(states_so_far)
    out_specs = [pl.BlockSpec((1, GROUP_ROWS, D_MODEL), lambda b: (b, 0, 0))]
    out_shape = [jax.ShapeDtypeStruct((n_groups, GROUP_ROWS, D_MODEL), F32)]
    if emit_state:
        out_specs.append(pl.BlockSpec((n_seq, 1, 2, GLA_HEADS, GLA_DK, GLA_DV), lambda b: (b, gla_idx, 0, 0, 0, 0)))
        out_shape.append(jax.ShapeDtypeStruct((batch, n_gla, 2, GLA_HEADS, GLA_DK, GLA_DV), F32))
    scratch = [
        pltpu.VMEM((GROUP_ROWS, GLA_QD), F32),
        pltpu.VMEM((GROUP_ROWS, GLA_QD), F32),
        pltpu.VMEM((2, GROUP_ROWS, GLA_QD), F32),
        pltpu.VMEM((GROUP_ROWS, GLA_VD), BF16),
        pltpu.VMEM((GROUP_ROWS, GLA_VD), F32),
        pltpu.VMEM((GROUP_ROWS, GLA_VD), F32),
    ]
    if not emit_state:
        scratch.append(pltpu.VMEM((2, GLA_HEADS, GLA_DK, GLA_DV), F32))
    outs = pl.pallas_call(
        functools.partial(_gla_kernel, seq=seq, has_s0=has_s0, emit_state=emit_state, state_alias=state_alias),
        grid=(n_groups,),
        in_specs=in_specs,
        out_specs=out_specs,
        out_shape=out_shape,
        scratch_shapes=scratch,
        input_output_aliases=aliases,
        compiler_params=pltpu.CompilerParams(dimension_semantics=("arbitrary",),
                                             vmem_limit_bytes=V7X_VMEM_LIMIT_BYTES),
        name=f"gla_layer_seq{seq}",
    )(*args)
    y = outs[0].reshape(batch, seq, D_MODEL)
    return (y, outs[1]) if emit_state else (y, None)


def _rope_swap(x):
    lane = lax.broadcasted_iota(jnp.int32, x.shape, 1)
    quarter = HEAD_DIM // 4
    first = (lane % (2 * quarter)) < quarter
    return jnp.where(first, pltpu.roll(x, HEAD_DIM - quarter, 1), pltpu.roll(x, quarter, 1))


def _att_kernel(*refs, seq, latent, cache_alias):
    n_tiles = seq // ROW_TILE
    n_keys = seq + (refs[9].shape[2] if latent else 0)
    it = iter(refs)
    x_ref, mod_ref, ng_ref, win_ref, qn_ref, kn_ref, wout_ref = (next(it) for _ in range(7))
    if latent:
        cos_ref, sin_ref, ck_ref, cv_ref = (next(it) for _ in range(4))
    if cache_alias:
        next(it), next(it)
    y_ref = next(it)
    if not latent:
        kout_ref, vout_ref = next(it), next(it)
    q_s, k_s, v_s, gate_s, ao_s = (next(it) for _ in range(5))

    ng = ng_ref[...]
    qn = qn_ref[...]
    kn = kn_ref[...]
    exp2_scale = (HEAD_DIM ** -0.5) * math.log2(math.e)

    def project(t, carry):
        rows = pl.ds(pl.multiple_of(t * ROW_TILE, ROW_TILE), ROW_TILE)
        hb = _modulated_norm(x_ref[0, rows, :], ng, mod_ref)
        q = _head_rms(_dot(hb, win_ref[0, :, 0:ATT_QD]), qn, HEAD_DIM)
        k = _head_rms(_dot(hb, win_ref[0, :, ATT_QD:ATT_QD + ATT_KD]), kn, HEAD_DIM)
        v = _dot(hb, win_ref[0, :, ATT_QD + ATT_KD:ATT_QD + 2 * ATT_KD])
        gate_s[rows, :] = _dot(hb, win_ref[0, :, ATT_QD + 2 * ATT_KD:])
        if latent:
            cos = cos_ref[rows, :]
            sin = sin_ref[rows, :]
            q = jnp.concatenate(
                [q[:, h * HEAD_DIM:(h + 1) * HEAD_DIM] * cos + _rope_swap(q[:, h * HEAD_DIM:(h + 1) * HEAD_DIM]) * sin
                 for h in range(ATT_HEADS)], axis=-1)
            k = jnp.concatenate(
                [k[:, h * HEAD_DIM:(h + 1) * HEAD_DIM] * cos + _rope_swap(k[:, h * HEAD_DIM:(h + 1) * HEAD_DIM]) * sin
                 for h in range(ATT_KV_HEADS)], axis=-1)
        else:
            for h in range(ATT_KV_HEADS):
                kout_ref[0, 0, rows, h, :] = k[:, h * HEAD_DIM:(h + 1) * HEAD_DIM]
                vout_ref[0, 0, rows, h, :] = v[:, h * HEAD_DIM:(h + 1) * HEAD_DIM]
        q_s[rows, :] = (q * exp2_scale).astype(BF16)
        k_s[rows, :] = k.astype(BF16)
        for h in range(ATT_KV_HEADS):
            v_s[rows, 2 * h * HEAD_DIM:(2 * h + 1) * HEAD_DIM] = v[:, h * HEAD_DIM:(h + 1) * HEAD_DIM].astype(BF16)
        return carry

    for h in range(ATT_KV_HEADS):
        v_s[:, (2 * h + 1) * HEAD_DIM:(2 * h + 2) * HEAD_DIM] = jnp.ones((n_keys, HEAD_DIM), BF16)
    lax.fori_loop(0, n_tiles, project, 0)
    if latent:
        for h in range(ATT_KV_HEADS):
            k_s[seq:n_keys, h * HEAD_DIM:(h + 1) * HEAD_DIM] = ck_ref[0, 0, :, h, :].astype(BF16)
            v_s[seq:n_keys, 2 * h * HEAD_DIM:(2 * h + 1) * HEAD_DIM] = cv_ref[0, 0, :, h, :].astype(BF16)

    def attend(t, carry):
        rows = pl.ds(pl.multiple_of(t * ROW_TILE, ROW_TILE), ROW_TILE)

        def scores(h):
            kc = slice((h // ATT_GROUP) * HEAD_DIM, (h // ATT_GROUP + 1) * HEAD_DIM)
            return _dot_nt(q_s[rows, h * HEAD_DIM:(h + 1) * HEAD_DIM], k_s[:, kc])

        s = scores(0)
        for h in range(ATT_HEADS):
            s_next = scores(h + 1) if h + 1 < ATT_HEADS else None
            vc = slice((h // ATT_GROUP) * 2 * HEAD_DIM, (h // ATT_GROUP + 1) * 2 * HEAD_DIM)
            p = jnp.exp2(s - jnp.max(s, axis=-1, keepdims=True))
            o = _dot(p.astype(BF16), v_s[:, vc])
            ao_s[rows, h * HEAD_DIM:(h + 1) * HEAD_DIM] = o[:, 0:HEAD_DIM] / o[:, HEAD_DIM:]
            s = s_next
        return carry

    lax.fori_loop(0, n_tiles, attend, 0)

    res_gate = mod_ref[0, 0, :, 2 * D_MODEL:]

    def finish(t, carry):
        rows = pl.ds(pl.multiple_of(t * ROW_TILE, ROW_TILE), ROW_TILE)
        y = _dot((ao_s[rows, :] * _silu(gate_s[rows, :])).astype(BF16), wout_ref[0])
        y_ref[0, rows, :] = x_ref[0, rows, :] + res_gate * y
        return carry

    lax.fori_loop(0, n_tiles, finish, 0)


def _att_layer(x, mods, layer, mod_row, ng, win, qn, kn, wout, latent_inputs, att_idx, n_att, caches_so_far):
    batch, seq, _ = x.shape
    latent = latent_inputs is not None
    in_specs = [
        pl.BlockSpec((1, seq, D_MODEL), lambda b: (b, 0, 0)),
        pl.BlockSpec((1, 1, 1, 3 * D_MODEL), lambda b: (layer, mod_row(b), 0, 0)),
        _resident((1, D_MODEL)),
        _resident((D_MODEL, 2 * ATT_QD + 2 * ATT_KD), att_idx),
        _resident((1, HEAD_DIM)),
        _resident((1, HEAD_DIM)),
        _resident((ATT_QD, D_MODEL), att_idx),
    ]
    args = [x, mods, ng, win, qn, kn, wout]
    n_keys = seq
    if latent:
        cos, sin, cache_k, cache_v = latent_inputs
        past = cache_k.shape[2]
        n_keys = seq + past
        in_specs += [
            _resident((seq, HEAD_DIM)),
            _resident((seq, HEAD_DIM)),
            pl.BlockSpec((1, 1, past, ATT_KV_HEADS, HEAD_DIM), lambda b: (b, att_idx, 0, 0, 0)),
            pl.BlockSpec((1, 1, past, ATT_KV_HEADS, HEAD_DIM), lambda b: (b, att_idx, 0, 0, 0)),
        ]
        args += [cos, sin, cache_k, cache_v]
    aliases = {}
    cache_alias = (not latent) and caches_so_far is not None
    if cache_alias:
        aliases = {len(args): 1, len(args) + 1: 2}
        in_specs += [pl.BlockSpec(memory_space=pl.ANY)] * 2
        args += list(caches_so_far)
    out_specs = [pl.BlockSpec((1, seq, D_MODEL), lambda b: (b, 0, 0))]
    out_shape = [jax.ShapeDtypeStruct((batch, seq, D_MODEL), F32)]
    if not latent:
        out_specs += [pl.BlockSpec((1, 1, seq, ATT_KV_HEADS, HEAD_DIM), lambda b: (b, att_idx, 0, 0, 0))] * 2
        out_shape += [jax.ShapeDtypeStruct((batch, n_att, seq, ATT_KV_HEADS, HEAD_DIM), F32)] * 2
    scratch = [
        pltpu.VMEM((seq, ATT_QD), BF16),
        pltpu.VMEM((n_keys, ATT_KD), BF16),
        pltpu.VMEM((n_keys, 2 * ATT_KD), BF16),
        pltpu.VMEM((seq, ATT_QD), F32),
        pltpu.VMEM((seq, ATT_QD), F32),
    ]
    outs = pl.pallas_call(
        functools.partial(_att_kernel, seq=seq, latent=latent, cache_alias=cache_alias),
        grid=(batch,),
        in_specs=in_specs,
        out_specs=out_specs,
        out_shape=out_shape,
        scratch_shapes=scratch,
        input_output_aliases=aliases,
        compiler_params=pltpu.CompilerParams(dimension_semantics=("arbitrary",),
                                             vmem_limit_bytes=V7X_VMEM_LIMIT_BYTES),
        name=f"att_layer_seq{seq}",
    )(*args)
    return outs


def _rope_tables(seq):
    half = HEAD_DIM // 2
    nf = half // 2
    pos = np.arange(seq)
    freqs = np.float32(ROPE_THETA) ** (-np.arange(nf, dtype=np.float32) / np.float32(nf))
    ang_row = (pos // GRID_W).astype(np.float32)[:, None] * freqs[None, :]
    ang_col = (pos % GRID_W).astype(np.float32)[:, None] * freqs[None, :]
    cos = np.concatenate([np.cos(ang_row)] * 2 + [np.cos(ang_col)] * 2, axis=-1)
    sin = np.concatenate([-np.sin(ang_row), np.sin(ang_row), -np.sin(ang_col), np.sin(ang_col)], axis=-1)
    return jnp.asarray(cos, F32), jnp.asarray(sin, F32)


def kernel(x_prompt, x_sample, state_gla, cache_k, cache_v, c, c_ctx, norm_g, w_ada, b_ada,
           gla_w_in, gla_wa1, gla_wa2, gla_ba, gla_onorm, gla_w_out,
           att_w_in, att_qnorm, att_knorm, att_w_out):
    n_dec = x_sample.shape[0]
    assert 1 + n_dec <= MOD_ROWS
    assert x_prompt.shape[1] % ROW_TILE == 0 and x_sample.shape[1] % ROW_TILE == 0

    cvec = jnp.concatenate([c_ctx[None], c, jnp.zeros((MOD_ROWS - 1 - n_dec, D_MODEL), F32)], axis=0)
    mods = _modulations(cvec, w_ada, b_ada)
    ctx_row = lambda b: 0
    dec_row = lambda b: b + 1

    cos, sin = _rope_tables(x_sample.shape[1])
    n_gla, n_att = gla_w_in.shape[0], att_w_in.shape[0]

    gla_win, gla_wout = gla_w_in.astype(BF16), gla_w_out.astype(BF16)
    att_win, att_wout = att_w_in.astype(BF16), att_w_out.astype(BF16)
    wa1 = jnp.concatenate([gla_wa1[:, 0], gla_wa1[:, 1],
                           jnp.zeros((n_gla, D_MODEL, RANK_PAD - 2 * GLA_RANK), F32)], axis=-1).astype(BF16)
    zeros_qd = jnp.zeros((n_gla, GLA_RANK, GLA_QD), F32)
    wa2 = jnp.concatenate([jnp.concatenate([gla_wa2[:, 0], zeros_qd], axis=-1),
                           jnp.concatenate([zeros_qd, gla_wa2[:, 1]], axis=-1),
                           jnp.zeros((n_gla, RANK_PAD - 2 * GLA_RANK, 2 * GLA_QD), F32)], axis=1).astype(BF16)

    xp, xs = x_prompt, x_sample
    states, caches = None, None
    for l in range(DEPTH):
        i = l // 2
        ng = norm_g[l].reshape(1, D_MODEL)
        if l % 2 == 0:
            ba = gla_ba[i].reshape(1, 2 * GLA_QD)
            on = gla_onorm[i].reshape(1, GLA_DV)
            common = (ng, gla_win, wa1, wa2, ba, on, gla_wout)
            xp, states = _gla_layer(xp, mods, l, ctx_row, *common, None, i, n_gla, states)
            xs, _ = _gla_layer(xs, mods, l, dec_row, *common, state_gla, i, n_gla, None)
        else:
            qn = att_qnorm[i].reshape(1, HEAD_DIM)
            kn = att_knorm[i].reshape(1, HEAD_DIM)
            xp, *caches = _att_layer(xp, mods, l, ctx_row, ng, att_win, qn, kn, att_wout, None, i, n_att, caches)
            (xs,) = _att_layer(xs, mods, l, dec_row, ng, att_win, qn, kn, att_wout, (cos, sin, cache_k, cache_v), i,
                               n_att, None)
    return (xp, xs, states, caches[0], caches[1])
```

```python
import functools
import math

import jax
import jax.numpy as jnp
import numpy as np
from jax import lax
from jax.experimental import pallas as pl
from jax.experimental.pallas import tpu as pltpu

D_MODEL = 1024
DEPTH = 4
GRID_W = 64
GLA_HEADS = 4
GLA_DK = 128
GLA_DV = 256
GLA_RANK = 16
GLA_TAU = 16.0
GLA_CHUNK = 64
GLA_QD = GLA_HEADS * GLA_DK
GLA_VD = GLA_HEADS * GLA_DV
HEAD_DIM = 128
ATT_HEADS = 8
ATT_KV_HEADS = 2
ATT_GROUP = ATT_HEADS // ATT_KV_HEADS
ATT_QD = ATT_HEADS * HEAD_DIM
ATT_KD = ATT_KV_HEADS * HEAD_DIM
ROPE_THETA = 10000.0
EPS = 1e-6

ROW_TILE = 256
GROUP_ROWS = 1024
MOD_ROWS = 8
RANK_PAD = 128
V7X_VMEM_LIMIT_BYTES = 60 * 1024 * 1024

F32 = jnp.float32
BF16 = jnp.bfloat16
_NT = (((1,), (1,)), ((), ()))
_TN = (((0,), (0,)), ((), ()))


def _dot(a, b):
    return jnp.dot(a, b, preferred_element_type=F32)


def _dot_nt(a, b):
    return lax.dot_general(a, b, _NT, preferred_element_type=F32)


def _dot_tn(a, b):
    return lax.dot_general(a, b, _TN, preferred_element_type=F32)


def _silu(x):
    return x * (1.0 / (1.0 + jnp.exp(-x)))


def _log_sigmoid(z):
    return jnp.minimum(z, 0.0) - jnp.log(1.0 + jnp.exp(-jnp.abs(z)))


def _split_top(x):
    top = pltpu.bitcast(pltpu.bitcast(x, jnp.uint32) & jnp.uint32(0xFFFF0000), F32)
    return top, x - top


def _modulated_norm(x, ng, mod_ref):
    shift = mod_ref[0, 0, :, 0:D_MODEL]
    scale = mod_ref[0, 0, :, D_MODEL:2 * D_MODEL]
    ms = jnp.mean(x * x, axis=-1, keepdims=True)
    return ((x * lax.rsqrt(ms + EPS)) * (ng * (1.0 + scale)) + shift).astype(BF16)


def _head_rms(x, g, width):
    outs = []
    for h in range(x.shape[-1] // width):
        xh = x[:, h * width:(h + 1) * width]
        ms = jnp.mean(xh * xh, axis=-1, keepdims=True)
        outs.append((xh * lax.rsqrt(ms + EPS)) * g)
    return jnp.concatenate(outs, axis=-1)


def _mod_kernel(c_ref, w_ref, b_ref, o_ref):
    acc = _dot(jnp.concatenate(_split_top(_silu(c_ref[...])), axis=0).astype(BF16), w_ref[0].astype(BF16))
    o_ref[0, :, 0, :] = acc[0:MOD_ROWS] + acc[MOD_ROWS:] + b_ref[pl.ds(pl.program_id(0), 1), :]


def _modulations(cvec, w_ada, b_ada):
    tn = D_MODEL
    n_tiles = 3 * D_MODEL // tn
    return pl.pallas_call(
        _mod_kernel,
        grid=(DEPTH, n_tiles),
        in_specs=[
            pl.BlockSpec((MOD_ROWS, D_MODEL), lambda l, j: (0, 0)),
            pl.BlockSpec((1, D_MODEL, tn), lambda l, j: (l, 0, j)),
            pl.BlockSpec((DEPTH, tn), lambda l, j: (0, j)),
        ],
        out_specs=pl.BlockSpec((1, MOD_ROWS, 1, tn), lambda l, j: (l, 0, 0, j)),
        out_shape=jax.ShapeDtypeStruct((DEPTH, MOD_ROWS, 1, 3 * D_MODEL), F32),
        compiler_params=pltpu.CompilerParams(dimension_semantics=("arbitrary", "arbitrary")),
        name="adaln_modulation",
    )(cvec, w_ada, b_ada)


def _chunk_sum_matrices():
    r = lax.broadcasted_iota(jnp.int32, (ROW_TILE, ROW_TILE), 0)
    c = lax.broadcasted_iota(jnp.int32, (ROW_TILE, ROW_TILE), 1)
    same = (r // GLA_CHUNK) == (c // GLA_CHUNK)
    prefix = jnp.where(same & (c <= r), 1.0, 0.0).astype(BF16)
    suffix = jnp.where(same & (c >= r), 1.0, 0.0).astype(BF16)
    return prefix, suffix


def _gla_kernel(*refs, seq, has_s0, emit_state, state_alias, convert_layer):
    n_seq = GROUP_ROWS // seq
    n_tiles = GROUP_ROWS // ROW_TILE
    n_pos = seq // GLA_CHUNK
    merge_products = n_seq > 1
    it = iter(refs)
    x_ref, mod_ref, ng_ref, win_ref, wa1_ref, wa2_ref, ba_ref, on_ref, wout_ref = (next(it) for _ in range(9))
    s0_ref = next(it) if has_s0 else None
    if state_alias:
        next(it)
    convert = convert_layer is not None
    conv_src = [next(it), next(it)] if convert else None
    y_ref = next(it)
    sout_ref = next(it) if emit_state else None
    conv_dst = [next(it), next(it)] if convert else None
    q_s, k_s, b_s, v_s, gate_s, o_s = (next(it) for _ in range(6))
    st_s = None if emit_state else next(it)
    if convert:
        conv_f32, conv_bf16, conv_rd, conv_wr = ([next(it), next(it)] for _ in range(4))
        slab_rows = conv_f32[0].shape[1]

        def slab(i):
            return pl.ds((pl.program_id(0) * n_pos + i) * slab_rows, slab_rows)

        def slab_read(w, i, buf):
            return pltpu.make_async_copy(conv_src[w].at[convert_layer, slab(i), :], conv_f32[w].at[buf],
                                         conv_rd[w].at[buf])

        def slab_write(w, i, buf):
            return pltpu.make_async_copy(conv_bf16[w].at[buf], conv_dst[w].at[0, slab(i), :], conv_wr[w].at[buf])

        def convert_slab(i):
            buf = lax.rem(i, 2)
            for w in range(2):
                @pl.when(i + 1 < n_pos)
                def _():
                    slab_read(w, i + 1, 1 - buf).start()

                slab_read(w, i, buf).wait()

                @pl.when(i >= 2)
                def _():
                    slab_write(w, i - 2, buf).wait()

                conv_bf16[w][buf] = conv_f32[w][buf].astype(BF16)
                slab_write(w, i, buf).start()

    def state_at(s, d, h):
        return sout_ref.at[s, 0, d, h] if emit_state else st_s.at[d, h]

    ng = ng_ref[...]
    prefix_m, suffix_m = _chunk_sum_matrices()

    def project_wide(t):
        rows = slice(t * ROW_TILE, (t + 1) * ROW_TILE)
        hb = _modulated_norm(x_ref[0, rows, :], ng, mod_ref)
        q_s[rows, :] = _dot(hb, win_ref[0, :, 0:GLA_QD]) * (GLA_DK ** -0.5)
        k_s[rows, :] = _dot(hb, win_ref[0, :, GLA_QD:2 * GLA_QD])
        v_s[rows, :] = _dot(hb, win_ref[0, :, 2 * GLA_QD:2 * GLA_QD + GLA_VD]).astype(BF16)
        gate_s[rows, :] = _dot(hb, win_ref[0, :, 2 * GLA_QD + GLA_VD:])
        low = _dot(hb, wa1_ref[0]).astype(BF16)
        return _dot(low, wa2_ref[0]) + ba_ref[...]

    def decay_sums(t, z):
        rows = slice(t * ROW_TILE, (t + 1) * ROW_TILE)
        logg2 = _log_sigmoid(z) * (math.log2(math.e) / GLA_TAU)
        for d, csum_m in enumerate((prefix_m, suffix_m)):
            parts = jnp.concatenate(_split_top(logg2[:, d * GLA_QD:(d + 1) * GLA_QD]), axis=0).astype(BF16)
            b_s[d, rows, :] = _dot(jnp.concatenate([csum_m, csum_m], axis=1), parts)

    z = project_wide(0)
    for t in range(n_tiles):
        z_next = project_wide(t + 1) if t + 1 < n_tiles else None
        decay_sums(t, z)
        z = z_next

    for s in range(n_seq):
        for d in range(2):
            for h in range(GLA_HEADS):
                state_at(s, d, h)[...] = s0_ref[0, 0, d, h] if has_s0 else jnp.zeros((GLA_DK, GLA_DV), F32)

    ri = lax.broadcasted_iota(jnp.int32, (GLA_CHUNK, GLA_CHUNK), 0)
    ci = lax.broadcasted_iota(jnp.int32, (GLA_CHUNK, GLA_CHUNK), 1)
    masks = (ci <= ri, ci >= ri)

    def scan_step(i, first_visit):
        if convert:
            convert_slab(i)
        chains = []
        for s in range(n_seq):
            for d in range(2):
                c = s * n_pos + (i if d == 0 else n_pos - 1 - i)
                rows = pl.ds(pl.multiple_of(c * GLA_CHUNK, GLA_CHUNK), GLA_CHUNK)
                b = b_s[d, rows, :]
                edge = GLA_CHUNK - 1 if d == 0 else 0
                total = b[edge:edge + 1, :]
                k = k_s[rows, :]
                qe = (q_s[rows, :] * jnp.exp2(b)).astype(BF16)
                ke = (k * jnp.exp2(-b)).astype(BF16)
                kdt = (k * jnp.exp2(total - b)).T.astype(BF16)
                dec = jnp.exp2(total)
                for h in range(GLA_HEADS):
                    kc = slice(h * GLA_DK, (h + 1) * GLA_DK)
                    chains.append((s, d, h, rows, slice(h * GLA_DV, (h + 1) * GLA_DV),
                                   qe[:, kc], ke[:, kc], kdt[kc, :], dec[:, kc]))
        scores = [_dot_nt(qe, ke) for (s, d, h, rows, vc, qe, ke, kdt, dec) in chains]
        if not merge_products:
            updates = [_dot(kdt, v_s[rows, vc]) for (s, d, h, rows, vc, qe, ke, kdt, dec) in chains]
        no_state = jnp.zeros((GLA_DK, GLA_DK), BF16)

        def decayed(st, dec):
            dec_col = jnp.broadcast_to(dec, (GLA_DK, GLA_DK)).T
            return jnp.concatenate([dec_col] * (GLA_DV // GLA_DK), axis=1) * st

        for (s, d, h, rows, vc, qe, ke, kdt, dec), sc in zip(chains, scores):
            a = jnp.where(masks[d], sc, 0.0).astype(BF16)
            st_ref = state_at(s, d, h)
            st = st_ref[...]
            lhs = jnp.concatenate([qe, a], axis=1)
            if merge_products:
                lhs = jnp.concatenate([lhs, jnp.concatenate([no_state, kdt], axis=1)], axis=0)
            both = _dot(lhs, jnp.concatenate([st.astype(BF16), v_s[rows, vc]], axis=0))
            if first_visit:
                o_s[rows, vc] = both[0:GLA_CHUNK]
            else:
                o_s[rows, vc] += both[0:GLA_CHUNK]
            if merge_products:
                st_ref[...] = decayed(st, dec) + both[GLA_CHUNK:]
        if not merge_products:
            for (s, d, h, rows, vc, qe, ke, kdt, dec), upd in zip(chains, updates):
                st_ref = state_at(s, d, h)
                st_ref[...] = decayed(st_ref[...], dec) + upd

    def first_half(i, carry):
        scan_step(i, True)
        return carry

    def second_half(i, carry):
        scan_step(i, False)
        return carry

    if convert:
        for w in range(2):
            slab_read(w, 0, 0).start()
    lax.fori_loop(0, n_pos // 2, first_half, 0)
    lax.fori_loop(n_pos // 2, n_pos, second_half, 0)
    if convert:
        for w in range(2):
            for i in (n_pos - 2, n_pos - 1):
                slab_write(w, i, i % 2).wait()

    on = on_ref[...]
    res_gate = mod_ref[0, 0, :, 2 * D_MODEL:]

    def gated(t):
        rows = slice(t * ROW_TILE, (t + 1) * ROW_TILE)
        return (_head_rms(o_s[rows, :], on, GLA_DV) * _silu(gate_s[rows, :])).astype(BF16)

    og = gated(0)
    for t in range(n_tiles):
        rows = slice(t * ROW_TILE, (t + 1) * ROW_TILE)
        og_next = gated(t + 1) if t + 1 < n_tiles else None
        y_ref[0, rows, :] = x_ref[0, rows, :] + res_gate * _dot(og, wout_ref[0])
        og = og_next


def _resident(shape, layer=None):
    if layer is None:
        return pl.BlockSpec(shape, lambda b: (0,) * len(shape), pipeline_mode=pl.Buffered(1))
    return pl.BlockSpec((1,) + shape, lambda b: (layer,) + (0,) * len(shape), pipeline_mode=pl.Buffered(1))


def _gla_layer(x, mods, layer, mod_row, ng, win, wa1, wa2, ba, on, wout, state_in, gla_idx, n_gla, states_so_far,
               convert=None):
    batch, seq, _ = x.shape
    n_seq = GROUP_ROWS // seq
    n_groups = batch // n_seq
    has_s0 = state_in is not None
    emit_state = not has_s0
    assert not has_s0 or n_seq == 1
    in_specs = [
        pl.BlockSpec((1, GROUP_ROWS, D_MODEL), lambda b: (b, 0, 0)),
        pl.BlockSpec((1, 1, 1, 3 * D_MODEL), lambda b: (layer, mod_row(b), 0, 0)),
        _resident((1, D_MODEL)),
        _resident((D_MODEL, 2 * GLA_QD + 2 * GLA_VD), gla_idx),
        _resident((D_MODEL, RANK_PAD), gla_idx),
        _resident((RANK_PAD, 2 * GLA_QD), gla_idx),
        _resident((1, 2 * GLA_QD)),
        _resident((1, GLA_DV)),
        _resident((GLA_VD, D_MODEL), gla_idx),
    ]
    args = [x.reshape(n_groups, GROUP_ROWS, D_MODEL), mods, ng, win, wa1, wa2, ba, on, wout]
    if has_s0:
        in_specs.append(pl.BlockSpec((1, 1, 2, GLA_HEADS, GLA_DK, GLA_DV),
                                     lambda b: (b, gla_idx, 0, 0, 0, 0)))
        args.append(state_in)
    aliases = {}
    state_alias = emit_state and states_so_far is not None
    if state_alias:
        aliases[len(args)] = 1
        in_specs.append(pl.BlockSpec(memory_space=pl.ANY))
        args.append(states_so_far)
    convert_layer = None
    if convert is not None:
        conv_weights, convert_layer = convert
        in_specs += [pl.BlockSpec(memory_space=pl.ANY)] * len(conv_weights)
        args += list(conv_weights)
    out_specs = [pl.BlockSpec((1, GROUP_ROWS, D_MODEL), lambda b: (b, 0, 0))]
    out_shape = [jax.ShapeDtypeStruct((n_groups, GROUP_ROWS, D_MODEL), F32)]
    if emit_state:
        out_specs.append(pl.BlockSpec((n_seq, 1, 2, GLA_HEADS, GLA_DK, GLA_DV), lambda b: (b, gla_idx, 0, 0, 0, 0)))
        out_shape.append(jax.ShapeDtypeStruct((batch, n_gla, 2, GLA_HEADS, GLA_DK, GLA_DV), F32))
    scratch = [
        pltpu.VMEM((GROUP_ROWS, GLA_QD), F32),
        pltpu.VMEM((GROUP_ROWS, GLA_QD), F32),
        pltpu.VMEM((2, GROUP_ROWS, GLA_QD), F32),
        pltpu.VMEM((GROUP_ROWS, GLA_VD), BF16),
        pltpu.VMEM((GROUP_ROWS, GLA_VD), F32),
        pltpu.VMEM((GROUP_ROWS, GLA_VD), F32),
    ]
    if not emit_state:
        scratch.append(pltpu.VMEM((2, GLA_HEADS, GLA_DK, GLA_DV), F32))
    if convert is not None:
        n_slabs = n_groups * (seq // GLA_CHUNK)
        widths = [w.shape[2] for w in conv_weights]
        assert all(w.shape[1] % n_slabs == 0 and (w.shape[1] // n_slabs) % 16 == 0 for w in conv_weights)
        slab_rows = [w.shape[1] // n_slabs for w in conv_weights]
        out_specs += [pl.BlockSpec(memory_space=pl.ANY)] * len(conv_weights)
        out_shape += [jax.ShapeDtypeStruct((1,) + w.shape[1:], BF16) for w in conv_weights]
        scratch += [pltpu.VMEM((2, r, n), F32) for r, n in zip(slab_rows, widths)]
        scratch += [pltpu.VMEM((2, r, n), BF16) for r, n in zip(slab_rows, widths)]
        scratch += [pltpu.SemaphoreType.DMA((2,)) for _ in range(2 * len(conv_weights))]
    outs = pl.pallas_call(
        functools.partial(_gla_kernel, seq=seq, has_s0=has_s0, emit_state=emit_state, state_alias=state_alias,
                          convert_layer=convert_layer),
        grid=(n_groups,),
        in_specs=in_specs,
        out_specs=out_specs,
        out_shape=out_shape,
        scratch_shapes=scratch,
        input_output_aliases=aliases,
        compiler_params=pltpu.CompilerParams(dimension_semantics=("arbitrary",),
                                             vmem_limit_bytes=V7X_VMEM_LIMIT_BYTES),
        name=f"gla_layer_seq{seq}",
    )(*args)
    y = outs[0].reshape(batch, seq, D_MODEL)
    if convert is not None:
        return y, list(outs[1:])
    return (y, outs[1]) if emit_state else (y, None)


def _rope_swap(x):
    lane = lax.broadcasted_iota(jnp.int32, x.shape, 1)
    quarter = HEAD_DIM // 4
    first = (lane % (2 * quarter)) < quarter
    return jnp.where(first, pltpu.roll(x, HEAD_DIM - quarter, 1), pltpu.roll(x, quarter, 1))


def _att_kernel(*refs, seq, latent, cache_alias):
    n_tiles = seq // ROW_TILE
    n_keys = seq + (refs[9].shape[2] if latent else 0)
    it = iter(refs)
    x_ref, mod_ref, ng_ref, win_ref, qn_ref, kn_ref, wout_ref = (next(it) for _ in range(7))
    if latent:
        cos_ref, sin_ref, ck_ref, cv_ref = (next(it) for _ in range(4))
    if cache_alias:
        next(it), next(it)
    y_ref = next(it)
    if not latent:
        kout_ref, vout_ref = next(it), next(it)
    q_s, k_s, v_s, gate_s, ao_s = (next(it) for _ in range(5))

    ng = ng_ref[...]
    qn = qn_ref[...]
    kn = kn_ref[...]
    exp2_scale = (HEAD_DIM ** -0.5) * math.log2(math.e)

    def project(t, carry):
        rows = pl.ds(pl.multiple_of(t * ROW_TILE, ROW_TILE), ROW_TILE)
        hb = _modulated_norm(x_ref[0, rows, :], ng, mod_ref)
        q = _head_rms(_dot(hb, win_ref[0, :, 0:ATT_QD]), qn, HEAD_DIM)
        k = _head_rms(_dot(hb, win_ref[0, :, ATT_QD:ATT_QD + ATT_KD]), kn, HEAD_DIM)
        v = _dot(hb, win_ref[0, :, ATT_QD + ATT_KD:ATT_QD + 2 * ATT_KD])
        gate_s[rows, :] = _dot(hb, win_ref[0, :, ATT_QD + 2 * ATT_KD:])
        if latent:
            cos = cos_ref[rows, :]
            sin = sin_ref[rows, :]
            q = jnp.concatenate(
                [q[:, h * HEAD_DIM:(h + 1) * HEAD_DIM] * cos + _rope_swap(q[:, h * HEAD_DIM:(h + 1) * HEAD_DIM]) * sin
                 for h in range(ATT_HEADS)], axis=-1)
            k = jnp.concatenate(
                [k[:, h * HEAD_DIM:(h + 1) * HEAD_DIM] * cos + _rope_swap(k[:, h * HEAD_DIM:(h + 1) * HEAD_DIM]) * sin
                 for h in range(ATT_KV_HEADS)], axis=-1)
        else:
            for h in range(ATT_KV_HEADS):
                kout_ref[0, 0, rows, h, :] = k[:, h * HEAD_DIM:(h + 1) * HEAD_DIM]
                vout_ref[0, 0, rows, h, :] = v[:, h * HEAD_DIM:(h + 1) * HEAD_DIM]
        q_s[rows, :] = (q * exp2_scale).astype(BF16)
        k_s[rows, :] = k.astype(BF16)
        for h in range(ATT_KV_HEADS):
            v_s[rows, 2 * h * HEAD_DIM:(2 * h + 1) * HEAD_DIM] = v[:, h * HEAD_DIM:(h + 1) * HEAD_DIM].astype(BF16)
        return carry

    for h in range(ATT_KV_HEADS):
        v_s[:, (2 * h + 1) * HEAD_DIM:(2 * h + 2) * HEAD_DIM] = jnp.ones((n_keys, HEAD_DIM), BF16)
    lax.fori_loop(0, n_tiles, project, 0)
    if latent:
        for h in range(ATT_KV_HEADS):
            k_s[seq:n_keys, h * HEAD_DIM:(h + 1) * HEAD_DIM] = ck_ref[0, 0, :, h, :].astype(BF16)
            v_s[seq:n_keys, 2 * h * HEAD_DIM:(2 * h + 1) * HEAD_DIM] = cv_ref[0, 0, :, h, :].astype(BF16)

    def attend(t, carry):
        rows = pl.ds(pl.multiple_of(t * ROW_TILE, ROW_TILE), ROW_TILE)

        def scores(h):
            kc = slice((h // ATT_GROUP) * HEAD_DIM, (h // ATT_GROUP + 1) * HEAD_DIM)
            return _dot_nt(q_s[rows, h * HEAD_DIM:(h + 1) * HEAD_DIM], k_s[:, kc])

        s = scores(0)
        for h in range(ATT_HEADS):
            s_next = scores(h + 1) if h + 1 < ATT_HEADS else None
            vc = slice((h // ATT_GROUP) * 2 * HEAD_DIM, (h // ATT_GROUP + 1) * 2 * HEAD_DIM)
            p = jnp.exp2(s - jnp.max(s, axis=-1, keepdims=True))
            o = _dot(p.astype(BF16), v_s[:, vc])
            ao_s[rows, h * HEAD_DIM:(h + 1) * HEAD_DIM] = o[:, 0:HEAD_DIM] / o[:, HEAD_DIM:]
            s = s_next
        return carry

    lax.fori_loop(0, n_tiles, attend, 0)

    res_gate = mod_ref[0, 0, :, 2 * D_MODEL:]

    def finish(t, carry):
        rows = pl.ds(pl.multiple_of(t * ROW_TILE, ROW_TILE), ROW_TILE)
        y = _dot((ao_s[rows, :] * _silu(gate_s[rows, :])).astype(BF16), wout_ref[0])
        y_ref[0, rows, :] = x_ref[0, rows, :] + res_gate * y
        return carry

    lax.fori_loop(0, n_tiles, finish, 0)


def _att_layer(x, mods, layer, mod_row, ng, win, qn, kn, wout, latent_inputs, att_idx, n_att, caches_so_far):
    batch, seq, _ = x.shape
    latent = latent_inputs is not None
    in_specs = [
        pl.BlockSpec((1, seq, D_MODEL), lambda b: (b, 0, 0)),
        pl.BlockSpec((1, 1, 1, 3 * D_MODEL), lambda b: (layer, mod_row(b), 0, 0)),
        _resident((1, D_MODEL)),
        _resident((D_MODEL, 2 * ATT_QD + 2 * ATT_KD), 0),
        _resident((1, HEAD_DIM)),
        _resident((1, HEAD_DIM)),
        _resident((ATT_QD, D_MODEL), 0),
    ]
    args = [x, mods, ng, win, qn, kn, wout]
    n_keys = seq
    if latent:
        cos, sin, cache_k, cache_v = latent_inputs
        past = cache_k.shape[2]
        n_keys = seq + past
        in_specs += [
            _resident((seq, HEAD_DIM)),
            _resident((seq, HEAD_DIM)),
            pl.BlockSpec((1, 1, past, ATT_KV_HEADS, HEAD_DIM), lambda b: (b, att_idx, 0, 0, 0)),
            pl.BlockSpec((1, 1, past, ATT_KV_HEADS, HEAD_DIM), lambda b: (b, att_idx, 0, 0, 0)),
        ]
        args += [cos, sin, cache_k, cache_v]
    aliases = {}
    cache_alias = (not latent) and caches_so_far is not None
    if cache_alias:
        aliases = {len(args): 1, len(args) + 1: 2}
        in_specs += [pl.BlockSpec(memory_space=pl.ANY)] * 2
        args += list(caches_so_far)
    out_specs = [pl.BlockSpec((1, seq, D_MODEL), lambda b: (b, 0, 0))]
    out_shape = [jax.ShapeDtypeStruct((batch, seq, D_MODEL), F32)]
    if not latent:
        out_specs += [pl.BlockSpec((1, 1, seq, ATT_KV_HEADS, HEAD_DIM), lambda b: (b, att_idx, 0, 0, 0))] * 2
        out_shape += [jax.ShapeDtypeStruct((batch, n_att, seq, ATT_KV_HEADS, HEAD_DIM), F32)] * 2
    scratch = [
        pltpu.VMEM((seq, ATT_QD), BF16),
        pltpu.VMEM((n_keys, ATT_KD), BF16),
        pltpu.VMEM((n_keys, 2 * ATT_KD), BF16),
        pltpu.VMEM((seq, ATT_QD), F32),
        pltpu.VMEM((seq, ATT_QD), F32),
    ]
    outs = pl.pallas_call(
        functools.partial(_att_kernel, seq=seq, latent=latent, cache_alias=cache_alias),
        grid=(batch,),
        in_specs=in_specs,
        out_specs=out_specs,
        out_shape=out_shape,
        scratch_shapes=scratch,
        input_output_aliases=aliases,
        compiler_params=pltpu.CompilerParams(dimension_semantics=("arbitrary",),
                                             vmem_limit_bytes=V7X_VMEM_LIMIT_BYTES),
        name=f"att_layer_seq{seq}",
    )(*args)
    return outs


def _rope_tables(seq):
    half = HEAD_DIM // 2
    nf = half // 2
    pos = np.arange(seq)
    freqs = np.float32(ROPE_THETA) ** (-np.arange(nf, dtype=np.float32) / np.float32(nf))
    ang_row = (pos // GRID_W).astype(np.float32)[:, None] * freqs[None, :]
    ang_col = (pos % GRID_W).astype(np.float32)[:, None] * freqs[None, :]
    cos = np.concatenate([np.cos(ang_row)] * 2 + [np.cos(ang_col)] * 2, axis=-1)
    sin = np.concatenate([-np.sin(ang_row), np.sin(ang_row), -np.sin(ang_col), np.sin(ang_col)], axis=-1)
    return jnp.asarray(cos, F32), jnp.asarray(sin, F32)


def kernel(x_prompt, x_sample, state_gla, cache_k, cache_v, c, c_ctx, norm_g, w_ada, b_ada,
           gla_w_in, gla_wa1, gla_wa2, gla_ba, gla_onorm, gla_w_out,
           att_w_in, att_qnorm, att_knorm, att_w_out):
    n_dec = x_sample.shape[0]
    assert 1 + n_dec <= MOD_ROWS
    assert x_prompt.shape[1] % ROW_TILE == 0 and x_sample.shape[1] % ROW_TILE == 0

    cvec = jnp.concatenate([c_ctx[None], c, jnp.zeros((MOD_ROWS - 1 - n_dec, D_MODEL), F32)], axis=0)
    mods = _modulations(cvec, w_ada, b_ada)
    ctx_row = lambda b: 0
    dec_row = lambda b: b + 1

    cos, sin = _rope_tables(x_sample.shape[1])
    n_gla, n_att = gla_w_in.shape[0], att_w_in.shape[0]

    gla_win, gla_wout = gla_w_in.astype(BF16), gla_w_out.astype(BF16)
    wa1 = jnp.concatenate([gla_wa1[:, 0], gla_wa1[:, 1],
                           jnp.zeros((n_gla, D_MODEL, RANK_PAD - 2 * GLA_RANK), F32)], axis=-1).astype(BF16)
    zeros_qd = jnp.zeros((n_gla, GLA_RANK, GLA_QD), F32)
    wa2 = jnp.concatenate([jnp.concatenate([gla_wa2[:, 0], zeros_qd], axis=-1),
                           jnp.concatenate([zeros_qd, gla_wa2[:, 1]], axis=-1),
                           jnp.zeros((n_gla, RANK_PAD - 2 * GLA_RANK, 2 * GLA_QD), F32)], axis=1).astype(BF16)

    xp, xs = x_prompt, x_sample
    states, caches = None, None
    for l in range(DEPTH):
        i = l // 2
        ng = norm_g[l].reshape(1, D_MODEL)
        if l % 2 == 0:
            ba = gla_ba[i].reshape(1, 2 * GLA_QD)
            on = gla_onorm[i].reshape(1, GLA_DV)
            common = (ng, gla_win, wa1, wa2, ba, on, gla_wout)
            xp, states = _gla_layer(xp, mods, l, ctx_row, *common, None, i, n_gla, states)
            xs, (att_win, att_wout) = _gla_layer(xs, mods, l, dec_row, *common, state_gla, i, n_gla, None,
                                                 convert=((att_w_in, att_w_out), i))
        else:
            qn = att_qnorm[i].reshape(1, HEAD_DIM)
            kn = att_knorm[i].reshape(1, HEAD_DIM)
            xp, *caches = _att_layer(xp, mods, l, ctx_row, ng, att_win, qn, kn, att_wout, None, i, n_att, caches)
            (xs,) = _att_layer(xs, mods, l, dec_row, ng, att_win, qn, kn, att_wout, (cos, sin, cache_k, cache_v), i,
                               n_att, None)
    return (xp, xs, states, caches[0], caches[1])
```

```python
import functools
import math

import jax
import jax.numpy as jnp
import numpy as np
from jax import lax
from jax.experimental import pallas as pl
from jax.experimental.pallas import tpu as pltpu

D_MODEL = 1024
DEPTH = 4
GRID_W = 64
GLA_HEADS = 4
GLA_DK = 128
GLA_DV = 256
GLA_RANK = 16
GLA_TAU = 16.0
GLA_CHUNK = 64
GLA_QD = GLA_HEADS * GLA_DK
GLA_VD = GLA_HEADS * GLA_DV
HEAD_DIM = 128
ATT_HEADS = 8
ATT_KV_HEADS = 2
ATT_GROUP = ATT_HEADS // ATT_KV_HEADS
ATT_QD = ATT_HEADS * HEAD_DIM
ATT_KD = ATT_KV_HEADS * HEAD_DIM
ROPE_THETA = 10000.0
EPS = 1e-6

ROW_TILE = 256
GROUP_ROWS = 1024
CONVERT_DEPTH = 4
MOD_ROWS = 8
RANK_PAD = 128
V7X_VMEM_LIMIT_BYTES = 60 * 1024 * 1024

F32 = jnp.float32
BF16 = jnp.bfloat16
_NT = (((1,), (1,)), ((), ()))
_TN = (((0,), (0,)), ((), ()))


def _dot(a, b):
    return jnp.dot(a, b, preferred_element_type=F32)


def _dot_nt(a, b):
    return lax.dot_general(a, b, _NT, preferred_element_type=F32)


def _dot_tn(a, b):
    return lax.dot_general(a, b, _TN, preferred_element_type=F32)


def _silu(x):
    return x * (1.0 / (1.0 + jnp.exp(-x)))


def _log_sigmoid(z):
    return jnp.minimum(z, 0.0) - jnp.log(1.0 + jnp.exp(-jnp.abs(z)))


def _split_top(x):
    top = pltpu.bitcast(pltpu.bitcast(x, jnp.uint32) & jnp.uint32(0xFFFF0000), F32)
    return top, x - top


def _modulated_norm(x, ng, mod_ref):
    shift = mod_ref[0, 0, :, 0:D_MODEL]
    scale = mod_ref[0, 0, :, D_MODEL:2 * D_MODEL]
    ms = jnp.mean(x * x, axis=-1, keepdims=True)
    return ((x * lax.rsqrt(ms + EPS)) * (ng * (1.0 + scale)) + shift).astype(BF16)


def _head_rms(x, g, width):
    outs = []
    for h in range(x.shape[-1] // width):
        xh = x[:, h * width:(h + 1) * width]
        ms = jnp.mean(xh * xh, axis=-1, keepdims=True)
        outs.append((xh * lax.rsqrt(ms + EPS)) * g)
    return jnp.concatenate(outs, axis=-1)


def _mod_kernel(c_ref, w_ref, b_ref, o_ref):
    acc = _dot(jnp.concatenate(_split_top(_silu(c_ref[...])), axis=0).astype(BF16), w_ref[0].astype(BF16))
    o_ref[0, :, 0, :] = acc[0:MOD_ROWS] + acc[MOD_ROWS:] + b_ref[pl.ds(pl.program_id(0), 1), :]


def _modulations(cvec, w_ada, b_ada):
    tn = D_MODEL
    n_tiles = 3 * D_MODEL // tn
    return pl.pallas_call(
        _mod_kernel,
        grid=(DEPTH, n_tiles),
        in_specs=[
            pl.BlockSpec((MOD_ROWS, D_MODEL), lambda l, j: (0, 0)),
            pl.BlockSpec((1, D_MODEL, tn), lambda l, j: (l, 0, j)),
            pl.BlockSpec((DEPTH, tn), lambda l, j: (0, j)),
        ],
        out_specs=pl.BlockSpec((1, MOD_ROWS, 1, tn), lambda l, j: (l, 0, 0, j)),
        out_shape=jax.ShapeDtypeStruct((DEPTH, MOD_ROWS, 1, 3 * D_MODEL), F32),
        compiler_params=pltpu.CompilerParams(dimension_semantics=("arbitrary", "arbitrary")),
        name="adaln_modulation",
    )(cvec, w_ada, b_ada)


def _chunk_sum_matrices():
    r = lax.broadcasted_iota(jnp.int32, (ROW_TILE, ROW_TILE), 0)
    c = lax.broadcasted_iota(jnp.int32, (ROW_TILE, ROW_TILE), 1)
    same = (r // GLA_CHUNK) == (c // GLA_CHUNK)
    prefix = jnp.where(same & (c <= r), 1.0, 0.0).astype(BF16)
    suffix = jnp.where(same & (c >= r), 1.0, 0.0).astype(BF16)
    return prefix, suffix


def _gla_kernel(*refs, seq, has_s0, emit_state, state_alias, convert_layer):
    n_seq = GROUP_ROWS // seq
    n_tiles = GROUP_ROWS // ROW_TILE
    n_pos = seq // GLA_CHUNK
    merge_products = n_seq > 1
    it = iter(refs)
    x_ref, mod_ref, ng_ref, win_ref, wa1_ref, wa2_ref, ba_ref, on_ref, wout_ref = (next(it) for _ in range(9))
    s0_ref = next(it) if has_s0 else None
    if state_alias:
        next(it)
    convert = convert_layer is not None
    conv_src = [next(it), next(it)] if convert else None
    y_ref = next(it)
    sout_ref = next(it) if emit_state else None
    conv_dst = [next(it), next(it)] if convert else None
    q_s, k_s, b_s, v_s, gate_s, o_s = (next(it) for _ in range(6))
    st_s = None if emit_state else next(it)
    if convert:
        conv_f32, conv_bf16, conv_rd, conv_wr = ([next(it), next(it)] for _ in range(4))
        depth, slab_rows = conv_f32[0].shape[0:2]

        def slab(i):
            return pl.ds((pl.program_id(0) * n_pos + i) * slab_rows, slab_rows)

        def slab_read(w, i):
            buf = i % depth if isinstance(i, int) else lax.rem(i, depth)
            return pltpu.make_async_copy(conv_src[w].at[convert_layer, slab(i), :], conv_f32[w].at[buf],
                                         conv_rd[w].at[buf])

        def slab_write(w, i):
            buf = i % depth if isinstance(i, int) else lax.rem(i, depth)
            return pltpu.make_async_copy(conv_bf16[w].at[buf], conv_dst[w].at[0, slab(i), :], conv_wr[w].at[buf])

        for w in range(2):
            for i in range(depth - 1):
                slab_read(w, i).start()

        def convert_slab(i):
            buf = lax.rem(i, depth)
            for w in range(2):
                @pl.when(i + depth - 1 < n_pos)
                def _():
                    slab_read(w, i + depth - 1).start()

                slab_read(w, i).wait()

                @pl.when(i >= depth)
                def _():
                    slab_write(w, i - depth).wait()

                conv_bf16[w][buf] = conv_f32[w][buf].astype(BF16)
                slab_write(w, i).start()

    def state_at(s, d, h):
        return sout_ref.at[s, 0, d, h] if emit_state else st_s.at[d, h]

    ng = ng_ref[...]
    prefix_m, suffix_m = _chunk_sum_matrices()

    def project_wide(t):
        rows = slice(t * ROW_TILE, (t + 1) * ROW_TILE)
        hb = _modulated_norm(x_ref[0, rows, :], ng, mod_ref)
        q_s[rows, :] = _dot(hb, win_ref[0, :, 0:GLA_QD]) * (GLA_DK ** -0.5)
        k_s[rows, :] = _dot(hb, win_ref[0, :, GLA_QD:2 * GLA_QD])
        v_s[rows, :] = _dot(hb, win_ref[0, :, 2 * GLA_QD:2 * GLA_QD + GLA_VD]).astype(BF16)
        gate_s[rows, :] = _dot(hb, win_ref[0, :, 2 * GLA_QD + GLA_VD:])
        low = _dot(hb, wa1_ref[0]).astype(BF16)
        return _dot(low, wa2_ref[0]) + ba_ref[...]

    def decay_sums(t, z):
        rows = slice(t * ROW_TILE, (t + 1) * ROW_TILE)
        logg2 = _log_sigmoid(z) * (math.log2(math.e) / GLA_TAU)
        for d, csum_m in enumerate((prefix_m, suffix_m)):
            parts = jnp.concatenate(_split_top(logg2[:, d * GLA_QD:(d + 1) * GLA_QD]), axis=0).astype(BF16)
            b_s[d, rows, :] = _dot(jnp.concatenate([csum_m, csum_m], axis=1), parts)

    z = project_wide(0)
    for t in range(n_tiles):
        z_next = project_wide(t + 1) if t + 1 < n_tiles else None
        decay_sums(t, z)
        z = z_next

    for s in range(n_seq):
        for d in range(2):
            for h in range(GLA_HEADS):
                state_at(s, d, h)[...] = s0_ref[0, 0, d, h] if has_s0 else jnp.zeros((GLA_DK, GLA_DV), F32)

    ri = lax.broadcasted_iota(jnp.int32, (GLA_CHUNK, GLA_CHUNK), 0)
    ci = lax.broadcasted_iota(jnp.int32, (GLA_CHUNK, GLA_CHUNK), 1)
    masks = (ci <= ri, ci >= ri)

    def scan_step(i, first_visit):
        if convert:
            convert_slab(i)
        chains = []
        for s in range(n_seq):
            for d in range(2):
                c = s * n_pos + (i if d == 0 else n_pos - 1 - i)
                rows = pl.ds(pl.multiple_of(c * GLA_CHUNK, GLA_CHUNK), GLA_CHUNK)
                b = b_s[d, rows, :]
                edge = GLA_CHUNK - 1 if d == 0 else 0
                total = b[edge:edge + 1, :]
                k = k_s[rows, :]
                qe = (q_s[rows, :] * jnp.exp2(b)).astype(BF16)
                ke = (k * jnp.exp2(-b)).astype(BF16)
                kdt = (k * jnp.exp2(total - b)).T.astype(BF16)
                dec = jnp.exp2(total)
                for h in range(GLA_HEADS):
                    kc = slice(h * GLA_DK, (h + 1) * GLA_DK)
                    chains.append((s, d, h, rows, slice(h * GLA_DV, (h + 1) * GLA_DV),
                                   qe[:, kc], ke[:, kc], kdt[kc, :], dec[:, kc]))
        scores = [_dot_nt(qe, ke) for (s, d, h, rows, vc, qe, ke, kdt, dec) in chains]
        if not merge_products:
            updates = [_dot(kdt, v_s[rows, vc]) for (s, d, h, rows, vc, qe, ke, kdt, dec) in chains]
        no_state = jnp.zeros((GLA_DK, GLA_DK), BF16)

        def decayed(st, dec):
            dec_col = jnp.broadcast_to(dec, (GLA_DK, GLA_DK)).T
            return jnp.concatenate([dec_col] * (GLA_DV // GLA_DK), axis=1) * st

        for (s, d, h, rows, vc, qe, ke, kdt, dec), sc in zip(chains, scores):
            a = jnp.where(masks[d], sc, 0.0).astype(BF16)
            st_ref = state_at(s, d, h)
            st = st_ref[...]
            lhs = jnp.concatenate([qe, a], axis=1)
            if merge_products:
                lhs = jnp.concatenate([lhs, jnp.concatenate([no_state, kdt], axis=1)], axis=0)
            both = _dot(lhs, jnp.concatenate([st.astype(BF16), v_s[rows, vc]], axis=0))
            if first_visit:
                o_s[rows, vc] = both[0:GLA_CHUNK]
            else:
                o_s[rows, vc] += both[0:GLA_CHUNK]
            if merge_products:
                st_ref[...] = decayed(st, dec) + both[GLA_CHUNK:]
        if not merge_products:
            for (s, d, h, rows, vc, qe, ke, kdt, dec), upd in zip(chains, updates):
                st_ref = state_at(s, d, h)
                st_ref[...] = decayed(st_ref[...], dec) + upd

    def first_half(i, carry):
        scan_step(i, True)
        return carry

    def second_half(i, carry):
        scan_step(i, False)
        return carry

    lax.fori_loop(0, n_pos // 2, first_half, 0)
    lax.fori_loop(n_pos // 2, n_pos, second_half, 0)
    if convert:
        for w in range(2):
            for i in range(n_pos - depth, n_pos):
                slab_write(w, i).wait()

    on = on_ref[...]
    res_gate = mod_ref[0, 0, :, 2 * D_MODEL:]

    def gated(t):
        rows = slice(t * ROW_TILE, (t + 1) * ROW_TILE)
        return (_head_rms(o_s[rows, :], on, GLA_DV) * _silu(gate_s[rows, :])).astype(BF16)

    og = gated(0)
    for t in range(n_tiles):
        rows = slice(t * ROW_TILE, (t + 1) * ROW_TILE)
        og_next = gated(t + 1) if t + 1 < n_tiles else None
        y_ref[0, rows, :] = x_ref[0, rows, :] + res_gate * _dot(og, wout_ref[0])
        og = og_next


def _resident(shape, layer=None):
    if layer is None:
        return pl.BlockSpec(shape, lambda b: (0,) * len(shape), pipeline_mode=pl.Buffered(1))
    return pl.BlockSpec((1,) + shape, lambda b: (layer,) + (0,) * len(shape), pipeline_mode=pl.Buffered(1))


def _gla_layer(x, mods, layer, mod_row, ng, win, wa1, wa2, ba, on, wout, state_in, gla_idx, n_gla, states_so_far,
               convert=None):
    batch, seq, _ = x.shape
    n_seq = GROUP_ROWS // seq
    n_groups = batch // n_seq
    has_s0 = state_in is not None
    emit_state = not has_s0
    assert not has_s0 or n_seq == 1
    in_specs = [
        pl.BlockSpec((1, GROUP_ROWS, D_MODEL), lambda b: (b, 0, 0)),
        pl.BlockSpec((1, 1, 1, 3 * D_MODEL), lambda b: (layer, mod_row(b), 0, 0)),
        _resident((1, D_MODEL)),
        _resident((D_MODEL, 2 * GLA_QD + 2 * GLA_VD), gla_idx),
        _resident((D_MODEL, RANK_PAD), gla_idx),
        _resident((RANK_PAD, 2 * GLA_QD), gla_idx),
        _resident((1, 2 * GLA_QD)),
        _resident((1, GLA_DV)),
        _resident((GLA_VD, D_MODEL), gla_idx),
    ]
    args = [x.reshape(n_groups, GROUP_ROWS, D_MODEL), mods, ng, win, wa1, wa2, ba, on, wout]
    if has_s0:
        in_specs.append(pl.BlockSpec((1, 1, 2, GLA_HEADS, GLA_DK, GLA_DV),
                                     lambda b: (b, gla_idx, 0, 0, 0, 0)))
        args.append(state_in)
    aliases = {}
    state_alias = emit_state and states_so_far is not None
    if state_alias:
        aliases[len(args)] = 1
        in_specs.append(pl.BlockSpec(memory_space=pl.ANY))
        args.append(states_so_far)
    convert_layer = None
    if convert is not None:
        conv_weights, convert_layer = convert
        in_specs += [pl.BlockSpec(memory_space=pl.ANY)] * len(conv_weights)
        args += list(conv_weights)
    out_specs = [pl.BlockSpec((1, GROUP_ROWS, D_MODEL), lambda b: (b, 0, 0))]
    out_shape = [jax.ShapeDtypeStruct((n_groups, GROUP_ROWS, D_MODEL), F32)]
    if emit_state:
        out_specs.append(pl.BlockSpec((n_seq, 1, 2, GLA_HEADS, GLA_DK, GLA_DV), lambda b: (b, gla_idx, 0, 0, 0, 0)))
        out_shape.append(jax.ShapeDtypeStruct((batch, n_gla, 2, GLA_HEADS, GLA_DK, GLA_DV), F32))
    scratch = [
        pltpu.VMEM((GROUP_ROWS, GLA_QD), F32),
        pltpu.VMEM((GROUP_ROWS, GLA_QD), F32),
        pltpu.VMEM((2, GROUP_ROWS, GLA_QD), F32),
        pltpu.VMEM((GROUP_ROWS, GLA_VD), BF16),
        pltpu.VMEM((GROUP_ROWS, GLA_VD), F32),
        pltpu.VMEM((GROUP_ROWS, GLA_VD), F32),
    ]
    if not emit_state:
        scratch.append(pltpu.VMEM((2, GLA_HEADS, GLA_DK, GLA_DV), F32))
    if convert is not None:
        n_slabs = n_groups * (seq // GLA_CHUNK)
        widths = [w.shape[2] for w in conv_weights]
        assert all(w.shape[1] % n_slabs == 0 and (w.shape[1] // n_slabs) % 16 == 0 for w in conv_weights)
        slab_rows = [w.shape[1] // n_slabs for w in conv_weights]
        out_specs += [pl.BlockSpec(memory_space=pl.ANY)] * len(conv_weights)
        out_shape += [jax.ShapeDtypeStruct((1,) + w.shape[1:], BF16) for w in conv_weights]
        scratch += [pltpu.VMEM((CONVERT_DEPTH, r, n), F32) for r, n in zip(slab_rows, widths)]
        scratch += [pltpu.VMEM((CONVERT_DEPTH, r, n), BF16) for r, n in zip(slab_rows, widths)]
        scratch += [pltpu.SemaphoreType.DMA((CONVERT_DEPTH,)) for _ in range(2 * len(conv_weights))]
    outs = pl.pallas_call(
        functools.partial(_gla_kernel, seq=seq, has_s0=has_s0, emit_state=emit_state, state_alias=state_alias,
                          convert_layer=convert_layer),
        grid=(n_groups,),
        in_specs=in_specs,
        out_specs=out_specs,
        out_shape=out_shape,
        scratch_shapes=scratch,
        input_output_aliases=aliases,
        compiler_params=pltpu.CompilerParams(dimension_semantics=("arbitrary",),
                                             vmem_limit_bytes=V7X_VMEM_LIMIT_BYTES),
        name=f"gla_layer_seq{seq}",
    )(*args)
    y = outs[0].reshape(batch, seq, D_MODEL)
    if convert is not None:
        return y, list(outs[1:])
    return (y, outs[1]) if emit_state else (y, None)


def _rope_swap(x):
    lane = lax.broadcasted_iota(jnp.int32, x.shape, 1)
    quarter = HEAD_DIM // 4
    first = (lane % (2 * quarter)) < quarter
    return jnp.where(first, pltpu.roll(x, HEAD_DIM - quarter, 1), pltpu.roll(x, quarter, 1))


def _att_kernel(*refs, seq, latent, cache_alias):
    n_tiles = seq // ROW_TILE
    n_keys = seq + (refs[9].shape[2] if latent else 0)
    it = iter(refs)
    x_ref, mod_ref, ng_ref, win_ref, qn_ref, kn_ref, wout_ref = (next(it) for _ in range(7))
    if latent:
        cos_ref, sin_ref, ck_ref, cv_ref = (next(it) for _ in range(4))
    if cache_alias:
        next(it), next(it)
    y_ref = next(it)
    if not latent:
        kout_ref, vout_ref = next(it), next(it)
    q_s, k_s, v_s, gate_s, ao_s = (next(it) for _ in range(5))

    ng = ng_ref[...]
    qn = qn_ref[...]
    kn = kn_ref[...]
    exp2_scale = (HEAD_DIM ** -0.5) * math.log2(math.e)

    def project(t, carry):
        rows = pl.ds(pl.multiple_of(t * ROW_TILE, ROW_TILE), ROW_TILE)
        hb = _modulated_norm(x_ref[0, rows, :], ng, mod_ref)
        q = _head_rms(_dot(hb, win_ref[0, :, 0:ATT_QD]), qn, HEAD_DIM)
        k = _head_rms(_dot(hb, win_ref[0, :, ATT_QD:ATT_QD + ATT_KD]), kn, HEAD_DIM)
        v = _dot(hb, win_ref[0, :, ATT_QD + ATT_KD:ATT_QD + 2 * ATT_KD])
        gate_s[rows, :] = _dot(hb, win_ref[0, :, ATT_QD + 2 * ATT_KD:])
        if latent:
            cos = cos_ref[rows, :]
            sin = sin_ref[rows, :]
            q = jnp.concatenate(
                [q[:, h * HEAD_DIM:(h + 1) * HEAD_DIM] * cos + _rope_swap(q[:, h * HEAD_DIM:(h + 1) * HEAD_DIM]) * sin
                 for h in range(ATT_HEADS)], axis=-1)
            k = jnp.concatenate(
                [k[:, h * HEAD_DIM:(h + 1) * HEAD_DIM] * cos + _rope_swap(k[:, h * HEAD_DIM:(h + 1) * HEAD_DIM]) * sin
                 for h in range(ATT_KV_HEADS)], axis=-1)
        else:
            for h in range(ATT_KV_HEADS):
                kout_ref[0, 0, rows, h, :] = k[:, h * HEAD_DIM:(h + 1) * HEAD_DIM]
                vout_ref[0, 0, rows, h, :] = v[:, h * HEAD_DIM:(h + 1) * HEAD_DIM]
        q_s[rows, :] = (q * exp2_scale).astype(BF16)
        k_s[rows, :] = k.astype(BF16)
        for h in range(ATT_KV_HEADS):
            v_s[rows, 2 * h * HEAD_DIM:(2 * h + 1) * HEAD_DIM] = v[:, h * HEAD_DIM:(h + 1) * HEAD_DIM].astype(BF16)
        return carry

    for h in range(ATT_KV_HEADS):
        v_s[:, (2 * h + 1) * HEAD_DIM:(2 * h + 2) * HEAD_DIM] = jnp.ones((n_keys, HEAD_DIM), BF16)
    lax.fori_loop(0, n_tiles, project, 0)
    if latent:
        for h in range(ATT_KV_HEADS):
            k_s[seq:n_keys, h * HEAD_DIM:(h + 1) * HEAD_DIM] = ck_ref[0, 0, :, h, :].astype(BF16)
            v_s[seq:n_keys, 2 * h * HEAD_DIM:(2 * h + 1) * HEAD_DIM] = cv_ref[0, 0, :, h, :].astype(BF16)

    def attend(t, carry):
        rows = pl.ds(pl.multiple_of(t * ROW_TILE, ROW_TILE), ROW_TILE)

        def scores(h):
            kc = slice((h // ATT_GROUP) * HEAD_DIM, (h // ATT_GROUP + 1) * HEAD_DIM)
            return _dot_nt(q_s[rows, h * HEAD_DIM:(h + 1) * HEAD_DIM], k_s[:, kc])

        s = scores(0)
        for h in range(ATT_HEADS):
            s_next = scores(h + 1) if h + 1 < ATT_HEADS else None
            vc = slice((h // ATT_GROUP) * 2 * HEAD_DIM, (h // ATT_GROUP + 1) * 2 * HEAD_DIM)
            p = jnp.exp2(s - jnp.max(s, axis=-1, keepdims=True))
            o = _dot(p.astype(BF16), v_s[:, vc])
            ao_s[rows, h * HEAD_DIM:(h + 1) * HEAD_DIM] = o[:, 0:HEAD_DIM] / o[:, HEAD_DIM:]
            s = s_next
        return carry

    lax.fori_loop(0, n_tiles, attend, 0)

    res_gate = mod_ref[0, 0, :, 2 * D_MODEL:]

    def finish(t, carry):
        rows = pl.ds(pl.multiple_of(t * ROW_TILE, ROW_TILE), ROW_TILE)
        y = _dot((ao_s[rows, :] * _silu(gate_s[rows, :])).astype(BF16), wout_ref[0])
        y_ref[0, rows, :] = x_ref[0, rows, :] + res_gate * y
        return carry

    lax.fori_loop(0, n_tiles, finish, 0)


def _att_layer(x, mods, layer, mod_row, ng, win, qn, kn, wout, latent_inputs, att_idx, n_att, caches_so_far):
    batch, seq, _ = x.shape
    latent = latent_inputs is not None
    in_specs = [
        pl.BlockSpec((1, seq, D_MODEL), lambda b: (b, 0, 0)),
        pl.BlockSpec((1, 1, 1, 3 * D_MODEL), lambda b: (layer, mod_row(b), 0, 0)),
        _resident((1, D_MODEL)),
        _resident((D_MODEL, 2 * ATT_QD + 2 * ATT_KD), 0),
        _resident((1, HEAD_DIM)),
        _resident((1, HEAD_DIM)),
        _resident((ATT_QD, D_MODEL), 0),
    ]
    args = [x, mods, ng, win, qn, kn, wout]
    n_keys = seq
    if latent:
        cos, sin, cache_k, cache_v = latent_inputs
        past = cache_k.shape[2]
        n_keys = seq + past
        in_specs += [
            _resident((seq, HEAD_DIM)),
            _resident((seq, HEAD_DIM)),
            pl.BlockSpec((1, 1, past, ATT_KV_HEADS, HEAD_DIM), lambda b: (b, att_idx, 0, 0, 0)),
            pl.BlockSpec((1, 1, past, ATT_KV_HEADS, HEAD_DIM), lambda b: (b, att_idx, 0, 0, 0)),
        ]
        args += [cos, sin, cache_k, cache_v]
    aliases = {}
    cache_alias = (not latent) and caches_so_far is not None
    if cache_alias:
        aliases = {len(args): 1, len(args) + 1: 2}
        in_specs += [pl.BlockSpec(memory_space=pl.ANY)] * 2
        args += list(caches_so_far)
    out_specs = [pl.BlockSpec((1, seq, D_MODEL), lambda b: (b, 0, 0))]
    out_shape = [jax.ShapeDtypeStruct((batch, seq, D_MODEL), F32)]
    if not latent:
        out_specs += [pl.BlockSpec((1, 1, seq, ATT_KV_HEADS, HEAD_DIM), lambda b: (b, att_idx, 0, 0, 0))] * 2
        out_shape += [jax.ShapeDtypeStruct((batch, n_att, seq, ATT_KV_HEADS, HEAD_DIM), F32)] * 2
    scratch = [
        pltpu.VMEM((seq, ATT_QD), BF16),
        pltpu.VMEM((n_keys, ATT_KD), BF16),
        pltpu.VMEM((n_keys, 2 * ATT_KD), BF16),
        pltpu.VMEM((seq, ATT_QD), F32),
        pltpu.VMEM((seq, ATT_QD), F32),
    ]
    outs = pl.pallas_call(
        functools.partial(_att_kernel, seq=seq, latent=latent, cache_alias=cache_alias),
        grid=(batch,),
        in_specs=in_specs,
        out_specs=out_specs,
        out_shape=out_shape,
        scratch_shapes=scratch,
        input_output_aliases=aliases,
        compiler_params=pltpu.CompilerParams(dimension_semantics=("arbitrary",),
                                             vmem_limit_bytes=V7X_VMEM_LIMIT_BYTES),
        name=f"att_layer_seq{seq}",
    )(*args)
    return outs


def _rope_tables(seq):
    half = HEAD_DIM // 2
    nf = half // 2
    pos = np.arange(seq)
    freqs = np.float32(ROPE_THETA) ** (-np.arange(nf, dtype=np.float32) / np.float32(nf))
    ang_row = (pos // GRID_W).astype(np.float32)[:, None] * freqs[None, :]
    ang_col = (pos % GRID_W).astype(np.float32)[:, None] * freqs[None, :]
    cos = np.concatenate([np.cos(ang_row)] * 2 + [np.cos(ang_col)] * 2, axis=-1)
    sin = np.concatenate([-np.sin(ang_row), np.sin(ang_row), -np.sin(ang_col), np.sin(ang_col)], axis=-1)
    return jnp.asarray(cos, F32), jnp.asarray(sin, F32)


def kernel(x_prompt, x_sample, state_gla, cache_k, cache_v, c, c_ctx, norm_g, w_ada, b_ada,
           gla_w_in, gla_wa1, gla_wa2, gla_ba, gla_onorm, gla_w_out,
           att_w_in, att_qnorm, att_knorm, att_w_out):
    n_dec = x_sample.shape[0]
    assert 1 + n_dec <= MOD_ROWS
    assert x_prompt.shape[1] % ROW_TILE == 0 and x_sample.shape[1] % ROW_TILE == 0

    cvec = jnp.concatenate([c_ctx[None], c, jnp.zeros((MOD_ROWS - 1 - n_dec, D_MODEL), F32)], axis=0)
    mods = _modulations(cvec, w_ada, b_ada)
    ctx_row = lambda b: 0
    dec_row = lambda b: b + 1

    cos, sin = _rope_tables(x_sample.shape[1])
    n_gla, n_att = gla_w_in.shape[0], att_w_in.shape[0]

    gla_win, gla_wout = gla_w_in.astype(BF16), gla_w_out.astype(BF16)
    wa1 = jnp.concatenate([gla_wa1[:, 0], gla_wa1[:, 1],
                           jnp.zeros((n_gla, D_MODEL, RANK_PAD - 2 * GLA_RANK), F32)], axis=-1).astype(BF16)
    zeros_qd = jnp.zeros((n_gla, GLA_RANK, GLA_QD), F32)
    wa2 = jnp.concatenate([jnp.concatenate([gla_wa2[:, 0], zeros_qd], axis=-1),
                           jnp.concatenate([zeros_qd, gla_wa2[:, 1]], axis=-1),
                           jnp.zeros((n_gla, RANK_PAD - 2 * GLA_RANK, 2 * GLA_QD), F32)], axis=1).astype(BF16)

    xp, xs = x_prompt, x_sample
    states, caches = None, None
    for l in range(DEPTH):
        i = l // 2
        ng = norm_g[l].reshape(1, D_MODEL)
        if l % 2 == 0:
            ba = gla_ba[i].reshape(1, 2 * GLA_QD)
            on = gla_onorm[i].reshape(1, GLA_DV)
            common = (ng, gla_win, wa1, wa2, ba, on, gla_wout)
            xp, states = _gla_layer(xp, mods, l, ctx_row, *common, None, i, n_gla, states)
            xs, (att_win, att_wout) = _gla_layer(xs, mods, l, dec_row, *common, state_gla, i, n_gla, None,
                                                 convert=((att_w_in, att_w_out), i))
        else:
            qn = att_qnorm[i].reshape(1, HEAD_DIM)
            kn = att_knorm[i].reshape(1, HEAD_DIM)
            xp, *caches = _att_layer(xp, mods, l, ctx_row, ng, att_win, qn, kn, att_wout, None, i, n_att, caches)
            (xs,) = _att_layer(xs, mods, l, dec_row, ng, att_win, qn, kn, att_wout, (cos, sin, cache_k, cache_v), i,
                               n_att, None)
    return (xp, xs, states, caches[0], caches[1])
```

```python
import functools
import math

import jax
import jax.numpy as jnp
import numpy as np
from jax import lax
from jax.experimental import pallas as pl
from jax.experimental.pallas import tpu as pltpu

D_MODEL = 1024
DEPTH = 4
GRID_W = 64
GLA_HEADS = 4
GLA_DK = 128
GLA_DV = 256
GLA_RANK = 16
GLA_TAU = 16.0
GLA_CHUNK = 64
GLA_QD = GLA_HEADS * GLA_DK
GLA_VD = GLA_HEADS * GLA_DV
HEAD_DIM = 128
ATT_HEADS = 8
ATT_KV_HEADS = 2
ATT_GROUP = ATT_HEADS // ATT_KV_HEADS
ATT_QD = ATT_HEADS * HEAD_DIM
ATT_KD = ATT_KV_HEADS * HEAD_DIM
ROPE_THETA = 10000.0
EPS = 1e-6

ROW_TILE = 256
GROUP_ROWS = 1024
MOD_ROWS = 8
RANK_PAD = 128
V7X_VMEM_LIMIT_BYTES = 60 * 1024 * 1024

F32 = jnp.float32
BF16 = jnp.bfloat16
_NT = (((1,), (1,)), ((), ()))
_TN = (((0,), (0,)), ((), ()))


def _dot(a, b):
    return jnp.dot(a, b, preferred_element_type=F32)


def _dot_nt(a, b):
    return lax.dot_general(a, b, _NT, preferred_element_type=F32)


def _dot_tn(a, b):
    return lax.dot_general(a, b, _TN, preferred_element_type=F32)


def _silu(x):
    return x * (1.0 / (1.0 + jnp.exp(-x)))


def _log_sigmoid(z):
    return jnp.minimum(z, 0.0) - jnp.log(1.0 + jnp.exp(-jnp.abs(z)))


def _split_top(x):
    top = pltpu.bitcast(pltpu.bitcast(x, jnp.uint32) & jnp.uint32(0xFFFF0000), F32)
    return top, x - top


def _modulated_norm(x, ng, mod_ref):
    shift = mod_ref[0, 0, :, 0:D_MODEL]
    scale = mod_ref[0, 0, :, D_MODEL:2 * D_MODEL]
    ms = jnp.mean(x * x, axis=-1, keepdims=True)
    return ((x * lax.rsqrt(ms + EPS)) * (ng * (1.0 + scale)) + shift).astype(BF16)


def _head_rms(x, g, width):
    outs = []
    for h in range(x.shape[-1] // width):
        xh = x[:, h * width:(h + 1) * width]
        ms = jnp.mean(xh * xh, axis=-1, keepdims=True)
        outs.append((xh * lax.rsqrt(ms + EPS)) * g)
    return jnp.concatenate(outs, axis=-1)


def _mod_kernel(c_ref, w_ref, b_ref, o_ref):
    acc = _dot(jnp.concatenate(_split_top(_silu(c_ref[...])), axis=0).astype(BF16), w_ref[0].astype(BF16))
    o_ref[0, :, 0, :] = acc[0:MOD_ROWS] + acc[MOD_ROWS:] + b_ref[pl.ds(pl.program_id(0), 1), :]


def _modulations(cvec, w_ada, b_ada):
    tn = D_MODEL
    n_tiles = 3 * D_MODEL // tn
    return pl.pallas_call(
        _mod_kernel,
        grid=(DEPTH, n_tiles),
        in_specs=[
            pl.BlockSpec((MOD_ROWS, D_MODEL), lambda l, j: (0, 0)),
            pl.BlockSpec((1, D_MODEL, tn), lambda l, j: (l, 0, j)),
            pl.BlockSpec((DEPTH, tn), lambda l, j: (0, j)),
        ],
        out_specs=pl.BlockSpec((1, MOD_ROWS, 1, tn), lambda l, j: (l, 0, 0, j)),
        out_shape=jax.ShapeDtypeStruct((DEPTH, MOD_ROWS, 1, 3 * D_MODEL), F32),
        compiler_params=pltpu.CompilerParams(dimension_semantics=("arbitrary", "arbitrary")),
        name="adaln_modulation",
    )(cvec, w_ada, b_ada)


def _chunk_sum_matrices():
    r = lax.broadcasted_iota(jnp.int32, (ROW_TILE, ROW_TILE), 0)
    c = lax.broadcasted_iota(jnp.int32, (ROW_TILE, ROW_TILE), 1)
    same = (r // GLA_CHUNK) == (c // GLA_CHUNK)
    prefix = jnp.where(same & (c <= r), 1.0, 0.0).astype(BF16)
    suffix = jnp.where(same & (c >= r), 1.0, 0.0).astype(BF16)
    return prefix, suffix


def _gla_kernel(*refs, seq, has_s0, emit_state, state_alias, convert_layer):
    n_seq = GROUP_ROWS // seq
    n_tiles = GROUP_ROWS // ROW_TILE
    n_pos = seq // GLA_CHUNK
    merge_products = n_seq > 1
    it = iter(refs)
    x_ref, mod_ref, ng_ref, win_ref, wa1_ref, wa2_ref, ba_ref, on_ref, wout_ref = (next(it) for _ in range(9))
    s0_ref = next(it) if has_s0 else None
    if state_alias:
        next(it)
    convert = convert_layer is not None
    conv_src = [next(it), next(it)] if convert else None
    y_ref = next(it)
    sout_ref = next(it) if emit_state else None
    conv_dst = [next(it), next(it)] if convert else None
    q_s, k_s, b_s, v_s, gate_s, o_s = (next(it) for _ in range(6))
    st_s = None if emit_state else next(it)
    if convert:
        conv_f32, conv_bf16, conv_rd, conv_wr = ([next(it), next(it)] for _ in range(4))
        slab_rows = conv_f32[0].shape[1]

        def slab(t):
            return pl.ds((pl.program_id(0) * n_tiles + t) * slab_rows, slab_rows)

        def slab_read(w, t):
            return pltpu.make_async_copy(conv_src[w].at[convert_layer, slab(t), :], conv_f32[w].at[t],
                                         conv_rd[w].at[t])

        def slab_write(w, t):
            return pltpu.make_async_copy(conv_bf16[w].at[t], conv_dst[w].at[0, slab(t), :], conv_wr[w].at[t])

        for w in range(2):
            for t in range(n_tiles):
                slab_read(w, t).start()

        def convert_slab(t):
            for w in range(2):
                slab_read(w, t).wait()
                conv_bf16[w][t] = conv_f32[w][t].astype(BF16)
                slab_write(w, t).start()

    def state_at(s, d, h):
        return sout_ref.at[s, 0, d, h] if emit_state else st_s.at[d, h]

    ng = ng_ref[...]
    prefix_m, suffix_m = _chunk_sum_matrices()

    def project_wide(t):
        rows = slice(t * ROW_TILE, (t + 1) * ROW_TILE)
        hb = _modulated_norm(x_ref[0, rows, :], ng, mod_ref)
        q_s[rows, :] = _dot(hb, win_ref[0, :, 0:GLA_QD]) * (GLA_DK ** -0.5)
        k_s[rows, :] = _dot(hb, win_ref[0, :, GLA_QD:2 * GLA_QD])
        v_s[rows, :] = _dot(hb, win_ref[0, :, 2 * GLA_QD:2 * GLA_QD + GLA_VD]).astype(BF16)
        gate_s[rows, :] = _dot(hb, win_ref[0, :, 2 * GLA_QD + GLA_VD:])
        low = _dot(hb, wa1_ref[0]).astype(BF16)
        return _dot(low, wa2_ref[0]) + ba_ref[...]

    def decay_sums(t, z):
        rows = slice(t * ROW_TILE, (t + 1) * ROW_TILE)
        logg2 = _log_sigmoid(z) * (math.log2(math.e) / GLA_TAU)
        for d, csum_m in enumerate((prefix_m, suffix_m)):
            parts = jnp.concatenate(_split_top(logg2[:, d * GLA_QD:(d + 1) * GLA_QD]), axis=0).astype(BF16)
            b_s[d, rows, :] = _dot(jnp.concatenate([csum_m, csum_m], axis=1), parts)

    z = project_wide(0)
    for t in range(n_tiles):
        z_next = project_wide(t + 1) if t + 1 < n_tiles else None
        decay_sums(t, z)
        if convert:
            convert_slab(t)
        z = z_next

    for s in range(n_seq):
        for d in range(2):
            for h in range(GLA_HEADS):
                state_at(s, d, h)[...] = s0_ref[0, 0, d, h] if has_s0 else jnp.zeros((GLA_DK, GLA_DV), F32)

    ri = lax.broadcasted_iota(jnp.int32, (GLA_CHUNK, GLA_CHUNK), 0)
    ci = lax.broadcasted_iota(jnp.int32, (GLA_CHUNK, GLA_CHUNK), 1)
    masks = (ci <= ri, ci >= ri)

    def scan_step(i, first_visit):
        chains = []
        for s in range(n_seq):
            for d in range(2):
                c = s * n_pos + (i if d == 0 else n_pos - 1 - i)
                rows = pl.ds(pl.multiple_of(c * GLA_CHUNK, GLA_CHUNK), GLA_CHUNK)
                b = b_s[d, rows, :]
                edge = GLA_CHUNK - 1 if d == 0 else 0
                total = b[edge:edge + 1, :]
                k = k_s[rows, :]
                qe = (q_s[rows, :] * jnp.exp2(b)).astype(BF16)
                ke = (k * jnp.exp2(-b)).astype(BF16)
                kdt = (k * jnp.exp2(total - b)).T.astype(BF16)
                dec = jnp.exp2(total)
                for h in range(GLA_HEADS):
                    kc = slice(h * GLA_DK, (h + 1) * GLA_DK)
                    chains.append((s, d, h, rows, slice(h * GLA_DV, (h + 1) * GLA_DV),
                                   qe[:, kc], ke[:, kc], kdt[kc, :], dec[:, kc]))
        scores = [_dot_nt(qe, ke) for (s, d, h, rows, vc, qe, ke, kdt, dec) in chains]
        if not merge_products:
            updates = [_dot(kdt, v_s[rows, vc]) for (s, d, h, rows, vc, qe, ke, kdt, dec) in chains]
        no_state = jnp.zeros((GLA_DK, GLA_DK), BF16)

        def decayed(st, dec):
            dec_col = jnp.broadcast_to(dec, (GLA_DK, GLA_DK)).T
            return jnp.concatenate([dec_col] * (GLA_DV // GLA_DK), axis=1) * st

        for (s, d, h, rows, vc, qe, ke, kdt, dec), sc in zip(chains, scores):
            a = jnp.where(masks[d], sc, 0.0).astype(BF16)
            st_ref = state_at(s, d, h)
            st = st_ref[...]
            lhs = jnp.concatenate([qe, a], axis=1)
            if merge_products:
                lhs = jnp.concatenate([lhs, jnp.concatenate([no_state, kdt], axis=1)], axis=0)
            both = _dot(lhs, jnp.concatenate([st.astype(BF16), v_s[rows, vc]], axis=0))
            if first_visit:
                o_s[rows, vc] = both[0:GLA_CHUNK]
            else:
                o_s[rows, vc] += both[0:GLA_CHUNK]
            if merge_products:
                st_ref[...] = decayed(st, dec) + both[GLA_CHUNK:]
        if not merge_products:
            for (s, d, h, rows, vc, qe, ke, kdt, dec), upd in zip(chains, updates):
                st_ref = state_at(s, d, h)
                st_ref[...] = decayed(st_ref[...], dec) + upd

    def first_half(i, carry):
        scan_step(i, True)
        return carry

    def second_half(i, carry):
        scan_step(i, False)
        return carry

    lax.fori_loop(0, n_pos // 2, first_half, 0)
    lax.fori_loop(n_pos // 2, n_pos, second_half, 0)
    on = on_ref[...]
    res_gate = mod_ref[0, 0, :, 2 * D_MODEL:]

    def gated(t):
        rows = slice(t * ROW_TILE, (t + 1) * ROW_TILE)
        return (_head_rms(o_s[rows, :], on, GLA_DV) * _silu(gate_s[rows, :])).astype(BF16)

    og = gated(0)
    for t in range(n_tiles):
        rows = slice(t * ROW_TILE, (t + 1) * ROW_TILE)
        og_next = gated(t + 1) if t + 1 < n_tiles else None
        y_ref[0, rows, :] = x_ref[0, rows, :] + res_gate * _dot(og, wout_ref[0])
        og = og_next

    if convert:
        for w in range(2):
            for t in range(n_tiles):
                slab_write(w, t).wait()


def _resident(shape, layer=None):
    if layer is None:
        return pl.BlockSpec(shape, lambda b: (0,) * len(shape), pipeline_mode=pl.Buffered(1))
    return pl.BlockSpec((1,) + shape, lambda b: (layer,) + (0,) * len(shape), pipeline_mode=pl.Buffered(1))


def _gla_layer(x, mods, layer, mod_row, ng, win, wa1, wa2, ba, on, wout, state_in, gla_idx, n_gla, states_so_far,
               convert=None):
    batch, seq, _ = x.shape
    n_seq = GROUP_ROWS // seq
    n_groups = batch // n_seq
    has_s0 = state_in is not None
    emit_state = not has_s0
    assert not has_s0 or n_seq == 1
    in_specs = [
        pl.BlockSpec((1, GROUP_ROWS, D_MODEL), lambda b: (b, 0, 0)),
        pl.BlockSpec((1, 1, 1, 3 * D_MODEL), lambda b: (layer, mod_row(b), 0, 0)),
        _resident((1, D_MODEL)),
        _resident((D_MODEL, 2 * GLA_QD + 2 * GLA_VD), gla_idx),
        _resident((D_MODEL, RANK_PAD), gla_idx),
        _resident((RANK_PAD, 2 * GLA_QD), gla_idx),
        _resident((1, 2 * GLA_QD)),
        _resident((1, GLA_DV)),
        _resident((GLA_VD, D_MODEL), gla_idx),
    ]
    args = [x.reshape(n_groups, GROUP_ROWS, D_MODEL), mods, ng, win, wa1, wa2, ba, on, wout]
    if has_s0:
        in_specs.append(pl.BlockSpec((1, 1, 2, GLA_HEADS, GLA_DK, GLA_DV),
                                     lambda b: (b, gla_idx, 0, 0, 0, 0)))
        args.append(state_in)
    aliases = {}
    state_alias = emit_state and states_so_far is not None
    if state_alias:
        aliases[len(args)] = 1
        in_specs.append(pl.BlockSpec(memory_space=pl.ANY))
        args.append(states_so_far)
    convert_layer = None
    if convert is not None:
        conv_weights, convert_layer = convert
        in_specs += [pl.BlockSpec(memory_space=pl.ANY)] * len(conv_weights)
        args += list(conv_weights)
    out_specs = [pl.BlockSpec((1, GROUP_ROWS, D_MODEL), lambda b: (b, 0, 0))]
    out_shape = [jax.ShapeDtypeStruct((n_groups, GROUP_ROWS, D_MODEL), F32)]
    if emit_state:
        out_specs.append(pl.BlockSpec((n_seq, 1, 2, GLA_HEADS, GLA_DK, GLA_DV), lambda b: (b, gla_idx, 0, 0, 0, 0)))
        out_shape.append(jax.ShapeDtypeStruct((batch, n_gla, 2, GLA_HEADS, GLA_DK, GLA_DV), F32))
    scratch = [
        pltpu.VMEM((GROUP_ROWS, GLA_QD), F32),
        pltpu.VMEM((GROUP_ROWS, GLA_QD), F32),
        pltpu.VMEM((2, GROUP_ROWS, GLA_QD), F32),
        pltpu.VMEM((GROUP_ROWS, GLA_VD), BF16),
        pltpu.VMEM((GROUP_ROWS, GLA_VD), F32),
        pltpu.VMEM((GROUP_ROWS, GLA_VD), F32),
    ]
    if not emit_state:
        scratch.append(pltpu.VMEM((2, GLA_HEADS, GLA_DK, GLA_DV), F32))
    if convert is not None:
        n_tiles = GROUP_ROWS // ROW_TILE
        n_slabs = n_groups * n_tiles
        widths = [w.shape[2] for w in conv_weights]
        assert all(w.shape[1] % n_slabs == 0 and (w.shape[1] // n_slabs) % 16 == 0 for w in conv_weights)
        slab_rows = [w.shape[1] // n_slabs for w in conv_weights]
        out_specs += [pl.BlockSpec(memory_space=pl.ANY)] * len(conv_weights)
        out_shape += [jax.ShapeDtypeStruct((1,) + w.shape[1:], BF16) for w in conv_weights]
        scratch += [pltpu.VMEM((n_tiles, r, n), F32) for r, n in zip(slab_rows, widths)]
        scratch += [pltpu.VMEM((n_tiles, r, n), BF16) for r, n in zip(slab_rows, widths)]
        scratch += [pltpu.SemaphoreType.DMA((n_tiles,)) for _ in range(2 * len(conv_weights))]
    outs = pl.pallas_call(
        functools.partial(_gla_kernel, seq=seq, has_s0=has_s0, emit_state=emit_state, state_alias=state_alias,
                          convert_layer=convert_layer),
        grid=(n_groups,),
        in_specs=in_specs,
        out_specs=out_specs,
        out_shape=out_shape,
        scratch_shapes=scratch,
        input_output_aliases=aliases,
        compiler_params=pltpu.CompilerParams(dimension_semantics=("arbitrary",),
                                             vmem_limit_bytes=V7X_VMEM_LIMIT_BYTES),
        name=f"gla_layer_seq{seq}",
    )(*args)
    y = outs[0].reshape(batch, seq, D_MODEL)
    if convert is not None:
        return y, list(outs[1:])
    return (y, outs[1]) if emit_state else (y, None)


def _rope_swap(x):
    lane = lax.broadcasted_iota(jnp.int32, x.shape, 1)
    quarter = HEAD_DIM // 4
    first = (lane % (2 * quarter)) < quarter
    return jnp.where(first, pltpu.roll(x, HEAD_DIM - quarter, 1), pltpu.roll(x, quarter, 1))


def _att_kernel(*refs, seq, latent, cache_alias):
    n_tiles = seq // ROW_TILE
    n_keys = seq + (refs[9].shape[2] if latent else 0)
    it = iter(refs)
    x_ref, mod_ref, ng_ref, win_ref, qn_ref, kn_ref, wout_ref = (next(it) for _ in range(7))
    if latent:
        cos_ref, sin_ref, ck_ref, cv_ref = (next(it) for _ in range(4))
    if cache_alias:
        next(it), next(it)
    y_ref = next(it)
    if not latent:
        kout_ref, vout_ref = next(it), next(it)
    q_s, k_s, v_s, gate_s, ao_s = (next(it) for _ in range(5))

    ng = ng_ref[...]
    qn = qn_ref[...]
    kn = kn_ref[...]
    exp2_scale = (HEAD_DIM ** -0.5) * math.log2(math.e)

    def project(t, carry):
        rows = pl.ds(pl.multiple_of(t * ROW_TILE, ROW_TILE), ROW_TILE)
        hb = _modulated_norm(x_ref[0, rows, :], ng, mod_ref)
        q = _head_rms(_dot(hb, win_ref[0, :, 0:ATT_QD]), qn, HEAD_DIM)
        k = _head_rms(_dot(hb, win_ref[0, :, ATT_QD:ATT_QD + ATT_KD]), kn, HEAD_DIM)
        v = _dot(hb, win_ref[0, :, ATT_QD + ATT_KD:ATT_QD + 2 * ATT_KD])
        gate_s[rows, :] = _dot(hb, win_ref[0, :, ATT_QD + 2 * ATT_KD:])
        if latent:
            cos = cos_ref[rows, :]
            sin = sin_ref[rows, :]
            q = jnp.concatenate(
                [q[:, h * HEAD_DIM:(h + 1) * HEAD_DIM] * cos + _rope_swap(q[:, h * HEAD_DIM:(h + 1) * HEAD_DIM]) * sin
                 for h in range(ATT_HEADS)], axis=-1)
            k = jnp.concatenate(
                [k[:, h * HEAD_DIM:(h + 1) * HEAD_DIM] * cos + _rope_swap(k[:, h * HEAD_DIM:(h + 1) * HEAD_DIM]) * sin
                 for h in range(ATT_KV_HEADS)], axis=-1)
        else:
            for h in range(ATT_KV_HEADS):
                kout_ref[0, 0, rows, h, :] = k[:, h * HEAD_DIM:(h + 1) * HEAD_DIM]
                vout_ref[0, 0, rows, h, :] = v[:, h * HEAD_DIM:(h + 1) * HEAD_DIM]
        q_s[rows, :] = (q * exp2_scale).astype(BF16)
        k_s[rows, :] = k.astype(BF16)
        for h in range(ATT_KV_HEADS):
            v_s[rows, 2 * h * HEAD_DIM:(2 * h + 1) * HEAD_DIM] = v[:, h * HEAD_DIM:(h + 1) * HEAD_DIM].astype(BF16)
        return carry

    for h in range(ATT_KV_HEADS):
        v_s[:, (2 * h + 1) * HEAD_DIM:(2 * h + 2) * HEAD_DIM] = jnp.ones((n_keys, HEAD_DIM), BF16)
    lax.fori_loop(0, n_tiles, project, 0)
    if latent:
        for h in range(ATT_KV_HEADS):
            k_s[seq:n_keys, h * HEAD_DIM:(h + 1) * HEAD_DIM] = ck_ref[0, 0, :, h, :].astype(BF16)
            v_s[seq:n_keys, 2 * h * HEAD_DIM:(2 * h + 1) * HEAD_DIM] = cv_ref[0, 0, :, h, :].astype(BF16)

    def attend(t, carry):
        rows = pl.ds(pl.multiple_of(t * ROW_TILE, ROW_TILE), ROW_TILE)

        def scores(h):
            kc = slice((h // ATT_GROUP) * HEAD_DIM, (h // ATT_GROUP + 1) * HEAD_DIM)
            return _dot_nt(q_s[rows, h * HEAD_DIM:(h + 1) * HEAD_DIM], k_s[:, kc])

        s = scores(0)
        for h in range(ATT_HEADS):
            s_next = scores(h + 1) if h + 1 < ATT_HEADS else None
            vc = slice((h // ATT_GROUP) * 2 * HEAD_DIM, (h // ATT_GROUP + 1) * 2 * HEAD_DIM)
            p = jnp.exp2(s - jnp.max(s, axis=-1, keepdims=True))
            o = _dot(p.astype(BF16), v_s[:, vc])
            ao_s[rows, h * HEAD_DIM:(h + 1) * HEAD_DIM] = o[:, 0:HEAD_DIM] / o[:, HEAD_DIM:]
            s = s_next
        return carry

    lax.fori_loop(0, n_tiles, attend, 0)

    res_gate = mod_ref[0, 0, :, 2 * D_MODEL:]

    def finish(t, carry):
        rows = pl.ds(pl.multiple_of(t * ROW_TILE, ROW_TILE), ROW_TILE)
        y = _dot((ao_s[rows, :] * _silu(gate_s[rows, :])).astype(BF16), wout_ref[0])
        y_ref[0, rows, :] = x_ref[0, rows, :] + res_gate * y
        return carry

    lax.fori_loop(0, n_tiles, finish, 0)


def _att_layer(x, mods, layer, mod_row, ng, win, qn, kn, wout, latent_inputs, att_idx, n_att, caches_so_far):
    batch, seq, _ = x.shape
    latent = latent_inputs is not None
    in_specs = [
        pl.BlockSpec((1, seq, D_MODEL), lambda b: (b, 0, 0)),
        pl.BlockSpec((1, 1, 1, 3 * D_MODEL), lambda b: (layer, mod_row(b), 0, 0)),
        _resident((1, D_MODEL)),
        _resident((D_MODEL, 2 * ATT_QD + 2 * ATT_KD), 0),
        _resident((1, HEAD_DIM)),
        _resident((1, HEAD_DIM)),
        _resident((ATT_QD, D_MODEL), 0),
    ]
    args = [x, mods, ng, win, qn, kn, wout]
    n_keys = seq
    if latent:
        cos, sin, cache_k, cache_v = latent_inputs
        past = cache_k.shape[2]
        n_keys = seq + past
        in_specs += [
            _resident((seq, HEAD_DIM)),
            _resident((seq, HEAD_DIM)),
            pl.BlockSpec((1, 1, past, ATT_KV_HEADS, HEAD_DIM), lambda b: (b, att_idx, 0, 0, 0)),
            pl.BlockSpec((1, 1, past, ATT_KV_HEADS, HEAD_DIM), lambda b: (b, att_idx, 0, 0, 0)),
        ]
        args += [cos, sin, cache_k, cache_v]
    aliases = {}
    cache_alias = (not latent) and caches_so_far is not None
    if cache_alias:
        aliases = {len(args): 1, len(args) + 1: 2}
        in_specs += [pl.BlockSpec(memory_space=pl.ANY)] * 2
        args += list(caches_so_far)
    out_specs = [pl.BlockSpec((1, seq, D_MODEL), lambda b: (b, 0, 0))]
    out_shape = [jax.ShapeDtypeStruct((batch, seq, D_MODEL), F32)]
    if not latent:
        out_specs += [pl.BlockSpec((1, 1, seq, ATT_KV_HEADS, HEAD_DIM), lambda b: (b, att_idx, 0, 0, 0))] * 2
        out_shape += [jax.ShapeDtypeStruct((batch, n_att, seq, ATT_KV_HEADS, HEAD_DIM), F32)] * 2
    scratch = [
        pltpu.VMEM((seq, ATT_QD), BF16),
        pltpu.VMEM((n_keys, ATT_KD), BF16),
        pltpu.VMEM((n_keys, 2 * ATT_KD), BF16),
        pltpu.VMEM((seq, ATT_QD), F32),
        pltpu.VMEM((seq, ATT_QD), F32),
    ]
    outs = pl.pallas_call(
        functools.partial(_att_kernel, seq=seq, latent=latent, cache_alias=cache_alias),
        grid=(batch,),
        in_specs=in_specs,
        out_specs=out_specs,
        out_shape=out_shape,
        scratch_shapes=scratch,
        input_output_aliases=aliases,
        compiler_params=pltpu.CompilerParams(dimension_semantics=("arbitrary",),
                                             vmem_limit_bytes=V7X_VMEM_LIMIT_BYTES),
        name=f"att_layer_seq{seq}",
    )(*args)
    return outs


def _rope_tables(seq):
    half = HEAD_DIM // 2
    nf = half // 2
    pos = np.arange(seq)
    freqs = np.float32(ROPE_THETA) ** (-np.arange(nf, dtype=np.float32) / np.float32(nf))
    ang_row = (pos // GRID_W).astype(np.float32)[:, None] * freqs[None, :]
    ang_col = (pos % GRID_W).astype(np.float32)[:, None] * freqs[None, :]
    cos = np.concatenate([np.cos(ang_row)] * 2 + [np.cos(ang_col)] * 2, axis=-1)
    sin = np.concatenate([-np.sin(ang_row), np.sin(ang_row), -np.sin(ang_col), np.sin(ang_col)], axis=-1)
    return jnp.asarray(cos, F32), jnp.asarray(sin, F32)


def kernel(x_prompt, x_sample, state_gla, cache_k, cache_v, c, c_ctx, norm_g, w_ada, b_ada,
           gla_w_in, gla_wa1, gla_wa2, gla_ba, gla_onorm, gla_w_out,
           att_w_in, att_qnorm, att_knorm, att_w_out):
    n_dec = x_sample.shape[0]
    assert 1 + n_dec <= MOD_ROWS
    assert x_prompt.shape[1] % ROW_TILE == 0 and x_sample.shape[1] % ROW_TILE == 0

    cvec = jnp.concatenate([c_ctx[None], c, jnp.zeros((MOD_ROWS - 1 - n_dec, D_MODEL), F32)], axis=0)
    mods = _modulations(cvec, w_ada, b_ada)
    ctx_row = lambda b: 0
    dec_row = lambda b: b + 1

    cos, sin = _rope_tables(x_sample.shape[1])
    n_gla, n_att = gla_w_in.shape[0], att_w_in.shape[0]

    gla_win, gla_wout = gla_w_in.astype(BF16), gla_w_out.astype(BF16)
    wa1 = jnp.concatenate([gla_wa1[:, 0], gla_wa1[:, 1],
                           jnp.zeros((n_gla, D_MODEL, RANK_PAD - 2 * GLA_RANK), F32)], axis=-1).astype(BF16)
    zeros_qd = jnp.zeros((n_gla, GLA_RANK, GLA_QD), F32)
    wa2 = jnp.concatenate([jnp.concatenate([gla_wa2[:, 0], zeros_qd], axis=-1),
                           jnp.concatenate([zeros_qd, gla_wa2[:, 1]], axis=-1),
                           jnp.zeros((n_gla, RANK_PAD - 2 * GLA_RANK, 2 * GLA_QD), F32)], axis=1).astype(BF16)

    xp, xs = x_prompt, x_sample
    states, caches = None, None
    for l in range(DEPTH):
        i = l // 2
        ng = norm_g[l].reshape(1, D_MODEL)
        if l % 2 == 0:
            ba = gla_ba[i].reshape(1, 2 * GLA_QD)
            on = gla_onorm[i].reshape(1, GLA_DV)
            common = (ng, gla_win, wa1, wa2, ba, on, gla_wout)
            xp, states = _gla_layer(xp, mods, l, ctx_row, *common, None, i, n_gla, states)
            xs, (att_win, att_wout) = _gla_layer(xs, mods, l, dec_row, *common, state_gla, i, n_gla, None,
                                                 convert=((att_w_in, att_w_out), i))
        else:
            qn = att_qnorm[i].reshape(1, HEAD_DIM)
            kn = att_knorm[i].reshape(1, HEAD_DIM)
            xp, *caches = _att_layer(xp, mods, l, ctx_row, ng, att_win, qn, kn, att_wout, None, i, n_att, caches)
            (xs,) = _att_layer(xs, mods, l, dec_row, ng, att_win, qn, kn, att_wout, (cos, sin, cache_k, cache_v), i,
                               n_att, None)
    return (xp, xs, states, caches[0], caches[1])
```

```python
import functools
import math

import jax
import jax.numpy as jnp
import numpy as np
from jax import lax
from jax.experimental import pallas as pl
from jax.experimental.pallas import tpu as pltpu

D_MODEL = 1024
DEPTH = 4
GRID_W = 64
GLA_HEADS = 4
GLA_DK = 128
GLA_DV = 256
GLA_RANK = 16
GLA_TAU = 16.0
GLA_CHUNK = 64
GLA_QD = GLA_HEADS * GLA_DK
GLA_VD = GLA_HEADS * GLA_DV
HEAD_DIM = 128
ATT_HEADS = 8
ATT_KV_HEADS = 2
ATT_GROUP = ATT_HEADS // ATT_KV_HEADS
ATT_QD = ATT_HEADS * HEAD_DIM
ATT_KD = ATT_KV_HEADS * HEAD_DIM
ROPE_THETA = 10000.0
EPS = 1e-6

ROW_TILE = 256
GROUP_ROWS = 1024
MOD_ROWS = 8
RANK_PAD = 128
V7X_VMEM_LIMIT_BYTES = 60 * 1024 * 1024

F32 = jnp.float32
BF16 = jnp.bfloat16
_NT = (((1,), (1,)), ((), ()))
_TN = (((0,), (0,)), ((), ()))


def _dot(a, b):
    return jnp.dot(a, b, preferred_element_type=F32)


def _dot_nt(a, b):
    return lax.dot_general(a, b, _NT, preferred_element_type=F32)


def _dot_tn(a, b):
    return lax.dot_general(a, b, _TN, preferred_element_type=F32)


def _silu(x):
    return x * (1.0 / (1.0 + jnp.exp(-x)))


def _log_sigmoid(z):
    return jnp.minimum(z, 0.0) - jnp.log(1.0 + jnp.exp(-jnp.abs(z)))


def _split_top(x):
    top = pltpu.bitcast(pltpu.bitcast(x, jnp.uint32) & jnp.uint32(0xFFFF0000), F32)
    return top, x - top


def _modulated_norm(x, ng, mod_ref):
    shift = mod_ref[0, 0, :, 0:D_MODEL]
    scale = mod_ref[0, 0, :, D_MODEL:2 * D_MODEL]
    ms = jnp.mean(x * x, axis=-1, keepdims=True)
    return ((x * lax.rsqrt(ms + EPS)) * (ng * (1.0 + scale)) + shift).astype(BF16)


def _head_rms(x, g, width):
    outs = []
    for h in range(x.shape[-1] // width):
        xh = x[:, h * width:(h + 1) * width]
        ms = jnp.mean(xh * xh, axis=-1, keepdims=True)
        outs.append((xh * lax.rsqrt(ms + EPS)) * g)
    return jnp.concatenate(outs, axis=-1)


def _mod_kernel(c_ref, w_ref, b_ref, o_ref):
    acc = _dot(jnp.concatenate(_split_top(_silu(c_ref[...])), axis=0).astype(BF16), w_ref[0].astype(BF16))
    o_ref[0, :, 0, :] = acc[0:MOD_ROWS] + acc[MOD_ROWS:] + b_ref[pl.ds(pl.program_id(0), 1), :]


def _modulations(cvec, w_ada, b_ada):
    tn = D_MODEL
    n_tiles = 3 * D_MODEL // tn
    return pl.pallas_call(
        _mod_kernel,
        grid=(DEPTH, n_tiles),
        in_specs=[
            pl.BlockSpec((MOD_ROWS, D_MODEL), lambda l, j: (0, 0)),
            pl.BlockSpec((1, D_MODEL, tn), lambda l, j: (l, 0, j)),
            pl.BlockSpec((DEPTH, tn), lambda l, j: (0, j)),
        ],
        out_specs=pl.BlockSpec((1, MOD_ROWS, 1, tn), lambda l, j: (l, 0, 0, j)),
        out_shape=jax.ShapeDtypeStruct((DEPTH, MOD_ROWS, 1, 3 * D_MODEL), F32),
        compiler_params=pltpu.CompilerParams(dimension_semantics=("arbitrary", "arbitrary")),
        name="adaln_modulation",
    )(cvec, w_ada, b_ada)


def _chunk_sum_matrices():
    r = lax.broadcasted_iota(jnp.int32, (ROW_TILE, ROW_TILE), 0)
    c = lax.broadcasted_iota(jnp.int32, (ROW_TILE, ROW_TILE), 1)
    same = (r // GLA_CHUNK) == (c // GLA_CHUNK)
    prefix = jnp.where(same & (c <= r), 1.0, 0.0).astype(BF16)
    suffix = jnp.where(same & (c >= r), 1.0, 0.0).astype(BF16)
    return prefix, suffix


def _gla_kernel(*refs, seq, has_s0, emit_state, state_alias, convert_layer):
    n_seq = GROUP_ROWS // seq
    n_tiles = GROUP_ROWS // ROW_TILE
    n_pos = seq // GLA_CHUNK
    merge_products = n_seq > 1
    it = iter(refs)
    x_ref, mod_ref, ng_ref, win_ref, wa1_ref, wa2_ref, ba_ref, on_ref, wout_ref = (next(it) for _ in range(9))
    s0_ref = next(it) if has_s0 else None
    if state_alias:
        next(it)
    convert = convert_layer is not None
    conv_src = [next(it), next(it)] if convert else None
    y_ref = next(it)
    sout_ref = next(it) if emit_state else None
    conv_dst = [next(it), next(it)] if convert else None
    q_s, k_s, b_s, v_s, gate_s, o_s = (next(it) for _ in range(6))
    st_s = None if emit_state else next(it)
    if convert:
        conv_f32, conv_bf16, conv_rd, conv_wr = ([next(it), next(it)] for _ in range(4))
        slab_rows = conv_f32[0].shape[0]
        step, last_step = pl.program_id(0), pl.num_programs(0) - 1

        def slab_read(w):
            return pltpu.make_async_copy(conv_src[w].at[convert_layer, pl.ds(step * slab_rows, slab_rows), :],
                                         conv_f32[w], conv_rd[w].at[0])

        def slab_write(w):
            return pltpu.make_async_copy(conv_bf16[w], conv_dst[w].at[0, pl.ds(step * slab_rows, slab_rows), :],
                                         conv_wr[w].at[0])

        for w in range(2):
            slab_read(w).start()

        def convert_slab():
            for w in range(2):
                slab_read(w).wait()

                @pl.when(step > 0)
                def _():
                    slab_write(w).wait()

                conv_bf16[w][...] = conv_f32[w][...].astype(BF16)
                slab_write(w).start()

                @pl.when(step == last_step)
                def _():
                    slab_write(w).wait()

    def state_at(s, d, h):
        return sout_ref.at[s, 0, d, h] if emit_state else st_s.at[d, h]

    ng = ng_ref[...]
    prefix_m, suffix_m = _chunk_sum_matrices()

    def project_wide(t):
        rows = slice(t * ROW_TILE, (t + 1) * ROW_TILE)
        hb = _modulated_norm(x_ref[0, rows, :], ng, mod_ref)
        q_s[rows, :] = _dot(hb, win_ref[0, :, 0:GLA_QD]) * (GLA_DK ** -0.5)
        k_s[rows, :] = _dot(hb, win_ref[0, :, GLA_QD:2 * GLA_QD])
        v_s[rows, :] = _dot(hb, win_ref[0, :, 2 * GLA_QD:2 * GLA_QD + GLA_VD]).astype(BF16)
        gate_s[rows, :] = _dot(hb, win_ref[0, :, 2 * GLA_QD + GLA_VD:])
        low = _dot(hb, wa1_ref[0]).astype(BF16)
        return _dot(low, wa2_ref[0]) + ba_ref[...]

    def decay_sums(t, z):
        rows = slice(t * ROW_TILE, (t + 1) * ROW_TILE)
        logg2 = _log_sigmoid(z) * (math.log2(math.e) / GLA_TAU)
        for d, csum_m in enumerate((prefix_m, suffix_m)):
            parts = jnp.concatenate(_split_top(logg2[:, d * GLA_QD:(d + 1) * GLA_QD]), axis=0).astype(BF16)
            b_s[d, rows, :] = _dot(jnp.concatenate([csum_m, csum_m], axis=1), parts)

    z = project_wide(0)
    for t in range(n_tiles):
        z_next = project_wide(t + 1) if t + 1 < n_tiles else None
        decay_sums(t, z)
        z = z_next

    for s in range(n_seq):
        for d in range(2):
            for h in range(GLA_HEADS):
                state_at(s, d, h)[...] = s0_ref[0, 0, d, h] if has_s0 else jnp.zeros((GLA_DK, GLA_DV), F32)

    ri = lax.broadcasted_iota(jnp.int32, (GLA_CHUNK, GLA_CHUNK), 0)
    ci = lax.broadcasted_iota(jnp.int32, (GLA_CHUNK, GLA_CHUNK), 1)
    masks = (ci <= ri, ci >= ri)

    def scan_step(i, first_visit):
        chains = []
        for s in range(n_seq):
            for d in range(2):
                c = s * n_pos + (i if d == 0 else n_pos - 1 - i)
                rows = pl.ds(pl.multiple_of(c * GLA_CHUNK, GLA_CHUNK), GLA_CHUNK)
                b = b_s[d, rows, :]
                edge = GLA_CHUNK - 1 if d == 0 else 0
                total = b[edge:edge + 1, :]
                k = k_s[rows, :]
                qe = (q_s[rows, :] * jnp.exp2(b)).astype(BF16)
                ke = (k * jnp.exp2(-b)).astype(BF16)
                kdt = (k * jnp.exp2(total - b)).T.astype(BF16)
                dec = jnp.exp2(total)
                for h in range(GLA_HEADS):
                    kc = slice(h * GLA_DK, (h + 1) * GLA_DK)
                    chains.append((s, d, h, rows, slice(h * GLA_DV, (h + 1) * GLA_DV),
                                   qe[:, kc], ke[:, kc], kdt[kc, :], dec[:, kc]))
        scores = [_dot_nt(qe, ke) for (s, d, h, rows, vc, qe, ke, kdt, dec) in chains]
        if not merge_products:
            updates = [_dot(kdt, v_s[rows, vc]) for (s, d, h, rows, vc, qe, ke, kdt, dec) in chains]
        no_state = jnp.zeros((GLA_DK, GLA_DK), BF16)

        def decayed(st, dec):
            dec_col = jnp.broadcast_to(dec, (GLA_DK, GLA_DK)).T
            return jnp.concatenate([dec_col] * (GLA_DV // GLA_DK), axis=1) * st

        for (s, d, h, rows, vc, qe, ke, kdt, dec), sc in zip(chains, scores):
            a = jnp.where(masks[d], sc, 0.0).astype(BF16)
            st_ref = state_at(s, d, h)
            st = st_ref[...]
            lhs = jnp.concatenate([qe, a], axis=1)
            if merge_products:
                lhs = jnp.concatenate([lhs, jnp.concatenate([no_state, kdt], axis=1)], axis=0)
            both = _dot(lhs, jnp.concatenate([st.astype(BF16), v_s[rows, vc]], axis=0))
            if first_visit:
                o_s[rows, vc] = both[0:GLA_CHUNK]
            else:
                o_s[rows, vc] += both[0:GLA_CHUNK]
            if merge_products:
                st_ref[...] = decayed(st, dec) + both[GLA_CHUNK:]
        if not merge_products:
            for (s, d, h, rows, vc, qe, ke, kdt, dec), upd in zip(chains, updates):
                st_ref = state_at(s, d, h)
                st_ref[...] = decayed(st_ref[...], dec) + upd

    def first_half(i, carry):
        scan_step(i, True)
        return carry

    def second_half(i, carry):
        scan_step(i, False)
        return carry

    lax.fori_loop(0, n_pos // 2, first_half, 0)
    lax.fori_loop(n_pos // 2, n_pos, second_half, 0)
    on = on_ref[...]
    res_gate = mod_ref[0, 0, :, 2 * D_MODEL:]

    def gated(t):
        rows = slice(t * ROW_TILE, (t + 1) * ROW_TILE)
        return (_head_rms(o_s[rows, :], on, GLA_DV) * _silu(gate_s[rows, :])).astype(BF16)

    og = gated(0)
    for t in range(n_tiles):
        rows = slice(t * ROW_TILE, (t + 1) * ROW_TILE)
        og_next = gated(t + 1) if t + 1 < n_tiles else None
        y_ref[0, rows, :] = x_ref[0, rows, :] + res_gate * _dot(og, wout_ref[0])
        og = og_next

    if convert:
        convert_slab()


def _resident(shape, layer=None):
    if layer is None:
        return pl.BlockSpec(shape, lambda b: (0,) * len(shape), pipeline_mode=pl.Buffered(1))
    return pl.BlockSpec((1,) + shape, lambda b: (layer,) + (0,) * len(shape), pipeline_mode=pl.Buffered(1))


def _gla_layer(x, mods, layer, mod_row, ng, win, wa1, wa2, ba, on, wout, state_in, gla_idx, n_gla, states_so_far,
               convert=None):
    batch, seq, _ = x.shape
    n_seq = GROUP_ROWS // seq
    n_groups = batch // n_seq
    has_s0 = state_in is not None
    emit_state = not has_s0
    assert not has_s0 or n_seq == 1
    in_specs = [
        pl.BlockSpec((1, GROUP_ROWS, D_MODEL), lambda b: (b, 0, 0)),
        pl.BlockSpec((1, 1, 1, 3 * D_MODEL), lambda b: (layer, mod_row(b), 0, 0)),
        _resident((1, D_MODEL)),
        _resident((D_MODEL, 2 * GLA_QD + 2 * GLA_VD), gla_idx),
        _resident((D_MODEL, RANK_PAD), gla_idx),
        _resident((RANK_PAD, 2 * GLA_QD), gla_idx),
        _resident((1, 2 * GLA_QD)),
        _resident((1, GLA_DV)),
        _resident((GLA_VD, D_MODEL), gla_idx),
    ]
    args = [x.reshape(n_groups, GROUP_ROWS, D_MODEL), mods, ng, win, wa1, wa2, ba, on, wout]
    if has_s0:
        in_specs.append(pl.BlockSpec((1, 1, 2, GLA_HEADS, GLA_DK, GLA_DV),
                                     lambda b: (b, gla_idx, 0, 0, 0, 0)))
        args.append(state_in)
    aliases = {}
    state_alias = emit_state and states_so_far is not None
    if state_alias:
        aliases[len(args)] = 1
        in_specs.append(pl.BlockSpec(memory_space=pl.ANY))
        args.append(states_so_far)
    convert_layer = None
    if convert is not None:
        conv_weights, convert_layer = convert
        in_specs += [pl.BlockSpec(memory_space=pl.ANY)] * len(conv_weights)
        args += list(conv_weights)
    out_specs = [pl.BlockSpec((1, GROUP_ROWS, D_MODEL), lambda b: (b, 0, 0))]
    out_shape = [jax.ShapeDtypeStruct((n_groups, GROUP_ROWS, D_MODEL), F32)]
    if emit_state:
        out_specs.append(pl.BlockSpec((n_seq, 1, 2, GLA_HEADS, GLA_DK, GLA_DV), lambda b: (b, gla_idx, 0, 0, 0, 0)))
        out_shape.append(jax.ShapeDtypeStruct((batch, n_gla, 2, GLA_HEADS, GLA_DK, GLA_DV), F32))
    scratch = [
        pltpu.VMEM((GROUP_ROWS, GLA_QD), F32),
        pltpu.VMEM((GROUP_ROWS, GLA_QD), F32),
        pltpu.VMEM((2, GROUP_ROWS, GLA_QD), F32),
        pltpu.VMEM((GROUP_ROWS, GLA_VD), BF16),
        pltpu.VMEM((GROUP_ROWS, GLA_VD), F32),
        pltpu.VMEM((GROUP_ROWS, GLA_VD), F32),
    ]
    if not emit_state:
        scratch.append(pltpu.VMEM((2, GLA_HEADS, GLA_DK, GLA_DV), F32))
    if convert is not None:
        assert all(w.shape[1] % n_groups == 0 and (w.shape[1] // n_groups) % 16 == 0 for w in conv_weights)
        slabs = [(w.shape[1] // n_groups, w.shape[2]) for w in conv_weights]
        out_specs += [pl.BlockSpec(memory_space=pl.ANY)] * len(conv_weights)
        out_shape += [jax.ShapeDtypeStruct((1,) + w.shape[1:], BF16) for w in conv_weights]
        scratch += [pltpu.VMEM(slab, F32) for slab in slabs]
        scratch += [pltpu.VMEM(slab, BF16) for slab in slabs]
        scratch += [pltpu.SemaphoreType.DMA((1,)) for _ in range(2 * len(conv_weights))]
    outs = pl.pallas_call(
        functools.partial(_gla_kernel, seq=seq, has_s0=has_s0, emit_state=emit_state, state_alias=state_alias,
                          convert_layer=convert_layer),
        grid=(n_groups,),
        in_specs=in_specs,
        out_specs=out_specs,
        out_shape=out_shape,
        scratch_shapes=scratch,
        input_output_aliases=aliases,
        compiler_params=pltpu.CompilerParams(dimension_semantics=("arbitrary",),
                                             vmem_limit_bytes=V7X_VMEM_LIMIT_BYTES),
        name=f"gla_layer_seq{seq}",
    )(*args)
    y = outs[0].reshape(batch, seq, D_MODEL)
    if convert is not None:
        return y, list(outs[1:])
    return (y, outs[1]) if emit_state else (y, None)


def _rope_swap(x):
    lane = lax.broadcasted_iota(jnp.int32, x.shape, 1)
    quarter = HEAD_DIM // 4
    first = (lane % (2 * quarter)) < quarter
    return jnp.where(first, pltpu.roll(x, HEAD_DIM - quarter, 1), pltpu.roll(x, quarter, 1))


def _att_kernel(*refs, seq, latent, cache_alias):
    n_tiles = seq // ROW_TILE
    n_keys = seq + (refs[9].shape[2] if latent else 0)
    it = iter(refs)
    x_ref, mod_ref, ng_ref, win_ref, qn_ref, kn_ref, wout_ref = (next(it) for _ in range(7))
    if latent:
        cos_ref, sin_ref, ck_ref, cv_ref = (next(it) for _ in range(4))
    if cache_alias:
        next(it), next(it)
    y_ref = next(it)
    if not latent:
        kout_ref, vout_ref = next(it), next(it)
    q_s, k_s, v_s, gate_s, ao_s = (next(it) for _ in range(5))

    ng = ng_ref[...]
    qn = qn_ref[...]
    kn = kn_ref[...]
    exp2_scale = (HEAD_DIM ** -0.5) * math.log2(math.e)

    def project(t, carry):
        rows = pl.ds(pl.multiple_of(t * ROW_TILE, ROW_TILE), ROW_TILE)
        hb = _modulated_norm(x_ref[0, rows, :], ng, mod_ref)
        q = _head_rms(_dot(hb, win_ref[0, :, 0:ATT_QD]), qn, HEAD_DIM)
        k = _head_rms(_dot(hb, win_ref[0, :, ATT_QD:ATT_QD + ATT_KD]), kn, HEAD_DIM)
        v = _dot(hb, win_ref[0, :, ATT_QD + ATT_KD:ATT_QD + 2 * ATT_KD])
        gate_s[rows, :] = _dot(hb, win_ref[0, :, ATT_QD + 2 * ATT_KD:])
        if latent:
            cos = cos_ref[rows, :]
            sin = sin_ref[rows, :]
            q = jnp.concatenate(
                [q[:, h * HEAD_DIM:(h + 1) * HEAD_DIM] * cos + _rope_swap(q[:, h * HEAD_DIM:(h + 1) * HEAD_DIM]) * sin
                 for h in range(ATT_HEADS)], axis=-1)
            k = jnp.concatenate(
                [k[:, h * HEAD_DIM:(h + 1) * HEAD_DIM] * cos + _rope_swap(k[:, h * HEAD_DIM:(h + 1) * HEAD_DIM]) * sin
                 for h in range(ATT_KV_HEADS)], axis=-1)
        else:
            for h in range(ATT_KV_HEADS):
                kout_ref[0, 0, rows, h, :] = k[:, h * HEAD_DIM:(h + 1) * HEAD_DIM]
                vout_ref[0, 0, rows, h, :] = v[:, h * HEAD_DIM:(h + 1) * HEAD_DIM]
        q_s[rows, :] = (q * exp2_scale).astype(BF16)
        k_s[rows, :] = k.astype(BF16)
        for h in range(ATT_KV_HEADS):
            v_s[rows, 2 * h * HEAD_DIM:(2 * h + 1) * HEAD_DIM] = v[:, h * HEAD_DIM:(h + 1) * HEAD_DIM].astype(BF16)
        return carry

    for h in range(ATT_KV_HEADS):
        v_s[:, (2 * h + 1) * HEAD_DIM:(2 * h + 2) * HEAD_DIM] = jnp.ones((n_keys, HEAD_DIM), BF16)
    lax.fori_loop(0, n_tiles, project, 0)
    if latent:
        for h in range(ATT_KV_HEADS):
            k_s[seq:n_keys, h * HEAD_DIM:(h + 1) * HEAD_DIM] = ck_ref[0, 0, :, h, :].astype(BF16)
            v_s[seq:n_keys, 2 * h * HEAD_DIM:(2 * h + 1) * HEAD_DIM] = cv_ref[0, 0, :, h, :].astype(BF16)

    def attend(t, carry):
        rows = pl.ds(pl.multiple_of(t * ROW_TILE, ROW_TILE), ROW_TILE)

        def scores(h):
            kc = slice((h // ATT_GROUP) * HEAD_DIM, (h // ATT_GROUP + 1) * HEAD_DIM)
            return _dot_nt(q_s[rows, h * HEAD_DIM:(h + 1) * HEAD_DIM], k_s[:, kc])

        s = scores(0)
        for h in range(ATT_HEADS):
            s_next = scores(h + 1) if h + 1 < ATT_HEADS else None
            vc = slice((h // ATT_GROUP) * 2 * HEAD_DIM, (h // ATT_GROUP + 1) * 2 * HEAD_DIM)
            p = jnp.exp2(s - jnp.max(s, axis=-1, keepdims=True))
            o = _dot(p.astype(BF16), v_s[:, vc])
            ao_s[rows, h * HEAD_DIM:(h + 1) * HEAD_DIM] = o[:, 0:HEAD_DIM] / o[:, HEAD_DIM:]
            s = s_next
        return carry

    lax.fori_loop(0, n_tiles, attend, 0)

    res_gate = mod_ref[0, 0, :, 2 * D_MODEL:]

    def finish(t, carry):
        rows = pl.ds(pl.multiple_of(t * ROW_TILE, ROW_TILE), ROW_TILE)
        y = _dot((ao_s[rows, :] * _silu(gate_s[rows, :])).astype(BF16), wout_ref[0])
        y_ref[0, rows, :] = x_ref[0, rows, :] + res_gate * y
        return carry

    lax.fori_loop(0, n_tiles, finish, 0)


def _att_layer(x, mods, layer, mod_row, ng, win, qn, kn, wout, latent_inputs, att_idx, n_att, caches_so_far):
    batch, seq, _ = x.shape
    latent = latent_inputs is not None
    in_specs = [
        pl.BlockSpec((1, seq, D_MODEL), lambda b: (b, 0, 0)),
        pl.BlockSpec((1, 1, 1, 3 * D_MODEL), lambda b: (layer, mod_row(b), 0, 0)),
        _resident((1, D_MODEL)),
        _resident((D_MODEL, 2 * ATT_QD + 2 * ATT_KD), 0),
        _resident((1, HEAD_DIM)),
        _resident((1, HEAD_DIM)),
        _resident((ATT_QD, D_MODEL), 0),
    ]
    args = [x, mods, ng, win, qn, kn, wout]
    n_keys = seq
    if latent:
        cos, sin, cache_k, cache_v = latent_inputs
        past = cache_k.shape[2]
        n_keys = seq + past
        in_specs += [
            _resident((seq, HEAD_DIM)),
            _resident((seq, HEAD_DIM)),
            pl.BlockSpec((1, 1, past, ATT_KV_HEADS, HEAD_DIM), lambda b: (b, att_idx, 0, 0, 0)),
            pl.BlockSpec((1, 1, past, ATT_KV_HEADS, HEAD_DIM), lambda b: (b, att_idx, 0, 0, 0)),
        ]
        args += [cos, sin, cache_k, cache_v]
    aliases = {}
    cache_alias = (not latent) and caches_so_far is not None
    if cache_alias:
        aliases = {len(args): 1, len(args) + 1: 2}
        in_specs += [pl.BlockSpec(memory_space=pl.ANY)] * 2
        args += list(caches_so_far)
    out_specs = [pl.BlockSpec((1, seq, D_MODEL), lambda b: (b, 0, 0))]
    out_shape = [jax.ShapeDtypeStruct((batch, seq, D_MODEL), F32)]
    if not latent:
        out_specs += [pl.BlockSpec((1, 1, seq, ATT_KV_HEADS, HEAD_DIM), lambda b: (b, att_idx, 0, 0, 0))] * 2
        out_shape += [jax.ShapeDtypeStruct((batch, n_att, seq, ATT_KV_HEADS, HEAD_DIM), F32)] * 2
    scratch = [
        pltpu.VMEM((seq, ATT_QD), BF16),
        pltpu.VMEM((n_keys, ATT_KD), BF16),
        pltpu.VMEM((n_keys, 2 * ATT_KD), BF16),
        pltpu.VMEM((seq, ATT_QD), F32),
        pltpu.VMEM((seq, ATT_QD), F32),
    ]
    outs = pl.pallas_call(
        functools.partial(_att_kernel, seq=seq, latent=latent, cache_alias=cache_alias),
        grid=(batch,),
        in_specs=in_specs,
        out_specs=out_specs,
        out_shape=out_shape,
        scratch_shapes=scratch,
        input_output_aliases=aliases,
        compiler_params=pltpu.CompilerParams(dimension_semantics=("arbitrary",),
                                             vmem_limit_bytes=V7X_VMEM_LIMIT_BYTES),
        name=f"att_layer_seq{seq}",
    )(*args)
    return outs


def _rope_tables(seq):
    half = HEAD_DIM // 2
    nf = half // 2
    pos = np.arange(seq)
    freqs = np.float32(ROPE_THETA) ** (-np.arange(nf, dtype=np.float32) / np.float32(nf))
    ang_row = (pos // GRID_W).astype(np.float32)[:, None] * freqs[None, :]
    ang_col = (pos % GRID_W).astype(np.float32)[:, None] * freqs[None, :]
    cos = np.concatenate([np.cos(ang_row)] * 2 + [np.cos(ang_col)] * 2, axis=-1)
    sin = np.concatenate([-np.sin(ang_row), np.sin(ang_row), -np.sin(ang_col), np.sin(ang_col)], axis=-1)
    return jnp.asarray(cos, F32), jnp.asarray(sin, F32)


def kernel(x_prompt, x_sample, state_gla, cache_k, cache_v, c, c_ctx, norm_g, w_ada, b_ada,
           gla_w_in, gla_wa1, gla_wa2, gla_ba, gla_onorm, gla_w_out,
           att_w_in, att_qnorm, att_knorm, att_w_out):
    n_dec = x_sample.shape[0]
    assert 1 + n_dec <= MOD_ROWS
    assert x_prompt.shape[1] % ROW_TILE == 0 and x_sample.shape[1] % ROW_TILE == 0

    cvec = jnp.concatenate([c_ctx[None], c, jnp.zeros((MOD_ROWS - 1 - n_dec, D_MODEL), F32)], axis=0)
    mods = _modulations(cvec, w_ada, b_ada)
    ctx_row = lambda b: 0
    dec_row = lambda b: b + 1

    cos, sin = _rope_tables(x_sample.shape[1])
    n_gla, n_att = gla_w_in.shape[0], att_w_in.shape[0]

    gla_win, gla_wout = gla_w_in.astype(BF16), gla_w_out.astype(BF16)
    wa1 = jnp.concatenate([gla_wa1[:, 0], gla_wa1[:, 1],
                           jnp.zeros((n_gla, D_MODEL, RANK_PAD - 2 * GLA_RANK), F32)], axis=-1).astype(BF16)
    zeros_qd = jnp.zeros((n_gla, GLA_RANK, GLA_QD), F32)
    wa2 = jnp.concatenate([jnp.concatenate([gla_wa2[:, 0], zeros_qd], axis=-1),
                           jnp.concatenate([zeros_qd, gla_wa2[:, 1]], axis=-1),
                           jnp.zeros((n_gla, RANK_PAD - 2 * GLA_RANK, 2 * GLA_QD), F32)], axis=1).astype(BF16)

    xp, xs = x_prompt, x_sample
    states, caches = None, None
    for l in range(DEPTH):
        i = l // 2
        ng = norm_g[l].reshape(1, D_MODEL)
        if l % 2 == 0:
            ba = gla_ba[i].reshape(1, 2 * GLA_QD)
            on = gla_onorm[i].reshape(1, GLA_DV)
            common = (ng, gla_win, wa1, wa2, ba, on, gla_wout)
            xp, states = _gla_layer(xp, mods, l, ctx_row, *common, None, i, n_gla, states)
            xs, (att_win, att_wout) = _gla_layer(xs, mods, l, dec_row, *common, state_gla, i, n_gla, None,
                                                 convert=((att_w_in, att_w_out), i))
        else:
            qn = att_qnorm[i].reshape(1, HEAD_DIM)
            kn = att_knorm[i].reshape(1, HEAD_DIM)
            xp, *caches = _att_layer(xp, mods, l, ctx_row, ng, att_win, qn, kn, att_wout, None, i, n_att, caches)
            (xs,) = _att_layer(xs, mods, l, dec_row, ng, att_win, qn, kn, att_wout, (cos, sin, cache_k, cache_v), i,
                               n_att, None)
    return (xp, xs, states, caches[0], caches[1])
```

```python
import functools
import math

import jax
import jax.numpy as jnp
import numpy as np
from jax import lax
from jax.experimental import pallas as pl
from jax.experimental.pallas import tpu as pltpu

D_MODEL = 1024
DEPTH = 4
GRID_W = 64
GLA_HEADS = 4
GLA_DK = 128
GLA_DV = 256
GLA_RANK = 16
GLA_TAU = 16.0
GLA_CHUNK = 64
GLA_QD = GLA_HEADS * GLA_DK
GLA_VD = GLA_HEADS * GLA_DV
HEAD_DIM = 128
ATT_HEADS = 8
ATT_KV_HEADS = 2
ATT_GROUP = ATT_HEADS // ATT_KV_HEADS
ATT_QD = ATT_HEADS * HEAD_DIM
ATT_KD = ATT_KV_HEADS * HEAD_DIM
ROPE_THETA = 10000.0
EPS = 1e-6

ROW_TILE = 256
GROUP_ROWS = 1024
MOD_ROWS = 8
RANK_PAD = 128
V7X_VMEM_LIMIT_BYTES = 60 * 1024 * 1024

F32 = jnp.float32
BF16 = jnp.bfloat16
_NT = (((1,), (1,)), ((), ()))


def _dot(a, b):
    return jnp.dot(a, b, preferred_element_type=F32)


def _dot_nt(a, b):
    return lax.dot_general(a, b, _NT, preferred_element_type=F32)


def _silu(x):
    return x * (1.0 / (1.0 + jnp.exp(-x)))


def _log_sigmoid(z):
    return jnp.minimum(z, 0.0) - jnp.log(1.0 + jnp.exp(-jnp.abs(z)))


def _split_top(x):
    top = pltpu.bitcast(pltpu.bitcast(x, jnp.uint32) & jnp.uint32(0xFFFF0000), F32)
    return top, x - top


def _modulated_norm(x, ng, mod_ref):
    shift = mod_ref[0, 0, :, 0:D_MODEL]
    scale = mod_ref[0, 0, :, D_MODEL:2 * D_MODEL]
    ms = jnp.mean(x * x, axis=-1, keepdims=True)
    return ((x * lax.rsqrt(ms + EPS)) * (ng * (1.0 + scale)) + shift).astype(BF16)


def _head_rms(x, g, width):
    outs = []
    for h in range(x.shape[-1] // width):
        xh = x[:, h * width:(h + 1) * width]
        ms = jnp.mean(xh * xh, axis=-1, keepdims=True)
        outs.append((xh * lax.rsqrt(ms + EPS)) * g)
    return jnp.concatenate(outs, axis=-1)


def _mod_kernel(c_ref, w_ref, b_ref, o_ref):
    acc = _dot(jnp.concatenate(_split_top(_silu(c_ref[...])), axis=0).astype(BF16), w_ref[0].astype(BF16))
    o_ref[0, :, 0, :] = acc[0:MOD_ROWS] + acc[MOD_ROWS:] + b_ref[pl.ds(pl.program_id(0), 1), :]


def _modulations(cvec, w_ada, b_ada):
    tn = D_MODEL
    n_tiles = 3 * D_MODEL // tn
    return pl.pallas_call(
        _mod_kernel,
        grid=(DEPTH, n_tiles),
        in_specs=[
            pl.BlockSpec((MOD_ROWS, D_MODEL), lambda l, j: (0, 0)),
            pl.BlockSpec((1, D_MODEL, tn), lambda l, j: (l, 0, j)),
            pl.BlockSpec((DEPTH, tn), lambda l, j: (0, j)),
        ],
        out_specs=pl.BlockSpec((1, MOD_ROWS, 1, tn), lambda l, j: (l, 0, 0, j)),
        out_shape=jax.ShapeDtypeStruct((DEPTH, MOD_ROWS, 1, 3 * D_MODEL), F32),
        compiler_params=pltpu.CompilerParams(dimension_semantics=("arbitrary", "arbitrary")),
        name="adaln_modulation",
    )(cvec, w_ada, b_ada)


def _chunk_sum_matrices():
    r = lax.broadcasted_iota(jnp.int32, (ROW_TILE, ROW_TILE), 0)
    c = lax.broadcasted_iota(jnp.int32, (ROW_TILE, ROW_TILE), 1)
    same = (r // GLA_CHUNK) == (c // GLA_CHUNK)
    prefix = jnp.where(same & (c <= r), 1.0, 0.0).astype(BF16)
    suffix = jnp.where(same & (c >= r), 1.0, 0.0).astype(BF16)
    return prefix, suffix


def _gla_kernel(*refs, seq, has_s0, emit_state, state_alias):
    n_seq = GROUP_ROWS // seq
    n_tiles = GROUP_ROWS // ROW_TILE
    n_pos = seq // GLA_CHUNK
    merge_products = n_seq > 1
    it = iter(refs)
    x_ref, mod_ref, ng_ref, win_ref, wa1_ref, wa2_ref, ba_ref, on_ref, wout_ref = (next(it) for _ in range(9))
    s0_ref = next(it) if has_s0 else None
    if state_alias:
        next(it)
    y_ref = next(it)
    sout_ref = next(it) if emit_state else None
    q_s, k_s, b_s, v_s, gate_s, o_s = (next(it) for _ in range(6))
    st_s = None if emit_state else next(it)

    def state_at(s, d, h):
        return sout_ref.at[s, 0, d, h] if emit_state else st_s.at[d, h]

    ng = ng_ref[...]
    prefix_m, suffix_m = _chunk_sum_matrices()

    def project_wide(t):
        rows = slice(t * ROW_TILE, (t + 1) * ROW_TILE)
        hb = _modulated_norm(x_ref[0, rows, :], ng, mod_ref)
        q_s[rows, :] = _dot(hb, win_ref[0, :, 0:GLA_QD]) * (GLA_DK ** -0.5)
        k_s[rows, :] = _dot(hb, win_ref[0, :, GLA_QD:2 * GLA_QD])
        v_s[rows, :] = _dot(hb, win_ref[0, :, 2 * GLA_QD:2 * GLA_QD + GLA_VD]).astype(BF16)
        gate_s[rows, :] = _dot(hb, win_ref[0, :, 2 * GLA_QD + GLA_VD:])
        low = _dot(hb, wa1_ref[0]).astype(BF16)
        return _dot(low, wa2_ref[0]) + ba_ref[...]

    def decay_sums(t, z):
        rows = slice(t * ROW_TILE, (t + 1) * ROW_TILE)
        logg2 = _log_sigmoid(z) * (math.log2(math.e) / GLA_TAU)
        for d, csum_m in enumerate((prefix_m, suffix_m)):
            parts = jnp.concatenate(_split_top(logg2[:, d * GLA_QD:(d + 1) * GLA_QD]), axis=0).astype(BF16)
            b_s[d, rows, :] = _dot(jnp.concatenate([csum_m, csum_m], axis=1), parts)

    z = project_wide(0)
    for t in range(n_tiles):
        z_next = project_wide(t + 1) if t + 1 < n_tiles else None
        decay_sums(t, z)
        z = z_next

    for s in range(n_seq):
        for d in range(2):
            for h in range(GLA_HEADS):
                state_at(s, d, h)[...] = s0_ref[0, 0, d, h] if has_s0 else jnp.zeros((GLA_DK, GLA_DV), F32)

    ri = lax.broadcasted_iota(jnp.int32, (GLA_CHUNK, GLA_CHUNK), 0)
    ci = lax.broadcasted_iota(jnp.int32, (GLA_CHUNK, GLA_CHUNK), 1)
    masks = (ci <= ri, ci >= ri)

    def scan_step(i, first_visit):
        chains = []
        for s in range(n_seq):
            for d in range(2):
                c = s * n_pos + (i if d == 0 else n_pos - 1 - i)
                rows = pl.ds(pl.multiple_of(c * GLA_CHUNK, GLA_CHUNK), GLA_CHUNK)
                b = b_s[d, rows, :]
                edge = GLA_CHUNK - 1 if d == 0 else 0
                total = b[edge:edge + 1, :]
                k = k_s[rows, :]
                qe = (q_s[rows, :] * jnp.exp2(b)).astype(BF16)
                ke = (k * jnp.exp2(-b)).astype(BF16)
                kdt = (k * jnp.exp2(total - b)).T.astype(BF16)
                dec = jnp.exp2(total)
                for h in range(GLA_HEADS):
                    kc = slice(h * GLA_DK, (h + 1) * GLA_DK)
                    chains.append((s, d, h, rows, slice(h * GLA_DV, (h + 1) * GLA_DV),
                                   qe[:, kc], ke[:, kc], kdt[kc, :], dec[:, kc]))
        scores = [_dot_nt(qe, ke) for (s, d, h, rows, vc, qe, ke, kdt, dec) in chains]
        if not merge_products:
            updates = [_dot(kdt, v_s[rows, vc]) for (s, d, h, rows, vc, qe, ke, kdt, dec) in chains]
        no_state = jnp.zeros((GLA_DK, GLA_DK), BF16)

        def decayed(st, dec):
            dec_col = jnp.broadcast_to(dec, (GLA_DK, GLA_DK)).T
            return jnp.concatenate([dec_col] * (GLA_DV // GLA_DK), axis=1) * st

        for (s, d, h, rows, vc, qe, ke, kdt, dec), sc in zip(chains, scores):
            a = jnp.where(masks[d], sc, 0.0).astype(BF16)
            st_ref = state_at(s, d, h)
            st = st_ref[...]
            lhs = jnp.concatenate([qe, a], axis=1)
            if merge_products:
                lhs = jnp.concatenate([lhs, jnp.concatenate([no_state, kdt], axis=1)], axis=0)
            both = _dot(lhs, jnp.concatenate([st.astype(BF16), v_s[rows, vc]], axis=0))
            if first_visit:
                o_s[rows, vc] = both[0:GLA_CHUNK]
            else:
                o_s[rows, vc] += both[0:GLA_CHUNK]
            if merge_products:
                st_ref[...] = decayed(st, dec) + both[GLA_CHUNK:]
        if not merge_products:
            for (s, d, h, rows, vc, qe, ke, kdt, dec), upd in zip(chains, updates):
                st_ref = state_at(s, d, h)
                st_ref[...] = decayed(st_ref[...], dec) + upd

    def first_half(i, carry):
        scan_step(i, True)
        return carry

    def second_half(i, carry):
        scan_step(i, False)
        return carry

    lax.fori_loop(0, n_pos // 2, first_half, 0)
    lax.fori_loop(n_pos // 2, n_pos, second_half, 0)

    on = on_ref[...]
    res_gate = mod_ref[0, 0, :, 2 * D_MODEL:]

    def gated(t):
        rows = slice(t * ROW_TILE, (t + 1) * ROW_TILE)
        return (_head_rms(o_s[rows, :], on, GLA_DV) * _silu(gate_s[rows, :])).astype(BF16)

    og = gated(0)
    for t in range(n_tiles):
        rows = slice(t * ROW_TILE, (t + 1) * ROW_TILE)
        og_next = gated(t + 1) if t + 1 < n_tiles else None
        y_ref[0, rows, :] = x_ref[0, rows, :] + res_gate * _dot(og, wout_ref[0])
        og = og_next


def _resident(shape, layer=None):
    if layer is None:
        return pl.BlockSpec(shape, lambda b: (0,) * len(shape), pipeline_mode=pl.Buffered(1))
    return pl.BlockSpec((1,) + shape, lambda b: (layer,) + (0,) * len(shape), pipeline_mode=pl.Buffered(1))


def _gla_layer(x, mods, layer, mod_row, ng, win, wa1, wa2, ba, on, wout, state_in, gla_idx, n_gla, states_so_far):
    batch, seq, _ = x.shape
    n_seq = GROUP_ROWS // seq
    n_groups = batch // n_seq
    has_s0 = state_in is not None
    emit_state = not has_s0
    assert not has_s0 or n_seq == 1
    in_specs = [
        pl.BlockSpec((1, GROUP_ROWS, D_MODEL), lambda b: (b, 0, 0)),
        pl.BlockSpec((1, 1, 1, 3 * D_MODEL), lambda b: (layer, mod_row(b), 0, 0)),
        _resident((1, D_MODEL)),
        _resident((D_MODEL, 2 * GLA_QD + 2 * GLA_VD), gla_idx),
        _resident((D_MODEL, RANK_PAD), gla_idx),
        _resident((RANK_PAD, 2 * GLA_QD), gla_idx),
        _resident((1, 2 * GLA_QD)),
        _resident((1, GLA_DV)),
        _resident((GLA_VD, D_MODEL), gla_idx),
    ]
    args = [x.reshape(n_groups, GROUP_ROWS, D_MODEL), mods, ng, win, wa1, wa2, ba, on, wout]
    if has_s0:
        in_specs.append(pl.BlockSpec((1, 1, 2, GLA_HEADS, GLA_DK, GLA_DV),
                                     lambda b: (b, gla_idx, 0, 0, 0, 0)))
        args.append(state_in)
    aliases = {}
    state_alias = emit_state and states_so_far is not None
    if state_alias:
        aliases[len(args)] = 1
        in_specs.append(pl.BlockSpec(memory_space=pl.ANY))
        args.append(states_so_far)
    out_specs = [pl.BlockSpec((1, GROUP_ROWS, D_MODEL), lambda b: (b, 0, 0))]
    out_shape = [jax.ShapeDtypeStruct((n_groups, GROUP_ROWS, D_MODEL), F32)]
    if emit_state:
        out_specs.append(pl.BlockSpec((n_seq, 1, 2, GLA_HEADS, GLA_DK, GLA_DV), lambda b: (b, gla_idx, 0, 0, 0, 0)))
        out_shape.append(jax.ShapeDtypeStruct((batch, n_gla, 2, GLA_HEADS, GLA_DK, GLA_DV), F32))
    scratch = [
        pltpu.VMEM((GROUP_ROWS, GLA_QD), F32),
        pltpu.VMEM((GROUP_ROWS, GLA_QD), F32),
        pltpu.VMEM((2, GROUP_ROWS, GLA_QD), F32),
        pltpu.VMEM((GROUP_ROWS, GLA_VD), BF16),
        pltpu.VMEM((GROUP_ROWS, GLA_VD), F32),
        pltpu.VMEM((GROUP_ROWS, GLA_VD), F32),
    ]
    if not emit_state:
        scratch.append(pltpu.VMEM((2, GLA_HEADS, GLA_DK, GLA_DV), F32))
    outs = pl.pallas_call(
        functools.partial(_gla_kernel, seq=seq, has_s0=has_s0, emit_state=emit_state, state_alias=state_alias),
        grid=(n_groups,),
        in_specs=in_specs,
        out_specs=out_specs,
        out_shape=out_shape,
        scratch_shapes=scratch,
        input_output_aliases=aliases,
        compiler_params=pltpu.CompilerParams(dimension_semantics=("arbitrary",),
                                             vmem_limit_bytes=V7X_VMEM_LIMIT_BYTES),
        name=f"gla_layer_seq{seq}",
    )(*args)
    y = outs[0].reshape(batch, seq, D_MODEL)
    return (y, outs[1]) if emit_state else (y, None)


def _rope_swap(x):
    lane = lax.broadcasted_iota(jnp.int32, x.shape, 1)
    quarter = HEAD_DIM // 4
    first = (lane % (2 * quarter)) < quarter
    return jnp.where(first, pltpu.roll(x, HEAD_DIM - quarter, 1), pltpu.roll(x, quarter, 1))


def _att_kernel(*refs, seq, latent, cache_alias):
    n_tiles = seq // ROW_TILE
    n_keys = seq + (refs[9].shape[2] if latent else 0)
    it = iter(refs)
    x_ref, mod_ref, ng_ref, win_ref, qn_ref, kn_ref, wout_ref = (next(it) for _ in range(7))
    if latent:
        cos_ref, sin_ref, ck_ref, cv_ref = (next(it) for _ in range(4))
    if cache_alias:
        next(it), next(it)
    y_ref = next(it)
    if not latent:
        kout_ref, vout_ref = next(it), next(it)
    q_s, k_s, v_s, gate_s, ao_s = (next(it) for _ in range(5))

    ng = ng_ref[...]
    qn = qn_ref[...]
    kn = kn_ref[...]
    exp2_scale = (HEAD_DIM ** -0.5) * math.log2(math.e)

    def project(t, carry):
        rows = pl.ds(pl.multiple_of(t * ROW_TILE, ROW_TILE), ROW_TILE)
        hb = _modulated_norm(x_ref[0, rows, :], ng, mod_ref)
        q = _head_rms(_dot(hb, win_ref[0, :, 0:ATT_QD]), qn, HEAD_DIM)
        k = _head_rms(_dot(hb, win_ref[0, :, ATT_QD:ATT_QD + ATT_KD]), kn, HEAD_DIM)
        v = _dot(hb, win_ref[0, :, ATT_QD + ATT_KD:ATT_QD + 2 * ATT_KD])
        gate_s[rows, :] = _dot(hb, win_ref[0, :, ATT_QD + 2 * ATT_KD:])
        if latent:
            cos = cos_ref[rows, :]
            sin = sin_ref[rows, :]
            q = jnp.concatenate(
                [q[:, h * HEAD_DIM:(h + 1) * HEAD_DIM] * cos + _rope_swap(q[:, h * HEAD_DIM:(h + 1) * HEAD_DIM]) * sin
                 for h in range(ATT_HEADS)], axis=-1)
            k = jnp.concatenate(
                [k[:, h * HEAD_DIM:(h + 1) * HEAD_DIM] * cos + _rope_swap(k[:, h * HEAD_DIM:(h + 1) * HEAD_DIM]) * sin
                 for h in range(ATT_KV_HEADS)], axis=-1)
        else:
            for h in range(ATT_KV_HEADS):
                kout_ref[0, 0, rows, h, :] = k[:, h * HEAD_DIM:(h + 1) * HEAD_DIM]
                vout_ref[0, 0, rows, h, :] = v[:, h * HEAD_DIM:(h + 1) * HEAD_DIM]
        q_s[rows, :] = (q * exp2_scale).astype(BF16)
        k_s[rows, :] = k.astype(BF16)
        for h in range(ATT_KV_HEADS):
            v_s[rows, 2 * h * HEAD_DIM:(2 * h + 1) * HEAD_DIM] = v[:, h * HEAD_DIM:(h + 1) * HEAD_DIM].astype(BF16)
        return carry

    for h in range(ATT_KV_HEADS):
        v_s[:, (2 * h + 1) * HEAD_DIM:(2 * h + 2) * HEAD_DIM] = jnp.ones((n_keys, HEAD_DIM), BF16)
    lax.fori_loop(0, n_tiles, project, 0)
    if latent:
        for h in range(ATT_KV_HEADS):
            k_s[seq:n_keys, h * HEAD_DIM:(h + 1) * HEAD_DIM] = ck_ref[0, 0, :, h, :].astype(BF16)
            v_s[seq:n_keys, 2 * h * HEAD_DIM:(2 * h + 1) * HEAD_DIM] = cv_ref[0, 0, :, h, :].astype(BF16)

    def attend(t, carry):
        rows = pl.ds(pl.multiple_of(t * ROW_TILE, ROW_TILE), ROW_TILE)

        def scores(h):
            kc = slice((h // ATT_GROUP) * HEAD_DIM, (h // ATT_GROUP + 1) * HEAD_DIM)
            return _dot_nt(q_s[rows, h * HEAD_DIM:(h + 1) * HEAD_DIM], k_s[:, kc])

        s = scores(0)
        for h in range(ATT_HEADS):
            s_next = scores(h + 1) if h + 1 < ATT_HEADS else None
            vc = slice((h // ATT_GROUP) * 2 * HEAD_DIM, (h // ATT_GROUP + 1) * 2 * HEAD_DIM)
            p = jnp.exp2(s - jnp.max(s, axis=-1, keepdims=True))
            o = _dot(p.astype(BF16), v_s[:, vc])
            ao_s[rows, h * HEAD_DIM:(h + 1) * HEAD_DIM] = o[:, 0:HEAD_DIM] / o[:, HEAD_DIM:]
            s = s_next
        return carry

    lax.fori_loop(0, n_tiles, attend, 0)

    res_gate = mod_ref[0, 0, :, 2 * D_MODEL:]

    def finish(t, carry):
        rows = pl.ds(pl.multiple_of(t * ROW_TILE, ROW_TILE), ROW_TILE)
        y = _dot((ao_s[rows, :] * _silu(gate_s[rows, :])).astype(BF16), wout_ref[0])
        y_ref[0, rows, :] = x_ref[0, rows, :] + res_gate * y
        return carry

    lax.fori_loop(0, n_tiles, finish, 0)


def _att_layer(x, mods, layer, mod_row, ng, win, qn, kn, wout, latent_inputs, att_idx, n_att, caches_so_far):
    batch, seq, _ = x.shape
    latent = latent_inputs is not None
    in_specs = [
        pl.BlockSpec((1, seq, D_MODEL), lambda b: (b, 0, 0)),
        pl.BlockSpec((1, 1, 1, 3 * D_MODEL), lambda b: (layer, mod_row(b), 0, 0)),
        _resident((1, D_MODEL)),
        _resident((D_MODEL, 2 * ATT_QD + 2 * ATT_KD), att_idx),
        _resident((1, HEAD_DIM)),
        _resident((1, HEAD_DIM)),
        _resident((ATT_QD, D_MODEL), att_idx),
    ]
    args = [x, mods, ng, win, qn, kn, wout]
    n_keys = seq
    if latent:
        cos, sin, cache_k, cache_v = latent_inputs
        past = cache_k.shape[2]
        n_keys = seq + past
        in_specs += [
            _resident((seq, HEAD_DIM)),
            _resident((seq, HEAD_DIM)),
            pl.BlockSpec((1, 1, past, ATT_KV_HEADS, HEAD_DIM), lambda b: (b, att_idx, 0, 0, 0)),
            pl.BlockSpec((1, 1, past, ATT_KV_HEADS, HEAD_DIM), lambda b: (b, att_idx, 0, 0, 0)),
        ]
        args += [cos, sin, cache_k, cache_v]
    aliases = {}
    cache_alias = (not latent) and caches_so_far is not None
    if cache_alias:
        aliases = {len(args): 1, len(args) + 1: 2}
        in_specs += [pl.BlockSpec(memory_space=pl.ANY)] * 2
        args += list(caches_so_far)
    out_specs = [pl.BlockSpec((1, seq, D_MODEL), lambda b: (b, 0, 0))]
    out_shape = [jax.ShapeDtypeStruct((batch, seq, D_MODEL), F32)]
    if not latent:
        out_specs += [pl.BlockSpec((1, 1, seq, ATT_KV_HEADS, HEAD_DIM), lambda b: (b, att_idx, 0, 0, 0))] * 2
        out_shape += [jax.ShapeDtypeStruct((batch, n_att, seq, ATT_KV_HEADS, HEAD_DIM), F32)] * 2
    scratch = [
        pltpu.VMEM((seq, ATT_QD), BF16),
        pltpu.VMEM((n_keys, ATT_KD), BF16),
        pltpu.VMEM((n_keys, 2 * ATT_KD), BF16),
        pltpu.VMEM((seq, ATT_QD), F32),
        pltpu.VMEM((seq, ATT_QD), F32),
    ]
    outs = pl.pallas_call(
        functools.partial(_att_kernel, seq=seq, latent=latent, cache_alias=cache_alias),
        grid=(batch,),
        in_specs=in_specs,
        out_specs=out_specs,
        out_shape=out_shape,
        scratch_shapes=scratch,
        input_output_aliases=aliases,
        compiler_params=pltpu.CompilerParams(dimension_semantics=("arbitrary",),
                                             vmem_limit_bytes=V7X_VMEM_LIMIT_BYTES),
        name=f"att_layer_seq{seq}",
    )(*args)
    return outs


def _rope_tables(seq):
    half = HEAD_DIM // 2
    nf = half // 2
    pos = np.arange(seq)
    freqs = np.float32(ROPE_THETA) ** (-np.arange(nf, dtype=np.float32) / np.float32(nf))
    ang_row = (pos // GRID_W).astype(np.float32)[:, None] * freqs[None, :]
    ang_col = (pos % GRID_W).astype(np.float32)[:, None] * freqs[None, :]
    cos = np.concatenate([np.cos(ang_row)] * 2 + [np.cos(ang_col)] * 2, axis=-1)
    sin = np.concatenate([-np.sin(ang_row), np.sin(ang_row), -np.sin(ang_col), np.sin(ang_col)], axis=-1)
    return jnp.asarray(cos, F32), jnp.asarray(sin, F32)


def kernel(x_prompt, x_sample, state_gla, cache_k, cache_v, c, c_ctx, norm_g, w_ada, b_ada,
           gla_w_in, gla_wa1, gla_wa2, gla_ba, gla_onorm, gla_w_out,
           att_w_in, att_qnorm, att_knorm, att_w_out):
    n_dec = x_sample.shape[0]
    assert 1 + n_dec <= MOD_ROWS
    assert x_prompt.shape[1] % ROW_TILE == 0 and x_sample.shape[1] % ROW_TILE == 0

    cvec = jnp.concatenate([c_ctx[None], c, jnp.zeros((MOD_ROWS - 1 - n_dec, D_MODEL), F32)], axis=0)
    mods = _modulations(cvec, w_ada, b_ada)
    ctx_row = lambda b: 0
    dec_row = lambda b: b + 1

    cos, sin = _rope_tables(x_sample.shape[1])
    n_gla, n_att = gla_w_in.shape[0], att_w_in.shape[0]

    gla_win, gla_wout = gla_w_in.astype(BF16), gla_w_out.astype(BF16)
    att_win, att_wout = att_w_in.astype(BF16), att_w_out.astype(BF16)
    wa1 = jnp.concatenate([gla_wa1[:, 0], gla_wa1[:, 1],
                           jnp.zeros((n_gla, D_MODEL, RANK_PAD - 2 * GLA_RANK), F32)], axis=-1).astype(BF16)
    zeros_qd = jnp.zeros((n_gla, GLA_RANK, GLA_QD), F32)
    wa2 = jnp.concatenate([jnp.concatenate([gla_wa2[:, 0], zeros_qd], axis=-1),
                           jnp.concatenate([zeros_qd, gla_wa2[:, 1]], axis=-1),
                           jnp.zeros((n_gla, RANK_PAD - 2 * GLA_RANK, 2 * GLA_QD), F32)], axis=1).astype(BF16)

    xp, xs = x_prompt, x_sample
    states, caches = None, None
    for l in range(DEPTH):
        i = l // 2
        ng = norm_g[l].reshape(1, D_MODEL)
        if l % 2 == 0:
            ba = gla_ba[i].reshape(1, 2 * GLA_QD)
            on = gla_onorm[i].reshape(1, GLA_DV)
            common = (ng, gla_win, wa1, wa2, ba, on, gla_wout)
            xp, states = _gla_layer(xp, mods, l, ctx_row, *common, None, i, n_gla, states)
            xs, _ = _gla_layer(xs, mods, l, dec_row, *common, state_gla, i, n_gla, None)
        else:
            qn = att_qnorm[i].reshape(1, HEAD_DIM)
            kn = att_knorm[i].reshape(1, HEAD_DIM)
            xp, *caches = _att_layer(xp, mods, l, ctx_row, ng, att_win, qn, kn, att_wout, None, i, n_att, caches)
            (xs,) = _att_layer(xs, mods, l, dec_row, ng, att_win, qn, kn, att_wout, (cos, sin, cache_k, cache_v), i,
                               n_att, None)
    return (xp, xs, states, caches[0], caches[1])
```

```python
import functools
import math

import jax
import jax.numpy as jnp
import numpy as np
from jax import lax
from jax.experimental import pallas as pl
from jax.experimental.pallas import tpu as pltpu

D_MODEL = 1024
DEPTH = 4
GRID_W = 64
GLA_HEADS = 4
GLA_DK = 128
GLA_DV = 256
GLA_RANK = 16
GLA_TAU = 16.0
GLA_CHUNK = 64
GLA_QD = GLA_HEADS * GLA_DK
GLA_VD = GLA_HEADS * GLA_DV
HEAD_DIM = 128
ATT_HEADS = 8
ATT_KV_HEADS = 2
ATT_GROUP = ATT_HEADS // ATT_KV_HEADS
ATT_QD = ATT_HEADS * HEAD_DIM
ATT_KD = ATT_KV_HEADS * HEAD_DIM
ROPE_THETA = 10000.0
EPS = 1e-6

ROW_TILE = 256
GROUP_ROWS = 1024
MOD_ROWS = 8
RANK_PAD = 128
V7X_VMEM_LIMIT_BYTES = 60 * 1024 * 1024

F32 = jnp.float32
BF16 = jnp.bfloat16
_NT = (((1,), (1,)), ((), ()))


def _dot(a, b):
    return jnp.dot(a, b, preferred_element_type=F32)


def _dot_nt(a, b):
    return lax.dot_general(a, b, _NT, preferred_element_type=F32)


def _silu(x):
    return x * (1.0 / (1.0 + jnp.exp(-x)))


def _log_sigmoid(z):
    return jnp.minimum(z, 0.0) - jnp.log(1.0 + jnp.exp(-jnp.abs(z)))


def _split_top(x):
    top = pltpu.bitcast(pltpu.bitcast(x, jnp.uint32) & jnp.uint32(0xFFFF0000), F32)
    return top, x - top


def _modulated_norm(x, ng, mod_ref):
    shift = mod_ref[0, 0, :, 0:D_MODEL]
    scale = mod_ref[0, 0, :, D_MODEL:2 * D_MODEL]
    ms = jnp.mean(x * x, axis=-1, keepdims=True)
    return ((x * lax.rsqrt(ms + EPS)) * (ng * (1.0 + scale)) + shift).astype(BF16)


def _head_rms(x, g, width):
    outs = []
    for h in range(x.shape[-1] // width):
        xh = x[:, h * width:(h + 1) * width]
        ms = jnp.mean(xh * xh, axis=-1, keepdims=True)
        outs.append((xh * lax.rsqrt(ms + EPS)) * g)
    return jnp.concatenate(outs, axis=-1)


def _mod_kernel(c_ref, w_ref, b_ref, o_ref):
    acc = _dot(jnp.concatenate(_split_top(_silu(c_ref[...])), axis=0).astype(BF16), w_ref[0].astype(BF16))
    o_ref[0, :, 0, :] = acc[0:MOD_ROWS] + acc[MOD_ROWS:] + b_ref[pl.ds(pl.program_id(0), 1), :]


def _modulations(cvec, w_ada, b_ada):
    tn = D_MODEL
    n_tiles = 3 * D_MODEL // tn
    return pl.pallas_call(
        _mod_kernel,
        grid=(DEPTH, n_tiles),
        in_specs=[
            pl.BlockSpec((MOD_ROWS, D_MODEL), lambda l, j: (0, 0)),
            pl.BlockSpec((1, D_MODEL, tn), lambda l, j: (l, 0, j)),
            pl.BlockSpec((DEPTH, tn), lambda l, j: (0, j)),
        ],
        out_specs=pl.BlockSpec((1, MOD_ROWS, 1, tn), lambda l, j: (l, 0, 0, j)),
        out_shape=jax.ShapeDtypeStruct((DEPTH, MOD_ROWS, 1, 3 * D_MODEL), F32),
        compiler_params=pltpu.CompilerParams(dimension_semantics=("arbitrary", "arbitrary")),
        name="adaln_modulation",
    )(cvec, w_ada, b_ada)


def _chunk_sum_matrices():
    r = lax.broadcasted_iota(jnp.int32, (ROW_TILE, ROW_TILE), 0)
    c = lax.broadcasted_iota(jnp.int32, (ROW_TILE, ROW_TILE), 1)
    same = (r // GLA_CHUNK) == (c // GLA_CHUNK)
    prefix = jnp.where(same & (c <= r), 1.0, 0.0).astype(BF16)
    suffix = jnp.where(same & (c >= r), 1.0, 0.0).astype(BF16)
    return prefix, suffix


def _gla_kernel(*refs, seq, has_s0, emit_state, state_alias):
    n_seq = GROUP_ROWS // seq
    n_tiles = GROUP_ROWS // ROW_TILE
    n_pos = seq // GLA_CHUNK
    merge_products = n_seq > 1
    it = iter(refs)
    x_ref, mod_ref, ng_ref, win_ref, wa1_ref, wa2_ref, ba_ref, on_ref, wout_ref = (next(it) for _ in range(9))
    s0_ref = next(it) if has_s0 else None
    if state_alias:
        next(it)
    y_ref = next(it)
    sout_ref = next(it) if emit_state else None
    q_s, k_s, b_s, v_s, gate_s, o_s = (next(it) for _ in range(6))
    st_s = None if emit_state else next(it)

    def state_at(s, d, h):
        return sout_ref.at[s, 0, d, h] if emit_state else st_s.at[d, h]

    ng = ng_ref[...]
    prefix_m, suffix_m = _chunk_sum_matrices()

    def project_wide(t):
        rows = slice(t * ROW_TILE, (t + 1) * ROW_TILE)
        hb = _modulated_norm(x_ref[0, rows, :], ng, mod_ref)
        q_s[rows, :] = _dot(hb, win_ref[0, :, 0:GLA_QD]) * (GLA_DK ** -0.5)
        k_s[rows, :] = _dot(hb, win_ref[0, :, GLA_QD:2 * GLA_QD])
        v_s[rows, :] = _dot(hb, win_ref[0, :, 2 * GLA_QD:2 * GLA_QD + GLA_VD]).astype(BF16)
        gate_s[rows, :] = _dot(hb, win_ref[0, :, 2 * GLA_QD + GLA_VD:])
        low = _dot(hb, wa1_ref[0]).astype(BF16)
        return _dot(low, wa2_ref[0]) + ba_ref[...]

    def decay_sums(t, z):
        rows = slice(t * ROW_TILE, (t + 1) * ROW_TILE)
        logg2 = _log_sigmoid(z) * (math.log2(math.e) / GLA_TAU)
        for d, csum_m in enumerate((prefix_m, suffix_m)):
            parts = jnp.concatenate(_split_top(logg2[:, d * GLA_QD:(d + 1) * GLA_QD]), axis=0).astype(BF16)
            b_s[d, rows, :] = _dot(jnp.concatenate([csum_m, csum_m], axis=1), parts)

    z = project_wide(0)
    for t in range(n_tiles):
        z_next = project_wide(t + 1) if t + 1 < n_tiles else None
        decay_sums(t, z)
        z = z_next

    for s in range(n_seq):
        for d in range(2):
            for h in range(GLA_HEADS):
                state_at(s, d, h)[...] = s0_ref[0, 0, d, h] if has_s0 else jnp.zeros((GLA_DK, GLA_DV), F32)

    ri = lax.broadcasted_iota(jnp.int32, (GLA_CHUNK, GLA_CHUNK), 0)
    ci = lax.broadcasted_iota(jnp.int32, (GLA_CHUNK, GLA_CHUNK), 1)
    masks = (ci <= ri, ci >= ri)

    def scan_step(i, first_visit):
        chains = []
        for s in range(n_seq):
            for d in range(2):
                c = s * n_pos + (i if d == 0 else n_pos - 1 - i)
                rows = pl.ds(pl.multiple_of(c * GLA_CHUNK, GLA_CHUNK), GLA_CHUNK)
                b = b_s[d, rows, :]
                edge = GLA_CHUNK - 1 if d == 0 else 0
                total = b[edge:edge + 1, :]
                k = k_s[rows, :]
                qe = (q_s[rows, :] * jnp.exp2(b)).astype(BF16)
                ke = (k * jnp.exp2(-b)).astype(BF16)
                kdt = (k * jnp.exp2(total - b)).T.astype(BF16)
                dec = jnp.exp2(total)
                for h in range(GLA_HEADS):
                    kc = slice(h * GLA_DK, (h + 1) * GLA_DK)
                    chains.append((s, d, h, rows, slice(h * GLA_DV, (h + 1) * GLA_DV),
                                   qe[:, kc], ke[:, kc], kdt[kc, :], dec[:, kc]))
        scores = [_dot_nt(qe, ke) for (s, d, h, rows, vc, qe, ke, kdt, dec) in chains]
        if not merge_products:
            updates = [_dot(kdt, v_s[rows, vc]) for (s, d, h, rows, vc, qe, ke, kdt, dec) in chains]
        no_state = jnp.zeros((GLA_DK, GLA_DK), BF16)

        def decayed(st, dec):
            dec_col = jnp.broadcast_to(dec, (GLA_DK, GLA_DK)).T
            return jnp.concatenate([dec_col] * (GLA_DV // GLA_DK), axis=1) * st

        for (s, d, h, rows, vc, qe, ke, kdt, dec), sc in zip(chains, scores):
            a = jnp.where(masks[d], sc, 0.0).astype(BF16)
            st_ref = state_at(s, d, h)
            st = st_ref[...]
            lhs = jnp.concatenate([qe, a], axis=1)
            if merge_products:
                lhs = jnp.concatenate([lhs, jnp.concatenate([no_state, kdt], axis=1)], axis=0)
            both = _dot(lhs, jnp.concatenate([st.astype(BF16), v_s[rows, vc]], axis=0))
            if first_visit:
                o_s[rows, vc] = both[0:GLA_CHUNK]
            else:
                o_s[rows, vc] += both[0:GLA_CHUNK]
            if merge_products:
                st_ref[...] = decayed(st, dec) + both[GLA_CHUNK:]
        if not merge_products:
            for (s, d, h, rows, vc, qe, ke, kdt, dec), upd in zip(chains, updates):
                st_ref = state_at(s, d, h)
                st_ref[...] = decayed(st_ref[...], dec) + upd

    def first_half(i, carry):
        scan_step(i, True)
        return carry

    def second_half(i, carry):
        scan_step(i, False)
        return carry

    lax.fori_loop(0, n_pos // 2, first_half, 0)
    lax.fori_loop(n_pos // 2, n_pos, second_half, 0)

    on = on_ref[...]
    res_gate = mod_ref[0, 0, :, 2 * D_MODEL:]

    def gated(t):
        rows = slice(t * ROW_TILE, (t + 1) * ROW_TILE)
        return (_head_rms(o_s[rows, :], on, GLA_DV) * _silu(gate_s[rows, :])).astype(BF16)

    og = gated(0)
    for t in range(n_tiles):
        rows = slice(t * ROW_TILE, (t + 1) * ROW_TILE)
        og_next = gated(t + 1) if t + 1 < n_tiles else None
        y_ref[0, rows, :] = x_ref[0, rows, :] + res_gate * _dot(og, wout_ref[0])
        og = og_next


def _resident(shape, layer=None):
    if layer is None:
        return pl.BlockSpec(shape, lambda b: (0,) * len(shape), pipeline_mode=pl.Buffered(1))
    return pl.BlockSpec((1,) + shape, lambda b: (layer,) + (0,) * len(shape), pipeline_mode=pl.Buffered(1))


def _gla_layer(x, mods, layer, mod_row, ng, win, wa1, wa2, ba, on, wout, state_in, gla_idx, n_gla, states_so_far):
    batch, seq, _ = x.shape
    n_seq = GROUP_ROWS // seq
    n_groups = batch // n_seq
    has_s0 = state_in is not None
    emit_state = not has_s0
    assert not has_s0 or n_seq == 1
    in_specs = [
        pl.BlockSpec((1, GROUP_ROWS, D_MODEL), lambda b: (b, 0, 0)),
        pl.BlockSpec((1, 1, 1, 3 * D_MODEL), lambda b: (layer, mod_row(b), 0, 0)),
        _resident((1, D_MODEL)),
        _resident((D_MODEL, 2 * GLA_QD + 2 * GLA_VD), gla_idx),
        _resident((D_MODEL, RANK_PAD), gla_idx),
        _resident((RANK_PAD, 2 * GLA_QD), gla_idx),
        _resident((1, 2 * GLA_QD)),
        _resident((1, GLA_DV)),
        _resident((GLA_VD, D_MODEL), gla_idx),
    ]
    args = [x.reshape(n_groups, GROUP_ROWS, D_MODEL), mods, ng, win, wa1, wa2, ba, on, wout]
    if has_s0:
        in_specs.append(pl.BlockSpec((1, 1, 2, GLA_HEADS, GLA_DK, GLA_DV),
                                     lambda b: (b, gla_idx, 0, 0, 0, 0)))
        args.append(state_in)
    aliases = {}
    state_alias = emit_state and states_so_far is not None
    if state_alias:
        aliases[len(args)] = 1
        in_specs.append(pl.BlockSpec(memory_space=pl.ANY))
        args.append(states_so_far)
    out_specs = [pl.BlockSpec((1, GROUP_ROWS, D_MODEL), lambda b: (b, 0, 0))]
    out_shape = [jax.ShapeDtypeStruct((n_groups, GROUP_ROWS, D_MODEL), F32)]
    if emit_state:
        out_specs.append(pl.BlockSpec((n_seq, 1, 2, GLA_HEADS, GLA_DK, GLA_DV), lambda b: (b, gla_idx, 0, 0, 0, 0)))
        out_shape.append(jax.ShapeDtypeStruct((batch, n_gla, 2, GLA_HEADS, GLA_DK, GLA_DV), F32))
    scratch = [
        pltpu.VMEM((GROUP_ROWS, GLA_QD), F32),
        pltpu.VMEM((GROUP_ROWS, GLA_QD), F32),
        pltpu.VMEM((2, GROUP_ROWS, GLA_QD), F32),
        pltpu.VMEM((GROUP_ROWS, GLA_VD), BF16),
        pltpu.VMEM((GROUP_ROWS, GLA_VD), F32),
        pltpu.VMEM((GROUP_ROWS, GLA_VD), F32),
    ]
    if not emit_state:
        scratch.append(pltpu.VMEM((2, GLA_HEADS, GLA_DK, GLA_DV), F32))
    outs = pl.pallas_call(
        functools.partial(_gla_kernel, seq=seq, has_s0=has_s0, emit_state=emit_state, state_alias=state_alias),
        grid=(n_groups,),
        in_specs=in_specs,
        out_specs=out_specs,
        out_shape=out_shape,
        scratch_shapes=scratch,
        input_output_aliases=aliases,
        compiler_params=pltpu.CompilerParams(dimension_semantics=("arbitrary",),
                                             vmem_limit_bytes=V7X_VMEM_LIMIT_BYTES),
        name=f"gla_layer_seq{seq}",
    )(*args)
    y = outs[0].reshape(batch, seq, D_MODEL)
    return (y, outs[1]) if emit_state else (y, None)


def _rope_swap(x):
    lane = lax.broadcasted_iota(jnp.int32, x.shape, 1)
    quarter = HEAD_DIM // 4
    first = (lane % (2 * quarter)) < quarter
    return jnp.where(first, pltpu.roll(x, HEAD_DIM - quarter, 1), pltpu.roll(x, quarter, 1))


def _att_kernel(*refs, seq, latent, cache_alias):
    n_tiles = seq // ROW_TILE
    n_keys = seq + (refs[9].shape[2] if latent else 0)
    it = iter(refs)
    x_ref, mod_ref, ng_ref, win_ref, qn_ref, kn_ref, wout_ref = (next(it) for _ in range(7))
    if latent:
        cos_ref, sin_ref, ck_ref, cv_ref = (next(it) for _ in range(4))
    if cache_alias:
        next(it), next(it)
    y_ref = next(it)
    if not latent:
        kout_ref, vout_ref = next(it), next(it)
    q_s, k_s, v_s, gate_s, ao_s = (next(it) for _ in range(5))

    ng = ng_ref[...]
    qn = qn_ref[...]
    kn = kn_ref[...]
    exp2_scale = (HEAD_DIM ** -0.5) * math.log2(math.e)

    def project(t, carry):
        rows = pl.ds(pl.multiple_of(t * ROW_TILE, ROW_TILE), ROW_TILE)
        hb = _modulated_norm(x_ref[0, rows, :], ng, mod_ref)
        q = _head_rms(_dot(hb, win_ref[0, :, 0:ATT_QD]), qn, HEAD_DIM)
        k = _head_rms(_dot(hb, win_ref[0, :, ATT_QD:ATT_QD + ATT_KD]), kn, HEAD_DIM)
        v = _dot(hb, win_ref[0, :, ATT_QD + ATT_KD:ATT_QD + 2 * ATT_KD])
        gate_s[rows, :] = _dot(hb, win_ref[0, :, ATT_QD + 2 * ATT_KD:])
        if latent:
            cos = cos_ref[rows, :]
            sin = sin_ref[rows, :]
            q = jnp.concatenate(
                [q[:, h * HEAD_DIM:(h + 1) * HEAD_DIM] * cos + _rope_swap(q[:, h * HEAD_DIM:(h + 1) * HEAD_DIM]) * sin
                 for h in range(ATT_HEADS)], axis=-1)
            k = jnp.concatenate(
                [k[:, h * HEAD_DIM:(h + 1) * HEAD_DIM] * cos + _rope_swap(k[:, h * HEAD_DIM:(h + 1) * HEAD_DIM]) * sin
                 for h in range(ATT_KV_HEADS)], axis=-1)
        else:
            for h in range(ATT_KV_HEADS):
                kout_ref[0, 0, rows, h, :] = k[:, h * HEAD_DIM:(h + 1) * HEAD_DIM]
                vout_ref[0, 0, rows, h, :] = v[:, h * HEAD_DIM:(h + 1) * HEAD_DIM]
        q_s[rows, :] = (q * exp2_scale).astype(BF16)
        k_s[rows, :] = k.astype(BF16)
        for h in range(ATT_KV_HEADS):
            v_s[rows, 2 * h * HEAD_DIM:(2 * h + 1) * HEAD_DIM] = v[:, h * HEAD_DIM:(h + 1) * HEAD_DIM].astype(BF16)
        return carry

    for h in range(ATT_KV_HEADS):
        v_s[:, (2 * h + 1) * HEAD_DIM:(2 * h + 2) * HEAD_DIM] = jnp.ones((n_keys, HEAD_DIM), BF16)
    lax.fori_loop(0, n_tiles, project, 0, unroll=True)
    if latent:
        for h in range(ATT_KV_HEADS):
            k_s[seq:n_keys, h * HEAD_DIM:(h + 1) * HEAD_DIM] = ck_ref[0, 0, :, h, :].astype(BF16)
            v_s[seq:n_keys, 2 * h * HEAD_DIM:(2 * h + 1) * HEAD_DIM] = cv_ref[0, 0, :, h, :].astype(BF16)

    def attend(t, carry):
        rows = pl.ds(pl.multiple_of(t * ROW_TILE, ROW_TILE), ROW_TILE)

        def scores(h):
            kc = slice((h // ATT_GROUP) * HEAD_DIM, (h // ATT_GROUP + 1) * HEAD_DIM)
            return _dot_nt(q_s[rows, h * HEAD_DIM:(h + 1) * HEAD_DIM], k_s[:, kc])

        s = scores(0)
        for h in range(ATT_HEADS):
            s_next = scores(h + 1) if h + 1 < ATT_HEADS else None
            vc = slice((h // ATT_GROUP) * 2 * HEAD_DIM, (h // ATT_GROUP + 1) * 2 * HEAD_DIM)
            p = jnp.exp2(s - jnp.max(s, axis=-1, keepdims=True))
            o = _dot(p.astype(BF16), v_s[:, vc])
            ao_s[rows, h * HEAD_DIM:(h + 1) * HEAD_DIM] = o[:, 0:HEAD_DIM] / o[:, HEAD_DIM:]
            s = s_next
        return carry

    lax.fori_loop(0, n_tiles, attend, 0)

    res_gate = mod_ref[0, 0, :, 2 * D_MODEL:]

    def finish(t, carry):
        rows = pl.ds(pl.multiple_of(t * ROW_TILE, ROW_TILE), ROW_TILE)
        y = _dot((ao_s[rows, :] * _silu(gate_s[rows, :])).astype(BF16), wout_ref[0])
        y_ref[0, rows, :] = x_ref[0, rows, :] + res_gate * y
        return carry

    lax.fori_loop(0, n_tiles, finish, 0, unroll=True)


def _att_layer(x, mods, layer, mod_row, ng, win, qn, kn, wout, latent_inputs, att_idx, n_att, caches_so_far):
    batch, seq, _ = x.shape
    latent = latent_inputs is not None
    in_specs = [
        pl.BlockSpec((1, seq, D_MODEL), lambda b: (b, 0, 0)),
        pl.BlockSpec((1, 1, 1, 3 * D_MODEL), lambda b: (layer, mod_row(b), 0, 0)),
        _resident((1, D_MODEL)),
        _resident((D_MODEL, 2 * ATT_QD + 2 * ATT_KD), att_idx),
        _resident((1, HEAD_DIM)),
        _resident((1, HEAD_DIM)),
        _resident((ATT_QD, D_MODEL), att_idx),
    ]
    args = [x, mods, ng, win, qn, kn, wout]
    n_keys = seq
    if latent:
        cos, sin, cache_k, cache_v = latent_inputs
        past = cache_k.shape[2]
        n_keys = seq + past
        in_specs += [
            _resident((seq, HEAD_DIM)),
            _resident((seq, HEAD_DIM)),
            pl.BlockSpec((1, 1, past, ATT_KV_HEADS, HEAD_DIM), lambda b: (b, att_idx, 0, 0, 0)),
            pl.BlockSpec((1, 1, past, ATT_KV_HEADS, HEAD_DIM), lambda b: (b, att_idx, 0, 0, 0)),
        ]
        args += [cos, sin, cache_k, cache_v]
    aliases = {}
    cache_alias = (not latent) and caches_so_far is not None
    if cache_alias:
        aliases = {len(args): 1, len(args) + 1: 2}
        in_specs += [pl.BlockSpec(memory_space=pl.ANY)] * 2
        args += list(caches_so_far)
    out_specs = [pl.BlockSpec((1, seq, D_MODEL), lambda b: (b, 0, 0))]
    out_shape = [jax.ShapeDtypeStruct((batch, seq, D_MODEL), F32)]
    if not latent:
        out_specs += [pl.BlockSpec((1, 1, seq, ATT_KV_HEADS, HEAD_DIM), lambda b: (b, att_idx, 0, 0, 0))] * 2
        out_shape += [jax.ShapeDtypeStruct((batch, n_att, seq, ATT_KV_HEADS, HEAD_DIM), F32)] * 2
    scratch = [
        pltpu.VMEM((seq, ATT_QD), BF16),
        pltpu.VMEM((n_keys, ATT_KD), BF16),
        pltpu.VMEM((n_keys, 2 * ATT_KD), BF16),
        pltpu.VMEM((seq, ATT_QD), F32),
        pltpu.VMEM((seq, ATT_QD), F32),
    ]
    outs = pl.pallas_call(
        functools.partial(_att_kernel, seq=seq, latent=latent, cache_alias=cache_alias),
        grid=(batch,),
        in_specs=in_specs,
        out_specs=out_specs,
        out_shape=out_shape,
        scratch_shapes=scratch,
        input_output_aliases=aliases,
        compiler_params=pltpu.CompilerParams(dimension_semantics=("arbitrary",),
                                             vmem_limit_bytes=V7X_VMEM_LIMIT_BYTES),
        name=f"att_layer_seq{seq}",
    )(*args)
    return outs


def _rope_tables(seq):
    half = HEAD_DIM // 2
    nf = half // 2
    pos = np.arange(seq)
    freqs = np.float32(ROPE_THETA) ** (-np.arange(nf, dtype=np.float32) / np.float32(nf))
    ang_row = (pos // GRID_W).astype(np.float32)[:, None] * freqs[None, :]
    ang_col = (pos % GRID_W).astype(np.float32)[:, None] * freqs[None, :]
    cos = np.concatenate([np.cos(ang_row)] * 2 + [np.cos(ang_col)] * 2, axis=-1)
    sin = np.concatenate([-np.sin(ang_row), np.sin(ang_row), -np.sin(ang_col), np.sin(ang_col)], axis=-1)
    return jnp.asarray(cos, F32), jnp.asarray(sin, F32)


def kernel(x_prompt, x_sample, state_gla, cache_k, cache_v, c, c_ctx, norm_g, w_ada, b_ada,
           gla_w_in, gla_wa1, gla_wa2, gla_ba, gla_onorm, gla_w_out,
           att_w_in, att_qnorm, att_knorm, att_w_out):
    n_dec = x_sample.shape[0]
    assert 1 + n_dec <= MOD_ROWS
    assert x_prompt.shape[1] % ROW_TILE == 0 and x_sample.shape[1] % ROW_TILE == 0

    cvec = jnp.concatenate([c_ctx[None], c, jnp.zeros((MOD_ROWS - 1 - n_dec, D_MODEL), F32)], axis=0)
    mods = _modulations(cvec, w_ada, b_ada)
    ctx_row = lambda b: 0
    dec_row = lambda b: b + 1

    cos, sin = _rope_tables(x_sample.shape[1])
    n_gla, n_att = gla_w_in.shape[0], att_w_in.shape[0]

    gla_win, gla_wout = gla_w_in.astype(BF16), gla_w_out.astype(BF16)
    att_win, att_wout = att_w_in.astype(BF16), att_w_out.astype(BF16)
    wa1 = jnp.concatenate([gla_wa1[:, 0], gla_wa1[:, 1],
                           jnp.zeros((n_gla, D_MODEL, RANK_PAD - 2 * GLA_RANK), F32)], axis=-1).astype(BF16)
    zeros_qd = jnp.zeros((n_gla, GLA_RANK, GLA_QD), F32)
    wa2 = jnp.concatenate([jnp.concatenate([gla_wa2[:, 0], zeros_qd], axis=-1),
                           jnp.concatenate([zeros_qd, gla_wa2[:, 1]], axis=-1),
                           jnp.zeros((n_gla, RANK_PAD - 2 * GLA_RANK, 2 * GLA_QD), F32)], axis=1).astype(BF16)

    xp, xs = x_prompt, x_sample
    states, caches = None, None
    for l in range(DEPTH):
        i = l // 2
        ng = norm_g[l].reshape(1, D_MODEL)
        if l % 2 == 0:
            ba = gla_ba[i].reshape(1, 2 * GLA_QD)
            on = gla_onorm[i].reshape(1, GLA_DV)
            common = (ng, gla_win, wa1, wa2, ba, on, gla_wout)
            xp, states = _gla_layer(xp, mods, l, ctx_row, *common, None, i, n_gla, states)
            xs, _ = _gla_layer(xs, mods, l, dec_row, *common, state_gla, i, n_gla, None)
        else:
            qn = att_qnorm[i].reshape(1, HEAD_DIM)
            kn = att_knorm[i].reshape(1, HEAD_DIM)
            xp, *caches = _att_layer(xp, mods, l, ctx_row, ng, att_win, qn, kn, att_wout, None, i, n_att, caches)
            (xs,) = _att_layer(xs, mods, l, dec_row, ng, att_win, qn, kn, att_wout, (cos, sin, cache_k, cache_v), i,
                               n_att, None)
    return (xp, xs, states, caches[0], caches[1])
```

```python
import functools
import math

import jax
import jax.numpy as jnp
import numpy as np
from jax import lax
from jax.experimental import pallas as pl
from jax.experimental.pallas import tpu as pltpu

D_MODEL = 1024
DEPTH = 4
GRID_W = 64
GLA_HEADS = 4
GLA_DK = 128
GLA_DV = 256
GLA_RANK = 16
GLA_TAU = 16.0
GLA_CHUNK = 64
GLA_QD = GLA_HEADS * GLA_DK
GLA_VD = GLA_HEADS * GLA_DV
HEAD_DIM = 128
ATT_HEADS = 8
ATT_KV_HEADS = 2
ATT_GROUP = ATT_HEADS // ATT_KV_HEADS
ATT_QD = ATT_HEADS * HEAD_DIM
ATT_KD = ATT_KV_HEADS * HEAD_DIM
ROPE_THETA = 10000.0
EPS = 1e-6

ROW_TILE = 256
GROUP_ROWS = 1024
MOD_ROWS = 8
RANK_PAD = 128
V7X_VMEM_LIMIT_BYTES = 60 * 1024 * 1024

F32 = jnp.float32
BF16 = jnp.bfloat16
_NT = (((1,), (1,)), ((), ()))


def _dot(a, b):
    return jnp.dot(a, b, preferred_element_type=F32)


def _dot_nt(a, b):
    return lax.dot_general(a, b, _NT, preferred_element_type=F32)


def _silu(x):
    return x * (1.0 / (1.0 + jnp.exp(-x)))


def _log_sigmoid(z):
    return jnp.minimum(z, 0.0) - jnp.log(1.0 + jnp.exp(-jnp.abs(z)))


def _split_top(x):
    top = pltpu.bitcast(pltpu.bitcast(x, jnp.uint32) & jnp.uint32(0xFFFF0000), F32)
    return top, x - top


def _modulated_norm(x, ng, mod_ref):
    shift = mod_ref[0, 0, :, 0:D_MODEL]
    scale = mod_ref[0, 0, :, D_MODEL:2 * D_MODEL]
    ms = jnp.mean(x * x, axis=-1, keepdims=True)
    return ((x * lax.rsqrt(ms + EPS)) * (ng * (1.0 + scale)) + shift).astype(BF16)


def _head_rms(x, g, width):
    outs = []
    for h in range(x.shape[-1] // width):
        xh = x[:, h * width:(h + 1) * width]
        ms = jnp.mean(xh * xh, axis=-1, keepdims=True)
        outs.append((xh * lax.rsqrt(ms + EPS)) * g)
    return jnp.concatenate(outs, axis=-1)


def _mod_kernel(c_ref, w_ref, b_ref, o_ref):
    acc = _dot(jnp.concatenate(_split_top(_silu(c_ref[...])), axis=0).astype(BF16), w_ref[0].astype(BF16))
    o_ref[0, :, 0, :] = acc[0:MOD_ROWS] + acc[MOD_ROWS:] + b_ref[pl.ds(pl.program_id(0), 1), :]


def _modulations(cvec, w_ada, b_ada):
    tn = D_MODEL
    n_tiles = 3 * D_MODEL // tn
    return pl.pallas_call(
        _mod_kernel,
        grid=(DEPTH, n_tiles),
        in_specs=[
            pl.BlockSpec((MOD_ROWS, D_MODEL), lambda l, j: (0, 0)),
            pl.BlockSpec((1, D_MODEL, tn), lambda l, j: (l, 0, j)),
            pl.BlockSpec((DEPTH, tn), lambda l, j: (0, j)),
        ],
        out_specs=pl.BlockSpec((1, MOD_ROWS, 1, tn), lambda l, j: (l, 0, 0, j)),
        out_shape=jax.ShapeDtypeStruct((DEPTH, MOD_ROWS, 1, 3 * D_MODEL), F32),
        compiler_params=pltpu.CompilerParams(dimension_semantics=("arbitrary", "arbitrary")),
        name="adaln_modulation",
    )(cvec, w_ada, b_ada)


def _cast_kernel(w_ref, o_ref):
    o_ref[...] = w_ref[...].astype(BF16)


def _to_bf16(w):
    layers, rows, cols = w.shape
    assert rows % ROW_TILE == 0
    spec = pl.BlockSpec((1, ROW_TILE, cols), lambda l, r: (l, r, 0))
    return pl.pallas_call(
        _cast_kernel,
        grid=(layers, rows // ROW_TILE),
        in_specs=[spec],
        out_specs=spec,
        out_shape=jax.ShapeDtypeStruct(w.shape, BF16),
        compiler_params=pltpu.CompilerParams(dimension_semantics=("arbitrary", "arbitrary")),
        name="weight_cast",
    )(w)


def _chunk_sum_matrices():
    r = lax.broadcasted_iota(jnp.int32, (ROW_TILE, ROW_TILE), 0)
    c = lax.broadcasted_iota(jnp.int32, (ROW_TILE, ROW_TILE), 1)
    same = (r // GLA_CHUNK) == (c // GLA_CHUNK)
    prefix = jnp.where(same & (c <= r), 1.0, 0.0).astype(BF16)
    suffix = jnp.where(same & (c >= r), 1.0, 0.0).astype(BF16)
    return prefix, suffix


def _gla_kernel(*refs, seq, has_s0, emit_state, state_alias):
    n_seq = GROUP_ROWS // seq
    n_tiles = GROUP_ROWS // ROW_TILE
    n_pos = seq // GLA_CHUNK
    merge_products = n_seq > 1
    it = iter(refs)
    x_ref, mod_ref, ng_ref, win_ref, wa1_ref, wa2_ref, ba_ref, on_ref, wout_ref = (next(it) for _ in range(9))
    s0_ref = next(it) if has_s0 else None
    if state_alias:
        next(it)
    y_ref = next(it)
    sout_ref = next(it) if emit_state else None
    q_s, k_s, b_s, v_s, gate_s, o_s = (next(it) for _ in range(6))
    st_s = None if emit_state else next(it)

    def state_at(s, d, h):
        return sout_ref.at[s, 0, d, h] if emit_state else st_s.at[d, h]

    ng = ng_ref[...]
    prefix_m, suffix_m = _chunk_sum_matrices()

    def project_wide(t):
        rows = slice(t * ROW_TILE, (t + 1) * ROW_TILE)
        hb = _modulated_norm(x_ref[0, rows, :], ng, mod_ref)
        q_s[rows, :] = _dot(hb, win_ref[0, :, 0:GLA_QD]) * (GLA_DK ** -0.5)
        k_s[rows, :] = _dot(hb, win_ref[0, :, GLA_QD:2 * GLA_QD])
        v_s[rows, :] = _dot(hb, win_ref[0, :, 2 * GLA_QD:2 * GLA_QD + GLA_VD]).astype(BF16)
        gate_s[rows, :] = _dot(hb, win_ref[0, :, 2 * GLA_QD + GLA_VD:])
        low = _dot(hb, wa1_ref[0]).astype(BF16)
        return _dot(low, wa2_ref[0]) + ba_ref[...]

    def decay_sums(t, z):
        rows = slice(t * ROW_TILE, (t + 1) * ROW_TILE)
        logg2 = _log_sigmoid(z) * (math.log2(math.e) / GLA_TAU)
        for d, csum_m in enumerate((prefix_m, suffix_m)):
            parts = jnp.concatenate(_split_top(logg2[:, d * GLA_QD:(d + 1) * GLA_QD]), axis=0).astype(BF16)
            b_s[d, rows, :] = _dot(jnp.concatenate([csum_m, csum_m], axis=1), parts)

    z = project_wide(0)
    for t in range(n_tiles):
        z_next = project_wide(t + 1) if t + 1 < n_tiles else None
        decay_sums(t, z)
        z = z_next

    for s in range(n_seq):
        for d in range(2):
            for h in range(GLA_HEADS):
                state_at(s, d, h)[...] = s0_ref[0, 0, d, h] if has_s0 else jnp.zeros((GLA_DK, GLA_DV), F32)

    ri = lax.broadcasted_iota(jnp.int32, (GLA_CHUNK, GLA_CHUNK), 0)
    ci = lax.broadcasted_iota(jnp.int32, (GLA_CHUNK, GLA_CHUNK), 1)
    masks = (ci <= ri, ci >= ri)

    def scan_step(i, first_visit):
        chains = []
        for s in range(n_seq):
            for d in range(2):
                c = s * n_pos + (i if d == 0 else n_pos - 1 - i)
                rows = pl.ds(pl.multiple_of(c * GLA_CHUNK, GLA_CHUNK), GLA_CHUNK)
                b = b_s[d, rows, :]
                edge = GLA_CHUNK - 1 if d == 0 else 0
                total = b[edge:edge + 1, :]
                k = k_s[rows, :]
                qe = (q_s[rows, :] * jnp.exp2(b)).astype(BF16)
                ke = (k * jnp.exp2(-b)).astype(BF16)
                kdt = (k * jnp.exp2(total - b)).T.astype(BF16)
                dec = jnp.exp2(total)
                for h in range(GLA_HEADS):
                    kc = slice(h * GLA_DK, (h + 1) * GLA_DK)
                    chains.append((s, d, h, rows, slice(h * GLA_DV, (h + 1) * GLA_DV),
                                   qe[:, kc], ke[:, kc], kdt[kc, :], dec[:, kc]))
        scores = [_dot_nt(qe, ke) for (s, d, h, rows, vc, qe, ke, kdt, dec) in chains]
        if not merge_products:
            updates = [_dot(kdt, v_s[rows, vc]) for (s, d, h, rows, vc, qe, ke, kdt, dec) in chains]
        no_state = jnp.zeros((GLA_DK, GLA_DK), BF16)

        def decayed(st, dec):
            dec_col = jnp.broadcast_to(dec, (GLA_DK, GLA_DK)).T
            return jnp.concatenate([dec_col] * (GLA_DV // GLA_DK), axis=1) * st

        for (s, d, h, rows, vc, qe, ke, kdt, dec), sc in zip(chains, scores):
            a = jnp.where(masks[d], sc, 0.0).astype(BF16)
            st_ref = state_at(s, d, h)
            st = st_ref[...]
            lhs = jnp.concatenate([qe, a], axis=1)
            if merge_products:
                lhs = jnp.concatenate([lhs, jnp.concatenate([no_state, kdt], axis=1)], axis=0)
            both = _dot(lhs, jnp.concatenate([st.astype(BF16), v_s[rows, vc]], axis=0))
            if first_visit:
                o_s[rows, vc] = both[0:GLA_CHUNK]
            else:
                o_s[rows, vc] += both[0:GLA_CHUNK]
            if merge_products:
                st_ref[...] = decayed(st, dec) + both[GLA_CHUNK:]
        if not merge_products:
            for (s, d, h, rows, vc, qe, ke, kdt, dec), upd in zip(chains, updates):
                st_ref = state_at(s, d, h)
                st_ref[...] = decayed(st_ref[...], dec) + upd

    def first_half(i, carry):
        scan_step(i, True)
        return carry

    def second_half(i, carry):
        scan_step(i, False)
        return carry

    lax.fori_loop(0, n_pos // 2, first_half, 0)
    lax.fori_loop(n_pos // 2, n_pos, second_half, 0)

    on = on_ref[...]
    res_gate = mod_ref[0, 0, :, 2 * D_MODEL:]

    def gated(t):
        rows = slice(t * ROW_TILE, (t + 1) * ROW_TILE)
        return (_head_rms(o_s[rows, :], on, GLA_DV) * _silu(gate_s[rows, :])).astype(BF16)

    og = gated(0)
    for t in range(n_tiles):
        rows = slice(t * ROW_TILE, (t + 1) * ROW_TILE)
        og_next = gated(t + 1) if t + 1 < n_tiles else None
        y_ref[0, rows, :] = x_ref[0, rows, :] + res_gate * _dot(og, wout_ref[0])
        og = og_next


def _resident(shape, layer=None):
    if layer is None:
        return pl.BlockSpec(shape, lambda b: (0,) * len(shape), pipeline_mode=pl.Buffered(1))
    return pl.BlockSpec((1,) + shape, lambda b: (layer,) + (0,) * len(shape), pipeline_mode=pl.Buffered(1))


def _gla_layer(x, mods, layer, mod_row, ng, win, wa1, wa2, ba, on, wout, state_in, gla_idx, n_gla, states_so_far):
    batch, seq, _ = x.shape
    n_seq = GROUP_ROWS // seq
    n_groups = batch // n_seq
    has_s0 = state_in is not None
    emit_state = not has_s0
    assert not has_s0 or n_seq == 1
    in_specs = [
        pl.BlockSpec((1, GROUP_ROWS, D_MODEL), lambda b: (b, 0, 0)),
        pl.BlockSpec((1, 1, 1, 3 * D_MODEL), lambda b: (layer, mod_row(b), 0, 0)),
        _resident((1, D_MODEL)),
        _resident((D_MODEL, 2 * GLA_QD + 2 * GLA_VD), gla_idx),
        _resident((D_MODEL, RANK_PAD), gla_idx),
        _resident((RANK_PAD, 2 * GLA_QD), gla_idx),
        _resident((1, 2 * GLA_QD)),
        _resident((1, GLA_DV)),
        _resident((GLA_VD, D_MODEL), gla_idx),
    ]
    args = [x.reshape(n_groups, GROUP_ROWS, D_MODEL), mods, ng, win, wa1, wa2, ba, on, wout]
    if has_s0:
        in_specs.append(pl.BlockSpec((1, 1, 2, GLA_HEADS, GLA_DK, GLA_DV),
                                     lambda b: (b, gla_idx, 0, 0, 0, 0)))
        args.append(state_in)
    aliases = {}
    state_alias = emit_state and states_so_far is not None
    if state_alias:
        aliases[len(args)] = 1
        in_specs.append(pl.BlockSpec(memory_space=pl.ANY))
        args.append(states_so_far)
    out_specs = [pl.BlockSpec((1, GROUP_ROWS, D_MODEL), lambda b: (b, 0, 0))]
    out_shape = [jax.ShapeDtypeStruct((n_groups, GROUP_ROWS, D_MODEL), F32)]
    if emit_state:
        out_specs.append(pl.BlockSpec((n_seq, 1, 2, GLA_HEADS, GLA_DK, GLA_DV), lambda b: (b, gla_idx, 0, 0, 0, 0)))
        out_shape.append(jax.ShapeDtypeStruct((batch, n_gla, 2, GLA_HEADS, GLA_DK, GLA_DV), F32))
    scratch = [
        pltpu.VMEM((GROUP_ROWS, GLA_QD), F32),
        pltpu.VMEM((GROUP_ROWS, GLA_QD), F32),
        pltpu.VMEM((2, GROUP_ROWS, GLA_QD), F32),
        pltpu.VMEM((GROUP_ROWS, GLA_VD), BF16),
        pltpu.VMEM((GROUP_ROWS, GLA_VD), F32),
        pltpu.VMEM((GROUP_ROWS, GLA_VD), F32),
    ]
    if not emit_state:
        scratch.append(pltpu.VMEM((2, GLA_HEADS, GLA_DK, GLA_DV), F32))
    outs = pl.pallas_call(
        functools.partial(_gla_kernel, seq=seq, has_s0=has_s0, emit_state=emit_state, state_alias=state_alias),
        grid=(n_groups,),
        in_specs=in_specs,
        out_specs=out_specs,
        out_shape=out_shape,
        scratch_shapes=scratch,
        input_output_aliases=aliases,
        compiler_params=pltpu.CompilerParams(dimension_semantics=("arbitrary",),
                                             vmem_limit_bytes=V7X_VMEM_LIMIT_BYTES),
        name=f"gla_layer_seq{seq}",
    )(*args)
    y = outs[0].reshape(batch, seq, D_MODEL)
    return (y, outs[1]) if emit_state else (y, None)


def _rope_swap(x):
    lane = lax.broadcasted_iota(jnp.int32, x.shape, 1)
    quarter = HEAD_DIM // 4
    first = (lane % (2 * quarter)) < quarter
    return jnp.where(first, pltpu.roll(x, HEAD_DIM - quarter, 1), pltpu.roll(x, quarter, 1))


def _att_kernel(*refs, seq, latent, cache_alias):
    n_tiles = seq // ROW_TILE
    n_keys = seq + (refs[9].shape[2] if latent else 0)
    it = iter(refs)
    x_ref, mod_ref, ng_ref, win_ref, qn_ref, kn_ref, wout_ref = (next(it) for _ in range(7))
    if latent:
        cos_ref, sin_ref, ck_ref, cv_ref = (next(it) for _ in range(4))
    if cache_alias:
        next(it), next(it)
    y_ref = next(it)
    if not latent:
        kout_ref, vout_ref = next(it), next(it)
    q_s, k_s, v_s, gate_s, ao_s = (next(it) for _ in range(5))

    ng = ng_ref[...]
    qn = qn_ref[...]
    kn = kn_ref[...]
    exp2_scale = (HEAD_DIM ** -0.5) * math.log2(math.e)

    def project(t, carry):
        rows = pl.ds(pl.multiple_of(t * ROW_TILE, ROW_TILE), ROW_TILE)
        hb = _modulated_norm(x_ref[0, rows, :], ng, mod_ref)
        q = _head_rms(_dot(hb, win_ref[0, :, 0:ATT_QD]), qn, HEAD_DIM)
        k = _head_rms(_dot(hb, win_ref[0, :, ATT_QD:ATT_QD + ATT_KD]), kn, HEAD_DIM)
        v = _dot(hb, win_ref[0, :, ATT_QD + ATT_KD:ATT_QD + 2 * ATT_KD])
        gate_s[rows, :] = _dot(hb, win_ref[0, :, ATT_QD + 2 * ATT_KD:])
        if latent:
            cos = cos_ref[rows, :]
            sin = sin_ref[rows, :]
            q = jnp.concatenate(
                [q[:, h * HEAD_DIM:(h + 1) * HEAD_DIM] * cos + _rope_swap(q[:, h * HEAD_DIM:(h + 1) * HEAD_DIM]) * sin
                 for h in range(ATT_HEADS)], axis=-1)
            k = jnp.concatenate(
                [k[:, h * HEAD_DIM:(h + 1) * HEAD_DIM] * cos + _rope_swap(k[:, h * HEAD_DIM:(h + 1) * HEAD_DIM]) * sin
                 for h in range(ATT_KV_HEADS)], axis=-1)
        else:
            for h in range(ATT_KV_HEADS):
                kout_ref[0, 0, rows, h, :] = k[:, h * HEAD_DIM:(h + 1) * HEAD_DIM]
                vout_ref[0, 0, rows, h, :] = v[:, h * HEAD_DIM:(h + 1) * HEAD_DIM]
        q_s[rows, :] = (q * exp2_scale).astype(BF16)
        k_s[rows, :] = k.astype(BF16)
        for h in range(ATT_KV_HEADS):
            v_s[rows, 2 * h * HEAD_DIM:(2 * h + 1) * HEAD_DIM] = v[:, h * HEAD_DIM:(h + 1) * HEAD_DIM].astype(BF16)
        return carry

    for h in range(ATT_KV_HEADS):
        v_s[:, (2 * h + 1) * HEAD_DIM:(2 * h + 2) * HEAD_DIM] = jnp.ones((n_keys, HEAD_DIM), BF16)
    lax.fori_loop(0, n_tiles, project, 0, unroll=True)
    if latent:
        for h in range(ATT_KV_HEADS):
            k_s[seq:n_keys, h * HEAD_DIM:(h + 1) * HEAD_DIM] = ck_ref[0, 0, :, h, :].astype(BF16)
            v_s[seq:n_keys, 2 * h * HEAD_DIM:(2 * h + 1) * HEAD_DIM] = cv_ref[0, 0, :, h, :].astype(BF16)

    def attend(t, carry):
        rows = pl.ds(pl.multiple_of(t * ROW_TILE, ROW_TILE), ROW_TILE)

        def scores(h):
            kc = slice((h // ATT_GROUP) * HEAD_DIM, (h // ATT_GROUP + 1) * HEAD_DIM)
            return _dot_nt(q_s[rows, h * HEAD_DIM:(h + 1) * HEAD_DIM], k_s[:, kc])

        s = scores(0)
        for h in range(ATT_HEADS):
            s_next = scores(h + 1) if h + 1 < ATT_HEADS else None
            vc = slice((h // ATT_GROUP) * 2 * HEAD_DIM, (h // ATT_GROUP + 1) * 2 * HEAD_DIM)
            p = jnp.exp2(s - jnp.max(s, axis=-1, keepdims=True))
            o = _dot(p.astype(BF16), v_s[:, vc])
            ao_s[rows, h * HEAD_DIM:(h + 1) * HEAD_DIM] = o[:, 0:HEAD_DIM] / o[:, HEAD_DIM:]
            s = s_next
        return carry

    lax.fori_loop(0, n_tiles, attend, 0)

    res_gate = mod_ref[0, 0, :, 2 * D_MODEL:]

    def finish(t, carry):
        rows = pl.ds(pl.multiple_of(t * ROW_TILE, ROW_TILE), ROW_TILE)
        y = _dot((ao_s[rows, :] * _silu(gate_s[rows, :])).astype(BF16), wout_ref[0])
        y_ref[0, rows, :] = x_ref[0, rows, :] + res_gate * y
        return carry

    lax.fori_loop(0, n_tiles, finish, 0, unroll=True)


def _att_layer(x, mods, layer, mod_row, ng, win, qn, kn, wout, latent_inputs, att_idx, n_att, caches_so_far):
    batch, seq, _ = x.shape
    latent = latent_inputs is not None
    in_specs = [
        pl.BlockSpec((1, seq, D_MODEL), lambda b: (b, 0, 0)),
        pl.BlockSpec((1, 1, 1, 3 * D_MODEL), lambda b: (layer, mod_row(b), 0, 0)),
        _resident((1, D_MODEL)),
        _resident((D_MODEL, 2 * ATT_QD + 2 * ATT_KD), att_idx),
        _resident((1, HEAD_DIM)),
        _resident((1, HEAD_DIM)),
        _resident((ATT_QD, D_MODEL), att_idx),
    ]
    args = [x, mods, ng, win, qn, kn, wout]
    n_keys = seq
    if latent:
        cos, sin, cache_k, cache_v = latent_inputs
        past = cache_k.shape[2]
        n_keys = seq + past
        in_specs += [
            _resident((seq, HEAD_DIM)),
            _resident((seq, HEAD_DIM)),
            pl.BlockSpec((1, 1, past, ATT_KV_HEADS, HEAD_DIM), lambda b: (b, att_idx, 0, 0, 0)),
            pl.BlockSpec((1, 1, past, ATT_KV_HEADS, HEAD_DIM), lambda b: (b, att_idx, 0, 0, 0)),
        ]
        args += [cos, sin, cache_k, cache_v]
    aliases = {}
    cache_alias = (not latent) and caches_so_far is not None
    if cache_alias:
        aliases = {len(args): 1, len(args) + 1: 2}
        in_specs += [pl.BlockSpec(memory_space=pl.ANY)] * 2
        args += list(caches_so_far)
    out_specs = [pl.BlockSpec((1, seq, D_MODEL), lambda b: (b, 0, 0))]
    out_shape = [jax.ShapeDtypeStruct((batch, seq, D_MODEL), F32)]
    if not latent:
        out_specs += [pl.BlockSpec((1, 1, seq, ATT_KV_HEADS, HEAD_DIM), lambda b: (b, att_idx, 0, 0, 0))] * 2
        out_shape += [jax.ShapeDtypeStruct((batch, n_att, seq, ATT_KV_HEADS, HEAD_DIM), F32)] * 2
    scratch = [
        pltpu.VMEM((seq, ATT_QD), BF16),
        pltpu.VMEM((n_keys, ATT_KD), BF16),
        pltpu.VMEM((n_keys, 2 * ATT_KD), BF16),
        pltpu.VMEM((seq, ATT_QD), F32),
        pltpu.VMEM((seq, ATT_QD), F32),
    ]
    outs = pl.pallas_call(
        functools.partial(_att_kernel, seq=seq, latent=latent, cache_alias=cache_alias),
        grid=(batch,),
        in_specs=in_specs,
        out_specs=out_specs,
        out_shape=out_shape,
        scratch_shapes=scratch,
        input_output_aliases=aliases,
        compiler_params=pltpu.CompilerParams(dimension_semantics=("arbitrary",),
                                             vmem_limit_bytes=V7X_VMEM_LIMIT_BYTES),
        name=f"att_layer_seq{seq}",
    )(*args)
    return outs


def _rope_tables(seq):
    half = HEAD_DIM // 2
    nf = half // 2
    pos = np.arange(seq)
    freqs = np.float32(ROPE_THETA) ** (-np.arange(nf, dtype=np.float32) / np.float32(nf))
    ang_row = (pos // GRID_W).astype(np.float32)[:, None] * freqs[None, :]
    ang_col = (pos % GRID_W).astype(np.float32)[:, None] * freqs[None, :]
    cos = np.concatenate([np.cos(ang_row)] * 2 + [np.cos(ang_col)] * 2, axis=-1)
    sin = np.concatenate([-np.sin(ang_row), np.sin(ang_row), -np.sin(ang_col), np.sin(ang_col)], axis=-1)
    return jnp.asarray(cos, F32), jnp.asarray(sin, F32)


def kernel(x_prompt, x_sample, state_gla, cache_k, cache_v, c, c_ctx, norm_g, w_ada, b_ada,
           gla_w_in, gla_wa1, gla_wa2, gla_ba, gla_onorm, gla_w_out,
           att_w_in, att_qnorm, att_knorm, att_w_out):
    n_dec = x_sample.shape[0]
    assert 1 + n_dec <= MOD_ROWS
    assert x_prompt.shape[1] % ROW_TILE == 0 and x_sample.shape[1] % ROW_TILE == 0

    cvec = jnp.concatenate([c_ctx[None], c, jnp.zeros((MOD_ROWS - 1 - n_dec, D_MODEL), F32)], axis=0)
    mods = _modulations(cvec, w_ada, b_ada)
    ctx_row = lambda b: 0
    dec_row = lambda b: b + 1

    cos, sin = _rope_tables(x_sample.shape[1])
    n_gla, n_att = gla_w_in.shape[0], att_w_in.shape[0]

    gla_win, gla_wout = _to_bf16(gla_w_in), _to_bf16(gla_w_out)
    att_win, att_wout = _to_bf16(att_w_in), _to_bf16(att_w_out)
    wa1 = jnp.concatenate([gla_wa1[:, 0], gla_wa1[:, 1],
                           jnp.zeros((n_gla, D_MODEL, RANK_PAD - 2 * GLA_RANK), F32)], axis=-1).astype(BF16)
    zeros_qd = jnp.zeros((n_gla, GLA_RANK, GLA_QD), F32)
    wa2 = jnp.concatenate([jnp.concatenate([gla_wa2[:, 0], zeros_qd], axis=-1),
                           jnp.concatenate([zeros_qd, gla_wa2[:, 1]], axis=-1),
                           jnp.zeros((n_gla, RANK_PAD - 2 * GLA_RANK, 2 * GLA_QD), F32)], axis=1).astype(BF16)

    xp, xs = x_prompt, x_sample
    states, caches = None, None
    for l in range(DEPTH):
        i = l // 2
        ng = norm_g[l].reshape(1, D_MODEL)
        if l % 2 == 0:
            ba = gla_ba[i].reshape(1, 2 * GLA_QD)
            on = gla_onorm[i].reshape(1, GLA_DV)
            common = (ng, gla_win, wa1, wa2, ba, on, gla_wout)
            xp, states = _gla_layer(xp, mods, l, ctx_row, *common, None, i, n_gla, states)
            xs, _ = _gla_layer(xs, mods, l, dec_row, *common, state_gla, i, n_gla, None)
        else:
            qn = att_qnorm[i].reshape(1, HEAD_DIM)
            kn = att_knorm[i].reshape(1, HEAD_DIM)
            xp, *caches = _att_layer(xp, mods, l, ctx_row, ng, att_win, qn, kn, att_wout, None, i, n_att, caches)
            (xs,) = _att_layer(xs, mods, l, dec_row, ng, att_win, qn, kn, att_wout, (cos, sin, cache_k, cache_v), i,
                               n_att, None)
    return (xp, xs, states, caches[0], caches[1])
```

```python
import functools
import math

import jax
import jax.numpy as jnp
import numpy as np
from jax import lax
from jax.experimental import pallas as pl
from jax.experimental.pallas import tpu as pltpu

D_MODEL = 1024
DEPTH = 4
GRID_W = 64
GLA_HEADS = 4
GLA_DK = 128
GLA_DV = 256
GLA_RANK = 16
GLA_TAU = 16.0
GLA_CHUNK = 64
GLA_QD = GLA_HEADS * GLA_DK
GLA_VD = GLA_HEADS * GLA_DV
HEAD_DIM = 128
ATT_HEADS = 8
ATT_KV_HEADS = 2
ATT_GROUP = ATT_HEADS // ATT_KV_HEADS
ATT_QD = ATT_HEADS * HEAD_DIM
ATT_KD = ATT_KV_HEADS * HEAD_DIM
ROPE_THETA = 10000.0
EPS = 1e-6

ROW_TILE = 256
GROUP_ROWS = 1024
MOD_ROWS = 8
RANK_PAD = 128
V7X_VMEM_LIMIT_BYTES = 60 * 1024 * 1024

F32 = jnp.float32
BF16 = jnp.bfloat16
_NT = (((1,), (1,)), ((), ()))


def _dot(a, b):
    return jnp.dot(a, b, preferred_element_type=F32)


def _dot_nt(a, b):
    return lax.dot_general(a, b, _NT, preferred_element_type=F32)


def _silu(x):
    return x * (1.0 / (1.0 + jnp.exp(-x)))


def _log_sigmoid(z):
    return jnp.minimum(z, 0.0) - jnp.log(1.0 + jnp.exp(-jnp.abs(z)))


def _split_top(x):
    top = pltpu.bitcast(pltpu.bitcast(x, jnp.uint32) & jnp.uint32(0xFFFF0000), F32)
    return top, x - top


def _modulated_norm(x, ng, mod_ref):
    shift = mod_ref[0, 0, :, 0:D_MODEL]
    scale = mod_ref[0, 0, :, D_MODEL:2 * D_MODEL]
    ms = jnp.mean(x * x, axis=-1, keepdims=True)
    return ((x * lax.rsqrt(ms + EPS)) * (ng * (1.0 + scale)) + shift).astype(BF16)


def _head_rms(x, g, width):
    outs = []
    for h in range(x.shape[-1] // width):
        xh = x[:, h * width:(h + 1) * width]
        ms = jnp.mean(xh * xh, axis=-1, keepdims=True)
        outs.append((xh * lax.rsqrt(ms + EPS)) * g)
    return jnp.concatenate(outs, axis=-1)


def _mod_kernel(c_ref, w_ref, b_ref, o_ref):
    acc = _dot(jnp.concatenate(_split_top(_silu(c_ref[...])), axis=0).astype(BF16), w_ref[0].astype(BF16))
    o_ref[0, :, 0, :] = acc[0:MOD_ROWS] + acc[MOD_ROWS:] + b_ref[pl.ds(pl.program_id(0), 1), :]


def _modulations(cvec, w_ada, b_ada):
    tn = D_MODEL
    n_tiles = 3 * D_MODEL // tn
    return pl.pallas_call(
        _mod_kernel,
        grid=(DEPTH, n_tiles),
        in_specs=[
            pl.BlockSpec((MOD_ROWS, D_MODEL), lambda l, j: (0, 0)),
            pl.BlockSpec((1, D_MODEL, tn), lambda l, j: (l, 0, j)),
            pl.BlockSpec((DEPTH, tn), lambda l, j: (0, j)),
        ],
        out_specs=pl.BlockSpec((1, MOD_ROWS, 1, tn), lambda l, j: (l, 0, 0, j)),
        out_shape=jax.ShapeDtypeStruct((DEPTH, MOD_ROWS, 1, 3 * D_MODEL), F32),
        compiler_params=pltpu.CompilerParams(dimension_semantics=("arbitrary", "arbitrary")),
        name="adaln_modulation",
    )(cvec, w_ada, b_ada)


def _chunk_sum_matrices():
    r = lax.broadcasted_iota(jnp.int32, (ROW_TILE, ROW_TILE), 0)
    c = lax.broadcasted_iota(jnp.int32, (ROW_TILE, ROW_TILE), 1)
    same = (r // GLA_CHUNK) == (c // GLA_CHUNK)
    prefix = jnp.where(same & (c <= r), 1.0, 0.0).astype(BF16)
    suffix = jnp.where(same & (c >= r), 1.0, 0.0).astype(BF16)
    return prefix, suffix


def _gla_kernel(*refs, seq, has_s0, emit_state, state_alias):
    n_seq = GROUP_ROWS // seq
    n_tiles = GROUP_ROWS // ROW_TILE
    n_pos = seq // GLA_CHUNK
    merge_products = n_seq > 1
    it = iter(refs)
    x_ref, mod_ref, ng_ref, win_ref, wa1_ref, wa2_ref, ba_ref, on_ref, wout_ref = (next(it) for _ in range(9))
    s0_ref = next(it) if has_s0 else None
    if state_alias:
        next(it)
    y_ref = next(it)
    sout_ref = next(it) if emit_state else None
    q_s, k_s, b_s, v_s, gate_s, o_s = (next(it) for _ in range(6))
    st_s = None if emit_state else next(it)

    def state_at(s, d, h):
        return sout_ref.at[s, 0, d, h] if emit_state else st_s.at[d, h]

    ng = ng_ref[...]
    prefix_m, suffix_m = _chunk_sum_matrices()

    def project_wide(t):
        rows = slice(t * ROW_TILE, (t + 1) * ROW_TILE)
        hb = _modulated_norm(x_ref[0, rows, :], ng, mod_ref)
        q_s[rows, :] = _dot(hb, win_ref[0, :, 0:GLA_QD]) * (GLA_DK ** -0.5)
        k_s[rows, :] = _dot(hb, win_ref[0, :, GLA_QD:2 * GLA_QD])
        v_s[rows, :] = _dot(hb, win_ref[0, :, 2 * GLA_QD:2 * GLA_QD + GLA_VD]).astype(BF16)
        gate_s[rows, :] = _dot(hb, win_ref[0, :, 2 * GLA_QD + GLA_VD:])
        low = _dot(hb, wa1_ref[0]).astype(BF16)
        return _dot(low, wa2_ref[0]) + ba_ref[...]

    def decay_sums(t, z):
        rows = slice(t * ROW_TILE, (t + 1) * ROW_TILE)
        logg2 = _log_sigmoid(z) * (math.log2(math.e) / GLA_TAU)
        for d, csum_m in enumerate((prefix_m, suffix_m)):
            parts = jnp.concatenate(_split_top(logg2[:, d * GLA_QD:(d + 1) * GLA_QD]), axis=0).astype(BF16)
            b_s[d, rows, :] = _dot(jnp.concatenate([csum_m, csum_m], axis=1), parts)

    z = project_wide(0)
    for t in range(n_tiles):
        z_next = project_wide(t + 1) if t + 1 < n_tiles else None
        decay_sums(t, z)
        z = z_next

    for s in range(n_seq):
        for d in range(2):
            for h in range(GLA_HEADS):
                state_at(s, d, h)[...] = s0_ref[0, 0, d, h] if has_s0 else jnp.zeros((GLA_DK, GLA_DV), F32)

    ri = lax.broadcasted_iota(jnp.int32, (GLA_CHUNK, GLA_CHUNK), 0)
    ci = lax.broadcasted_iota(jnp.int32, (GLA_CHUNK, GLA_CHUNK), 1)
    masks = (ci <= ri, ci >= ri)

    def scan_step(i, first_visit):
        chains = []
        for s in range(n_seq):
            for d in range(2):
                c = s * n_pos + (i if d == 0 else n_pos - 1 - i)
                rows = pl.ds(pl.multiple_of(c * GLA_CHUNK, GLA_CHUNK), GLA_CHUNK)
                b = b_s[d, rows, :]
                edge = GLA_CHUNK - 1 if d == 0 else 0
                total = b[edge:edge + 1, :]
                k = k_s[rows, :]
                qe = (q_s[rows, :] * jnp.exp2(b)).astype(BF16)
                ke = (k * jnp.exp2(-b)).astype(BF16)
                kdt = (k * jnp.exp2(total - b)).T.astype(BF16)
                dec = jnp.exp2(total)
                for h in range(GLA_HEADS):
                    kc = slice(h * GLA_DK, (h + 1) * GLA_DK)
                    chains.append((s, d, h, rows, slice(h * GLA_DV, (h + 1) * GLA_DV),
                                   qe[:, kc], ke[:, kc], kdt[kc, :], dec[:, kc]))
        scores = [_dot_nt(qe, ke) for (s, d, h, rows, vc, qe, ke, kdt, dec) in chains]
        if not merge_products:
            updates = [_dot(kdt, v_s[rows, vc]) for (s, d, h, rows, vc, qe, ke, kdt, dec) in chains]
        no_state = jnp.zeros((GLA_DK, GLA_DK), BF16)

        def decayed(st, dec):
            dec_col = jnp.broadcast_to(dec, (GLA_DK, GLA_DK)).T
            return jnp.concatenate([dec_col] * (GLA_DV // GLA_DK), axis=1) * st

        for (s, d, h, rows, vc, qe, ke, kdt, dec), sc in zip(chains, scores):
            a = jnp.where(masks[d], sc, 0.0).astype(BF16)
            st_ref = state_at(s, d, h)
            st = st_ref[...]
            lhs = jnp.concatenate([qe, a], axis=1)
            if merge_products:
                lhs = jnp.concatenate([lhs, jnp.concatenate([no_state, kdt], axis=1)], axis=0)
            both = _dot(lhs, jnp.concatenate([st.astype(BF16), v_s[rows, vc]], axis=0))
            if first_visit:
                o_s[rows, vc] = both[0:GLA_CHUNK]
            else:
                o_s[rows, vc] += both[0:GLA_CHUNK]
            if merge_products:
                st_ref[...] = decayed(st, dec) + both[GLA_CHUNK:]
        if not merge_products:
            for (s, d, h, rows, vc, qe, ke, kdt, dec), upd in zip(chains, updates):
                st_ref = state_at(s, d, h)
                st_ref[...] = decayed(st_ref[...], dec) + upd

    def first_half(i, carry):
        scan_step(i, True)
        return carry

    def second_half(i, carry):
        scan_step(i, False)
        return carry

    lax.fori_loop(0, n_pos // 2, first_half, 0, unroll=2)
    lax.fori_loop(n_pos // 2, n_pos, second_half, 0, unroll=2)

    on = on_ref[...]
    res_gate = mod_ref[0, 0, :, 2 * D_MODEL:]

    def gated(t):
        rows = slice(t * ROW_TILE, (t + 1) * ROW_TILE)
        return (_head_rms(o_s[rows, :], on, GLA_DV) * _silu(gate_s[rows, :])).astype(BF16)

    og = gated(0)
    for t in range(n_tiles):
        rows = slice(t * ROW_TILE, (t + 1) * ROW_TILE)
        og_next = gated(t + 1) if t + 1 < n_tiles else None
        y_ref[0, rows, :] = x_ref[0, rows, :] + res_gate * _dot(og, wout_ref[0])
        og = og_next


def _resident(shape, layer=None):
    if layer is None:
        return pl.BlockSpec(shape, lambda b: (0,) * len(shape), pipeline_mode=pl.Buffered(1))
    return pl.BlockSpec((1,) + shape, lambda b: (layer,) + (0,) * len(shape), pipeline_mode=pl.Buffered(1))


def _gla_layer(x, mods, layer, mod_row, ng, win, wa1, wa2, ba, on, wout, state_in, gla_idx, n_gla, states_so_far):
    batch, seq, _ = x.shape
    n_seq = GROUP_ROWS // seq
    n_groups = batch // n_seq
    has_s0 = state_in is not None
    emit_state = not has_s0
    assert not has_s0 or n_seq == 1
    in_specs = [
        pl.BlockSpec((1, GROUP_ROWS, D_MODEL), lambda b: (b, 0, 0)),
        pl.BlockSpec((1, 1, 1, 3 * D_MODEL), lambda b: (layer, mod_row(b), 0, 0)),
        _resident((1, D_MODEL)),
        _resident((D_MODEL, 2 * GLA_QD + 2 * GLA_VD), gla_idx),
        _resident((D_MODEL, RANK_PAD), gla_idx),
        _resident((RANK_PAD, 2 * GLA_QD), gla_idx),
        _resident((1, 2 * GLA_QD)),
        _resident((1, GLA_DV)),
        _resident((GLA_VD, D_MODEL), gla_idx),
    ]
    args = [x.reshape(n_groups, GROUP_ROWS, D_MODEL), mods, ng, win, wa1, wa2, ba, on, wout]
    if has_s0:
        in_specs.append(pl.BlockSpec((1, 1, 2, GLA_HEADS, GLA_DK, GLA_DV),
                                     lambda b: (b, gla_idx, 0, 0, 0, 0)))
        args.append(state_in)
    aliases = {}
    state_alias = emit_state and states_so_far is not None
    if state_alias:
        aliases[len(args)] = 1
        in_specs.append(pl.BlockSpec(memory_space=pl.ANY))
        args.append(states_so_far)
    out_specs = [pl.BlockSpec((1, GROUP_ROWS, D_MODEL), lambda b: (b, 0, 0))]
    out_shape = [jax.ShapeDtypeStruct((n_groups, GROUP_ROWS, D_MODEL), F32)]
    if emit_state:
        out_specs.append(pl.BlockSpec((n_seq, 1, 2, GLA_HEADS, GLA_DK, GLA_DV), lambda b: (b, gla_idx, 0, 0, 0, 0)))
        out_shape.append(jax.ShapeDtypeStruct((batch, n_gla, 2, GLA_HEADS, GLA_DK, GLA_DV), F32))
    scratch = [
        pltpu.VMEM((GROUP_ROWS, GLA_QD), F32),
        pltpu.VMEM((GROUP_ROWS, GLA_QD), F32),
        pltpu.VMEM((2, GROUP_ROWS, GLA_QD), F32),
        pltpu.VMEM((GROUP_ROWS, GLA_VD), BF16),
        pltpu.VMEM((GROUP_ROWS, GLA_VD), F32),
        pltpu.VMEM((GROUP_ROWS, GLA_VD), F32),
    ]
    if not emit_state:
        scratch.append(pltpu.VMEM((2, GLA_HEADS, GLA_DK, GLA_DV), F32))
    outs = pl.pallas_call(
        functools.partial(_gla_kernel, seq=seq, has_s0=has_s0, emit_state=emit_state, state_alias=state_alias),
        grid=(n_groups,),
        in_specs=in_specs,
        out_specs=out_specs,
        out_shape=out_shape,
        scratch_shapes=scratch,
        input_output_aliases=aliases,
        compiler_params=pltpu.CompilerParams(dimension_semantics=("arbitrary",),
                                             vmem_limit_bytes=V7X_VMEM_LIMIT_BYTES),
        name=f"gla_layer_seq{seq}",
    )(*args)
    y = outs[0].reshape(batch, seq, D_MODEL)
    return (y, outs[1]) if emit_state else (y, None)


def _rope_swap(x):
    lane = lax.broadcasted_iota(jnp.int32, x.shape, 1)
    quarter = HEAD_DIM // 4
    first = (lane % (2 * quarter)) < quarter
    return jnp.where(first, pltpu.roll(x, HEAD_DIM - quarter, 1), pltpu.roll(x, quarter, 1))


def _att_kernel(*refs, seq, latent, cache_alias):
    n_tiles = seq // ROW_TILE
    n_keys = seq + (refs[9].shape[2] if latent else 0)
    it = iter(refs)
    x_ref, mod_ref, ng_ref, win_ref, qn_ref, kn_ref, wout_ref = (next(it) for _ in range(7))
    if latent:
        cos_ref, sin_ref, ck_ref, cv_ref = (next(it) for _ in range(4))
    if cache_alias:
        next(it), next(it)
    y_ref = next(it)
    if not latent:
        kout_ref, vout_ref = next(it), next(it)
    q_s, k_s, v_s, gate_s, ao_s = (next(it) for _ in range(5))

    ng = ng_ref[...]
    qn = qn_ref[...]
    kn = kn_ref[...]
    exp2_scale = (HEAD_DIM ** -0.5) * math.log2(math.e)

    def project(t, carry):
        rows = pl.ds(pl.multiple_of(t * ROW_TILE, ROW_TILE), ROW_TILE)
        hb = _modulated_norm(x_ref[0, rows, :], ng, mod_ref)
        q = _head_rms(_dot(hb, win_ref[0, :, 0:ATT_QD]), qn, HEAD_DIM)
        k = _head_rms(_dot(hb, win_ref[0, :, ATT_QD:ATT_QD + ATT_KD]), kn, HEAD_DIM)
        v = _dot(hb, win_ref[0, :, ATT_QD + ATT_KD:ATT_QD + 2 * ATT_KD])
        gate_s[rows, :] = _dot(hb, win_ref[0, :, ATT_QD + 2 * ATT_KD:])
        if latent:
            cos = cos_ref[rows, :]
            sin = sin_ref[rows, :]
            q = jnp.concatenate(
                [q[:, h * HEAD_DIM:(h + 1) * HEAD_DIM] * cos + _rope_swap(q[:, h * HEAD_DIM:(h + 1) * HEAD_DIM]) * sin
                 for h in range(ATT_HEADS)], axis=-1)
            k = jnp.concatenate(
                [k[:, h * HEAD_DIM:(h + 1) * HEAD_DIM] * cos + _rope_swap(k[:, h * HEAD_DIM:(h + 1) * HEAD_DIM]) * sin
                 for h in range(ATT_KV_HEADS)], axis=-1)
        else:
            for h in range(ATT_KV_HEADS):
                kout_ref[0, 0, rows, h, :] = k[:, h * HEAD_DIM:(h + 1) * HEAD_DIM]
                vout_ref[0, 0, rows, h, :] = v[:, h * HEAD_DIM:(h + 1) * HEAD_DIM]
        q_s[rows, :] = (q * exp2_scale).astype(BF16)
        k_s[rows, :] = k.astype(BF16)
        for h in range(ATT_KV_HEADS):
            v_s[rows, 2 * h * HEAD_DIM:(2 * h + 1) * HEAD_DIM] = v[:, h * HEAD_DIM:(h + 1) * HEAD_DIM].astype(BF16)
        return carry

    for h in range(ATT_KV_HEADS):
        v_s[:, (2 * h + 1) * HEAD_DIM:(2 * h + 2) * HEAD_DIM] = jnp.ones((n_keys, HEAD_DIM), BF16)
    lax.fori_loop(0, n_tiles, project, 0, unroll=True)
    if latent:
        for h in range(ATT_KV_HEADS):
            k_s[seq:n_keys, h * HEAD_DIM:(h + 1) * HEAD_DIM] = ck_ref[0, 0, :, h, :].astype(BF16)
            v_s[seq:n_keys, 2 * h * HEAD_DIM:(2 * h + 1) * HEAD_DIM] = cv_ref[0, 0, :, h, :].astype(BF16)

    def attend(t, carry):
        rows = pl.ds(pl.multiple_of(t * ROW_TILE, ROW_TILE), ROW_TILE)

        def scores(h):
            kc = slice((h // ATT_GROUP) * HEAD_DIM, (h // ATT_GROUP + 1) * HEAD_DIM)
            return _dot_nt(q_s[rows, h * HEAD_DIM:(h + 1) * HEAD_DIM], k_s[:, kc])

        s = scores(0)
        for h in range(ATT_HEADS):
            s_next = scores(h + 1) if h + 1 < ATT_HEADS else None
            vc = slice((h // ATT_GROUP) * 2 * HEAD_DIM, (h // ATT_GROUP + 1) * 2 * HEAD_DIM)
            p = jnp.exp2(s - jnp.max(s, axis=-1, keepdims=True))
            o = _dot(p.astype(BF16), v_s[:, vc])
            ao_s[rows, h * HEAD_DIM:(h + 1) * HEAD_DIM] = o[:, 0:HEAD_DIM] / o[:, HEAD_DIM:]
            s = s_next
        return carry

    lax.fori_loop(0, n_tiles, attend, 0)

    res_gate = mod_ref[0, 0, :, 2 * D_MODEL:]

    def finish(t, carry):
        rows = pl.ds(pl.multiple_of(t * ROW_TILE, ROW_TILE), ROW_TILE)
        y = _dot((ao_s[rows, :] * _silu(gate_s[rows, :])).astype(BF16), wout_ref[0])
        y_ref[0, rows, :] = x_ref[0, rows, :] + res_gate * y
        return carry

    lax.fori_loop(0, n_tiles, finish, 0, unroll=True)


def _att_layer(x, mods, layer, mod_row, ng, win, qn, kn, wout, latent_inputs, att_idx, n_att, caches_so_far):
    batch, seq, _ = x.shape
    latent = latent_inputs is not None
    in_specs = [
        pl.BlockSpec((1, seq, D_MODEL), lambda b: (b, 0, 0)),
        pl.BlockSpec((1, 1, 1, 3 * D_MODEL), lambda b: (layer, mod_row(b), 0, 0)),
        _resident((1, D_MODEL)),
        _resident((D_MODEL, 2 * ATT_QD + 2 * ATT_KD), att_idx),
        _resident((1, HEAD_DIM)),
        _resident((1, HEAD_DIM)),
        _resident((ATT_QD, D_MODEL), att_idx),
    ]
    args = [x, mods, ng, win, qn, kn, wout]
    n_keys = seq
    if latent:
        cos, sin, cache_k, cache_v = latent_inputs
        past = cache_k.shape[2]
        n_keys = seq + past
        in_specs += [
            _resident((seq, HEAD_DIM)),
            _resident((seq, HEAD_DIM)),
            pl.BlockSpec((1, 1, past, ATT_KV_HEADS, HEAD_DIM), lambda b: (b, att_idx, 0, 0, 0)),
            pl.BlockSpec((1, 1, past, ATT_KV_HEADS, HEAD_DIM), lambda b: (b, att_idx, 0, 0, 0)),
        ]
        args += [cos, sin, cache_k, cache_v]
    aliases = {}
    cache_alias = (not latent) and caches_so_far is not None
    if cache_alias:
        aliases = {len(args): 1, len(args) + 1: 2}
        in_specs += [pl.BlockSpec(memory_space=pl.ANY)] * 2
        args += list(caches_so_far)
    out_specs = [pl.BlockSpec((1, seq, D_MODEL), lambda b: (b, 0, 0))]
    out_shape = [jax.ShapeDtypeStruct((batch, seq, D_MODEL), F32)]
    if not latent:
        out_specs += [pl.BlockSpec((1, 1, seq, ATT_KV_HEADS, HEAD_DIM), lambda b: (b, att_idx, 0, 0, 0))] * 2
        out_shape += [jax.ShapeDtypeStruct((batch, n_att, seq, ATT_KV_HEADS, HEAD_DIM), F32)] * 2
    scratch = [
        pltpu.VMEM((seq, ATT_QD), BF16),
        pltpu.VMEM((n_keys, ATT_KD), BF16),
        pltpu.VMEM((n_keys, 2 * ATT_KD), BF16),
        pltpu.VMEM((seq, ATT_QD), F32),
        pltpu.VMEM((seq, ATT_QD), F32),
    ]
    outs = pl.pallas_call(
        functools.partial(_att_kernel, seq=seq, latent=latent, cache_alias=cache_alias),
        grid=(batch,),
        in_specs=in_specs,
        out_specs=out_specs,
        out_shape=out_shape,
        scratch_shapes=scratch,
        input_output_aliases=aliases,
        compiler_params=pltpu.CompilerParams(dimension_semantics=("arbitrary",),
                                             vmem_limit_bytes=V7X_VMEM_LIMIT_BYTES),
        name=f"att_layer_seq{seq}",
    )(*args)
    return outs


def _rope_tables(seq):
    half = HEAD_DIM // 2
    nf = half // 2
    pos = np.arange(seq)
    freqs = np.float32(ROPE_THETA) ** (-np.arange(nf, dtype=np.float32) / np.float32(nf))
    ang_row = (pos // GRID_W).astype(np.float32)[:, None] * freqs[None, :]
    ang_col = (pos % GRID_W).astype(np.float32)[:, None] * freqs[None, :]
    cos = np.concatenate([np.cos(ang_row)] * 2 + [np.cos(ang_col)] * 2, axis=-1)
    sin = np.concatenate([-np.sin(ang_row), np.sin(ang_row), -np.sin(ang_col), np.sin(ang_col)], axis=-1)
    return jnp.asarray(cos, F32), jnp.asarray(sin, F32)


def kernel(x_prompt, x_sample, state_gla, cache_k, cache_v, c, c_ctx, norm_g, w_ada, b_ada,
           gla_w_in, gla_wa1, gla_wa2, gla_ba, gla_onorm, gla_w_out,
           att_w_in, att_qnorm, att_knorm, att_w_out):
    n_dec = x_sample.shape[0]
    assert 1 + n_dec <= MOD_ROWS
    assert x_prompt.shape[1] % ROW_TILE == 0 and x_sample.shape[1] % ROW_TILE == 0

    cvec = jnp.concatenate([c_ctx[None], c, jnp.zeros((MOD_ROWS - 1 - n_dec, D_MODEL), F32)], axis=0)
    mods = _modulations(cvec, w_ada, b_ada)
    ctx_row = lambda b: 0
    dec_row = lambda b: b + 1

    cos, sin = _rope_tables(x_sample.shape[1])
    n_gla, n_att = gla_w_in.shape[0], att_w_in.shape[0]

    gla_win, gla_wout = gla_w_in.astype(BF16), gla_w_out.astype(BF16)
    att_win, att_wout = att_w_in.astype(BF16), att_w_out.astype(BF16)
    wa1 = jnp.concatenate([gla_wa1[:, 0], gla_wa1[:, 1],
                           jnp.zeros((n_gla, D_MODEL, RANK_PAD - 2 * GLA_RANK), F32)], axis=-1).astype(BF16)
    zeros_qd = jnp.zeros((n_gla, GLA_RANK, GLA_QD), F32)
    wa2 = jnp.concatenate([jnp.concatenate([gla_wa2[:, 0], zeros_qd], axis=-1),
                           jnp.concatenate([zeros_qd, gla_wa2[:, 1]], axis=-1),
                           jnp.zeros((n_gla, RANK_PAD - 2 * GLA_RANK, 2 * GLA_QD), F32)], axis=1).astype(BF16)

    xp, xs = x_prompt, x_sample
    states, caches = None, None
    for l in range(DEPTH):
        i = l // 2
        ng = norm_g[l].reshape(1, D_MODEL)
        if l % 2 == 0:
            ba = gla_ba[i].reshape(1, 2 * GLA_QD)
            on = gla_onorm[i].reshape(1, GLA_DV)
            common = (ng, gla_win, wa1, wa2, ba, on, gla_wout)
            xp, states = _gla_layer(xp, mods, l, ctx_row, *common, None, i, n_gla, states)
            xs, _ = _gla_layer(xs, mods, l, dec_row, *common, state_gla, i, n_gla, None)
        else:
            qn = att_qnorm[i].reshape(1, HEAD_DIM)
            kn = att_knorm[i].reshape(1, HEAD_DIM)
            xp, *caches = _att_layer(xp, mods, l, ctx_row, ng, att_win, qn, kn, att_wout, None, i, n_att, caches)
            (xs,) = _att_layer(xs, mods, l, dec_row, ng, att_win, qn, kn, att_wout, (cos, sin, cache_k, cache_v), i,
                               n_att, None)
    return (xp, xs, states, caches[0], caches[1])
```

```python
import functools
import math

import jax
import jax.numpy as jnp
import numpy as np
from jax import lax
from jax.experimental import pallas as pl
from jax.experimental.pallas import tpu as pltpu

D_MODEL = 1024
DEPTH = 4
GRID_W = 64
GLA_HEADS = 4
GLA_DK = 128
GLA_DV = 256
GLA_RANK = 16
GLA_TAU = 16.0
GLA_CHUNK = 64
GLA_QD = GLA_HEADS * GLA_DK
GLA_VD = GLA_HEADS * GLA_DV
HEAD_DIM = 128
ATT_HEADS = 8
ATT_KV_HEADS = 2
ATT_GROUP = ATT_HEADS // ATT_KV_HEADS
ATT_QD = ATT_HEADS * HEAD_DIM
ATT_KD = ATT_KV_HEADS * HEAD_DIM
ROPE_THETA = 10000.0
EPS = 1e-6

ROW_TILE = 256
GROUP_ROWS = 1024
MOD_ROWS = 8
RANK_PAD = 128
V7X_VMEM_LIMIT_BYTES = 60 * 1024 * 1024

F32 = jnp.float32
BF16 = jnp.bfloat16
_NT = (((1,), (1,)), ((), ()))


def _dot(a, b):
    return jnp.dot(a, b, preferred_element_type=F32)


def _dot_nt(a, b):
    return lax.dot_general(a, b, _NT, preferred_element_type=F32)


def _silu(x):
    return x * (1.0 / (1.0 + jnp.exp(-x)))


def _log_sigmoid(z):
    return jnp.minimum(z, 0.0) - jnp.log(1.0 + jnp.exp(-jnp.abs(z)))


def _split_top(x):
    top = pltpu.bitcast(pltpu.bitcast(x, jnp.uint32) & jnp.uint32(0xFFFF0000), F32)
    return top, x - top


def _modulated_norm(x, ng, mod_ref):
    shift = mod_ref[0, 0, :, 0:D_MODEL]
    scale = mod_ref[0, 0, :, D_MODEL:2 * D_MODEL]
    ms = jnp.mean(x * x, axis=-1, keepdims=True)
    return ((x * lax.rsqrt(ms + EPS)) * (ng * (1.0 + scale)) + shift).astype(BF16)


def _head_rms(x, g, width):
    outs = []
    for h in range(x.shape[-1] // width):
        xh = x[:, h * width:(h + 1) * width]
        ms = jnp.mean(xh * xh, axis=-1, keepdims=True)
        outs.append((xh * lax.rsqrt(ms + EPS)) * g)
    return jnp.concatenate(outs, axis=-1)


def _mod_kernel(c_ref, w_ref, b_ref, o_ref):
    acc = _dot(jnp.concatenate(_split_top(_silu(c_ref[...])), axis=0).astype(BF16), w_ref[0].astype(BF16))
    o_ref[0, :, 0, :] = acc[0:MOD_ROWS] + acc[MOD_ROWS:] + b_ref[pl.ds(pl.program_id(0), 1), :]


def _modulations(cvec, w_ada, b_ada):
    tn = D_MODEL
    n_tiles = 3 * D_MODEL // tn
    return pl.pallas_call(
        _mod_kernel,
        grid=(DEPTH, n_tiles),
        in_specs=[
            pl.BlockSpec((MOD_ROWS, D_MODEL), lambda l, j: (0, 0)),
            pl.BlockSpec((1, D_MODEL, tn), lambda l, j: (l, 0, j)),
            pl.BlockSpec((DEPTH, tn), lambda l, j: (0, j)),
        ],
        out_specs=pl.BlockSpec((1, MOD_ROWS, 1, tn), lambda l, j: (l, 0, 0, j)),
        out_shape=jax.ShapeDtypeStruct((DEPTH, MOD_ROWS, 1, 3 * D_MODEL), F32),
        compiler_params=pltpu.CompilerParams(dimension_semantics=("arbitrary", "arbitrary")),
        name="adaln_modulation",
    )(cvec, w_ada, b_ada)


def _chunk_sum_matrices():
    r = lax.broadcasted_iota(jnp.int32, (ROW_TILE, ROW_TILE), 0)
    c = lax.broadcasted_iota(jnp.int32, (ROW_TILE, ROW_TILE), 1)
    same = (r // GLA_CHUNK) == (c // GLA_CHUNK)
    prefix = jnp.where(same & (c <= r), 1.0, 0.0).astype(BF16)
    suffix = jnp.where(same & (c >= r), 1.0, 0.0).astype(BF16)
    return prefix, suffix


def _gla_kernel(*refs, seq, has_s0, emit_state, state_alias):
    n_seq = GROUP_ROWS // seq
    n_tiles = GROUP_ROWS // ROW_TILE
    n_pos = seq // GLA_CHUNK
    merge_products = n_seq > 1
    it = iter(refs)
    x_ref, mod_ref, ng_ref, win_ref, wa1_ref, wa2_ref, ba_ref, on_ref, wout_ref = (next(it) for _ in range(9))
    s0_ref = next(it) if has_s0 else None
    if state_alias:
        next(it)
    y_ref = next(it)
    sout_ref = next(it) if emit_state else None
    q_s, k_s, b_s, v_s, gate_s, o_s = (next(it) for _ in range(6))
    st_s = None if emit_state else next(it)

    def state_at(s, d, h):
        return sout_ref.at[s, 0, d, h] if emit_state else st_s.at[d, h]

    ng = ng_ref[...]
    prefix_m, suffix_m = _chunk_sum_matrices()

    def project_wide(t):
        rows = slice(t * ROW_TILE, (t + 1) * ROW_TILE)
        hb = _modulated_norm(x_ref[0, rows, :], ng, mod_ref)
        q_s[rows, :] = _dot(hb, win_ref[0, :, 0:GLA_QD]) * (GLA_DK ** -0.5)
        k_s[rows, :] = _dot(hb, win_ref[0, :, GLA_QD:2 * GLA_QD])
        v_s[rows, :] = _dot(hb, win_ref[0, :, 2 * GLA_QD:2 * GLA_QD + GLA_VD]).astype(BF16)
        gate_s[rows, :] = _dot(hb, win_ref[0, :, 2 * GLA_QD + GLA_VD:])
        low = _dot(hb, wa1_ref[0]).astype(BF16)
        return _dot(low, wa2_ref[0]) + ba_ref[...]

    def decay_sums(t, z):
        rows = slice(t * ROW_TILE, (t + 1) * ROW_TILE)
        logg2 = _log_sigmoid(z) * (math.log2(math.e) / GLA_TAU)
        for d, csum_m in enumerate((prefix_m, suffix_m)):
            parts = jnp.concatenate(_split_top(logg2[:, d * GLA_QD:(d + 1) * GLA_QD]), axis=0).astype(BF16)
            b_s[d, rows, :] = _dot(jnp.concatenate([csum_m, csum_m], axis=1), parts)

    z = project_wide(0)
    for t in range(n_tiles):
        z_next = project_wide(t + 1) if t + 1 < n_tiles else None
        decay_sums(t, z)
        z = z_next

    for s in range(n_seq):
        for d in range(2):
            for h in range(GLA_HEADS):
                state_at(s, d, h)[...] = s0_ref[0, 0, d, h] if has_s0 else jnp.zeros((GLA_DK, GLA_DV), F32)

    ri = lax.broadcasted_iota(jnp.int32, (GLA_CHUNK, GLA_CHUNK), 0)
    ci = lax.broadcasted_iota(jnp.int32, (GLA_CHUNK, GLA_CHUNK), 1)
    masks = (ci <= ri, ci >= ri)

    def scan_step(i, first_visit):
        chains = []
        for s in range(n_seq):
            for d in range(2):
                c = s * n_pos + (i if d == 0 else n_pos - 1 - i)
                rows = pl.ds(pl.multiple_of(c * GLA_CHUNK, GLA_CHUNK), GLA_CHUNK)
                b = b_s[d, rows, :]
                edge = GLA_CHUNK - 1 if d == 0 else 0
                total = b[edge:edge + 1, :]
                k = k_s[rows, :]
                qe = (q_s[rows, :] * jnp.exp2(b)).astype(BF16)
                ke = (k * jnp.exp2(-b)).astype(BF16)
                kdt = (k * jnp.exp2(total - b)).T.astype(BF16)
                dec = jnp.exp2(total)
                for h in range(GLA_HEADS):
                    kc = slice(h * GLA_DK, (h + 1) * GLA_DK)
                    chains.append((s, d, h, rows, slice(h * GLA_DV, (h + 1) * GLA_DV),
                                   qe[:, kc], ke[:, kc], kdt[kc, :], dec[:, kc]))
        scores = [_dot_nt(qe, ke) for (s, d, h, rows, vc, qe, ke, kdt, dec) in chains]
        if not merge_products:
            updates = [_dot(kdt, v_s[rows, vc]) for (s, d, h, rows, vc, qe, ke, kdt, dec) in chains]
        no_state = jnp.zeros((GLA_DK, GLA_DK), BF16)

        def decayed(st, dec):
            dec_col = jnp.broadcast_to(dec, (GLA_DK, GLA_DK)).T
            return jnp.concatenate([dec_col] * (GLA_DV // GLA_DK), axis=1) * st

        for (s, d, h, rows, vc, qe, ke, kdt, dec), sc in zip(chains, scores):
            a = jnp.where(masks[d], sc, 0.0).astype(BF16)
            st_ref = state_at(s, d, h)
            st = st_ref[...]
            lhs = jnp.concatenate([qe, a], axis=1)
            if merge_products:
                lhs = jnp.concatenate([lhs, jnp.concatenate([no_state, kdt], axis=1)], axis=0)
            both = _dot(lhs, jnp.concatenate([st.astype(BF16), v_s[rows, vc]], axis=0))
            if first_visit:
                o_s[rows, vc] = both[0:GLA_CHUNK]
            else:
                o_s[rows, vc] += both[0:GLA_CHUNK]
            if merge_products:
                st_ref[...] = decayed(st, dec) + both[GLA_CHUNK:]
        if not merge_products:
            for (s, d, h, rows, vc, qe, ke, kdt, dec), upd in zip(chains, updates):
                st_ref = state_at(s, d, h)
                st_ref[...] = decayed(st_ref[...], dec) + upd

    def first_half(i, carry):
        scan_step(i, True)
        return carry

    def second_half(i, carry):
        scan_step(i, False)
        return carry

    lax.fori_loop(0, n_pos // 2, first_half, 0, unroll=2)
    lax.fori_loop(n_pos // 2, n_pos, second_half, 0, unroll=2)

    on = on_ref[...]
    res_gate = mod_ref[0, 0, :, 2 * D_MODEL:]

    def gated(t):
        rows = slice(t * ROW_TILE, (t + 1) * ROW_TILE)
        return (_head_rms(o_s[rows, :], on, GLA_DV) * _silu(gate_s[rows, :])).astype(BF16)

    og = gated(0)
    for t in range(n_tiles):
        rows = slice(t * ROW_TILE, (t + 1) * ROW_TILE)
        og_next = gated(t + 1) if t + 1 < n_tiles else None
        y_ref[0, rows, :] = x_ref[0, rows, :] + res_gate * _dot(og, wout_ref[0])
        og = og_next


def _resident(shape, layer=None):
    if layer is None:
        return pl.BlockSpec(shape, lambda b: (0,) * len(shape), pipeline_mode=pl.Buffered(1))
    return pl.BlockSpec((1,) + shape, lambda b: (layer,) + (0,) * len(shape), pipeline_mode=pl.Buffered(1))


def _gla_layer(x, mods, layer, mod_row, ng, win, wa1, wa2, ba, on, wout, state_in, gla_idx, n_gla, states_so_far):
    batch, seq, _ = x.shape
    n_seq = GROUP_ROWS // seq
    n_groups = batch // n_seq
    has_s0 = state_in is not None
    emit_state = not has_s0
    assert not has_s0 or n_seq == 1
    in_specs = [
        pl.BlockSpec((1, GROUP_ROWS, D_MODEL), lambda b: (b, 0, 0)),
        pl.BlockSpec((1, 1, 1, 3 * D_MODEL), lambda b: (layer, mod_row(b), 0, 0)),
        _resident((1, D_MODEL)),
        _resident((D_MODEL, 2 * GLA_QD + 2 * GLA_VD), gla_idx),
        _resident((D_MODEL, RANK_PAD), gla_idx),
        _resident((RANK_PAD, 2 * GLA_QD), gla_idx),
        _resident((1, 2 * GLA_QD)),
        _resident((1, GLA_DV)),
        _resident((GLA_VD, D_MODEL), gla_idx),
    ]
    args = [x.reshape(n_groups, GROUP_ROWS, D_MODEL), mods, ng, win, wa1, wa2, ba, on, wout]
    if has_s0:
        in_specs.append(pl.BlockSpec((1, 1, 2, GLA_HEADS, GLA_DK, GLA_DV),
                                     lambda b: (b, gla_idx, 0, 0, 0, 0)))
        args.append(state_in)
    aliases = {}
    state_alias = emit_state and states_so_far is not None
    if state_alias:
        aliases[len(args)] = 1
        in_specs.append(pl.BlockSpec(memory_space=pl.ANY))
        args.append(states_so_far)
    out_specs = [pl.BlockSpec((1, GROUP_ROWS, D_MODEL), lambda b: (b, 0, 0))]
    out_shape = [jax.ShapeDtypeStruct((n_groups, GROUP_ROWS, D_MODEL), F32)]
    if emit_state:
        out_specs.append(pl.BlockSpec((n_seq, 1, 2, GLA_HEADS, GLA_DK, GLA_DV), lambda b: (b, gla_idx, 0, 0, 0, 0)))
        out_shape.append(jax.ShapeDtypeStruct((batch, n_gla, 2, GLA_HEADS, GLA_DK, GLA_DV), F32))
    scratch = [
        pltpu.VMEM((GROUP_ROWS, GLA_QD), F32),
        pltpu.VMEM((GROUP_ROWS, GLA_QD), F32),
        pltpu.VMEM((2, GROUP_ROWS, GLA_QD), F32),
        pltpu.VMEM((GROUP_ROWS, GLA_VD), BF16),
        pltpu.VMEM((GROUP_ROWS, GLA_VD), F32),
        pltpu.VMEM((GROUP_ROWS, GLA_VD), F32),
    ]
    if not emit_state:
        scratch.append(pltpu.VMEM((2, GLA_HEADS, GLA_DK, GLA_DV), F32))
    outs = pl.pallas_call(
        functools.partial(_gla_kernel, seq=seq, has_s0=has_s0, emit_state=emit_state, state_alias=state_alias),
        grid=(n_groups,),
        in_specs=in_specs,
        out_specs=out_specs,
        out_shape=out_shape,
        scratch_shapes=scratch,
        input_output_aliases=aliases,
        compiler_params=pltpu.CompilerParams(dimension_semantics=("arbitrary",),
                                             vmem_limit_bytes=V7X_VMEM_LIMIT_BYTES),
        name=f"gla_layer_seq{seq}",
    )(*args)
    y = outs[0].reshape(batch, seq, D_MODEL)
    return (y, outs[1]) if emit_state else (y, None)


def _rope_swap(x):
    lane = lax.broadcasted_iota(jnp.int32, x.shape, 1)
    quarter = HEAD_DIM // 4
    first = (lane % (2 * quarter)) < quarter
    return jnp.where(first, pltpu.roll(x, HEAD_DIM - quarter, 1), pltpu.roll(x, quarter, 1))


def _att_kernel(*refs, seq, latent, cache_alias):
    n_tiles = seq // ROW_TILE
    n_keys = seq + (refs[9].shape[2] if latent else 0)
    it = iter(refs)
    x_ref, mod_ref, ng_ref, win_ref, qn_ref, kn_ref, wout_ref = (next(it) for _ in range(7))
    if latent:
        cos_ref, sin_ref, ck_ref, cv_ref = (next(it) for _ in range(4))
    if cache_alias:
        next(it), next(it)
    y_ref = next(it)
    if not latent:
        kout_ref, vout_ref = next(it), next(it)
    q_s, k_s, v_s, gate_s, ao_s = (next(it) for _ in range(5))

    ng = ng_ref[...]
    qn = qn_ref[...]
    kn = kn_ref[...]
    exp2_scale = (HEAD_DIM ** -0.5) * math.log2(math.e)

    def project(t, carry):
        rows = pl.ds(pl.multiple_of(t * ROW_TILE, ROW_TILE), ROW_TILE)
        hb = _modulated_norm(x_ref[0, rows, :], ng, mod_ref)
        q = _head_rms(_dot(hb, win_ref[0, :, 0:ATT_QD]), qn, HEAD_DIM)
        k = _head_rms(_dot(hb, win_ref[0, :, ATT_QD:ATT_QD + ATT_KD]), kn, HEAD_DIM)
        v = _dot(hb, win_ref[0, :, ATT_QD + ATT_KD:ATT_QD + 2 * ATT_KD])
        gate_s[rows, :] = _dot(hb, win_ref[0, :, ATT_QD + 2 * ATT_KD:])
        if latent:
            cos = cos_ref[rows, :]
            sin = sin_ref[rows, :]
            q = jnp.concatenate(
                [q[:, h * HEAD_DIM:(h + 1) * HEAD_DIM] * cos + _rope_swap(q[:, h * HEAD_DIM:(h + 1) * HEAD_DIM]) * sin
                 for h in range(ATT_HEADS)], axis=-1)
            k = jnp.concatenate(
                [k[:, h * HEAD_DIM:(h + 1) * HEAD_DIM] * cos + _rope_swap(k[:, h * HEAD_DIM:(h + 1) * HEAD_DIM]) * sin
                 for h in range(ATT_KV_HEADS)], axis=-1)
        else:
            for h in range(ATT_KV_HEADS):
                kout_ref[0, 0, rows, h, :] = k[:, h * HEAD_DIM:(h + 1) * HEAD_DIM]
                vout_ref[0, 0, rows, h, :] = v[:, h * HEAD_DIM:(h + 1) * HEAD_DIM]
        q_s[rows, :] = (q * exp2_scale).astype(BF16)
        k_s[rows, :] = k.astype(BF16)
        for h in range(ATT_KV_HEADS):
            v_s[rows, 2 * h * HEAD_DIM:(2 * h + 1) * HEAD_DIM] = v[:, h * HEAD_DIM:(h + 1) * HEAD_DIM].astype(BF16)
        return carry

    for h in range(ATT_KV_HEADS):
        v_s[:, (2 * h + 1) * HEAD_DIM:(2 * h + 2) * HEAD_DIM] = jnp.ones((n_keys, HEAD_DIM), BF16)
    lax.fori_loop(0, n_tiles, project, 0, unroll=True)
    if latent:
        for h in range(ATT_KV_HEADS):
            k_s[seq:n_keys, h * HEAD_DIM:(h + 1) * HEAD_DIM] = ck_ref[0, 0, :, h, :].astype(BF16)
            v_s[seq:n_keys, 2 * h * HEAD_DIM:(2 * h + 1) * HEAD_DIM] = cv_ref[0, 0, :, h, :].astype(BF16)

    def attend(t, carry):
        rows = pl.ds(pl.multiple_of(t * ROW_TILE, ROW_TILE), ROW_TILE)

        def scores(h):
            kc = slice((h // ATT_GROUP) * HEAD_DIM, (h // ATT_GROUP + 1) * HEAD_DIM)
            return _dot_nt(q_s[rows, h * HEAD_DIM:(h + 1) * HEAD_DIM], k_s[:, kc])

        s = scores(0)
        for h in range(ATT_HEADS):
            s_next = scores(h + 1) if h + 1 < ATT_HEADS else None
            vc = slice((h // ATT_GROUP) * 2 * HEAD_DIM, (h // ATT_GROUP + 1) * 2 * HEAD_DIM)
            p = jnp.exp2(s - jnp.max(s, axis=-1, keepdims=True))
            o = _dot(p.astype(BF16), v_s[:, vc])
            ao_s[rows, h * HEAD_DIM:(h + 1) * HEAD_DIM] = o[:, 0:HEAD_DIM] / o[:, HEAD_DIM:]
            s = s_next
        return carry

    lax.fori_loop(0, n_tiles, attend, 0, unroll=2)

    res_gate = mod_ref[0, 0, :, 2 * D_MODEL:]

    def finish(t, carry):
        rows = pl.ds(pl.multiple_of(t * ROW_TILE, ROW_TILE), ROW_TILE)
        y = _dot((ao_s[rows, :] * _silu(gate_s[rows, :])).astype(BF16), wout_ref[0])
        y_ref[0, rows, :] = x_ref[0, rows, :] + res_gate * y
        return carry

    lax.fori_loop(0, n_tiles, finish, 0, unroll=True)


def _att_layer(x, mods, layer, mod_row, ng, win, qn, kn, wout, latent_inputs, att_idx, n_att, caches_so_far):
    batch, seq, _ = x.shape
    latent = latent_inputs is not None
    in_specs = [
        pl.BlockSpec((1, seq, D_MODEL), lambda b: (b, 0, 0)),
        pl.BlockSpec((1, 1, 1, 3 * D_MODEL), lambda b: (layer, mod_row(b), 0, 0)),
        _resident((1, D_MODEL)),
        _resident((D_MODEL, 2 * ATT_QD + 2 * ATT_KD), att_idx),
        _resident((1, HEAD_DIM)),
        _resident((1, HEAD_DIM)),
        _resident((ATT_QD, D_MODEL), att_idx),
    ]
    args = [x, mods, ng, win, qn, kn, wout]
    n_keys = seq
    if latent:
        cos, sin, cache_k, cache_v = latent_inputs
        past = cache_k.shape[2]
        n_keys = seq + past
        in_specs += [
            _resident((seq, HEAD_DIM)),
            _resident((seq, HEAD_DIM)),
            pl.BlockSpec((1, 1, past, ATT_KV_HEADS, HEAD_DIM), lambda b: (b, att_idx, 0, 0, 0)),
            pl.BlockSpec((1, 1, past, ATT_KV_HEADS, HEAD_DIM), lambda b: (b, att_idx, 0, 0, 0)),
        ]
        args += [cos, sin, cache_k, cache_v]
    aliases = {}
    cache_alias = (not latent) and caches_so_far is not None
    if cache_alias:
        aliases = {len(args): 1, len(args) + 1: 2}
        in_specs += [pl.BlockSpec(memory_space=pl.ANY)] * 2
        args += list(caches_so_far)
    out_specs = [pl.BlockSpec((1, seq, D_MODEL), lambda b: (b, 0, 0))]
    out_shape = [jax.ShapeDtypeStruct((batch, seq, D_MODEL), F32)]
    if not latent:
        out_specs += [pl.BlockSpec((1, 1, seq, ATT_KV_HEADS, HEAD_DIM), lambda b: (b, att_idx, 0, 0, 0))] * 2
        out_shape += [jax.ShapeDtypeStruct((batch, n_att, seq, ATT_KV_HEADS, HEAD_DIM), F32)] * 2
    scratch = [
        pltpu.VMEM((seq, ATT_QD), BF16),
        pltpu.VMEM((n_keys, ATT_KD), BF16),
        pltpu.VMEM((n_keys, 2 * ATT_KD), BF16),
        pltpu.VMEM((seq, ATT_QD), F32),
        pltpu.VMEM((seq, ATT_QD), F32),
    ]
    outs = pl.pallas_call(
        functools.partial(_att_kernel, seq=seq, latent=latent, cache_alias=cache_alias),
        grid=(batch,),
        in_specs=in_specs,
        out_specs=out_specs,
        out_shape=out_shape,
        scratch_shapes=scratch,
        input_output_aliases=aliases,
        compiler_params=pltpu.CompilerParams(dimension_semantics=("arbitrary",),
                                             vmem_limit_bytes=V7X_VMEM_LIMIT_BYTES),
        name=f"att_layer_seq{seq}",
    )(*args)
    return outs


def _rope_tables(seq):
    half = HEAD_DIM // 2
    nf = half // 2
    pos = np.arange(seq)
    freqs = np.float32(ROPE_THETA) ** (-np.arange(nf, dtype=np.float32) / np.float32(nf))
    ang_row = (pos // GRID_W).astype(np.float32)[:, None] * freqs[None, :]
    ang_col = (pos % GRID_W).astype(np.float32)[:, None] * freqs[None, :]
    cos = np.concatenate([np.cos(ang_row)] * 2 + [np.cos(ang_col)] * 2, axis=-1)
    sin = np.concatenate([-np.sin(ang_row), np.sin(ang_row), -np.sin(ang_col), np.sin(ang_col)], axis=-1)
    return jnp.asarray(cos, F32), jnp.asarray(sin, F32)


def kernel(x_prompt, x_sample, state_gla, cache_k, cache_v, c, c_ctx, norm_g, w_ada, b_ada,
           gla_w_in, gla_wa1, gla_wa2, gla_ba, gla_onorm, gla_w_out,
           att_w_in, att_qnorm, att_knorm, att_w_out):
    n_dec = x_sample.shape[0]
    assert 1 + n_dec <= MOD_ROWS
    assert x_prompt.shape[1] % ROW_TILE == 0 and x_sample.shape[1] % ROW_TILE == 0

    cvec = jnp.concatenate([c_ctx[None], c, jnp.zeros((MOD_ROWS - 1 - n_dec, D_MODEL), F32)], axis=0)
    mods = _modulations(cvec, w_ada, b_ada)
    ctx_row = lambda b: 0
    dec_row = lambda b: b + 1

    cos, sin = _rope_tables(x_sample.shape[1])
    n_gla, n_att = gla_w_in.shape[0], att_w_in.shape[0]

    gla_win, gla_wout = gla_w_in.astype(BF16), gla_w_out.astype(BF16)
    att_win, att_wout = att_w_in.astype(BF16), att_w_out.astype(BF16)
    wa1 = jnp.concatenate([gla_wa1[:, 0], gla_wa1[:, 1],
                           jnp.zeros((n_gla, D_MODEL, RANK_PAD - 2 * GLA_RANK), F32)], axis=-1).astype(BF16)
    zeros_qd = jnp.zeros((n_gla, GLA_RANK, GLA_QD), F32)
    wa2 = jnp.concatenate([jnp.concatenate([gla_wa2[:, 0], zeros_qd], axis=-1),
                           jnp.concatenate([zeros_qd, gla_wa2[:, 1]], axis=-1),
                           jnp.zeros((n_gla, RANK_PAD - 2 * GLA_RANK, 2 * GLA_QD), F32)], axis=1).astype(BF16)

    xp, xs = x_prompt, x_sample
    states, caches = None, None
    for l in range(DEPTH):
        i = l // 2
        ng = norm_g[l].reshape(1, D_MODEL)
        if l % 2 == 0:
            ba = gla_ba[i].reshape(1, 2 * GLA_QD)
            on = gla_onorm[i].reshape(1, GLA_DV)
            common = (ng, gla_win, wa1, wa2, ba, on, gla_wout)
            xp, states = _gla_layer(xp, mods, l, ctx_row, *common, None, i, n_gla, states)
            xs, _ = _gla_layer(xs, mods, l, dec_row, *common, state_gla, i, n_gla, None)
        else:
            qn = att_qnorm[i].reshape(1, HEAD_DIM)
            kn = att_knorm[i].reshape(1, HEAD_DIM)
            xp, *caches = _att_layer(xp, mods, l, ctx_row, ng, att_win, qn, kn, att_wout, None, i, n_att, caches)
            (xs,) = _att_layer(xs, mods, l, dec_row, ng, att_win, qn, kn, att_wout, (cos, sin, cache_k, cache_v), i,
                               n_att, None)
    return (xp, xs, states, caches[0], caches[1])
```

```python
import functools
import math

import jax
import jax.numpy as jnp
import numpy as np
from jax import lax
from jax.experimental import pallas as pl
from jax.experimental.pallas import tpu as pltpu

D_MODEL = 1024
DEPTH = 4
GRID_W = 64
GLA_HEADS = 4
GLA_DK = 128
GLA_DV = 256
GLA_RANK = 16
GLA_TAU = 16.0
GLA_CHUNK = 64
GLA_QD = GLA_HEADS * GLA_DK
GLA_VD = GLA_HEADS * GLA_DV
HEAD_DIM = 128
ATT_HEADS = 8
ATT_KV_HEADS = 2
ATT_GROUP = ATT_HEADS // ATT_KV_HEADS
ATT_QD = ATT_HEADS * HEAD_DIM
ATT_KD = ATT_KV_HEADS * HEAD_DIM
ROPE_THETA = 10000.0
EPS = 1e-6

ROW_TILE = 256
GROUP_ROWS = 1024
MOD_ROWS = 8
RANK_PAD = 128
V7X_VMEM_LIMIT_BYTES = 60 * 1024 * 1024

F32 = jnp.float32
BF16 = jnp.bfloat16
_NT = (((1,), (1,)), ((), ()))


def _dot(a, b):
    return jnp.dot(a, b, preferred_element_type=F32)


def _dot_nt(a, b):
    return lax.dot_general(a, b, _NT, preferred_element_type=F32)


def _silu(x):
    return x * (1.0 / (1.0 + jnp.exp(-x)))


def _log_sigmoid(z):
    return jnp.minimum(z, 0.0) - jnp.log(1.0 + jnp.exp(-jnp.abs(z)))


def _split_top(x):
    top = pltpu.bitcast(pltpu.bitcast(x, jnp.uint32) & jnp.uint32(0xFFFF0000), F32)
    return top, x - top


def _modulated_norm(x, ng, mod_ref):
    shift = mod_ref[0, 0, :, 0:D_MODEL]
    scale = mod_ref[0, 0, :, D_MODEL:2 * D_MODEL]
    ms = jnp.mean(x * x, axis=-1, keepdims=True)
    return ((x * lax.rsqrt(ms + EPS)) * (ng * (1.0 + scale)) + shift).astype(BF16)


def _head_rms(x, g, width):
    outs = []
    for h in range(x.shape[-1] // width):
        xh = x[:, h * width:(h + 1) * width]
        ms = jnp.mean(xh * xh, axis=-1, keepdims=True)
        outs.append((xh * lax.rsqrt(ms + EPS)) * g)
    return jnp.concatenate(outs, axis=-1)


def _mod_kernel(c_ref, w_ref, b_ref, o_ref):
    acc = _dot(jnp.concatenate(_split_top(_silu(c_ref[...])), axis=0).astype(BF16), w_ref[0].astype(BF16))
    o_ref[0, :, 0, :] = acc[0:MOD_ROWS] + acc[MOD_ROWS:] + b_ref[pl.ds(pl.program_id(0), 1), :]


def _modulations(cvec, w_ada, b_ada):
    tn = D_MODEL
    n_tiles = 3 * D_MODEL // tn
    return pl.pallas_call(
        _mod_kernel,
        grid=(DEPTH, n_tiles),
        in_specs=[
            pl.BlockSpec((MOD_ROWS, D_MODEL), lambda l, j: (0, 0)),
            pl.BlockSpec((1, D_MODEL, tn), lambda l, j: (l, 0, j)),
            pl.BlockSpec((DEPTH, tn), lambda l, j: (0, j)),
        ],
        out_specs=pl.BlockSpec((1, MOD_ROWS, 1, tn), lambda l, j: (l, 0, 0, j)),
        out_shape=jax.ShapeDtypeStruct((DEPTH, MOD_ROWS, 1, 3 * D_MODEL), F32),
        compiler_params=pltpu.CompilerParams(dimension_semantics=("arbitrary", "arbitrary")),
        name="adaln_modulation",
    )(cvec, w_ada, b_ada)


def _chunk_sum_matrices():
    r = lax.broadcasted_iota(jnp.int32, (ROW_TILE, ROW_TILE), 0)
    c = lax.broadcasted_iota(jnp.int32, (ROW_TILE, ROW_TILE), 1)
    same = (r // GLA_CHUNK) == (c // GLA_CHUNK)
    prefix = jnp.where(same & (c <= r), 1.0, 0.0).astype(BF16)
    suffix = jnp.where(same & (c >= r), 1.0, 0.0).astype(BF16)
    return prefix, suffix


def _gla_kernel(*refs, seq, has_s0, emit_state, state_alias, layer, gla_idx):
    n_seq = GROUP_ROWS // seq
    n_tiles = GROUP_ROWS // ROW_TILE
    n_pos = seq // GLA_CHUNK
    merge_products = n_seq > 1
    it = iter(refs)
    x_ref, mod_ref, ng_ref, win_ref, wa1_ref, wa2_ref, ba_ref, on_ref, wout_ref = (next(it) for _ in range(9))
    s0_ref = next(it) if has_s0 else None
    if state_alias:
        next(it)
    y_ref = next(it)
    sout_ref = next(it) if emit_state else None
    q_s, k_s, b_s, v_s, gate_s, o_s = (next(it) for _ in range(6))
    st_s = None if emit_state else next(it)

    def state_at(s, d, h):
        return sout_ref.at[s, 0, d, h] if emit_state else st_s.at[d, h]

    ng = ng_ref[layer:layer + 1, :]
    ba = jnp.concatenate([ba_ref[gla_idx, 0:1, :], ba_ref[gla_idx, 1:2, :]], axis=1)
    prefix_m, suffix_m = _chunk_sum_matrices()

    def project_wide(t):
        rows = slice(t * ROW_TILE, (t + 1) * ROW_TILE)
        hb = _modulated_norm(x_ref[0, rows, :], ng, mod_ref)
        q_s[rows, :] = _dot(hb, win_ref[0, :, 0:GLA_QD]) * (GLA_DK ** -0.5)
        k_s[rows, :] = _dot(hb, win_ref[0, :, GLA_QD:2 * GLA_QD])
        v_s[rows, :] = _dot(hb, win_ref[0, :, 2 * GLA_QD:2 * GLA_QD + GLA_VD]).astype(BF16)
        gate_s[rows, :] = _dot(hb, win_ref[0, :, 2 * GLA_QD + GLA_VD:])
        low = _dot(hb, wa1_ref[0]).astype(BF16)
        return _dot(low, wa2_ref[0]) + ba

    def decay_sums(t, z):
        rows = slice(t * ROW_TILE, (t + 1) * ROW_TILE)
        logg2 = _log_sigmoid(z) * (math.log2(math.e) / GLA_TAU)
        for d, csum_m in enumerate((prefix_m, suffix_m)):
            parts = jnp.concatenate(_split_top(logg2[:, d * GLA_QD:(d + 1) * GLA_QD]), axis=0).astype(BF16)
            b_s[d, rows, :] = _dot(jnp.concatenate([csum_m, csum_m], axis=1), parts)

    z = project_wide(0)
    for t in range(n_tiles):
        z_next = project_wide(t + 1) if t + 1 < n_tiles else None
        decay_sums(t, z)
        z = z_next

    for s in range(n_seq):
        for d in range(2):
            for h in range(GLA_HEADS):
                state_at(s, d, h)[...] = s0_ref[0, 0, d, h] if has_s0 else jnp.zeros((GLA_DK, GLA_DV), F32)

    ri = lax.broadcasted_iota(jnp.int32, (GLA_CHUNK, GLA_CHUNK), 0)
    ci = lax.broadcasted_iota(jnp.int32, (GLA_CHUNK, GLA_CHUNK), 1)
    masks = (ci <= ri, ci >= ri)

    def scan_step(i, first_visit):
        chains = []
        for s in range(n_seq):
            for d in range(2):
                c = s * n_pos + (i if d == 0 else n_pos - 1 - i)
                rows = pl.ds(pl.multiple_of(c * GLA_CHUNK, GLA_CHUNK), GLA_CHUNK)
                b = b_s[d, rows, :]
                edge = GLA_CHUNK - 1 if d == 0 else 0
                total = b[edge:edge + 1, :]
                k = k_s[rows, :]
                qe = (q_s[rows, :] * jnp.exp2(b)).astype(BF16)
                ke = (k * jnp.exp2(-b)).astype(BF16)
                kdt = (k * jnp.exp2(total - b)).T.astype(BF16)
                dec = jnp.exp2(total)
                for h in range(GLA_HEADS):
                    kc = slice(h * GLA_DK, (h + 1) * GLA_DK)
                    chains.append((s, d, h, rows, slice(h * GLA_DV, (h + 1) * GLA_DV),
                                   qe[:, kc], ke[:, kc], kdt[kc, :], dec[:, kc]))
        scores = [_dot_nt(qe, ke) for (s, d, h, rows, vc, qe, ke, kdt, dec) in chains]
        if not merge_products:
            updates = [_dot(kdt, v_s[rows, vc]) for (s, d, h, rows, vc, qe, ke, kdt, dec) in chains]
        no_state = jnp.zeros((GLA_DK, GLA_DK), BF16)

        def decayed(st, dec):
            dec_col = jnp.broadcast_to(dec, (GLA_DK, GLA_DK)).T
            return jnp.concatenate([dec_col] * (GLA_DV // GLA_DK), axis=1) * st

        for (s, d, h, rows, vc, qe, ke, kdt, dec), sc in zip(chains, scores):
            a = jnp.where(masks[d], sc, 0.0).astype(BF16)
            st_ref = state_at(s, d, h)
            st = st_ref[...]
            lhs = jnp.concatenate([qe, a], axis=1)
            if merge_products:
                lhs = jnp.concatenate([lhs, jnp.concatenate([no_state, kdt], axis=1)], axis=0)
            both = _dot(lhs, jnp.concatenate([st.astype(BF16), v_s[rows, vc]], axis=0))
            if first_visit:
                o_s[rows, vc] = both[0:GLA_CHUNK]
            else:
                o_s[rows, vc] += both[0:GLA_CHUNK]
            if merge_products:
                st_ref[...] = decayed(st, dec) + both[GLA_CHUNK:]
        if not merge_products:
            for (s, d, h, rows, vc, qe, ke, kdt, dec), upd in zip(chains, updates):
                st_ref = state_at(s, d, h)
                st_ref[...] = decayed(st_ref[...], dec) + upd

    def first_half(i, carry):
        scan_step(i, True)
        return carry

    def second_half(i, carry):
        scan_step(i, False)
        return carry

    lax.fori_loop(0, n_pos // 2, first_half, 0, unroll=2)
    lax.fori_loop(n_pos // 2, n_pos, second_half, 0, unroll=2)

    on = on_ref[gla_idx:gla_idx + 1, :]
    res_gate = mod_ref[0, 0, :, 2 * D_MODEL:]

    def gated(t):
        rows = slice(t * ROW_TILE, (t + 1) * ROW_TILE)
        return (_head_rms(o_s[rows, :], on, GLA_DV) * _silu(gate_s[rows, :])).astype(BF16)

    og = gated(0)
    for t in range(n_tiles):
        rows = slice(t * ROW_TILE, (t + 1) * ROW_TILE)
        og_next = gated(t + 1) if t + 1 < n_tiles else None
        y_ref[0, rows, :] = x_ref[0, rows, :] + res_gate * _dot(og, wout_ref[0])
        og = og_next


def _resident(shape, layer=None):
    if layer is None:
        return pl.BlockSpec(shape, lambda b: (0,) * len(shape), pipeline_mode=pl.Buffered(1))
    return pl.BlockSpec((1,) + shape, lambda b: (layer,) + (0,) * len(shape), pipeline_mode=pl.Buffered(1))


def _gla_layer(x, mods, layer, mod_row, ng, win, wa1, wa2, ba, on, wout, state_in, gla_idx, n_gla, states_so_far):
    batch, seq, _ = x.shape
    n_seq = GROUP_ROWS // seq
    n_groups = batch // n_seq
    has_s0 = state_in is not None
    emit_state = not has_s0
    assert not has_s0 or n_seq == 1
    in_specs = [
        pl.BlockSpec((1, GROUP_ROWS, D_MODEL), lambda b: (b, 0, 0)),
        pl.BlockSpec((1, 1, 1, 3 * D_MODEL), lambda b: (layer, mod_row(b), 0, 0)),
        _resident(ng.shape),
        _resident((D_MODEL, 2 * GLA_QD + 2 * GLA_VD), gla_idx),
        _resident((D_MODEL, RANK_PAD), gla_idx),
        _resident((RANK_PAD, 2 * GLA_QD), gla_idx),
        _resident(ba.shape),
        _resident(on.shape),
        _resident((GLA_VD, D_MODEL), gla_idx),
    ]
    args = [x.reshape(n_groups, GROUP_ROWS, D_MODEL), mods, ng, win, wa1, wa2, ba, on, wout]
    if has_s0:
        in_specs.append(pl.BlockSpec((1, 1, 2, GLA_HEADS, GLA_DK, GLA_DV),
                                     lambda b: (b, gla_idx, 0, 0, 0, 0)))
        args.append(state_in)
    aliases = {}
    state_alias = emit_state and states_so_far is not None
    if state_alias:
        aliases[len(args)] = 1
        in_specs.append(pl.BlockSpec(memory_space=pl.ANY))
        args.append(states_so_far)
    out_specs = [pl.BlockSpec((1, GROUP_ROWS, D_MODEL), lambda b: (b, 0, 0))]
    out_shape = [jax.ShapeDtypeStruct((n_groups, GROUP_ROWS, D_MODEL), F32)]
    if emit_state:
        out_specs.append(pl.BlockSpec((n_seq, 1, 2, GLA_HEADS, GLA_DK, GLA_DV), lambda b: (b, gla_idx, 0, 0, 0, 0)))
        out_shape.append(jax.ShapeDtypeStruct((batch, n_gla, 2, GLA_HEADS, GLA_DK, GLA_DV), F32))
    scratch = [
        pltpu.VMEM((GROUP_ROWS, GLA_QD), F32),
        pltpu.VMEM((GROUP_ROWS, GLA_QD), F32),
        pltpu.VMEM((2, GROUP_ROWS, GLA_QD), F32),
        pltpu.VMEM((GROUP_ROWS, GLA_VD), BF16),
        pltpu.VMEM((GROUP_ROWS, GLA_VD), F32),
        pltpu.VMEM((GROUP_ROWS, GLA_VD), F32),
    ]
    if not emit_state:
        scratch.append(pltpu.VMEM((2, GLA_HEADS, GLA_DK, GLA_DV), F32))
    outs = pl.pallas_call(
        functools.partial(_gla_kernel, seq=seq, has_s0=has_s0, emit_state=emit_state, state_alias=state_alias,
                          layer=layer, gla_idx=gla_idx),
        grid=(n_groups,),
        in_specs=in_specs,
        out_specs=out_specs,
        out_shape=out_shape,
        scratch_shapes=scratch,
        input_output_aliases=aliases,
        compiler_params=pltpu.CompilerParams(dimension_semantics=("arbitrary",),
                                             vmem_limit_bytes=V7X_VMEM_LIMIT_BYTES),
        name=f"gla_layer_seq{seq}",
    )(*args)
    y = outs[0].reshape(batch, seq, D_MODEL)
    return (y, outs[1]) if emit_state else (y, None)


def _rope_swap(x):
    lane = lax.broadcasted_iota(jnp.int32, x.shape, 1)
    quarter = HEAD_DIM // 4
    first = (lane % (2 * quarter)) < quarter
    return jnp.where(first, pltpu.roll(x, HEAD_DIM - quarter, 1), pltpu.roll(x, quarter, 1))


def _att_kernel(*refs, seq, latent, cache_alias, layer, att_idx):
    n_tiles = seq // ROW_TILE
    n_keys = seq + (refs[9].shape[2] if latent else 0)
    it = iter(refs)
    x_ref, mod_ref, ng_ref, win_ref, qn_ref, kn_ref, wout_ref = (next(it) for _ in range(7))
    if latent:
        cos_ref, sin_ref, ck_ref, cv_ref = (next(it) for _ in range(4))
    if cache_alias:
        next(it), next(it)
    y_ref = next(it)
    if not latent:
        kout_ref, vout_ref = next(it), next(it)
    q_s, k_s, v_s, gate_s, ao_s = (next(it) for _ in range(5))

    ng = ng_ref[layer:layer + 1, :]
    qn = qn_ref[att_idx:att_idx + 1, :]
    kn = kn_ref[att_idx:att_idx + 1, :]
    exp2_scale = (HEAD_DIM ** -0.5) * math.log2(math.e)

    def project(t, carry):
        rows = pl.ds(pl.multiple_of(t * ROW_TILE, ROW_TILE), ROW_TILE)
        hb = _modulated_norm(x_ref[0, rows, :], ng, mod_ref)
        q = _head_rms(_dot(hb, win_ref[0, :, 0:ATT_QD]), qn, HEAD_DIM)
        k = _head_rms(_dot(hb, win_ref[0, :, ATT_QD:ATT_QD + ATT_KD]), kn, HEAD_DIM)
        v = _dot(hb, win_ref[0, :, ATT_QD + ATT_KD:ATT_QD + 2 * ATT_KD])
        gate_s[rows, :] = _dot(hb, win_ref[0, :, ATT_QD + 2 * ATT_KD:])
        if latent:
            cos = cos_ref[rows, :]
            sin = sin_ref[rows, :]
            q = jnp.concatenate(
                [q[:, h * HEAD_DIM:(h + 1) * HEAD_DIM] * cos + _rope_swap(q[:, h * HEAD_DIM:(h + 1) * HEAD_DIM]) * sin
                 for h in range(ATT_HEADS)], axis=-1)
            k = jnp.concatenate(
                [k[:, h * HEAD_DIM:(h + 1) * HEAD_DIM] * cos + _rope_swap(k[:, h * HEAD_DIM:(h + 1) * HEAD_DIM]) * sin
                 for h in range(ATT_KV_HEADS)], axis=-1)
        else:
            for h in range(ATT_KV_HEADS):
                kout_ref[0, 0, rows, h, :] = k[:, h * HEAD_DIM:(h + 1) * HEAD_DIM]
                vout_ref[0, 0, rows, h, :] = v[:, h * HEAD_DIM:(h + 1) * HEAD_DIM]
        q_s[rows, :] = (q * exp2_scale).astype(BF16)
        k_s[rows, :] = k.astype(BF16)
        for h in range(ATT_KV_HEADS):
            v_s[rows, 2 * h * HEAD_DIM:(2 * h + 1) * HEAD_DIM] = v[:, h * HEAD_DIM:(h + 1) * HEAD_DIM].astype(BF16)
        return carry

    for h in range(ATT_KV_HEADS):
        v_s[:, (2 * h + 1) * HEAD_DIM:(2 * h + 2) * HEAD_DIM] = jnp.ones((n_keys, HEAD_DIM), BF16)
    lax.fori_loop(0, n_tiles, project, 0, unroll=True)
    if latent:
        for h in range(ATT_KV_HEADS):
            k_s[seq:n_keys, h * HEAD_DIM:(h + 1) * HEAD_DIM] = ck_ref[0, 0, :, h, :].astype(BF16)
            v_s[seq:n_keys, 2 * h * HEAD_DIM:(2 * h + 1) * HEAD_DIM] = cv_ref[0, 0, :, h, :].astype(BF16)

    def attend(t, carry):
        rows = pl.ds(pl.multiple_of(t * ROW_TILE, ROW_TILE), ROW_TILE)

        def scores(h):
            kc = slice((h // ATT_GROUP) * HEAD_DIM, (h // ATT_GROUP + 1) * HEAD_DIM)
            return _dot_nt(q_s[rows, h * HEAD_DIM:(h + 1) * HEAD_DIM], k_s[:, kc])

        s = scores(0)
        for h in range(ATT_HEADS):
            s_next = scores(h + 1) if h + 1 < ATT_HEADS else None
            vc = slice((h // ATT_GROUP) * 2 * HEAD_DIM, (h // ATT_GROUP + 1) * 2 * HEAD_DIM)
            p = jnp.exp2(s - jnp.max(s, axis=-1, keepdims=True))
            o = _dot(p.astype(BF16), v_s[:, vc])
            ao_s[rows, h * HEAD_DIM:(h + 1) * HEAD_DIM] = o[:, 0:HEAD_DIM] / o[:, HEAD_DIM:]
            s = s_next
        return carry

    lax.fori_loop(0, n_tiles, attend, 0, unroll=2)

    res_gate = mod_ref[0, 0, :, 2 * D_MODEL:]

    def finish(t, carry):
        rows = pl.ds(pl.multiple_of(t * ROW_TILE, ROW_TILE), ROW_TILE)
        y = _dot((ao_s[rows, :] * _silu(gate_s[rows, :])).astype(BF16), wout_ref[0])
        y_ref[0, rows, :] = x_ref[0, rows, :] + res_gate * y
        return carry

    lax.fori_loop(0, n_tiles, finish, 0, unroll=True)


def _att_layer(x, mods, layer, mod_row, ng, win, qn, kn, wout, latent_inputs, att_idx, n_att, caches_so_far):
    batch, seq, _ = x.shape
    latent = latent_inputs is not None
    in_specs = [
        pl.BlockSpec((1, seq, D_MODEL), lambda b: (b, 0, 0)),
        pl.BlockSpec((1, 1, 1, 3 * D_MODEL), lambda b: (layer, mod_row(b), 0, 0)),
        _resident(ng.shape),
        _resident((D_MODEL, 2 * ATT_QD + 2 * ATT_KD), att_idx),
        _resident(qn.shape),
        _resident(kn.shape),
        _resident((ATT_QD, D_MODEL), att_idx),
    ]
    args = [x, mods, ng, win, qn, kn, wout]
    n_keys = seq
    if latent:
        cos, sin, cache_k, cache_v = latent_inputs
        past = cache_k.shape[2]
        n_keys = seq + past
        in_specs += [
            _resident((seq, HEAD_DIM)),
            _resident((seq, HEAD_DIM)),
            pl.BlockSpec((1, 1, past, ATT_KV_HEADS, HEAD_DIM), lambda b: (b, att_idx, 0, 0, 0)),
            pl.BlockSpec((1, 1, past, ATT_KV_HEADS, HEAD_DIM), lambda b: (b, att_idx, 0, 0, 0)),
        ]
        args += [cos, sin, cache_k, cache_v]
    aliases = {}
    cache_alias = (not latent) and caches_so_far is not None
    if cache_alias:
        aliases = {len(args): 1, len(args) + 1: 2}
        in_specs += [pl.BlockSpec(memory_space=pl.ANY)] * 2
        args += list(caches_so_far)
    out_specs = [pl.BlockSpec((1, seq, D_MODEL), lambda b: (b, 0, 0))]
    out_shape = [jax.ShapeDtypeStruct((batch, seq, D_MODEL), F32)]
    if not latent:
        out_specs += [pl.BlockSpec((1, 1, seq, ATT_KV_HEADS, HEAD_DIM), lambda b: (b, att_idx, 0, 0, 0))] * 2
        out_shape += [jax.ShapeDtypeStruct((batch, n_att, seq, ATT_KV_HEADS, HEAD_DIM), F32)] * 2
    scratch = [
        pltpu.VMEM((seq, ATT_QD), BF16),
        pltpu.VMEM((n_keys, ATT_KD), BF16),
        pltpu.VMEM((n_keys, 2 * ATT_KD), BF16),
        pltpu.VMEM((seq, ATT_QD), F32),
        pltpu.VMEM((seq, ATT_QD), F32),
    ]
    outs = pl.pallas_call(
        functools.partial(_att_kernel, seq=seq, latent=latent, cache_alias=cache_alias, layer=layer, att_idx=att_idx),
        grid=(batch,),
        in_specs=in_specs,
        out_specs=out_specs,
        out_shape=out_shape,
        scratch_shapes=scratch,
        input_output_aliases=aliases,
        compiler_params=pltpu.CompilerParams(dimension_semantics=("arbitrary",),
                                             vmem_limit_bytes=V7X_VMEM_LIMIT_BYTES),
        name=f"att_layer_seq{seq}",
    )(*args)
    return outs


def _rope_tables(seq):
    half = HEAD_DIM // 2
    nf = half // 2
    pos = np.arange(seq)
    freqs = np.float32(ROPE_THETA) ** (-np.arange(nf, dtype=np.float32) / np.float32(nf))
    ang_row = (pos // GRID_W).astype(np.float32)[:, None] * freqs[None, :]
    ang_col = (pos % GRID_W).astype(np.float32)[:, None] * freqs[None, :]
    cos = np.concatenate([np.cos(ang_row)] * 2 + [np.cos(ang_col)] * 2, axis=-1)
    sin = np.concatenate([-np.sin(ang_row), np.sin(ang_row), -np.sin(ang_col), np.sin(ang_col)], axis=-1)
    return jnp.asarray(cos, F32), jnp.asarray(sin, F32)


def kernel(x_prompt, x_sample, state_gla, cache_k, cache_v, c, c_ctx, norm_g, w_ada, b_ada,
           gla_w_in, gla_wa1, gla_wa2, gla_ba, gla_onorm, gla_w_out,
           att_w_in, att_qnorm, att_knorm, att_w_out):
    n_dec = x_sample.shape[0]
    assert 1 + n_dec <= MOD_ROWS
    assert x_prompt.shape[1] % ROW_TILE == 0 and x_sample.shape[1] % ROW_TILE == 0

    cvec = jnp.concatenate([c_ctx[None], c, jnp.zeros((MOD_ROWS - 1 - n_dec, D_MODEL), F32)], axis=0)
    mods = _modulations(cvec, w_ada, b_ada)
    ctx_row = lambda b: 0
    dec_row = lambda b: b + 1

    cos, sin = _rope_tables(x_sample.shape[1])
    n_gla, n_att = gla_w_in.shape[0], att_w_in.shape[0]

    gla_win, gla_wout = gla_w_in.astype(BF16), gla_w_out.astype(BF16)
    att_win, att_wout = att_w_in.astype(BF16), att_w_out.astype(BF16)
    wa1 = jnp.concatenate([gla_wa1[:, 0], gla_wa1[:, 1],
                           jnp.zeros((n_gla, D_MODEL, RANK_PAD - 2 * GLA_RANK), F32)], axis=-1).astype(BF16)
    zeros_qd = jnp.zeros((n_gla, GLA_RANK, GLA_QD), F32)
    wa2 = jnp.concatenate([jnp.concatenate([gla_wa2[:, 0], zeros_qd], axis=-1),
                           jnp.concatenate([zeros_qd, gla_wa2[:, 1]], axis=-1),
                           jnp.zeros((n_gla, RANK_PAD - 2 * GLA_RANK, 2 * GLA_QD), F32)], axis=1).astype(BF16)

    xp, xs = x_prompt, x_sample
    states, caches = None, None
    for l in range(DEPTH):
        i = l // 2
        if l % 2 == 0:
            common = (norm_g, gla_win, wa1, wa2, gla_ba, gla_onorm, gla_wout)
            xp, states = _gla_layer(xp, mods, l, ctx_row, *common, None, i, n_gla, states)
            xs, _ = _gla_layer(xs, mods, l, dec_row, *common, state_gla, i, n_gla, None)
        else:
            xp, *caches = _att_layer(xp, mods, l, ctx_row, norm_g, att_win, att_qnorm, att_knorm, att_wout, None, i, n_att,
                                     caches)
            (xs,) = _att_layer(xs, mods, l, dec_row, norm_g, att_win, att_qnorm, att_knorm, att_wout,
                               (cos, sin, cache_k, cache_v), i, n_att, None)
    return (xp, xs, states, caches[0], caches[1])
```

```python
import functools
import math

import jax
import jax.numpy as jnp
import numpy as np
from jax import lax
from jax.experimental import pallas as pl
from jax.experimental.pallas import tpu as pltpu

D_MODEL = 1024
DEPTH = 4
GRID_W = 64
GLA_HEADS = 4
GLA_DK = 128
GLA_DV = 256
GLA_RANK = 16
GLA_TAU = 16.0
GLA_CHUNK = 64
GLA_QD = GLA_HEADS * GLA_DK
GLA_VD = GLA_HEADS * GLA_DV
HEAD_DIM = 128
ATT_HEADS = 8
ATT_KV_HEADS = 2
ATT_GROUP = ATT_HEADS // ATT_KV_HEADS
ATT_QD = ATT_HEADS * HEAD_DIM
ATT_KD = ATT_KV_HEADS * HEAD_DIM
ROPE_THETA = 10000.0
EPS = 1e-6

ROW_TILE = 256
GROUP_ROWS = 1024
MOD_ROWS = 8
RANK_PAD = 128
V7X_VMEM_LIMIT_BYTES = 60 * 1024 * 1024

F32 = jnp.float32
BF16 = jnp.bfloat16
_NT = (((1,), (1,)), ((), ()))


def _dot(a, b):
    return jnp.dot(a, b, preferred_element_type=F32)


def _dot_nt(a, b):
    return lax.dot_general(a, b, _NT, preferred_element_type=F32)


def _silu(x):
    return x * (1.0 / (1.0 + jnp.exp(-x)))


def _log_sigmoid(z):
    return jnp.minimum(z, 0.0) - jnp.log(1.0 + jnp.exp(-jnp.abs(z)))


def _split_top(x):
    top = pltpu.bitcast(pltpu.bitcast(x, jnp.uint32) & jnp.uint32(0xFFFF0000), F32)
    return top, x - top


def _modulated_norm(x, ng, mod_ref):
    shift = mod_ref[0, 0, :, 0:D_MODEL]
    scale = mod_ref[0, 0, :, D_MODEL:2 * D_MODEL]
    ms = jnp.mean(x * x, axis=-1, keepdims=True)
    return ((x * lax.rsqrt(ms + EPS)) * (ng * (1.0 + scale)) + shift).astype(BF16)


def _head_rms(x, g, width):
    outs = []
    for h in range(x.shape[-1] // width):
        xh = x[:, h * width:(h + 1) * width]
        ms = jnp.mean(xh * xh, axis=-1, keepdims=True)
        outs.append((xh * lax.rsqrt(ms + EPS)) * g)
    return jnp.concatenate(outs, axis=-1)


def _mod_kernel(c_ref, w_ref, b_ref, o_ref):
    acc = _dot(jnp.concatenate(_split_top(_silu(c_ref[...])), axis=0).astype(BF16), w_ref[0].astype(BF16))
    o_ref[0, :, 0, :] = acc[0:MOD_ROWS] + acc[MOD_ROWS:] + b_ref[pl.ds(pl.program_id(0), 1), :]


def _modulations(cvec, w_ada, b_ada):
    tn = D_MODEL
    n_tiles = 3 * D_MODEL // tn
    return pl.pallas_call(
        _mod_kernel,
        grid=(DEPTH, n_tiles),
        in_specs=[
            pl.BlockSpec((MOD_ROWS, D_MODEL), lambda l, j: (0, 0)),
            pl.BlockSpec((1, D_MODEL, tn), lambda l, j: (l, 0, j)),
            pl.BlockSpec((DEPTH, tn), lambda l, j: (0, j)),
        ],
        out_specs=pl.BlockSpec((1, MOD_ROWS, 1, tn), lambda l, j: (l, 0, 0, j)),
        out_shape=jax.ShapeDtypeStruct((DEPTH, MOD_ROWS, 1, 3 * D_MODEL), F32),
        compiler_params=pltpu.CompilerParams(dimension_semantics=("arbitrary", "arbitrary")),
        name="adaln_modulation",
    )(cvec, w_ada, b_ada)


def _chunk_sum_matrices():
    r = lax.broadcasted_iota(jnp.int32, (ROW_TILE, ROW_TILE), 0)
    c = lax.broadcasted_iota(jnp.int32, (ROW_TILE, ROW_TILE), 1)
    same = (r // GLA_CHUNK) == (c // GLA_CHUNK)
    prefix = jnp.where(same & (c <= r), 1.0, 0.0).astype(BF16)
    suffix = jnp.where(same & (c >= r), 1.0, 0.0).astype(BF16)
    return prefix, suffix


def _gla_kernel(*refs, seq, has_s0, emit_state, state_alias, layer, gla_idx):
    n_seq = GROUP_ROWS // seq
    n_tiles = GROUP_ROWS // ROW_TILE
    n_pos = seq // GLA_CHUNK
    it = iter(refs)
    x_ref, mod_ref, ng_ref, win_ref, wa1_ref, wa2_ref, ba_ref, on_ref, wout_ref = (next(it) for _ in range(9))
    s0_ref = next(it) if has_s0 else None
    if state_alias:
        next(it)
    y_ref = next(it)
    sout_ref = next(it) if emit_state else None
    q_s, k_s, b_s, v_s, gate_s, o_s = (next(it) for _ in range(6))
    st_s = None if emit_state else next(it)

    def state_at(s, d, h):
        return sout_ref.at[s, 0, d, h] if emit_state else st_s.at[d, h]

    ng = ng_ref[layer:layer + 1, :]
    ba = jnp.concatenate([ba_ref[gla_idx, 0:1, :], ba_ref[gla_idx, 1:2, :]], axis=1)
    prefix_m, suffix_m = _chunk_sum_matrices()

    def project_wide(t):
        rows = slice(t * ROW_TILE, (t + 1) * ROW_TILE)
        hb = _modulated_norm(x_ref[0, rows, :], ng, mod_ref)
        q_s[rows, :] = _dot(hb, win_ref[0, :, 0:GLA_QD]) * (GLA_DK ** -0.5)
        k_s[rows, :] = _dot(hb, win_ref[0, :, GLA_QD:2 * GLA_QD])
        v_s[rows, :] = _dot(hb, win_ref[0, :, 2 * GLA_QD:2 * GLA_QD + GLA_VD]).astype(BF16)
        gate_s[rows, :] = _dot(hb, win_ref[0, :, 2 * GLA_QD + GLA_VD:])
        low = _dot(hb, wa1_ref[0]).astype(BF16)
        return _dot(low, wa2_ref[0]) + ba

    def decay_sums(t, z):
        rows = slice(t * ROW_TILE, (t + 1) * ROW_TILE)
        logg2 = _log_sigmoid(z) * (math.log2(math.e) / GLA_TAU)
        for d, csum_m in enumerate((prefix_m, suffix_m)):
            parts = jnp.concatenate(_split_top(logg2[:, d * GLA_QD:(d + 1) * GLA_QD]), axis=0).astype(BF16)
            b_s[d, rows, :] = _dot(jnp.concatenate([csum_m, csum_m], axis=1), parts)

    z = project_wide(0)
    for t in range(n_tiles):
        z_next = project_wide(t + 1) if t + 1 < n_tiles else None
        decay_sums(t, z)
        z = z_next

    for s in range(n_seq):
        for d in range(2):
            for h in range(GLA_HEADS):
                state_at(s, d, h)[...] = s0_ref[0, 0, d, h] if has_s0 else jnp.zeros((GLA_DK, GLA_DV), F32)

    ri = lax.broadcasted_iota(jnp.int32, (GLA_CHUNK, GLA_CHUNK), 0)
    ci = lax.broadcasted_iota(jnp.int32, (GLA_CHUNK, GLA_CHUNK), 1)
    masks = (ci <= ri, ci >= ri)

    def scan_step(i, first_visit):
        chains = []
        for s in range(n_seq):
            for d in range(2):
                c = s * n_pos + (i if d == 0 else n_pos - 1 - i)
                rows = pl.ds(pl.multiple_of(c * GLA_CHUNK, GLA_CHUNK), GLA_CHUNK)
                b = b_s[d, rows, :]
                edge = GLA_CHUNK - 1 if d == 0 else 0
                total = b[edge:edge + 1, :]
                k = k_s[rows, :]
                qe = (q_s[rows, :] * jnp.exp2(b)).astype(BF16)
                ke = (k * jnp.exp2(-b)).astype(BF16)
                kdt = (k * jnp.exp2(total - b)).T.astype(BF16)
                dec = jnp.exp2(total)
                for h in range(GLA_HEADS):
                    kc = slice(h * GLA_DK, (h + 1) * GLA_DK)
                    chains.append((s, d, h, rows, slice(h * GLA_DV, (h + 1) * GLA_DV),
                                   qe[:, kc], ke[:, kc], kdt[kc, :], dec[:, kc]))
        scores = [_dot_nt(qe, ke) for (s, d, h, rows, vc, qe, ke, kdt, dec) in chains]
        no_state = jnp.zeros((GLA_DK, GLA_DK), BF16)
        for (s, d, h, rows, vc, qe, ke, kdt, dec), sc in zip(chains, scores):
            a = jnp.where(masks[d], sc, 0.0).astype(BF16)
            st_ref = state_at(s, d, h)
            st = st_ref[...]
            both = _dot(jnp.concatenate([jnp.concatenate([qe, a], axis=1),
                                         jnp.concatenate([no_state, kdt], axis=1)], axis=0),
                        jnp.concatenate([st.astype(BF16), v_s[rows, vc]], axis=0))
            if first_visit:
                o_s[rows, vc] = both[0:GLA_CHUNK]
            else:
                o_s[rows, vc] += both[0:GLA_CHUNK]
            dec_col = jnp.broadcast_to(dec, (GLA_DK, GLA_DK)).T
            st_ref[...] = jnp.concatenate([dec_col] * (GLA_DV // GLA_DK), axis=1) * st + both[GLA_CHUNK:]

    def first_half(i, carry):
        scan_step(i, True)
        return carry

    def second_half(i, carry):
        scan_step(i, False)
        return carry

    lax.fori_loop(0, n_pos // 2, first_half, 0, unroll=2)
    lax.fori_loop(n_pos // 2, n_pos, second_half, 0, unroll=2)

    on = on_ref[gla_idx:gla_idx + 1, :]
    res_gate = mod_ref[0, 0, :, 2 * D_MODEL:]

    def gated(t):
        rows = slice(t * ROW_TILE, (t + 1) * ROW_TILE)
        return (_head_rms(o_s[rows, :], on, GLA_DV) * _silu(gate_s[rows, :])).astype(BF16)

    og = gated(0)
    for t in range(n_tiles):
        rows = slice(t * ROW_TILE, (t + 1) * ROW_TILE)
        og_next = gated(t + 1) if t + 1 < n_tiles else None
        y_ref[0, rows, :] = x_ref[0, rows, :] + res_gate * _dot(og, wout_ref[0])
        og = og_next


def _resident(shape, layer=None):
    if layer is None:
        return pl.BlockSpec(shape, lambda b: (0,) * len(shape), pipeline_mode=pl.Buffered(1))
    return pl.BlockSpec((1,) + shape, lambda b: (layer,) + (0,) * len(shape), pipeline_mode=pl.Buffered(1))


def _gla_layer(x, mods, layer, mod_row, ng, win, wa1, wa2, ba, on, wout, state_in, gla_idx, n_gla, states_so_far):
    batch, seq, _ = x.shape
    n_seq = GROUP_ROWS // seq
    n_groups = batch // n_seq
    has_s0 = state_in is not None
    emit_state = not has_s0
    assert not has_s0 or n_seq == 1
    in_specs = [
        pl.BlockSpec((1, GROUP_ROWS, D_MODEL), lambda b: (b, 0, 0)),
        pl.BlockSpec((1, 1, 1, 3 * D_MODEL), lambda b: (layer, mod_row(b), 0, 0)),
        _resident(ng.shape),
        _resident((D_MODEL, 2 * GLA_QD + 2 * GLA_VD), gla_idx),
        _resident((D_MODEL, RANK_PAD), gla_idx),
        _resident((RANK_PAD, 2 * GLA_QD), gla_idx),
        _resident(ba.shape),
        _resident(on.shape),
        _resident((GLA_VD, D_MODEL), gla_idx),
    ]
    args = [x.reshape(n_groups, GROUP_ROWS, D_MODEL), mods, ng, win, wa1, wa2, ba, on, wout]
    if has_s0:
        in_specs.append(pl.BlockSpec((1, 1, 2, GLA_HEADS, GLA_DK, GLA_DV),
                                     lambda b: (b, gla_idx, 0, 0, 0, 0)))
        args.append(state_in)
    aliases = {}
    state_alias = emit_state and states_so_far is not None
    if state_alias:
        aliases[len(args)] = 1
        in_specs.append(pl.BlockSpec(memory_space=pl.ANY))
        args.append(states_so_far)
    out_specs = [pl.BlockSpec((1, GROUP_ROWS, D_MODEL), lambda b: (b, 0, 0))]
    out_shape = [jax.ShapeDtypeStruct((n_groups, GROUP_ROWS, D_MODEL), F32)]
    if emit_state:
        out_specs.append(pl.BlockSpec((n_seq, 1, 2, GLA_HEADS, GLA_DK, GLA_DV), lambda b: (b, gla_idx, 0, 0, 0, 0)))
        out_shape.append(jax.ShapeDtypeStruct((batch, n_gla, 2, GLA_HEADS, GLA_DK, GLA_DV), F32))
    scratch = [
        pltpu.VMEM((GROUP_ROWS, GLA_QD), F32),
        pltpu.VMEM((GROUP_ROWS, GLA_QD), F32),
        pltpu.VMEM((2, GROUP_ROWS, GLA_QD), F32),
        pltpu.VMEM((GROUP_ROWS, GLA_VD), BF16),
        pltpu.VMEM((GROUP_ROWS, GLA_VD), F32),
        pltpu.VMEM((GROUP_ROWS, GLA_VD), F32),
    ]
    if not emit_state:
        scratch.append(pltpu.VMEM((2, GLA_HEADS, GLA_DK, GLA_DV), F32))
    outs = pl.pallas_call(
        functools.partial(_gla_kernel, seq=seq, has_s0=has_s0, emit_state=emit_state, state_alias=state_alias,
                          layer=layer, gla_idx=gla_idx),
        grid=(n_groups,),
        in_specs=in_specs,
        out_specs=out_specs,
        out_shape=out_shape,
        scratch_shapes=scratch,
        input_output_aliases=aliases,
        compiler_params=pltpu.CompilerParams(dimension_semantics=("arbitrary",),
                                             vmem_limit_bytes=V7X_VMEM_LIMIT_BYTES),
        name=f"gla_layer_seq{seq}",
    )(*args)
    y = outs[0].reshape(batch, seq, D_MODEL)
    return (y, outs[1]) if emit_state else (y, None)


def _rope_swap(x):
    lane = lax.broadcasted_iota(jnp.int32, x.shape, 1)
    quarter = HEAD_DIM // 4
    first = (lane % (2 * quarter)) < quarter
    return jnp.where(first, pltpu.roll(x, HEAD_DIM - quarter, 1), pltpu.roll(x, quarter, 1))


def _att_kernel(*refs, seq, latent, cache_alias, layer, att_idx):
    n_tiles = seq // ROW_TILE
    n_keys = seq + (refs[9].shape[2] if latent else 0)
    it = iter(refs)
    x_ref, mod_ref, ng_ref, win_ref, qn_ref, kn_ref, wout_ref = (next(it) for _ in range(7))
    if latent:
        cos_ref, sin_ref, ck_ref, cv_ref = (next(it) for _ in range(4))
    if cache_alias:
        next(it), next(it)
    y_ref = next(it)
    if not latent:
        kout_ref, vout_ref = next(it), next(it)
    q_s, k_s, v_s, gate_s, ao_s = (next(it) for _ in range(5))

    ng = ng_ref[layer:layer + 1, :]
    qn = qn_ref[att_idx:att_idx + 1, :]
    kn = kn_ref[att_idx:att_idx + 1, :]
    exp2_scale = (HEAD_DIM ** -0.5) * math.log2(math.e)

    def project(t, carry):
        rows = pl.ds(pl.multiple_of(t * ROW_TILE, ROW_TILE), ROW_TILE)
        hb = _modulated_norm(x_ref[0, rows, :], ng, mod_ref)
        q = _head_rms(_dot(hb, win_ref[0, :, 0:ATT_QD]), qn, HEAD_DIM)
        k = _head_rms(_dot(hb, win_ref[0, :, ATT_QD:ATT_QD + ATT_KD]), kn, HEAD_DIM)
        v = _dot(hb, win_ref[0, :, ATT_QD + ATT_KD:ATT_QD + 2 * ATT_KD])
        gate_s[rows, :] = _dot(hb, win_ref[0, :, ATT_QD + 2 * ATT_KD:])
        if latent:
            cos = cos_ref[rows, :]
            sin = sin_ref[rows, :]
            q = jnp.concatenate(
                [q[:, h * HEAD_DIM:(h + 1) * HEAD_DIM] * cos + _rope_swap(q[:, h * HEAD_DIM:(h + 1) * HEAD_DIM]) * sin
                 for h in range(ATT_HEADS)], axis=-1)
            k = jnp.concatenate(
                [k[:, h * HEAD_DIM:(h + 1) * HEAD_DIM] * cos + _rope_swap(k[:, h * HEAD_DIM:(h + 1) * HEAD_DIM]) * sin
                 for h in range(ATT_KV_HEADS)], axis=-1)
        else:
            for h in range(ATT_KV_HEADS):
                kout_ref[0, 0, rows, h, :] = k[:, h * HEAD_DIM:(h + 1) * HEAD_DIM]
                vout_ref[0, 0, rows, h, :] = v[:, h * HEAD_DIM:(h + 1) * HEAD_DIM]
        q_s[rows, :] = (q * exp2_scale).astype(BF16)
        k_s[rows, :] = k.astype(BF16)
        for h in range(ATT_KV_HEADS):
            v_s[rows, 2 * h * HEAD_DIM:(2 * h + 1) * HEAD_DIM] = v[:, h * HEAD_DIM:(h + 1) * HEAD_DIM].astype(BF16)
        return carry

    for h in range(ATT_KV_HEADS):
        v_s[:, (2 * h + 1) * HEAD_DIM:(2 * h + 2) * HEAD_DIM] = jnp.ones((n_keys, HEAD_DIM), BF16)
    lax.fori_loop(0, n_tiles, project, 0, unroll=True)
    if latent:
        for h in range(ATT_KV_HEADS):
            k_s[seq:n_keys, h * HEAD_DIM:(h + 1) * HEAD_DIM] = ck_ref[0, 0, :, h, :].astype(BF16)
            v_s[seq:n_keys, 2 * h * HEAD_DIM:(2 * h + 1) * HEAD_DIM] = cv_ref[0, 0, :, h, :].astype(BF16)

    def attend(t, carry):
        rows = pl.ds(pl.multiple_of(t * ROW_TILE, ROW_TILE), ROW_TILE)

        def scores(h):
            kc = slice((h // ATT_GROUP) * HEAD_DIM, (h // ATT_GROUP + 1) * HEAD_DIM)
            return _dot_nt(q_s[rows, h * HEAD_DIM:(h + 1) * HEAD_DIM], k_s[:, kc])

        s = scores(0)
        for h in range(ATT_HEADS):
            s_next = scores(h + 1) if h + 1 < ATT_HEADS else None
            vc = slice((h // ATT_GROUP) * 2 * HEAD_DIM, (h // ATT_GROUP + 1) * 2 * HEAD_DIM)
            p = jnp.exp2(s - jnp.max(s, axis=-1, keepdims=True))
            o = _dot(p.astype(BF16), v_s[:, vc])
            ao_s[rows, h * HEAD_DIM:(h + 1) * HEAD_DIM] = o[:, 0:HEAD_DIM] / o[:, HEAD_DIM:]
            s = s_next
        return carry

    lax.fori_loop(0, n_tiles, attend, 0, unroll=2)

    res_gate = mod_ref[0, 0, :, 2 * D_MODEL:]

    def finish(t, carry):
        rows = pl.ds(pl.multiple_of(t * ROW_TILE, ROW_TILE), ROW_TILE)
        y = _dot((ao_s[rows, :] * _silu(gate_s[rows, :])).astype(BF16), wout_ref[0])
        y_ref[0, rows, :] = x_ref[0, rows, :] + res_gate * y
        return carry

    lax.fori_loop(0, n_tiles, finish, 0, unroll=True)


def _att_layer(x, mods, layer, mod_row, ng, win, qn, kn, wout, latent_inputs, att_idx, n_att, caches_so_far):
    batch, seq, _ = x.shape
    latent = latent_inputs is not None
    in_specs = [
        pl.BlockSpec((1, seq, D_MODEL), lambda b: (b, 0, 0)),
        pl.BlockSpec((1, 1, 1, 3 * D_MODEL), lambda b: (layer, mod_row(b), 0, 0)),
        _resident(ng.shape),
        _resident((D_MODEL, 2 * ATT_QD + 2 * ATT_KD), att_idx),
        _resident(qn.shape),
        _resident(kn.shape),
        _resident((ATT_QD, D_MODEL), att_idx),
    ]
    args = [x, mods, ng, win, qn, kn, wout]
    n_keys = seq
    if latent:
        cos, sin, cache_k, cache_v = latent_inputs
        past = cache_k.shape[2]
        n_keys = seq + past
        in_specs += [
            _resident((seq, HEAD_DIM)),
            _resident((seq, HEAD_DIM)),
            pl.BlockSpec((1, 1, past, ATT_KV_HEADS, HEAD_DIM), lambda b: (b, att_idx, 0, 0, 0)),
            pl.BlockSpec((1, 1, past, ATT_KV_HEADS, HEAD_DIM), lambda b: (b, att_idx, 0, 0, 0)),
        ]
        args += [cos, sin, cache_k, cache_v]
    aliases = {}
    cache_alias = (not latent) and caches_so_far is not None
    if cache_alias:
        aliases = {len(args): 1, len(args) + 1: 2}
        in_specs += [pl.BlockSpec(memory_space=pl.ANY)] * 2
        args += list(caches_so_far)
    out_specs = [pl.BlockSpec((1, seq, D_MODEL), lambda b: (b, 0, 0))]
    out_shape = [jax.ShapeDtypeStruct((batch, seq, D_MODEL), F32)]
    if not latent:
        out_specs += [pl.BlockSpec((1, 1, seq, ATT_KV_HEADS, HEAD_DIM), lambda b: (b, att_idx, 0, 0, 0))] * 2
        out_shape += [jax.ShapeDtypeStruct((batch, n_att, seq, ATT_KV_HEADS, HEAD_DIM), F32)] * 2
    scratch = [
        pltpu.VMEM((seq, ATT_QD), BF16),
        pltpu.VMEM((n_keys, ATT_KD), BF16),
        pltpu.VMEM((n_keys, 2 * ATT_KD), BF16),
        pltpu.VMEM((seq, ATT_QD), F32),
        pltpu.VMEM((seq, ATT_QD), F32),
    ]
    outs = pl.pallas_call(
        functools.partial(_att_kernel, seq=seq, latent=latent, cache_alias=cache_alias, layer=layer, att_idx=att_idx),
        grid=(batch,),
        in_specs=in_specs,
        out_specs=out_specs,
        out_shape=out_shape,
        scratch_shapes=scratch,
        input_output_aliases=aliases,
        compiler_params=pltpu.CompilerParams(dimension_semantics=("arbitrary",),
                                             vmem_limit_bytes=V7X_VMEM_LIMIT_BYTES),
        name=f"att_layer_seq{seq}",
    )(*args)
    return outs


def _rope_tables(seq):
    half = HEAD_DIM // 2
    nf = half // 2
    pos = np.arange(seq)
    freqs = np.float32(ROPE_THETA) ** (-np.arange(nf, dtype=np.float32) / np.float32(nf))
    ang_row = (pos // GRID_W).astype(np.float32)[:, None] * freqs[None, :]
    ang_col = (pos % GRID_W).astype(np.float32)[:, None] * freqs[None, :]
    cos = np.concatenate([np.cos(ang_row)] * 2 + [np.cos(ang_col)] * 2, axis=-1)
    sin = np.concatenate([-np.sin(ang_row), np.sin(ang_row), -np.sin(ang_col), np.sin(ang_col)], axis=-1)
    return jnp.asarray(cos, F32), jnp.asarray(sin, F32)


def kernel(x_prompt, x_sample, state_gla, cache_k, cache_v, c, c_ctx, norm_g, w_ada, b_ada,
           gla_w_in, gla_wa1, gla_wa2, gla_ba, gla_onorm, gla_w_out,
           att_w_in, att_qnorm, att_knorm, att_w_out):
    n_dec = x_sample.shape[0]
    assert 1 + n_dec <= MOD_ROWS
    assert x_prompt.shape[1] % ROW_TILE == 0 and x_sample.shape[1] % ROW_TILE == 0

    cvec = jnp.concatenate([c_ctx[None], c, jnp.zeros((MOD_ROWS - 1 - n_dec, D_MODEL), F32)], axis=0)
    mods = _modulations(cvec, w_ada, b_ada)
    ctx_row = lambda b: 0
    dec_row = lambda b: b + 1

    cos, sin = _rope_tables(x_sample.shape[1])
    n_gla, n_att = gla_w_in.shape[0], att_w_in.shape[0]

    gla_win, gla_wout = gla_w_in.astype(BF16), gla_w_out.astype(BF16)
    att_win, att_wout = att_w_in.astype(BF16), att_w_out.astype(BF16)
    wa1 = jnp.concatenate([gla_wa1[:, 0], gla_wa1[:, 1],
                           jnp.zeros((n_gla, D_MODEL, RANK_PAD - 2 * GLA_RANK), F32)], axis=-1).astype(BF16)
    zeros_qd = jnp.zeros((n_gla, GLA_RANK, GLA_QD), F32)
    wa2 = jnp.concatenate([jnp.concatenate([gla_wa2[:, 0], zeros_qd], axis=-1),
                           jnp.concatenate([zeros_qd, gla_wa2[:, 1]], axis=-1),
                           jnp.zeros((n_gla, RANK_PAD - 2 * GLA_RANK, 2 * GLA_QD), F32)], axis=1).astype(BF16)

    xp, xs = x_prompt, x_sample
    states, caches = None, None
    for l in range(DEPTH):
        i = l // 2
        if l % 2 == 0:
            common = (norm_g, gla_win, wa1, wa2, gla_ba, gla_onorm, gla_wout)
            xp, states = _gla_layer(xp, mods, l, ctx_row, *common, None, i, n_gla, states)
            xs, _ = _gla_layer(xs, mods, l, dec_row, *common, state_gla, i, n_gla, None)
        else:
            xp, *caches = _att_layer(xp, mods, l, ctx_row, norm_g, att_win, att_qnorm, att_knorm, att_wout, None, i, n_att,
                                     caches)
            (xs,) = _att_layer(xs, mods, l, dec_row, norm_g, att_win, att_qnorm, att_knorm, att_wout,
                               (cos, sin, cache_k, cache_v), i, n_att, None)
    return (xp, xs, states, caches[0], caches[1])
```

```python
import functools
import math

import jax
import jax.numpy as jnp
import numpy as np
from jax import lax
from jax.experimental import pallas as pl
from jax.experimental.pallas import tpu as pltpu

D_MODEL = 1024
DEPTH = 4
GRID_W = 64
GLA_HEADS = 4
GLA_DK = 128
GLA_DV = 256
GLA_RANK = 16
GLA_TAU = 16.0
GLA_CHUNK = 64
GLA_QD = GLA_HEADS * GLA_DK
GLA_VD = GLA_HEADS * GLA_DV
HEAD_DIM = 128
ATT_HEADS = 8
ATT_KV_HEADS = 2
ATT_GROUP = ATT_HEADS // ATT_KV_HEADS
ATT_QD = ATT_HEADS * HEAD_DIM
ATT_KD = ATT_KV_HEADS * HEAD_DIM
ROPE_THETA = 10000.0
EPS = 1e-6

ROW_TILE = 256
GROUP_ROWS = 1024
CTX_SEQS_PER_STEP = 2
MOD_ROWS = 8
RANK_PAD = 128
V7X_VMEM_LIMIT_BYTES = 60 * 1024 * 1024

F32 = jnp.float32
BF16 = jnp.bfloat16
_NT = (((1,), (1,)), ((), ()))


def _dot(a, b):
    return jnp.dot(a, b, preferred_element_type=F32)


def _dot_nt(a, b):
    return lax.dot_general(a, b, _NT, preferred_element_type=F32)


def _silu(x):
    return x * (1.0 / (1.0 + jnp.exp(-x)))


def _log_sigmoid(z):
    return jnp.minimum(z, 0.0) - jnp.log(1.0 + jnp.exp(-jnp.abs(z)))


def _split_top(x):
    top = pltpu.bitcast(pltpu.bitcast(x, jnp.uint32) & jnp.uint32(0xFFFF0000), F32)
    return top, x - top


def _modulated_norm(x, ng, mod_ref):
    shift = mod_ref[0, 0, :, 0:D_MODEL]
    scale = mod_ref[0, 0, :, D_MODEL:2 * D_MODEL]
    ms = jnp.mean(x * x, axis=-1, keepdims=True)
    return ((x * lax.rsqrt(ms + EPS)) * (ng * (1.0 + scale)) + shift).astype(BF16)


def _head_rms(x, g, width):
    outs = []
    for h in range(x.shape[-1] // width):
        xh = x[:, h * width:(h + 1) * width]
        ms = jnp.mean(xh * xh, axis=-1, keepdims=True)
        outs.append((xh * lax.rsqrt(ms + EPS)) * g)
    return jnp.concatenate(outs, axis=-1)


def _mod_kernel(c_ref, w_ref, b_ref, o_ref):
    acc = _dot(jnp.concatenate(_split_top(_silu(c_ref[...])), axis=0).astype(BF16), w_ref[0].astype(BF16))
    o_ref[0, :, 0, :] = acc[0:MOD_ROWS] + acc[MOD_ROWS:] + b_ref[pl.ds(pl.program_id(0), 1), :]


def _modulations(cvec, w_ada, b_ada):
    tn = D_MODEL
    n_tiles = 3 * D_MODEL // tn
    return pl.pallas_call(
        _mod_kernel,
        grid=(DEPTH, n_tiles),
        in_specs=[
            pl.BlockSpec((MOD_ROWS, D_MODEL), lambda l, j: (0, 0)),
            pl.BlockSpec((1, D_MODEL, tn), lambda l, j: (l, 0, j)),
            pl.BlockSpec((DEPTH, tn), lambda l, j: (0, j)),
        ],
        out_specs=pl.BlockSpec((1, MOD_ROWS, 1, tn), lambda l, j: (l, 0, 0, j)),
        out_shape=jax.ShapeDtypeStruct((DEPTH, MOD_ROWS, 1, 3 * D_MODEL), F32),
        compiler_params=pltpu.CompilerParams(dimension_semantics=("arbitrary", "arbitrary")),
        name="adaln_modulation",
    )(cvec, w_ada, b_ada)


def _chunk_sum_matrices():
    r = lax.broadcasted_iota(jnp.int32, (ROW_TILE, ROW_TILE), 0)
    c = lax.broadcasted_iota(jnp.int32, (ROW_TILE, ROW_TILE), 1)
    same = (r // GLA_CHUNK) == (c // GLA_CHUNK)
    prefix = jnp.where(same & (c <= r), 1.0, 0.0).astype(BF16)
    suffix = jnp.where(same & (c >= r), 1.0, 0.0).astype(BF16)
    return prefix, suffix


def _gla_kernel(*refs, seq, has_s0, emit_state, state_alias, layer, gla_idx):
    n_seq = GROUP_ROWS // seq
    n_tiles = GROUP_ROWS // ROW_TILE
    n_pos = seq // GLA_CHUNK
    it = iter(refs)
    x_ref, mod_ref, ng_ref, win_ref, wa1_ref, wa2_ref, ba_ref, on_ref, wout_ref = (next(it) for _ in range(9))
    s0_ref = next(it) if has_s0 else None
    if state_alias:
        next(it)
    y_ref = next(it)
    sout_ref = next(it) if emit_state else None
    q_s, k_s, b_s, v_s, gate_s, o_s = (next(it) for _ in range(6))
    st_s = None if emit_state else next(it)

    def state_at(s, d, h):
        return sout_ref.at[s, 0, d, h] if emit_state else st_s.at[d, h]

    ng = ng_ref[layer:layer + 1, :]
    ba = jnp.concatenate([ba_ref[gla_idx, 0:1, :], ba_ref[gla_idx, 1:2, :]], axis=1)
    prefix_m, suffix_m = _chunk_sum_matrices()

    def project_wide(t):
        rows = slice(t * ROW_TILE, (t + 1) * ROW_TILE)
        hb = _modulated_norm(x_ref[0, rows, :], ng, mod_ref)
        q_s[rows, :] = _dot(hb, win_ref[0, :, 0:GLA_QD]) * (GLA_DK ** -0.5)
        k_s[rows, :] = _dot(hb, win_ref[0, :, GLA_QD:2 * GLA_QD])
        v_s[rows, :] = _dot(hb, win_ref[0, :, 2 * GLA_QD:2 * GLA_QD + GLA_VD]).astype(BF16)
        gate_s[rows, :] = _dot(hb, win_ref[0, :, 2 * GLA_QD + GLA_VD:])
        low = _dot(hb, wa1_ref[0]).astype(BF16)
        return _dot(low, wa2_ref[0]) + ba

    def decay_sums(t, z):
        rows = slice(t * ROW_TILE, (t + 1) * ROW_TILE)
        logg2 = _log_sigmoid(z) * (math.log2(math.e) / GLA_TAU)
        for d, csum_m in enumerate((prefix_m, suffix_m)):
            parts = jnp.concatenate(_split_top(logg2[:, d * GLA_QD:(d + 1) * GLA_QD]), axis=0).astype(BF16)
            b_s[d, rows, :] = _dot(jnp.concatenate([csum_m, csum_m], axis=1), parts)

    z = project_wide(0)
    for t in range(n_tiles):
        z_next = project_wide(t + 1) if t + 1 < n_tiles else None
        decay_sums(t, z)
        z = z_next

    for s in range(n_seq):
        for d in range(2):
            for h in range(GLA_HEADS):
                state_at(s, d, h)[...] = s0_ref[0, 0, d, h] if has_s0 else jnp.zeros((GLA_DK, GLA_DV), F32)

    ri = lax.broadcasted_iota(jnp.int32, (GLA_CHUNK, GLA_CHUNK), 0)
    ci = lax.broadcasted_iota(jnp.int32, (GLA_CHUNK, GLA_CHUNK), 1)
    masks = (ci <= ri, ci >= ri)

    def scan_step(i, first_visit):
        chains = []
        for s in range(n_seq):
            for d in range(2):
                c = s * n_pos + (i if d == 0 else n_pos - 1 - i)
                rows = pl.ds(pl.multiple_of(c * GLA_CHUNK, GLA_CHUNK), GLA_CHUNK)
                b = b_s[d, rows, :]
                edge = GLA_CHUNK - 1 if d == 0 else 0
                total = b[edge:edge + 1, :]
                k = k_s[rows, :]
                qe = (q_s[rows, :] * jnp.exp2(b)).astype(BF16)
                ke = (k * jnp.exp2(-b)).astype(BF16)
                kdt = (k * jnp.exp2(total - b)).T.astype(BF16)
                dec = jnp.exp2(total)
                for h in range(GLA_HEADS):
                    kc = slice(h * GLA_DK, (h + 1) * GLA_DK)
                    chains.append((s, d, h, rows, slice(h * GLA_DV, (h + 1) * GLA_DV),
                                   qe[:, kc], ke[:, kc], kdt[kc, :], dec[:, kc]))
        scores = [_dot_nt(qe, ke) for (s, d, h, rows, vc, qe, ke, kdt, dec) in chains]
        no_state = jnp.zeros((GLA_DK, GLA_DK), BF16)
        for (s, d, h, rows, vc, qe, ke, kdt, dec), sc in zip(chains, scores):
            a = jnp.where(masks[d], sc, 0.0).astype(BF16)
            st_ref = state_at(s, d, h)
            st = st_ref[...]
            both = _dot(jnp.concatenate([jnp.concatenate([qe, a], axis=1),
                                         jnp.concatenate([no_state, kdt], axis=1)], axis=0),
                        jnp.concatenate([st.astype(BF16), v_s[rows, vc]], axis=0))
            if first_visit:
                o_s[rows, vc] = both[0:GLA_CHUNK]
            else:
                o_s[rows, vc] += both[0:GLA_CHUNK]
            dec_col = jnp.broadcast_to(dec, (GLA_DK, GLA_DK)).T
            st_ref[...] = jnp.concatenate([dec_col] * (GLA_DV // GLA_DK), axis=1) * st + both[GLA_CHUNK:]

    def first_half(i, carry):
        scan_step(i, True)
        return carry

    def second_half(i, carry):
        scan_step(i, False)
        return carry

    lax.fori_loop(0, n_pos // 2, first_half, 0, unroll=2)
    lax.fori_loop(n_pos // 2, n_pos, second_half, 0, unroll=2)

    on = on_ref[gla_idx:gla_idx + 1, :]
    res_gate = mod_ref[0, 0, :, 2 * D_MODEL:]

    def gated(t):
        rows = slice(t * ROW_TILE, (t + 1) * ROW_TILE)
        return (_head_rms(o_s[rows, :], on, GLA_DV) * _silu(gate_s[rows, :])).astype(BF16)

    og = gated(0)
    for t in range(n_tiles):
        rows = slice(t * ROW_TILE, (t + 1) * ROW_TILE)
        og_next = gated(t + 1) if t + 1 < n_tiles else None
        y_ref[0, rows, :] = x_ref[0, rows, :] + res_gate * _dot(og, wout_ref[0])
        og = og_next


def _resident(shape, layer=None):
    if layer is None:
        return pl.BlockSpec(shape, lambda b: (0,) * len(shape), pipeline_mode=pl.Buffered(1))
    return pl.BlockSpec((1,) + shape, lambda b: (layer,) + (0,) * len(shape), pipeline_mode=pl.Buffered(1))


def _gla_layer(x, mods, layer, mod_row, ng, win, wa1, wa2, ba, on, wout, state_in, gla_idx, n_gla, states_so_far):
    batch, seq, _ = x.shape
    n_seq = GROUP_ROWS // seq
    n_groups = batch // n_seq
    has_s0 = state_in is not None
    emit_state = not has_s0
    assert not has_s0 or n_seq == 1
    in_specs = [
        pl.BlockSpec((1, GROUP_ROWS, D_MODEL), lambda b: (b, 0, 0)),
        pl.BlockSpec((1, 1, 1, 3 * D_MODEL), lambda b: (layer, mod_row(b), 0, 0)),
        _resident(ng.shape),
        _resident((D_MODEL, 2 * GLA_QD + 2 * GLA_VD), gla_idx),
        _resident((D_MODEL, RANK_PAD), gla_idx),
        _resident((RANK_PAD, 2 * GLA_QD), gla_idx),
        _resident(ba.shape),
        _resident(on.shape),
        _resident((GLA_VD, D_MODEL), gla_idx),
    ]
    args = [x.reshape(n_groups, GROUP_ROWS, D_MODEL), mods, ng, win, wa1, wa2, ba, on, wout]
    if has_s0:
        in_specs.append(pl.BlockSpec((1, 1, 2, GLA_HEADS, GLA_DK, GLA_DV),
                                     lambda b: (b, gla_idx, 0, 0, 0, 0)))
        args.append(state_in)
    aliases = {}
    state_alias = emit_state and states_so_far is not None
    if state_alias:
        aliases[len(args)] = 1
        in_specs.append(pl.BlockSpec(memory_space=pl.ANY))
        args.append(states_so_far)
    out_specs = [pl.BlockSpec((1, GROUP_ROWS, D_MODEL), lambda b: (b, 0, 0))]
    out_shape = [jax.ShapeDtypeStruct((n_groups, GROUP_ROWS, D_MODEL), F32)]
    if emit_state:
        out_specs.append(pl.BlockSpec((n_seq, 1, 2, GLA_HEADS, GLA_DK, GLA_DV), lambda b: (b, gla_idx, 0, 0, 0, 0)))
        out_shape.append(jax.ShapeDtypeStruct((batch, n_gla, 2, GLA_HEADS, GLA_DK, GLA_DV), F32))
    scratch = [
        pltpu.VMEM((GROUP_ROWS, GLA_QD), F32),
        pltpu.VMEM((GROUP_ROWS, GLA_QD), F32),
        pltpu.VMEM((2, GROUP_ROWS, GLA_QD), F32),
        pltpu.VMEM((GROUP_ROWS, GLA_VD), BF16),
        pltpu.VMEM((GROUP_ROWS, GLA_VD), F32),
        pltpu.VMEM((GROUP_ROWS, GLA_VD), F32),
    ]
    if not emit_state:
        scratch.append(pltpu.VMEM((2, GLA_HEADS, GLA_DK, GLA_DV), F32))
    outs = pl.pallas_call(
        functools.partial(_gla_kernel, seq=seq, has_s0=has_s0, emit_state=emit_state, state_alias=state_alias,
                          layer=layer, gla_idx=gla_idx),
        grid=(n_groups,),
        in_specs=in_specs,
        out_specs=out_specs,
        out_shape=out_shape,
        scratch_shapes=scratch,
        input_output_aliases=aliases,
        compiler_params=pltpu.CompilerParams(dimension_semantics=("arbitrary",),
                                             vmem_limit_bytes=V7X_VMEM_LIMIT_BYTES),
        name=f"gla_layer_seq{seq}",
    )(*args)
    y = outs[0].reshape(batch, seq, D_MODEL)
    return (y, outs[1]) if emit_state else (y, None)


def _rope_swap(x):
    lane = lax.broadcasted_iota(jnp.int32, x.shape, 1)
    quarter = HEAD_DIM // 4
    first = (lane % (2 * quarter)) < quarter
    return jnp.where(first, pltpu.roll(x, HEAD_DIM - quarter, 1), pltpu.roll(x, quarter, 1))


def _att_kernel(*refs, seq, latent, cache_alias, layer, att_idx):
    n_tiles = refs[0].shape[1] // ROW_TILE
    n_keys = seq + (refs[9].shape[2] if latent else 0)
    it = iter(refs)
    x_ref, mod_ref, ng_ref, win_ref, qn_ref, kn_ref, wout_ref = (next(it) for _ in range(7))
    if latent:
        cos_ref, sin_ref, ck_ref, cv_ref = (next(it) for _ in range(4))
    if cache_alias:
        next(it), next(it)
    y_ref = next(it)
    if not latent:
        kout_ref, vout_ref = next(it), next(it)
    q_s, k_s, v_s, gate_s, ao_s = (next(it) for _ in range(5))

    ng = ng_ref[layer:layer + 1, :]
    qn = qn_ref[att_idx:att_idx + 1, :]
    kn = kn_ref[att_idx:att_idx + 1, :]
    exp2_scale = (HEAD_DIM ** -0.5) * math.log2(math.e)

    def project(t, carry):
        rows = pl.ds(pl.multiple_of(t * ROW_TILE, ROW_TILE), ROW_TILE)
        hb = _modulated_norm(x_ref[0, rows, :], ng, mod_ref)
        q = _head_rms(_dot(hb, win_ref[0, :, 0:ATT_QD]), qn, HEAD_DIM)
        k = _head_rms(_dot(hb, win_ref[0, :, ATT_QD:ATT_QD + ATT_KD]), kn, HEAD_DIM)
        v = _dot(hb, win_ref[0, :, ATT_QD + ATT_KD:ATT_QD + 2 * ATT_KD])
        gate_s[rows, :] = _dot(hb, win_ref[0, :, ATT_QD + 2 * ATT_KD:])
        if latent:
            cos = cos_ref[rows, :]
            sin = sin_ref[rows, :]
            q = jnp.concatenate(
                [q[:, h * HEAD_DIM:(h + 1) * HEAD_DIM] * cos + _rope_swap(q[:, h * HEAD_DIM:(h + 1) * HEAD_DIM]) * sin
                 for h in range(ATT_HEADS)], axis=-1)
            k = jnp.concatenate(
                [k[:, h * HEAD_DIM:(h + 1) * HEAD_DIM] * cos + _rope_swap(k[:, h * HEAD_DIM:(h + 1) * HEAD_DIM]) * sin
                 for h in range(ATT_KV_HEADS)], axis=-1)
        else:
            for h in range(ATT_KV_HEADS):
                kout_ref[t, 0, :, h, :] = k[:, h * HEAD_DIM:(h + 1) * HEAD_DIM]
                vout_ref[t, 0, :, h, :] = v[:, h * HEAD_DIM:(h + 1) * HEAD_DIM]
        q_s[rows, :] = (q * exp2_scale).astype(BF16)
        k_s[rows, :] = k.astype(BF16)
        for h in range(ATT_KV_HEADS):
            v_s[rows, 2 * h * HEAD_DIM:(2 * h + 1) * HEAD_DIM] = v[:, h * HEAD_DIM:(h + 1) * HEAD_DIM].astype(BF16)
        return carry

    for h in range(ATT_KV_HEADS):
        v_s[:, (2 * h + 1) * HEAD_DIM:(2 * h + 2) * HEAD_DIM] = jnp.ones((v_s.shape[0], HEAD_DIM), BF16)
    lax.fori_loop(0, n_tiles, project, 0, unroll=True)
    if latent:
        for h in range(ATT_KV_HEADS):
            k_s[seq:n_keys, h * HEAD_DIM:(h + 1) * HEAD_DIM] = ck_ref[0, 0, :, h, :].astype(BF16)
            v_s[seq:n_keys, 2 * h * HEAD_DIM:(2 * h + 1) * HEAD_DIM] = cv_ref[0, 0, :, h, :].astype(BF16)

    def attend(t, carry):
        rows = pl.ds(pl.multiple_of(t * ROW_TILE, ROW_TILE), ROW_TILE)
        keys = slice(None) if latent else rows

        def scores(h):
            kc = slice((h // ATT_GROUP) * HEAD_DIM, (h // ATT_GROUP + 1) * HEAD_DIM)
            return _dot_nt(q_s[rows, h * HEAD_DIM:(h + 1) * HEAD_DIM], k_s[keys, kc])

        s = scores(0)
        for h in range(ATT_HEADS):
            s_next = scores(h + 1) if h + 1 < ATT_HEADS else None
            vc = slice((h // ATT_GROUP) * 2 * HEAD_DIM, (h // ATT_GROUP + 1) * 2 * HEAD_DIM)
            p = jnp.exp2(s - jnp.max(s, axis=-1, keepdims=True))
            o = _dot(p.astype(BF16), v_s[keys, vc])
            ao_s[rows, h * HEAD_DIM:(h + 1) * HEAD_DIM] = o[:, 0:HEAD_DIM] / o[:, HEAD_DIM:]
            s = s_next
        return carry

    lax.fori_loop(0, n_tiles, attend, 0, unroll=2)

    res_gate = mod_ref[0, 0, :, 2 * D_MODEL:]

    def finish(t, carry):
        rows = pl.ds(pl.multiple_of(t * ROW_TILE, ROW_TILE), ROW_TILE)
        y = _dot((ao_s[rows, :] * _silu(gate_s[rows, :])).astype(BF16), wout_ref[0])
        y_ref[0, rows, :] = x_ref[0, rows, :] + res_gate * y
        return carry

    lax.fori_loop(0, n_tiles, finish, 0, unroll=True)


def _att_layer(x, mods, layer, mod_row, ng, win, qn, kn, wout, latent_inputs, att_idx, n_att, caches_so_far):
    batch, seq, _ = x.shape
    latent = latent_inputs is not None
    assert latent or seq == ROW_TILE
    n_seq = 1 if latent else CTX_SEQS_PER_STEP
    rows = n_seq * seq
    n_groups = batch // n_seq
    in_specs = [
        pl.BlockSpec((1, rows, D_MODEL), lambda b: (b, 0, 0)),
        pl.BlockSpec((1, 1, 1, 3 * D_MODEL), lambda b: (layer, mod_row(b), 0, 0)),
        _resident(ng.shape),
        _resident((D_MODEL, 2 * ATT_QD + 2 * ATT_KD), att_idx),
        _resident(qn.shape),
        _resident(kn.shape),
        _resident((ATT_QD, D_MODEL), att_idx),
    ]
    args = [x.reshape(n_groups, rows, D_MODEL), mods, ng, win, qn, kn, wout]
    n_keys = rows
    if latent:
        cos, sin, cache_k, cache_v = latent_inputs
        past = cache_k.shape[2]
        n_keys = seq + past
        in_specs += [
            _resident((seq, HEAD_DIM)),
            _resident((seq, HEAD_DIM)),
            pl.BlockSpec((1, 1, past, ATT_KV_HEADS, HEAD_DIM), lambda b: (b, att_idx, 0, 0, 0)),
            pl.BlockSpec((1, 1, past, ATT_KV_HEADS, HEAD_DIM), lambda b: (b, att_idx, 0, 0, 0)),
        ]
        args += [cos, sin, cache_k, cache_v]
    aliases = {}
    cache_alias = (not latent) and caches_so_far is not None
    if cache_alias:
        aliases = {len(args): 1, len(args) + 1: 2}
        in_specs += [pl.BlockSpec(memory_space=pl.ANY)] * 2
        args += list(caches_so_far)
    out_specs = [pl.BlockSpec((1, rows, D_MODEL), lambda b: (b, 0, 0))]
    out_shape = [jax.ShapeDtypeStruct((n_groups, rows, D_MODEL), F32)]
    if not latent:
        out_specs += [pl.BlockSpec((n_seq, 1, seq, ATT_KV_HEADS, HEAD_DIM), lambda b: (b, att_idx, 0, 0, 0))] * 2
        out_shape += [jax.ShapeDtypeStruct((batch, n_att, seq, ATT_KV_HEADS, HEAD_DIM), F32)] * 2
    scratch = [
        pltpu.VMEM((rows, ATT_QD), BF16),
        pltpu.VMEM((n_keys, ATT_KD), BF16),
        pltpu.VMEM((n_keys, 2 * ATT_KD), BF16),
        pltpu.VMEM((rows, ATT_QD), F32),
        pltpu.VMEM((rows, ATT_QD), F32),
    ]
    outs = pl.pallas_call(
        functools.partial(_att_kernel, seq=seq, latent=latent, cache_alias=cache_alias, layer=layer, att_idx=att_idx),
        grid=(n_groups,),
        in_specs=in_specs,
        out_specs=out_specs,
        out_shape=out_shape,
        scratch_shapes=scratch,
        input_output_aliases=aliases,
        compiler_params=pltpu.CompilerParams(dimension_semantics=("arbitrary",),
                                             vmem_limit_bytes=V7X_VMEM_LIMIT_BYTES),
        name=f"att_layer_seq{seq}",
    )(*args)
    return [outs[0].reshape(batch, seq, D_MODEL)] + list(outs[1:])


def _rope_tables(seq):
    half = HEAD_DIM // 2
    nf = half // 2
    pos = np.arange(seq)
    freqs = np.float32(ROPE_THETA) ** (-np.arange(nf, dtype=np.float32) / np.float32(nf))
    ang_row = (pos // GRID_W).astype(np.float32)[:, None] * freqs[None, :]
    ang_col = (pos % GRID_W).astype(np.float32)[:, None] * freqs[None, :]
    cos = np.concatenate([np.cos(ang_row)] * 2 + [np.cos(ang_col)] * 2, axis=-1)
    sin = np.concatenate([-np.sin(ang_row), np.sin(ang_row), -np.sin(ang_col), np.sin(ang_col)], axis=-1)
    return jnp.asarray(cos, F32), jnp.asarray(sin, F32)


def kernel(x_prompt, x_sample, state_gla, cache_k, cache_v, c, c_ctx, norm_g, w_ada, b_ada,
           gla_w_in, gla_wa1, gla_wa2, gla_ba, gla_onorm, gla_w_out,
           att_w_in, att_qnorm, att_knorm, att_w_out):
    n_dec = x_sample.shape[0]
    assert 1 + n_dec <= MOD_ROWS
    assert x_prompt.shape[1] % ROW_TILE == 0 and x_sample.shape[1] % ROW_TILE == 0

    cvec = jnp.concatenate([c_ctx[None], c, jnp.zeros((MOD_ROWS - 1 - n_dec, D_MODEL), F32)], axis=0)
    mods = _modulations(cvec, w_ada, b_ada)
    ctx_row = lambda b: 0
    dec_row = lambda b: b + 1

    cos, sin = _rope_tables(x_sample.shape[1])
    n_gla, n_att = gla_w_in.shape[0], att_w_in.shape[0]

    gla_win, gla_wout = gla_w_in.astype(BF16), gla_w_out.astype(BF16)
    att_win, att_wout = att_w_in.astype(BF16), att_w_out.astype(BF16)
    wa1 = jnp.concatenate([gla_wa1[:, 0], gla_wa1[:, 1],
                           jnp.zeros((n_gla, D_MODEL, RANK_PAD - 2 * GLA_RANK), F32)], axis=-1).astype(BF16)
    zeros_qd = jnp.zeros((n_gla, GLA_RANK, GLA_QD), F32)
    wa2 = jnp.concatenate([jnp.concatenate([gla_wa2[:, 0], zeros_qd], axis=-1),
                           jnp.concatenate([zeros_qd, gla_wa2[:, 1]], axis=-1),
                           jnp.zeros((n_gla, RANK_PAD - 2 * GLA_RANK, 2 * GLA_QD), F32)], axis=1).astype(BF16)

    xp, xs = x_prompt, x_sample
    states, caches = None, None
    for l in range(DEPTH):
        i = l // 2
        if l % 2 == 0:
            common = (norm_g, gla_win, wa1, wa2, gla_ba, gla_onorm, gla_wout)
            xp, states = _gla_layer(xp, mods, l, ctx_row, *common, None, i, n_gla, states)
            xs, _ = _gla_layer(xs, mods, l, dec_row, *common, state_gla, i, n_gla, None)
        else:
            xp, *caches = _att_layer(xp, mods, l, ctx_row, norm_g, att_win, att_qnorm, att_knorm, att_wout, None, i, n_att,
                                     caches)
            (xs,) = _att_layer(xs, mods, l, dec_row, norm_g, att_win, att_qnorm, att_knorm, att_wout,
                               (cos, sin, cache_k, cache_v), i, n_att, None)
    return (xp, xs, states, caches[0], caches[1])
```

```python
import functools
import math

import jax
import jax.numpy as jnp
import numpy as np
from jax import lax
from jax.experimental import pallas as pl
from jax.experimental.pallas import tpu as pltpu

D_MODEL = 1024
DEPTH = 4
GRID_W = 64
GLA_HEADS = 4
GLA_DK = 128
GLA_DV = 256
GLA_RANK = 16
GLA_TAU = 16.0
GLA_CHUNK = 64
GLA_QD = GLA_HEADS * GLA_DK
GLA_VD = GLA_HEADS * GLA_DV
HEAD_DIM = 128
ATT_HEADS = 8
ATT_KV_HEADS = 2
ATT_GROUP = ATT_HEADS // ATT_KV_HEADS
ATT_QD = ATT_HEADS * HEAD_DIM
ATT_KD = ATT_KV_HEADS * HEAD_DIM
ROPE_THETA = 10000.0
EPS = 1e-6

ROW_TILE = 256
GROUP_ROWS = 1024
MOD_ROWS = 8
RANK_PAD = 128
V7X_VMEM_LIMIT_BYTES = 60 * 1024 * 1024

F32 = jnp.float32
BF16 = jnp.bfloat16
_NT = (((1,), (1,)), ((), ()))


def _dot(a, b):
    return jnp.dot(a, b, preferred_element_type=F32)


def _dot_nt(a, b):
    return lax.dot_general(a, b, _NT, preferred_element_type=F32)


def _silu(x):
    return x * (1.0 / (1.0 + jnp.exp(-x)))


def _log_sigmoid(z):
    return jnp.minimum(z, 0.0) - jnp.log(1.0 + jnp.exp(-jnp.abs(z)))


def _split_top(x):
    top = pltpu.bitcast(pltpu.bitcast(x, jnp.uint32) & jnp.uint32(0xFFFF0000), F32)
    return top, x - top


def _modulated_norm(x, ng, mod_ref):
    shift = mod_ref[0, 0, :, 0:D_MODEL]
    scale = mod_ref[0, 0, :, D_MODEL:2 * D_MODEL]
    ms = jnp.mean(x * x, axis=-1, keepdims=True)
    return ((x * lax.rsqrt(ms + EPS)) * (ng * (1.0 + scale)) + shift).astype(BF16)


def _head_rms(x, g, width):
    outs = []
    for h in range(x.shape[-1] // width):
        xh = x[:, h * width:(h + 1) * width]
        ms = jnp.mean(xh * xh, axis=-1, keepdims=True)
        outs.append((xh * lax.rsqrt(ms + EPS)) * g)
    return jnp.concatenate(outs, axis=-1)


def _mod_kernel(c_ref, w_ref, b_ref, o_ref):
    acc = _dot(jnp.concatenate(_split_top(_silu(c_ref[...])), axis=0).astype(BF16), w_ref[0].astype(BF16))
    o_ref[0, :, 0, :] = acc[0:MOD_ROWS] + acc[MOD_ROWS:] + b_ref[pl.ds(pl.program_id(0), 1), :]


def _modulations(cvec, w_ada, b_ada):
    tn = D_MODEL
    n_tiles = 3 * D_MODEL // tn
    return pl.pallas_call(
        _mod_kernel,
        grid=(DEPTH, n_tiles),
        in_specs=[
            pl.BlockSpec((MOD_ROWS, D_MODEL), lambda l, j: (0, 0)),
            pl.BlockSpec((1, D_MODEL, tn), lambda l, j: (l, 0, j)),
            pl.BlockSpec((DEPTH, tn), lambda l, j: (0, j)),
        ],
        out_specs=pl.BlockSpec((1, MOD_ROWS, 1, tn), lambda l, j: (l, 0, 0, j)),
        out_shape=jax.ShapeDtypeStruct((DEPTH, MOD_ROWS, 1, 3 * D_MODEL), F32),
        compiler_params=pltpu.CompilerParams(dimension_semantics=("arbitrary", "arbitrary")),
        name="adaln_modulation",
    )(cvec, w_ada, b_ada)


def _chunk_sum_matrices():
    r = lax.broadcasted_iota(jnp.int32, (ROW_TILE, ROW_TILE), 0)
    c = lax.broadcasted_iota(jnp.int32, (ROW_TILE, ROW_TILE), 1)
    same = (r // GLA_CHUNK) == (c // GLA_CHUNK)
    prefix = jnp.where(same & (c <= r), 1.0, 0.0).astype(BF16)
    suffix = jnp.where(same & (c >= r), 1.0, 0.0).astype(BF16)
    return prefix, suffix


def _gla_kernel(*refs, seq, has_s0, emit_state, state_alias, layer, gla_idx):
    n_seq = GROUP_ROWS // seq
    n_tiles = GROUP_ROWS // ROW_TILE
    n_pos = seq // GLA_CHUNK
    it = iter(refs)
    x_ref, mod_ref, ng_ref, win_ref, wa1_ref, wa2_ref, ba_ref, on_ref, wout_ref = (next(it) for _ in range(9))
    s0_ref = next(it) if has_s0 else None
    if state_alias:
        next(it)
    y_ref = next(it)
    sout_ref = next(it) if emit_state else None
    q_s, k_s, b_s, v_s, gate_s, o_s = (next(it) for _ in range(6))
    st_s = None if emit_state else next(it)

    def state_at(s, d, h):
        return sout_ref.at[s, 0, d, h] if emit_state else st_s.at[d, h]

    ng = ng_ref[layer:layer + 1, :]
    ba = jnp.concatenate([ba_ref[gla_idx, 0:1, :], ba_ref[gla_idx, 1:2, :]], axis=1)
    prefix_m, suffix_m = _chunk_sum_matrices()

    def project_wide(t):
        rows = slice(t * ROW_TILE, (t + 1) * ROW_TILE)
        hb = _modulated_norm(x_ref[0, rows, :], ng, mod_ref)
        q_s[rows, :] = _dot(hb, win_ref[0, :, 0:GLA_QD]) * (GLA_DK ** -0.5)
        k_s[rows, :] = _dot(hb, win_ref[0, :, GLA_QD:2 * GLA_QD])
        v_s[rows, :] = _dot(hb, win_ref[0, :, 2 * GLA_QD:2 * GLA_QD + GLA_VD]).astype(BF16)
        gate_s[rows, :] = _dot(hb, win_ref[0, :, 2 * GLA_QD + GLA_VD:])
        low = _dot(hb, wa1_ref[0]).astype(BF16)
        return _dot(low, wa2_ref[0]) + ba

    def decay_sums(t, z):
        rows = slice(t * ROW_TILE, (t + 1) * ROW_TILE)
        logg2 = _log_sigmoid(z) * (math.log2(math.e) / GLA_TAU)
        for d, csum_m in enumerate((prefix_m, suffix_m)):
            parts = jnp.concatenate(_split_top(logg2[:, d * GLA_QD:(d + 1) * GLA_QD]), axis=0).astype(BF16)
            b_s[d, rows, :] = _dot(jnp.concatenate([csum_m, csum_m], axis=1), parts)

    z = project_wide(0)
    for t in range(n_tiles):
        z_next = project_wide(t + 1) if t + 1 < n_tiles else None
        decay_sums(t, z)
        z = z_next

    for s in range(n_seq):
        for d in range(2):
            for h in range(GLA_HEADS):
                state_at(s, d, h)[...] = s0_ref[0, 0, d, h] if has_s0 else jnp.zeros((GLA_DK, GLA_DV), F32)

    ri = lax.broadcasted_iota(jnp.int32, (GLA_CHUNK, GLA_CHUNK), 0)
    ci = lax.broadcasted_iota(jnp.int32, (GLA_CHUNK, GLA_CHUNK), 1)
    masks = (ci <= ri, ci >= ri)

    def scan_step(i, first_visit):
        chains = []
        for s in range(n_seq):
            for d in range(2):
                c = s * n_pos + (i if d == 0 else n_pos - 1 - i)
                rows = pl.ds(pl.multiple_of(c * GLA_CHUNK, GLA_CHUNK), GLA_CHUNK)
                b = b_s[d, rows, :]
                edge = GLA_CHUNK - 1 if d == 0 else 0
                total = b[edge:edge + 1, :]
                k = k_s[rows, :]
                qe = (q_s[rows, :] * jnp.exp2(b)).astype(BF16)
                ke = (k * jnp.exp2(-b)).astype(BF16)
                kdt = (k * jnp.exp2(total - b)).T.astype(BF16)
                dec = jnp.exp2(total)
                for h in range(GLA_HEADS):
                    kc = slice(h * GLA_DK, (h + 1) * GLA_DK)
                    chains.append((s, d, h, rows, slice(h * GLA_DV, (h + 1) * GLA_DV),
                                   qe[:, kc], ke[:, kc], kdt[kc, :], dec[:, kc]))
        scores = [_dot_nt(qe, ke) for (s, d, h, rows, vc, qe, ke, kdt, dec) in chains]
        no_state = jnp.zeros((GLA_DK, GLA_DK), BF16)
        for (s, d, h, rows, vc, qe, ke, kdt, dec), sc in zip(chains, scores):
            a = jnp.where(masks[d], sc, 0.0).astype(BF16)
            st_ref = state_at(s, d, h)
            st = st_ref[...]
            both = _dot(jnp.concatenate([jnp.concatenate([qe, a], axis=1),
                                         jnp.concatenate([no_state, kdt], axis=1)], axis=0),
                        jnp.concatenate([st.astype(BF16), v_s[rows, vc]], axis=0))
            if first_visit:
                o_s[rows, vc] = both[0:GLA_CHUNK]
            else:
                o_s[rows, vc] += both[0:GLA_CHUNK]
            dec_col = jnp.broadcast_to(dec, (GLA_DK, GLA_DK)).T
            st_ref[...] = jnp.concatenate([dec_col] * (GLA_DV // GLA_DK), axis=1) * st + both[GLA_CHUNK:]

    def first_half(i, carry):
        scan_step(i, True)
        return carry

    def second_half(i, carry):
        scan_step(i, False)
        return carry

    lax.fori_loop(0, n_pos // 2, first_half, 0, unroll=2)
    lax.fori_loop(n_pos // 2, n_pos, second_half, 0, unroll=2)

    on = on_ref[gla_idx:gla_idx + 1, :]
    res_gate = mod_ref[0, 0, :, 2 * D_MODEL:]

    def gated(t):
        rows = slice(t * ROW_TILE, (t + 1) * ROW_TILE)
        return (_head_rms(o_s[rows, :], on, GLA_DV) * _silu(gate_s[rows, :])).astype(BF16)

    og = gated(0)
    for t in range(n_tiles):
        rows = slice(t * ROW_TILE, (t + 1) * ROW_TILE)
        og_next = gated(t + 1) if t + 1 < n_tiles else None
        y_ref[0, rows, :] = x_ref[0, rows, :] + res_gate * _dot(og, wout_ref[0])
        og = og_next


def _resident(shape, layer=None):
    if layer is None:
        return pl.BlockSpec(shape, lambda b: (0,) * len(shape), pipeline_mode=pl.Buffered(1))
    return pl.BlockSpec((1,) + shape, lambda b: (layer,) + (0,) * len(shape), pipeline_mode=pl.Buffered(1))


def _gla_layer(x, mods, layer, mod_row, ng, win, wa1, wa2, ba, on, wout, state_in, gla_idx, n_gla, states_so_far):
    batch, seq, _ = x.shape
    n_seq = GROUP_ROWS // seq
    n_groups = batch // n_seq
    has_s0 = state_in is not None
    emit_state = not has_s0
    assert not has_s0 or n_seq == 1
    in_specs = [
        pl.BlockSpec((1, GROUP_ROWS, D_MODEL), lambda b: (b, 0, 0)),
        pl.BlockSpec((1, 1, 1, 3 * D_MODEL), lambda b: (layer, mod_row(b), 0, 0)),
        _resident(ng.shape),
        _resident((D_MODEL, 2 * GLA_QD + 2 * GLA_VD), gla_idx),
        _resident((D_MODEL, RANK_PAD), gla_idx),
        _resident((RANK_PAD, 2 * GLA_QD), gla_idx),
        _resident(ba.shape),
        _resident(on.shape),
        _resident((GLA_VD, D_MODEL), gla_idx),
    ]
    args = [x.reshape(n_groups, GROUP_ROWS, D_MODEL), mods, ng, win, wa1, wa2, ba, on, wout]
    if has_s0:
        in_specs.append(pl.BlockSpec((1, 1, 2, GLA_HEADS, GLA_DK, GLA_DV),
                                     lambda b: (b, gla_idx, 0, 0, 0, 0)))
        args.append(state_in)
    aliases = {}
    state_alias = emit_state and states_so_far is not None
    if state_alias:
        aliases[len(args)] = 1
        in_specs.append(pl.BlockSpec(memory_space=pl.ANY))
        args.append(states_so_far)
    out_specs = [pl.BlockSpec((1, GROUP_ROWS, D_MODEL), lambda b: (b, 0, 0))]
    out_shape = [jax.ShapeDtypeStruct((n_groups, GROUP_ROWS, D_MODEL), F32)]
    if emit_state:
        out_specs.append(pl.BlockSpec((n_seq, 1, 2, GLA_HEADS, GLA_DK, GLA_DV), lambda b: (b, gla_idx, 0, 0, 0, 0)))
        out_shape.append(jax.ShapeDtypeStruct((batch, n_gla, 2, GLA_HEADS, GLA_DK, GLA_DV), F32))
    scratch = [
        pltpu.VMEM((GROUP_ROWS, GLA_QD), F32),
        pltpu.VMEM((GROUP_ROWS, GLA_QD), F32),
        pltpu.VMEM((2, GROUP_ROWS, GLA_QD), F32),
        pltpu.VMEM((GROUP_ROWS, GLA_VD), BF16),
        pltpu.VMEM((GROUP_ROWS, GLA_VD), F32),
        pltpu.VMEM((GROUP_ROWS, GLA_VD), F32),
    ]
    if not emit_state:
        scratch.append(pltpu.VMEM((2, GLA_HEADS, GLA_DK, GLA_DV), F32))
    outs = pl.pallas_call(
        functools.partial(_gla_kernel, seq=seq, has_s0=has_s0, emit_state=emit_state, state_alias=state_alias,
                          layer=layer, gla_idx=gla_idx),
        grid=(n_groups,),
        in_specs=in_specs,
        out_specs=out_specs,
        out_shape=out_shape,
        scratch_shapes=scratch,
        input_output_aliases=aliases,
        compiler_params=pltpu.CompilerParams(dimension_semantics=("arbitrary",),
                                             vmem_limit_bytes=V7X_VMEM_LIMIT_BYTES),
        name=f"gla_layer_seq{seq}",
    )(*args)
    y = outs[0].reshape(batch, seq, D_MODEL)
    return (y, outs[1]) if emit_state else (y, None)


def _rope_swap(x):
    lane = lax.broadcasted_iota(jnp.int32, x.shape, 1)
    quarter = HEAD_DIM // 4
    first = (lane % (2 * quarter)) < quarter
    return jnp.where(first, pltpu.roll(x, HEAD_DIM - quarter, 1), pltpu.roll(x, quarter, 1))


def _att_kernel(*refs, seq, latent, cache_alias, layer, att_idx):
    n_tiles = seq // ROW_TILE
    n_keys = seq + (refs[9].shape[2] if latent else 0)
    it = iter(refs)
    x_ref, mod_ref, ng_ref, win_ref, qn_ref, kn_ref, wout_ref = (next(it) for _ in range(7))
    if latent:
        cos_ref, sin_ref, ck_ref, cv_ref = (next(it) for _ in range(4))
    if cache_alias:
        next(it), next(it)
    y_ref = next(it)
    if not latent:
        kout_ref, vout_ref = next(it), next(it)
    q_s, k_s, v_s, gate_s, ao_s = (next(it) for _ in range(5))

    ng = ng_ref[layer:layer + 1, :]
    qn = qn_ref[att_idx:att_idx + 1, :]
    kn = kn_ref[att_idx:att_idx + 1, :]
    exp2_scale = (HEAD_DIM ** -0.5) * math.log2(math.e)

    def project(t, carry):
        rows = pl.ds(pl.multiple_of(t * ROW_TILE, ROW_TILE), ROW_TILE)
        hb = _modulated_norm(x_ref[0, rows, :], ng, mod_ref)
        q = _head_rms(_dot(hb, win_ref[0, :, 0:ATT_QD]), qn, HEAD_DIM)
        k = _head_rms(_dot(hb, win_ref[0, :, ATT_QD:ATT_QD + ATT_KD]), kn, HEAD_DIM)
        v = _dot(hb, win_ref[0, :, ATT_QD + ATT_KD:ATT_QD + 2 * ATT_KD])
        gate_s[rows, :] = _dot(hb, win_ref[0, :, ATT_QD + 2 * ATT_KD:])
        if latent:
            cos = cos_ref[rows, :]
            sin = sin_ref[rows, :]
            q = jnp.concatenate(
                [q[:, h * HEAD_DIM:(h + 1) * HEAD_DIM] * cos + _rope_swap(q[:, h * HEAD_DIM:(h + 1) * HEAD_DIM]) * sin
                 for h in range(ATT_HEADS)], axis=-1)
            k = jnp.concatenate(
                [k[:, h * HEAD_DIM:(h + 1) * HEAD_DIM] * cos + _rope_swap(k[:, h * HEAD_DIM:(h + 1) * HEAD_DIM]) * sin
                 for h in range(ATT_KV_HEADS)], axis=-1)
        else:
            for h in range(ATT_KV_HEADS):
                kout_ref[0, 0, rows, h, :] = k[:, h * HEAD_DIM:(h + 1) * HEAD_DIM]
                vout_ref[0, 0, rows, h, :] = v[:, h * HEAD_DIM:(h + 1) * HEAD_DIM]
        q_s[rows, :] = (q * exp2_scale).astype(BF16)
        k_s[rows, :] = k.astype(BF16)
        for h in range(ATT_KV_HEADS):
            v_s[rows, 2 * h * HEAD_DIM:(2 * h + 1) * HEAD_DIM] = v[:, h * HEAD_DIM:(h + 1) * HEAD_DIM].astype(BF16)
        return carry

    for h in range(ATT_KV_HEADS):
        v_s[:, (2 * h + 1) * HEAD_DIM:(2 * h + 2) * HEAD_DIM] = jnp.ones((n_keys, HEAD_DIM), BF16)
    lax.fori_loop(0, n_tiles, project, 0, unroll=True)
    if latent:
        for h in range(ATT_KV_HEADS):
            k_s[seq:n_keys, h * HEAD_DIM:(h + 1) * HEAD_DIM] = ck_ref[0, 0, :, h, :].astype(BF16)
            v_s[seq:n_keys, 2 * h * HEAD_DIM:(2 * h + 1) * HEAD_DIM] = cv_ref[0, 0, :, h, :].astype(BF16)

    q_tile = min(seq, 2 * ROW_TILE)

    def attend(t, carry):
        rows = pl.ds(pl.multiple_of(t * q_tile, q_tile), q_tile)

        def scores(h):
            kc = slice((h // ATT_GROUP) * HEAD_DIM, (h // ATT_GROUP + 1) * HEAD_DIM)
            return _dot_nt(q_s[rows, h * HEAD_DIM:(h + 1) * HEAD_DIM], k_s[:, kc])

        s = scores(0)
        for h in range(ATT_HEADS):
            s_next = scores(h + 1) if h + 1 < ATT_HEADS else None
            vc = slice((h // ATT_GROUP) * 2 * HEAD_DIM, (h // ATT_GROUP + 1) * 2 * HEAD_DIM)
            p = jnp.exp2(s - jnp.max(s, axis=-1, keepdims=True))
            o = _dot(p.astype(BF16), v_s[:, vc])
            ao_s[rows, h * HEAD_DIM:(h + 1) * HEAD_DIM] = o[:, 0:HEAD_DIM] / o[:, HEAD_DIM:]
            s = s_next
        return carry

    lax.fori_loop(0, seq // q_tile, attend, 0)

    res_gate = mod_ref[0, 0, :, 2 * D_MODEL:]

    def finish(t, carry):
        rows = pl.ds(pl.multiple_of(t * ROW_TILE, ROW_TILE), ROW_TILE)
        y = _dot((ao_s[rows, :] * _silu(gate_s[rows, :])).astype(BF16), wout_ref[0])
        y_ref[0, rows, :] = x_ref[0, rows, :] + res_gate * y
        return carry

    lax.fori_loop(0, n_tiles, finish, 0, unroll=True)


def _att_layer(x, mods, layer, mod_row, ng, win, qn, kn, wout, latent_inputs, att_idx, n_att, caches_so_far):
    batch, seq, _ = x.shape
    latent = latent_inputs is not None
    in_specs = [
        pl.BlockSpec((1, seq, D_MODEL), lambda b: (b, 0, 0)),
        pl.BlockSpec((1, 1, 1, 3 * D_MODEL), lambda b: (layer, mod_row(b), 0, 0)),
        _resident(ng.shape),
        _resident((D_MODEL, 2 * ATT_QD + 2 * ATT_KD), att_idx),
        _resident(qn.shape),
        _resident(kn.shape),
        _resident((ATT_QD, D_MODEL), att_idx),
    ]
    args = [x, mods, ng, win, qn, kn, wout]
    n_keys = seq
    if latent:
        cos, sin, cache_k, cache_v = latent_inputs
        past = cache_k.shape[2]
        n_keys = seq + past
        in_specs += [
            _resident((seq, HEAD_DIM)),
            _resident((seq, HEAD_DIM)),
            pl.BlockSpec((1, 1, past, ATT_KV_HEADS, HEAD_DIM), lambda b: (b, att_idx, 0, 0, 0)),
            pl.BlockSpec((1, 1, past, ATT_KV_HEADS, HEAD_DIM), lambda b: (b, att_idx, 0, 0, 0)),
        ]
        args += [cos, sin, cache_k, cache_v]
    aliases = {}
    cache_alias = (not latent) and caches_so_far is not None
    if cache_alias:
        aliases = {len(args): 1, len(args) + 1: 2}
        in_specs += [pl.BlockSpec(memory_space=pl.ANY)] * 2
        args += list(caches_so_far)
    out_specs = [pl.BlockSpec((1, seq, D_MODEL), lambda b: (b, 0, 0))]
    out_shape = [jax.ShapeDtypeStruct((batch, seq, D_MODEL), F32)]
    if not latent:
        out_specs += [pl.BlockSpec((1, 1, seq, ATT_KV_HEADS, HEAD_DIM), lambda b: (b, att_idx, 0, 0, 0))] * 2
        out_shape += [jax.ShapeDtypeStruct((batch, n_att, seq, ATT_KV_HEADS, HEAD_DIM), F32)] * 2
    scratch = [
        pltpu.VMEM((seq, ATT_QD), BF16),
        pltpu.VMEM((n_keys, ATT_KD), BF16),
        pltpu.VMEM((n_keys, 2 * ATT_KD), BF16),
        pltpu.VMEM((seq, ATT_QD), F32),
        pltpu.VMEM((seq, ATT_QD), F32),
    ]
    outs = pl.pallas_call(
        functools.partial(_att_kernel, seq=seq, latent=latent, cache_alias=cache_alias, layer=layer, att_idx=att_idx),
        grid=(batch,),
        in_specs=in_specs,
        out_specs=out_specs,
        out_shape=out_shape,
        scratch_shapes=scratch,
        input_output_aliases=aliases,
        compiler_params=pltpu.CompilerParams(dimension_semantics=("arbitrary",),
                                             vmem_limit_bytes=V7X_VMEM_LIMIT_BYTES),
        name=f"att_layer_seq{seq}",
    )(*args)
    return outs


def _rope_tables(seq):
    half = HEAD_DIM // 2
    nf = half // 2
    pos = np.arange(seq)
    freqs = np.float32(ROPE_THETA) ** (-np.arange(nf, dtype=np.float32) / np.float32(nf))
    ang_row = (pos // GRID_W).astype(np.float32)[:, None] * freqs[None, :]
    ang_col = (pos % GRID_W).astype(np.float32)[:, None] * freqs[None, :]
    cos = np.concatenate([np.cos(ang_row)] * 2 + [np.cos(ang_col)] * 2, axis=-1)
    sin = np.concatenate([-np.sin(ang_row), np.sin(ang_row), -np.sin(ang_col), np.sin(ang_col)], axis=-1)
    return jnp.asarray(cos, F32), jnp.asarray(sin, F32)


def kernel(x_prompt, x_sample, state_gla, cache_k, cache_v, c, c_ctx, norm_g, w_ada, b_ada,
           gla_w_in, gla_wa1, gla_wa2, gla_ba, gla_onorm, gla_w_out,
           att_w_in, att_qnorm, att_knorm, att_w_out):
    n_dec = x_sample.shape[0]
    assert 1 + n_dec <= MOD_ROWS
    assert x_prompt.shape[1] % ROW_TILE == 0 and x_sample.shape[1] % ROW_TILE == 0

    cvec = jnp.concatenate([c_ctx[None], c, jnp.zeros((MOD_ROWS - 1 - n_dec, D_MODEL), F32)], axis=0)
    mods = _modulations(cvec, w_ada, b_ada)
    ctx_row = lambda b: 0
    dec_row = lambda b: b + 1

    cos, sin = _rope_tables(x_sample.shape[1])
    n_gla, n_att = gla_w_in.shape[0], att_w_in.shape[0]

    gla_win, gla_wout = gla_w_in.astype(BF16), gla_w_out.astype(BF16)
    att_win, att_wout = att_w_in.astype(BF16), att_w_out.astype(BF16)
    wa1 = jnp.concatenate([gla_wa1[:, 0], gla_wa1[:, 1],
                           jnp.zeros((n_gla, D_MODEL, RANK_PAD - 2 * GLA_RANK), F32)], axis=-1).astype(BF16)
    zeros_qd = jnp.zeros((n_gla, GLA_RANK, GLA_QD), F32)
    wa2 = jnp.concatenate([jnp.concatenate([gla_wa2[:, 0], zeros_qd], axis=-1),
                           jnp.concatenate([zeros_qd, gla_wa2[:, 1]], axis=-1),
                           jnp.zeros((n_gla, RANK_PAD - 2 * GLA_RANK, 2 * GLA_QD), F32)], axis=1).astype(BF16)

    xp, xs = x_prompt, x_sample
    states, caches = None, None
    for l in range(DEPTH):
        i = l // 2
        if l % 2 == 0:
            common = (norm_g, gla_win, wa1, wa2, gla_ba, gla_onorm, gla_wout)
            xp, states = _gla_layer(xp, mods, l, ctx_row, *common, None, i, n_gla, states)
            xs, _ = _gla_layer(xs, mods, l, dec_row, *common, state_gla, i, n_gla, None)
        else:
            xp, *caches = _att_layer(xp, mods, l, ctx_row, norm_g, att_win, att_qnorm, att_knorm, att_wout, None, i, n_att,
                                     caches)
            (xs,) = _att_layer(xs, mods, l, dec_row, norm_g, att_win, att_qnorm, att_knorm, att_wout,
                               (cos, sin, cache_k, cache_v), i, n_att, None)
    return (xp, xs, states, caches[0], caches[1])
```

```python
import functools
import math

import jax
import jax.numpy as jnp
import numpy as np
from jax import lax
from jax.experimental import pallas as pl
from jax.experimental.pallas import tpu as pltpu

D_MODEL = 1024
DEPTH = 4
GRID_W = 64
GLA_HEADS = 4
GLA_DK = 128
GLA_DV = 256
GLA_RANK = 16
GLA_TAU = 16.0
GLA_CHUNK = 64
GLA_QD = GLA_HEADS * GLA_DK
GLA_VD = GLA_HEADS * GLA_DV
HEAD_DIM = 128
ATT_HEADS = 8
ATT_KV_HEADS = 2
ATT_GROUP = ATT_HEADS // ATT_KV_HEADS
ATT_QD = ATT_HEADS * HEAD_DIM
ATT_KD = ATT_KV_HEADS * HEAD_DIM
ROPE_THETA = 10000.0
EPS = 1e-6

ROW_TILE = 256
GROUP_ROWS = 1024
MOD_ROWS = 8
RANK_PAD = 128
V7X_VMEM_LIMIT_BYTES = 60 * 1024 * 1024

F32 = jnp.float32
BF16 = jnp.bfloat16
_NT = (((1,), (1,)), ((), ()))


def _dot(a, b):
    return jnp.dot(a, b, preferred_element_type=F32)


def _dot_nt(a, b):
    return lax.dot_general(a, b, _NT, preferred_element_type=F32)


def _silu(x):
    return x * (1.0 / (1.0 + jnp.exp(-x)))


def _log_sigmoid(z):
    return jnp.minimum(z, 0.0) - jnp.log(1.0 + jnp.exp(-jnp.abs(z)))


def _split_top(x):
    top = pltpu.bitcast(pltpu.bitcast(x, jnp.uint32) & jnp.uint32(0xFFFF0000), F32)
    return top, x - top


def _modulated_norm(x, ng, mod_ref):
    shift = mod_ref[0, 0, :, 0:D_MODEL]
    scale = mod_ref[0, 0, :, D_MODEL:2 * D_MODEL]
    ms = jnp.mean(x * x, axis=-1, keepdims=True)
    return ((x * lax.rsqrt(ms + EPS)) * (ng * (1.0 + scale)) + shift).astype(BF16)


def _head_rms(x, g, width):
    outs = []
    for h in range(x.shape[-1] // width):
        xh = x[:, h * width:(h + 1) * width]
        ms = jnp.mean(xh * xh, axis=-1, keepdims=True)
        outs.append((xh * lax.rsqrt(ms + EPS)) * g)
    return jnp.concatenate(outs, axis=-1)


def _mod_kernel(c_ref, w_ref, b_ref, o_ref):
    acc = _dot(jnp.concatenate(_split_top(_silu(c_ref[...])), axis=0).astype(BF16), w_ref[0].astype(BF16))
    o_ref[0, :, 0, :] = acc[0:MOD_ROWS] + acc[MOD_ROWS:] + b_ref[pl.ds(pl.program_id(0), 1), :]


def _modulations(cvec, w_ada, b_ada):
    tn = D_MODEL
    n_tiles = 3 * D_MODEL // tn
    return pl.pallas_call(
        _mod_kernel,
        grid=(DEPTH, n_tiles),
        in_specs=[
            pl.BlockSpec((MOD_ROWS, D_MODEL), lambda l, j: (0, 0)),
            pl.BlockSpec((1, D_MODEL, tn), lambda l, j: (l, 0, j)),
            pl.BlockSpec((DEPTH, tn), lambda l, j: (0, j)),
        ],
        out_specs=pl.BlockSpec((1, MOD_ROWS, 1, tn), lambda l, j: (l, 0, 0, j)),
        out_shape=jax.ShapeDtypeStruct((DEPTH, MOD_ROWS, 1, 3 * D_MODEL), F32),
        compiler_params=pltpu.CompilerParams(dimension_semantics=("arbitrary", "arbitrary")),
        name="adaln_modulation",
    )(cvec, w_ada, b_ada)


def _chunk_sum_matrices():
    r = lax.broadcasted_iota(jnp.int32, (ROW_TILE, ROW_TILE), 0)
    c = lax.broadcasted_iota(jnp.int32, (ROW_TILE, ROW_TILE), 1)
    same = (r // GLA_CHUNK) == (c // GLA_CHUNK)
    prefix = jnp.where(same & (c <= r), 1.0, 0.0).astype(BF16)
    suffix = jnp.where(same & (c >= r), 1.0, 0.0).astype(BF16)
    return prefix, suffix


def _gla_kernel(*refs, seq, has_s0, emit_state, state_alias, layer, gla_idx):
    n_seq = GROUP_ROWS // seq
    n_tiles = GROUP_ROWS // ROW_TILE
    n_pos = seq // GLA_CHUNK
    it = iter(refs)
    x_ref, mod_ref, ng_ref, win_ref, wa1_ref, wa2_ref, ba_ref, on_ref, wout_ref = (next(it) for _ in range(9))
    s0_ref = next(it) if has_s0 else None
    if state_alias:
        next(it)
    y_ref = next(it)
    sout_ref = next(it) if emit_state else None
    q_s, k_s, b_s, v_s, gate_s, o_s = (next(it) for _ in range(6))
    st_s = None if emit_state else next(it)

    def state_at(s, d, h):
        return sout_ref.at[s, 0, d, h] if emit_state else st_s.at[d, h]

    ng = ng_ref[layer:layer + 1, :]
    ba = jnp.concatenate([ba_ref[gla_idx, 0:1, :], ba_ref[gla_idx, 1:2, :]], axis=1)
    prefix_m, suffix_m = _chunk_sum_matrices()

    def project_wide(t):
        rows = slice(t * ROW_TILE, (t + 1) * ROW_TILE)
        hb = _modulated_norm(x_ref[0, rows, :], ng, mod_ref)
        q_s[rows, :] = _dot(hb, win_ref[0, :, 0:GLA_QD]) * (GLA_DK ** -0.5)
        k_s[rows, :] = _dot(hb, win_ref[0, :, GLA_QD:2 * GLA_QD])
        v_s[rows, :] = _dot(hb, win_ref[0, :, 2 * GLA_QD:2 * GLA_QD + GLA_VD]).astype(BF16)
        gate_s[rows, :] = _dot(hb, win_ref[0, :, 2 * GLA_QD + GLA_VD:])
        low = _dot(hb, wa1_ref[0]).astype(BF16)
        return _dot(low, wa2_ref[0]) + ba

    def decay_sums(t, z):
        rows = slice(t * ROW_TILE, (t + 1) * ROW_TILE)
        logg2 = _log_sigmoid(z) * (math.log2(math.e) / GLA_TAU)
        for d, csum_m in enumerate((prefix_m, suffix_m)):
            parts = jnp.concatenate(_split_top(logg2[:, d * GLA_QD:(d + 1) * GLA_QD]), axis=0).astype(BF16)
            b_s[d, rows, :] = _dot(jnp.concatenate([csum_m, csum_m], axis=1), parts)

    z = project_wide(0)
    for t in range(n_tiles):
        z_next = project_wide(t + 1) if t + 1 < n_tiles else None
        decay_sums(t, z)
        z = z_next

    for s in range(n_seq):
        for d in range(2):
            for h in range(GLA_HEADS):
                state_at(s, d, h)[...] = s0_ref[0, 0, d, h] if has_s0 else jnp.zeros((GLA_DK, GLA_DV), F32)

    ri = lax.broadcasted_iota(jnp.int32, (GLA_CHUNK, GLA_CHUNK), 0)
    ci = lax.broadcasted_iota(jnp.int32, (GLA_CHUNK, GLA_CHUNK), 1)
    masks = (ci <= ri, ci >= ri)

    def scan_step(i, first_visit):
        chains = []
        for s in range(n_seq):
            for d in range(2):
                c = s * n_pos + (i if d == 0 else n_pos - 1 - i)
                rows = pl.ds(pl.multiple_of(c * GLA_CHUNK, GLA_CHUNK), GLA_CHUNK)
                b = b_s[d, rows, :]
                edge = GLA_CHUNK - 1 if d == 0 else 0
                total = b[edge:edge + 1, :]
                k = k_s[rows, :]
                qe = (q_s[rows, :] * jnp.exp2(b)).astype(BF16)
                ke = (k * jnp.exp2(-b)).astype(BF16)
                kdt = (k * jnp.exp2(total - b)).T.astype(BF16)
                dec = jnp.exp2(total)
                for h in range(GLA_HEADS):
                    kc = slice(h * GLA_DK, (h + 1) * GLA_DK)
                    chains.append((s, d, h, rows, slice(h * GLA_DV, (h + 1) * GLA_DV),
                                   qe[:, kc], ke[:, kc], kdt[kc, :], dec[:, kc]))
        scores = [_dot_nt(qe, ke) for (s, d, h, rows, vc, qe, ke, kdt, dec) in chains]
        no_state = jnp.zeros((GLA_DK, GLA_DK), BF16)
        for (s, d, h, rows, vc, qe, ke, kdt, dec), sc in zip(chains, scores):
            a = jnp.where(masks[d], sc, 0.0).astype(BF16)
            st_ref = state_at(s, d, h)
            st = st_ref[...]
            both = _dot(jnp.concatenate([jnp.concatenate([qe, a], axis=1),
                                         jnp.concatenate([no_state, kdt], axis=1)], axis=0),
                        jnp.concatenate([st.astype(BF16), v_s[rows, vc]], axis=0))
            if first_visit:
                o_s[rows, vc] = both[0:GLA_CHUNK]
            else:
                o_s[rows, vc] += both[0:GLA_CHUNK]
            dec_col = jnp.broadcast_to(dec, (GLA_DK, GLA_DK)).T
            st_ref[...] = jnp.concatenate([dec_col] * (GLA_DV // GLA_DK), axis=1) * st + both[GLA_CHUNK:]

    def first_half(i, carry):
        scan_step(i, True)
        return carry

    def second_half(i, carry):
        scan_step(i, False)
        return carry

    lax.fori_loop(0, n_pos // 2, first_half, 0, unroll=2)
    lax.fori_loop(n_pos // 2, n_pos, second_half, 0, unroll=2)

    on = on_ref[gla_idx:gla_idx + 1, :]
    res_gate = mod_ref[0, 0, :, 2 * D_MODEL:]

    def gated(t):
        rows = slice(t * ROW_TILE, (t + 1) * ROW_TILE)
        return (_head_rms(o_s[rows, :], on, GLA_DV) * _silu(gate_s[rows, :])).astype(BF16)

    og = gated(0)
    for t in range(n_tiles):
        rows = slice(t * ROW_TILE, (t + 1) * ROW_TILE)
        og_next = gated(t + 1) if t + 1 < n_tiles else None
        y_ref[0, rows, :] = x_ref[0, rows, :] + res_gate * _dot(og, wout_ref[0])
        og = og_next


def _resident(shape, layer=None):
    if layer is None:
        return pl.BlockSpec(shape, lambda b: (0,) * len(shape), pipeline_mode=pl.Buffered(1))
    return pl.BlockSpec((1,) + shape, lambda b: (layer,) + (0,) * len(shape), pipeline_mode=pl.Buffered(1))


def _gla_layer(x, mods, layer, mod_row, ng, win, wa1, wa2, ba, on, wout, state_in, gla_idx, n_gla, states_so_far):
    batch, seq, _ = x.shape
    n_seq = GROUP_ROWS // seq
    n_groups = batch // n_seq
    has_s0 = state_in is not None
    emit_state = not has_s0
    assert not has_s0 or n_seq == 1
    in_specs = [
        pl.BlockSpec((1, GROUP_ROWS, D_MODEL), lambda b: (b, 0, 0)),
        pl.BlockSpec((1, 1, 1, 3 * D_MODEL), lambda b: (layer, mod_row(b), 0, 0)),
        _resident(ng.shape),
        _resident((D_MODEL, 2 * GLA_QD + 2 * GLA_VD), gla_idx),
        _resident((D_MODEL, RANK_PAD), gla_idx),
        _resident((RANK_PAD, 2 * GLA_QD), gla_idx),
        _resident(ba.shape),
        _resident(on.shape),
        _resident((GLA_VD, D_MODEL), gla_idx),
    ]
    args = [x.reshape(n_groups, GROUP_ROWS, D_MODEL), mods, ng, win, wa1, wa2, ba, on, wout]
    if has_s0:
        in_specs.append(pl.BlockSpec((1, 1, 2, GLA_HEADS, GLA_DK, GLA_DV),
                                     lambda b: (b, gla_idx, 0, 0, 0, 0)))
        args.append(state_in)
    aliases = {}
    state_alias = emit_state and states_so_far is not None
    if state_alias:
        aliases[len(args)] = 1
        in_specs.append(pl.BlockSpec(memory_space=pl.ANY))
        args.append(states_so_far)
    out_specs = [pl.BlockSpec((1, GROUP_ROWS, D_MODEL), lambda b: (b, 0, 0))]
    out_shape = [jax.ShapeDtypeStruct((n_groups, GROUP_ROWS, D_MODEL), F32)]
    if emit_state:
        out_specs.append(pl.BlockSpec((n_seq, 1, 2, GLA_HEADS, GLA_DK, GLA_DV), lambda b: (b, gla_idx, 0, 0, 0, 0)))
        out_shape.append(jax.ShapeDtypeStruct((batch, n_gla, 2, GLA_HEADS, GLA_DK, GLA_DV), F32))
    scratch = [
        pltpu.VMEM((GROUP_ROWS, GLA_QD), F32),
        pltpu.VMEM((GROUP_ROWS, GLA_QD), F32),
        pltpu.VMEM((2, GROUP_ROWS, GLA_QD), F32),
        pltpu.VMEM((GROUP_ROWS, GLA_VD), BF16),
        pltpu.VMEM((GROUP_ROWS, GLA_VD), F32),
        pltpu.VMEM((GROUP_ROWS, GLA_VD), F32),
    ]
    if not emit_state:
        scratch.append(pltpu.VMEM((2, GLA_HEADS, GLA_DK, GLA_DV), F32))
    outs = pl.pallas_call(
        functools.partial(_gla_kernel, seq=seq, has_s0=has_s0, emit_state=emit_state, state_alias=state_alias,
                          layer=layer, gla_idx=gla_idx),
        grid=(n_groups,),
        in_specs=in_specs,
        out_specs=out_specs,
        out_shape=out_shape,
        scratch_shapes=scratch,
        input_output_aliases=aliases,
        compiler_params=pltpu.CompilerParams(dimension_semantics=("arbitrary",),
                                             vmem_limit_bytes=V7X_VMEM_LIMIT_BYTES),
        name=f"gla_layer_seq{seq}",
    )(*args)
    y = outs[0].reshape(batch, seq, D_MODEL)
    return (y, outs[1]) if emit_state else (y, None)


def _rope_swap(x):
    lane = lax.broadcasted_iota(jnp.int32, x.shape, 1)
    quarter = HEAD_DIM // 4
    first = (lane % (2 * quarter)) < quarter
    return jnp.where(first, pltpu.roll(x, HEAD_DIM - quarter, 1), pltpu.roll(x, quarter, 1))


def _att_kernel(*refs, seq, latent, cache_alias, layer, att_idx):
    tile = min(seq, 2 * ROW_TILE)
    n_tiles = seq // tile
    n_keys = seq + (refs[9].shape[2] if latent else 0)
    it = iter(refs)
    x_ref, mod_ref, ng_ref, win_ref, qn_ref, kn_ref, wout_ref = (next(it) for _ in range(7))
    if latent:
        cos_ref, sin_ref, ck_ref, cv_ref = (next(it) for _ in range(4))
    if cache_alias:
        next(it), next(it)
    y_ref = next(it)
    if not latent:
        kout_ref, vout_ref = next(it), next(it)
    q_s, k_s, v_s, gate_s, ao_s = (next(it) for _ in range(5))

    ng = ng_ref[layer:layer + 1, :]
    qn = qn_ref[att_idx:att_idx + 1, :]
    kn = kn_ref[att_idx:att_idx + 1, :]
    exp2_scale = (HEAD_DIM ** -0.5) * math.log2(math.e)

    def project(t, carry):
        rows = pl.ds(pl.multiple_of(t * tile, tile), tile)
        hb = _modulated_norm(x_ref[0, rows, :], ng, mod_ref)
        q = _head_rms(_dot(hb, win_ref[0, :, 0:ATT_QD]), qn, HEAD_DIM)
        k = _head_rms(_dot(hb, win_ref[0, :, ATT_QD:ATT_QD + ATT_KD]), kn, HEAD_DIM)
        v = _dot(hb, win_ref[0, :, ATT_QD + ATT_KD:ATT_QD + 2 * ATT_KD])
        gate_s[rows, :] = _dot(hb, win_ref[0, :, ATT_QD + 2 * ATT_KD:])
        if latent:
            cos = cos_ref[rows, :]
            sin = sin_ref[rows, :]
            q = jnp.concatenate(
                [q[:, h * HEAD_DIM:(h + 1) * HEAD_DIM] * cos + _rope_swap(q[:, h * HEAD_DIM:(h + 1) * HEAD_DIM]) * sin
                 for h in range(ATT_HEADS)], axis=-1)
            k = jnp.concatenate(
                [k[:, h * HEAD_DIM:(h + 1) * HEAD_DIM] * cos + _rope_swap(k[:, h * HEAD_DIM:(h + 1) * HEAD_DIM]) * sin
                 for h in range(ATT_KV_HEADS)], axis=-1)
        else:
            for h in range(ATT_KV_HEADS):
                kout_ref[0, 0, rows, h, :] = k[:, h * HEAD_DIM:(h + 1) * HEAD_DIM]
                vout_ref[0, 0, rows, h, :] = v[:, h * HEAD_DIM:(h + 1) * HEAD_DIM]
        q_s[rows, :] = (q * exp2_scale).astype(BF16)
        k_s[rows, :] = k.astype(BF16)
        for h in range(ATT_KV_HEADS):
            v_s[rows, 2 * h * HEAD_DIM:(2 * h + 1) * HEAD_DIM] = v[:, h * HEAD_DIM:(h + 1) * HEAD_DIM].astype(BF16)
        return carry

    for h in range(ATT_KV_HEADS):
        v_s[:, (2 * h + 1) * HEAD_DIM:(2 * h + 2) * HEAD_DIM] = jnp.ones((n_keys, HEAD_DIM), BF16)
    lax.fori_loop(0, n_tiles, project, 0, unroll=True)
    if latent:
        for h in range(ATT_KV_HEADS):
            k_s[seq:n_keys, h * HEAD_DIM:(h + 1) * HEAD_DIM] = ck_ref[0, 0, :, h, :].astype(BF16)
            v_s[seq:n_keys, 2 * h * HEAD_DIM:(2 * h + 1) * HEAD_DIM] = cv_ref[0, 0, :, h, :].astype(BF16)

    q_tile = min(seq, 4 * ROW_TILE)

    def attend(t, carry):
        rows = pl.ds(pl.multiple_of(t * q_tile, q_tile), q_tile)

        def scores(h):
            kc = slice((h // ATT_GROUP) * HEAD_DIM, (h // ATT_GROUP + 1) * HEAD_DIM)
            return _dot_nt(q_s[rows, h * HEAD_DIM:(h + 1) * HEAD_DIM], k_s[:, kc])

        s = scores(0)
        for h in range(ATT_HEADS):
            s_next = scores(h + 1) if h + 1 < ATT_HEADS else None
            vc = slice((h // ATT_GROUP) * 2 * HEAD_DIM, (h // ATT_GROUP + 1) * 2 * HEAD_DIM)
            p = jnp.exp2(s - jnp.max(s, axis=-1, keepdims=True))
            o = _dot(p.astype(BF16), v_s[:, vc])
            ao_s[rows, h * HEAD_DIM:(h + 1) * HEAD_DIM] = o[:, 0:HEAD_DIM] / o[:, HEAD_DIM:]
            s = s_next
        return carry

    lax.fori_loop(0, seq // q_tile, attend, 0)

    res_gate = mod_ref[0, 0, :, 2 * D_MODEL:]

    def finish(t, carry):
        rows = pl.ds(pl.multiple_of(t * tile, tile), tile)
        y = _dot((ao_s[rows, :] * _silu(gate_s[rows, :])).astype(BF16), wout_ref[0])
        y_ref[0, rows, :] = x_ref[0, rows, :] + res_gate * y
        return carry

    lax.fori_loop(0, n_tiles, finish, 0, unroll=True)


def _att_layer(x, mods, layer, mod_row, ng, win, qn, kn, wout, latent_inputs, att_idx, n_att, caches_so_far):
    batch, seq, _ = x.shape
    latent = latent_inputs is not None
    in_specs = [
        pl.BlockSpec((1, seq, D_MODEL), lambda b: (b, 0, 0)),
        pl.BlockSpec((1, 1, 1, 3 * D_MODEL), lambda b: (layer, mod_row(b), 0, 0)),
        _resident(ng.shape),
        _resident((D_MODEL, 2 * ATT_QD + 2 * ATT_KD), att_idx),
        _resident(qn.shape),
        _resident(kn.shape),
        _resident((ATT_QD, D_MODEL), att_idx),
    ]
    args = [x, mods, ng, win, qn, kn, wout]
    n_keys = seq
    if latent:
        cos, sin, cache_k, cache_v = latent_inputs
        past = cache_k.shape[2]
        n_keys = seq + past
        in_specs += [
            _resident((seq, HEAD_DIM)),
            _resident((seq, HEAD_DIM)),
            pl.BlockSpec((1, 1, past, ATT_KV_HEADS, HEAD_DIM), lambda b: (b, att_idx, 0, 0, 0)),
            pl.BlockSpec((1, 1, past, ATT_KV_HEADS, HEAD_DIM), lambda b: (b, att_idx, 0, 0, 0)),
        ]
        args += [cos, sin, cache_k, cache_v]
    aliases = {}
    cache_alias = (not latent) and caches_so_far is not None
    if cache_alias:
        aliases = {len(args): 1, len(args) + 1: 2}
        in_specs += [pl.BlockSpec(memory_space=pl.ANY)] * 2
        args += list(caches_so_far)
    out_specs = [pl.BlockSpec((1, seq, D_MODEL), lambda b: (b, 0, 0))]
    out_shape = [jax.ShapeDtypeStruct((batch, seq, D_MODEL), F32)]
    if not latent:
        out_specs += [pl.BlockSpec((1, 1, seq, ATT_KV_HEADS, HEAD_DIM), lambda b: (b, att_idx, 0, 0, 0))] * 2
        out_shape += [jax.ShapeDtypeStruct((batch, n_att, seq, ATT_KV_HEADS, HEAD_DIM), F32)] * 2
    scratch = [
        pltpu.VMEM((seq, ATT_QD), BF16),
        pltpu.VMEM((n_keys, ATT_KD), BF16),
        pltpu.VMEM((n_keys, 2 * ATT_KD), BF16),
        pltpu.VMEM((seq, ATT_QD), F32),
        pltpu.VMEM((seq, ATT_QD), F32),
    ]
    outs = pl.pallas_call(
        functools.partial(_att_kernel, seq=seq, latent=latent, cache_alias=cache_alias, layer=layer, att_idx=att_idx),
        grid=(batch,),
        in_specs=in_specs,
        out_specs=out_specs,
        out_shape=out_shape,
        scratch_shapes=scratch,
        input_output_aliases=aliases,
        compiler_params=pltpu.CompilerParams(dimension_semantics=("arbitrary",),
                                             vmem_limit_bytes=V7X_VMEM_LIMIT_BYTES),
        name=f"att_layer_seq{seq}",
    )(*args)
    return outs


def _rope_tables(seq):
    half = HEAD_DIM // 2
    nf = half // 2
    pos = np.arange(seq)
    freqs = np.float32(ROPE_THETA) ** (-np.arange(nf, dtype=np.float32) / np.float32(nf))
    ang_row = (pos // GRID_W).astype(np.float32)[:, None] * freqs[None, :]
    ang_col = (pos % GRID_W).astype(np.float32)[:, None] * freqs[None, :]
    cos = np.concatenate([np.cos(ang_row)] * 2 + [np.cos(ang_col)] * 2, axis=-1)
    sin = np.concatenate([-np.sin(ang_row), np.sin(ang_row), -np.sin(ang_col), np.sin(ang_col)], axis=-1)
    return jnp.asarray(cos, F32), jnp.asarray(sin, F32)


def kernel(x_prompt, x_sample, state_gla, cache_k, cache_v, c, c_ctx, norm_g, w_ada, b_ada,
           gla_w_in, gla_wa1, gla_wa2, gla_ba, gla_onorm, gla_w_out,
           att_w_in, att_qnorm, att_knorm, att_w_out):
    n_dec = x_sample.shape[0]
    assert 1 + n_dec <= MOD_ROWS
    assert x_prompt.shape[1] % ROW_TILE == 0 and x_sample.shape[1] % ROW_TILE == 0

    cvec = jnp.concatenate([c_ctx[None], c, jnp.zeros((MOD_ROWS - 1 - n_dec, D_MODEL), F32)], axis=0)
    mods = _modulations(cvec, w_ada, b_ada)
    ctx_row = lambda b: 0
    dec_row = lambda b: b + 1

    cos, sin = _rope_tables(x_sample.shape[1])
    n_gla, n_att = gla_w_in.shape[0], att_w_in.shape[0]

    gla_win, gla_wout = gla_w_in.astype(BF16), gla_w_out.astype(BF16)
    att_win, att_wout = att_w_in.astype(BF16), att_w_out.astype(BF16)
    wa1 = jnp.concatenate([gla_wa1[:, 0], gla_wa1[:, 1],
                           jnp.zeros((n_gla, D_MODEL, RANK_PAD - 2 * GLA_RANK), F32)], axis=-1).astype(BF16)
    zeros_qd = jnp.zeros((n_gla, GLA_RANK, GLA_QD), F32)
    wa2 = jnp.concatenate([jnp.concatenate([gla_wa2[:, 0], zeros_qd], axis=-1),
                           jnp.concatenate([zeros_qd, gla_wa2[:, 1]], axis=-1),
                           jnp.zeros((n_gla, RANK_PAD - 2 * GLA_RANK, 2 * GLA_QD), F32)], axis=1).astype(BF16)

    xp, xs = x_prompt, x_sample
    states, caches = None, None
    for l in range(DEPTH):
        i = l // 2
        if l % 2 == 0:
            common = (norm_g, gla_win, wa1, wa2, gla_ba, gla_onorm, gla_wout)
            xp, states = _gla_layer(xp, mods, l, ctx_row, *common, None, i, n_gla, states)
            xs, _ = _gla_layer(xs, mods, l, dec_row, *common, state_gla, i, n_gla, None)
        else:
            xp, *caches = _att_layer(xp, mods, l, ctx_row, norm_g, att_win, att_qnorm, att_knorm, att_wout, None, i, n_att,
                                     caches)
            (xs,) = _att_layer(xs, mods, l, dec_row, norm_g, att_win, att_qnorm, att_knorm, att_wout,
                               (cos, sin, cache_k, cache_v), i, n_att, None)
    return (xp, xs, states, caches[0], caches[1])
```

```python
import functools
import math

import jax
import jax.numpy as jnp
import numpy as np
from jax import lax
from jax.experimental import pallas as pl
from jax.experimental.pallas import tpu as pltpu

D_MODEL = 1024
DEPTH = 4
GRID_W = 64
GLA_HEADS = 4
GLA_DK = 128
GLA_DV = 256
GLA_RANK = 16
GLA_TAU = 16.0
GLA_CHUNK = 64
GLA_QD = GLA_HEADS * GLA_DK
GLA_VD = GLA_HEADS * GLA_DV
HEAD_DIM = 128
ATT_HEADS = 8
ATT_KV_HEADS = 2
ATT_GROUP = ATT_HEADS // ATT_KV_HEADS
ATT_QD = ATT_HEADS * HEAD_DIM
ATT_KD = ATT_KV_HEADS * HEAD_DIM
ROPE_THETA = 10000.0
EPS = 1e-6

ROW_TILE = 256
GROUP_ROWS = 1024
MOD_ROWS = 8
RANK_PAD = 128
V7X_VMEM_LIMIT_BYTES = 60 * 1024 * 1024

F32 = jnp.float32
BF16 = jnp.bfloat16
_NT = (((1,), (1,)), ((), ()))


def _dot(a, b):
    return jnp.dot(a, b, preferred_element_type=F32)


def _dot_nt(a, b):
    return lax.dot_general(a, b, _NT, preferred_element_type=F32)


def _silu(x):
    return x * (1.0 / (1.0 + jnp.exp(-x)))


def _log_sigmoid(z):
    return jnp.minimum(z, 0.0) - jnp.log(1.0 + jnp.exp(-jnp.abs(z)))


def _split_top(x):
    top = pltpu.bitcast(pltpu.bitcast(x, jnp.uint32) & jnp.uint32(0xFFFF0000), F32)
    return top, x - top


def _modulated_norm(x, ng, mod_ref):
    shift = mod_ref[0, 0, :, 0:D_MODEL]
    scale = mod_ref[0, 0, :, D_MODEL:2 * D_MODEL]
    ms = jnp.mean(x * x, axis=-1, keepdims=True)
    return ((x * lax.rsqrt(ms + EPS)) * (ng * (1.0 + scale)) + shift).astype(BF16)


def _head_rms(x, g, width):
    outs = []
    for h in range(x.shape[-1] // width):
        xh = x[:, h * width:(h + 1) * width]
        ms = jnp.mean(xh * xh, axis=-1, keepdims=True)
        outs.append((xh * lax.rsqrt(ms + EPS)) * g)
    return jnp.concatenate(outs, axis=-1)


def _mod_kernel(c_ref, w_ref, b_ref, o_ref):
    acc = _dot(jnp.concatenate(_split_top(_silu(c_ref[...])), axis=0).astype(BF16), w_ref[0].astype(BF16))
    o_ref[0, :, 0, :] = acc[0:MOD_ROWS] + acc[MOD_ROWS:] + b_ref[pl.ds(pl.program_id(0), 1), :]


def _modulations(cvec, w_ada, b_ada):
    tn = D_MODEL
    n_tiles = 3 * D_MODEL // tn
    return pl.pallas_call(
        _mod_kernel,
        grid=(DEPTH, n_tiles),
        in_specs=[
            pl.BlockSpec((MOD_ROWS, D_MODEL), lambda l, j: (0, 0)),
            pl.BlockSpec((1, D_MODEL, tn), lambda l, j: (l, 0, j)),
            pl.BlockSpec((DEPTH, tn), lambda l, j: (0, j)),
        ],
        out_specs=pl.BlockSpec((1, MOD_ROWS, 1, tn), lambda l, j: (l, 0, 0, j)),
        out_shape=jax.ShapeDtypeStruct((DEPTH, MOD_ROWS, 1, 3 * D_MODEL), F32),
        compiler_params=pltpu.CompilerParams(dimension_semantics=("arbitrary", "arbitrary")),
        name="adaln_modulation",
    )(cvec, w_ada, b_ada)


def _chunk_sum_matrices():
    r = lax.broadcasted_iota(jnp.int32, (ROW_TILE, ROW_TILE), 0)
    c = lax.broadcasted_iota(jnp.int32, (ROW_TILE, ROW_TILE), 1)
    same = (r // GLA_CHUNK) == (c // GLA_CHUNK)
    prefix = jnp.where(same & (c <= r), 1.0, 0.0).astype(BF16)
    suffix = jnp.where(same & (c >= r), 1.0, 0.0).astype(BF16)
    return prefix, suffix


def _gla_kernel(*refs, seq, has_s0, emit_state, state_alias, layer, gla_idx):
    n_seq = GROUP_ROWS // seq
    n_tiles = GROUP_ROWS // ROW_TILE
    n_pos = seq // GLA_CHUNK
    it = iter(refs)
    x_ref, mod_ref, ng_ref, win_ref, wa1_ref, wa2_ref, ba_ref, on_ref, wout_ref = (next(it) for _ in range(9))
    s0_ref = next(it) if has_s0 else None
    if state_alias:
        next(it)
    y_ref = next(it)
    sout_ref = next(it) if emit_state else None
    q_s, k_s, b_s, v_s, gate_s, o_s = (next(it) for _ in range(6))
    st_s = None if emit_state else next(it)

    def state_at(s, d, h):
        return sout_ref.at[s, 0, d, h] if emit_state else st_s.at[d, h]

    ng = ng_ref[layer:layer + 1, :]
    ba = jnp.concatenate([ba_ref[gla_idx, 0:1, :], ba_ref[gla_idx, 1:2, :]], axis=1)
    prefix_m, suffix_m = _chunk_sum_matrices()

    def project_wide(t):
        rows = slice(t * ROW_TILE, (t + 1) * ROW_TILE)
        hb = _modulated_norm(x_ref[0, rows, :], ng, mod_ref)
        q_s[rows, :] = _dot(hb, win_ref[0, :, 0:GLA_QD]) * (GLA_DK ** -0.5)
        k_s[rows, :] = _dot(hb, win_ref[0, :, GLA_QD:2 * GLA_QD])
        v_s[rows, :] = _dot(hb, win_ref[0, :, 2 * GLA_QD:2 * GLA_QD + GLA_VD]).astype(BF16)
        gate_s[rows, :] = _dot(hb, win_ref[0, :, 2 * GLA_QD + GLA_VD:])
        low = _dot(hb, wa1_ref[0]).astype(BF16)
        return _dot(low, wa2_ref[0]) + ba

    def decay_sums(t, z):
        rows = slice(t * ROW_TILE, (t + 1) * ROW_TILE)
        logg2 = _log_sigmoid(z) * (math.log2(math.e) / GLA_TAU)
        for d, csum_m in enumerate((prefix_m, suffix_m)):
            parts = jnp.concatenate(_split_top(logg2[:, d * GLA_QD:(d + 1) * GLA_QD]), axis=0).astype(BF16)
            b_s[d, rows, :] = _dot(jnp.concatenate([csum_m, csum_m], axis=1), parts)

    z = project_wide(0)
    for t in range(n_tiles):
        z_next = project_wide(t + 1) if t + 1 < n_tiles else None
        decay_sums(t, z)
        z = z_next

    for s in range(n_seq):
        for d in range(2):
            for h in range(GLA_HEADS):
                state_at(s, d, h)[...] = s0_ref[0, 0, d, h] if has_s0 else jnp.zeros((GLA_DK, GLA_DV), F32)

    ri = lax.broadcasted_iota(jnp.int32, (GLA_CHUNK, GLA_CHUNK), 0)
    ci = lax.broadcasted_iota(jnp.int32, (GLA_CHUNK, GLA_CHUNK), 1)
    masks = (ci <= ri, ci >= ri)

    def scan_step(i, first_visit):
        chains = []
        for s in range(n_seq):
            for d in range(2):
                c = s * n_pos + (i if d == 0 else n_pos - 1 - i)
                rows = pl.ds(pl.multiple_of(c * GLA_CHUNK, GLA_CHUNK), GLA_CHUNK)
                b = b_s[d, rows, :]
                edge = GLA_CHUNK - 1 if d == 0 else 0
                total = b[edge:edge + 1, :]
                k = k_s[rows, :]
                qe = (q_s[rows, :] * jnp.exp2(b)).astype(BF16)
                ke = (k * jnp.exp2(-b)).astype(BF16)
                kdt = (k * jnp.exp2(total - b)).T.astype(BF16)
                dec = jnp.exp2(total)
                for h in range(GLA_HEADS):
                    kc = slice(h * GLA_DK, (h + 1) * GLA_DK)
                    chains.append((s, d, h, rows, slice(h * GLA_DV, (h + 1) * GLA_DV),
                                   qe[:, kc], ke[:, kc], kdt[kc, :], dec[:, kc]))
        scores = [_dot_nt(qe, ke) for (s, d, h, rows, vc, qe, ke, kdt, dec) in chains]
        no_state = jnp.zeros((GLA_DK, GLA_DK), BF16)
        for (s, d, h, rows, vc, qe, ke, kdt, dec), sc in zip(chains, scores):
            a = jnp.where(masks[d], sc, 0.0).astype(BF16)
            st_ref = state_at(s, d, h)
            st = st_ref[...]
            both = _dot(jnp.concatenate([jnp.concatenate([qe, a], axis=1),
                                         jnp.concatenate([no_state, kdt], axis=1)], axis=0),
                        jnp.concatenate([st.astype(BF16), v_s[rows, vc]], axis=0))
            if first_visit:
                o_s[rows, vc] = both[0:GLA_CHUNK]
            else:
                o_s[rows, vc] += both[0:GLA_CHUNK]
            dec_col = jnp.broadcast_to(dec, (GLA_DK, GLA_DK)).T
            st_ref[...] = jnp.concatenate([dec_col] * (GLA_DV // GLA_DK), axis=1) * st + both[GLA_CHUNK:]

    def first_half(i, carry):
        scan_step(i, True)
        return carry

    def second_half(i, carry):
        scan_step(i, False)
        return carry

    lax.fori_loop(0, n_pos // 2, first_half, 0, unroll=min(4, n_pos // 2))
    lax.fori_loop(n_pos // 2, n_pos, second_half, 0, unroll=min(4, n_pos // 2))

    on = on_ref[gla_idx:gla_idx + 1, :]
    res_gate = mod_ref[0, 0, :, 2 * D_MODEL:]

    def gated(t):
        rows = slice(t * ROW_TILE, (t + 1) * ROW_TILE)
        return (_head_rms(o_s[rows, :], on, GLA_DV) * _silu(gate_s[rows, :])).astype(BF16)

    og = gated(0)
    for t in range(n_tiles):
        rows = slice(t * ROW_TILE, (t + 1) * ROW_TILE)
        og_next = gated(t + 1) if t + 1 < n_tiles else None
        y_ref[0, rows, :] = x_ref[0, rows, :] + res_gate * _dot(og, wout_ref[0])
        og = og_next


def _resident(shape, layer=None):
    if layer is None:
        return pl.BlockSpec(shape, lambda b: (0,) * len(shape), pipeline_mode=pl.Buffered(1))
    return pl.BlockSpec((1,) + shape, lambda b: (layer,) + (0,) * len(shape), pipeline_mode=pl.Buffered(1))


def _gla_layer(x, mods, layer, mod_row, ng, win, wa1, wa2, ba, on, wout, state_in, gla_idx, n_gla, states_so_far):
    batch, seq, _ = x.shape
    n_seq = GROUP_ROWS // seq
    n_groups = batch // n_seq
    has_s0 = state_in is not None
    emit_state = not has_s0
    assert not has_s0 or n_seq == 1
    in_specs = [
        pl.BlockSpec((1, GROUP_ROWS, D_MODEL), lambda b: (b, 0, 0)),
        pl.BlockSpec((1, 1, 1, 3 * D_MODEL), lambda b: (layer, mod_row(b), 0, 0)),
        _resident(ng.shape),
        _resident((D_MODEL, 2 * GLA_QD + 2 * GLA_VD), gla_idx),
        _resident((D_MODEL, RANK_PAD), gla_idx),
        _resident((RANK_PAD, 2 * GLA_QD), gla_idx),
        _resident(ba.shape),
        _resident(on.shape),
        _resident((GLA_VD, D_MODEL), gla_idx),
    ]
    args = [x.reshape(n_groups, GROUP_ROWS, D_MODEL), mods, ng, win, wa1, wa2, ba, on, wout]
    if has_s0:
        in_specs.append(pl.BlockSpec((1, 1, 2, GLA_HEADS, GLA_DK, GLA_DV),
                                     lambda b: (b, gla_idx, 0, 0, 0, 0)))
        args.append(state_in)
    aliases = {}
    state_alias = emit_state and states_so_far is not None
    if state_alias:
        aliases[len(args)] = 1
        in_specs.append(pl.BlockSpec(memory_space=pl.ANY))
        args.append(states_so_far)
    out_specs = [pl.BlockSpec((1, GROUP_ROWS, D_MODEL), lambda b: (b, 0, 0))]
    out_shape = [jax.ShapeDtypeStruct((n_groups, GROUP_ROWS, D_MODEL), F32)]
    if emit_state:
        out_specs.append(pl.BlockSpec((n_seq, 1, 2, GLA_HEADS, GLA_DK, GLA_DV), lambda b: (b, gla_idx, 0, 0, 0, 0)))
        out_shape.append(jax.ShapeDtypeStruct((batch, n_gla, 2, GLA_HEADS, GLA_DK, GLA_DV), F32))
    scratch = [
        pltpu.VMEM((GROUP_ROWS, GLA_QD), F32),
        pltpu.VMEM((GROUP_ROWS, GLA_QD), F32),
        pltpu.VMEM((2, GROUP_ROWS, GLA_QD), F32),
        pltpu.VMEM((GROUP_ROWS, GLA_VD), BF16),
        pltpu.VMEM((GROUP_ROWS, GLA_VD), F32),
        pltpu.VMEM((GROUP_ROWS, GLA_VD), F32),
    ]
    if not emit_state:
        scratch.append(pltpu.VMEM((2, GLA_HEADS, GLA_DK, GLA_DV), F32))
    outs = pl.pallas_call(
        functools.partial(_gla_kernel, seq=seq, has_s0=has_s0, emit_state=emit_state, state_alias=state_alias,
                          layer=layer, gla_idx=gla_idx),
        grid=(n_groups,),
        in_specs=in_specs,
        out_specs=out_specs,
        out_shape=out_shape,
        scratch_shapes=scratch,
        input_output_aliases=aliases,
        compiler_params=pltpu.CompilerParams(dimension_semantics=("arbitrary",),
                                             vmem_limit_bytes=V7X_VMEM_LIMIT_BYTES),
        name=f"gla_layer_seq{seq}",
    )(*args)
    y = outs[0].reshape(batch, seq, D_MODEL)
    return (y, outs[1]) if emit_state else (y, None)


def _rope_swap(x):
    lane = lax.broadcasted_iota(jnp.int32, x.shape, 1)
    quarter = HEAD_DIM // 4
    first = (lane % (2 * quarter)) < quarter
    return jnp.where(first, pltpu.roll(x, HEAD_DIM - quarter, 1), pltpu.roll(x, quarter, 1))


def _att_kernel(*refs, seq, latent, cache_alias, layer, att_idx):
    n_tiles = seq // ROW_TILE
    n_keys = seq + (refs[9].shape[2] if latent else 0)
    it = iter(refs)
    x_ref, mod_ref, ng_ref, win_ref, qn_ref, kn_ref, wout_ref = (next(it) for _ in range(7))
    if latent:
        cos_ref, sin_ref, ck_ref, cv_ref = (next(it) for _ in range(4))
    if cache_alias:
        next(it), next(it)
    y_ref = next(it)
    if not latent:
        kout_ref, vout_ref = next(it), next(it)
    q_s, k_s, v_s, gate_s, ao_s = (next(it) for _ in range(5))

    ng = ng_ref[layer:layer + 1, :]
    qn = qn_ref[att_idx:att_idx + 1, :]
    kn = kn_ref[att_idx:att_idx + 1, :]
    exp2_scale = (HEAD_DIM ** -0.5) * math.log2(math.e)

    def project(t, carry):
        rows = pl.ds(pl.multiple_of(t * ROW_TILE, ROW_TILE), ROW_TILE)
        hb = _modulated_norm(x_ref[0, rows, :], ng, mod_ref)
        q = _head_rms(_dot(hb, win_ref[0, :, 0:ATT_QD]), qn, HEAD_DIM)
        k = _head_rms(_dot(hb, win_ref[0, :, ATT_QD:ATT_QD + ATT_KD]), kn, HEAD_DIM)
        v = _dot(hb, win_ref[0, :, ATT_QD + ATT_KD:ATT_QD + 2 * ATT_KD])
        gate_s[rows, :] = _dot(hb, win_ref[0, :, ATT_QD + 2 * ATT_KD:])
        if latent:
            cos = cos_ref[rows, :]
            sin = sin_ref[rows, :]
            q = jnp.concatenate(
                [q[:, h * HEAD_DIM:(h + 1) * HEAD_DIM] * cos + _rope_swap(q[:, h * HEAD_DIM:(h + 1) * HEAD_DIM]) * sin
                 for h in range(ATT_HEADS)], axis=-1)
            k = jnp.concatenate(
                [k[:, h * HEAD_DIM:(h + 1) * HEAD_DIM] * cos + _rope_swap(k[:, h * HEAD_DIM:(h + 1) * HEAD_DIM]) * sin
                 for h in range(ATT_KV_HEADS)], axis=-1)
        else:
            for h in range(ATT_KV_HEADS):
                kout_ref[0, 0, rows, h, :] = k[:, h * HEAD_DIM:(h + 1) * HEAD_DIM]
                vout_ref[0, 0, rows, h, :] = v[:, h * HEAD_DIM:(h + 1) * HEAD_DIM]
        q_s[rows, :] = (q * exp2_scale).astype(BF16)
        k_s[rows, :] = k.astype(BF16)
        for h in range(ATT_KV_HEADS):
            v_s[rows, 2 * h * HEAD_DIM:(2 * h + 1) * HEAD_DIM] = v[:, h * HEAD_DIM:(h + 1) * HEAD_DIM].astype(BF16)
        return carry

    for h in range(ATT_KV_HEADS):
        v_s[:, (2 * h + 1) * HEAD_DIM:(2 * h + 2) * HEAD_DIM] = jnp.ones((n_keys, HEAD_DIM), BF16)
    lax.fori_loop(0, n_tiles, project, 0, unroll=True)
    if latent:
        for h in range(ATT_KV_HEADS):
            k_s[seq:n_keys, h * HEAD_DIM:(h + 1) * HEAD_DIM] = ck_ref[0, 0, :, h, :].astype(BF16)
            v_s[seq:n_keys, 2 * h * HEAD_DIM:(2 * h + 1) * HEAD_DIM] = cv_ref[0, 0, :, h, :].astype(BF16)

    q_tile = min(seq, 4 * ROW_TILE)

    def attend(t, carry):
        rows = pl.ds(pl.multiple_of(t * q_tile, q_tile), q_tile)

        def scores(h):
            kc = slice((h // ATT_GROUP) * HEAD_DIM, (h // ATT_GROUP + 1) * HEAD_DIM)
            return _dot_nt(q_s[rows, h * HEAD_DIM:(h + 1) * HEAD_DIM], k_s[:, kc])

        s = scores(0)
        for h in range(ATT_HEADS):
            s_next = scores(h + 1) if h + 1 < ATT_HEADS else None
            vc = slice((h // ATT_GROUP) * 2 * HEAD_DIM, (h // ATT_GROUP + 1) * 2 * HEAD_DIM)
            p = jnp.exp2(s - jnp.max(s, axis=-1, keepdims=True))
            o = _dot(p.astype(BF16), v_s[:, vc])
            ao_s[rows, h * HEAD_DIM:(h + 1) * HEAD_DIM] = o[:, 0:HEAD_DIM] / o[:, HEAD_DIM:]
            s = s_next
        return carry

    lax.fori_loop(0, seq // q_tile, attend, 0)

    res_gate = mod_ref[0, 0, :, 2 * D_MODEL:]

    def finish(t, carry):
        rows = pl.ds(pl.multiple_of(t * ROW_TILE, ROW_TILE), ROW_TILE)
        y = _dot((ao_s[rows, :] * _silu(gate_s[rows, :])).astype(BF16), wout_ref[0])
        y_ref[0, rows, :] = x_ref[0, rows, :] + res_gate * y
        return carry

    lax.fori_loop(0, n_tiles, finish, 0, unroll=True)


def _att_layer(x, mods, layer, mod_row, ng, win, qn, kn, wout, latent_inputs, att_idx, n_att, caches_so_far):
    batch, seq, _ = x.shape
    latent = latent_inputs is not None
    in_specs = [
        pl.BlockSpec((1, seq, D_MODEL), lambda b: (b, 0, 0)),
        pl.BlockSpec((1, 1, 1, 3 * D_MODEL), lambda b: (layer, mod_row(b), 0, 0)),
        _resident(ng.shape),
        _resident((D_MODEL, 2 * ATT_QD + 2 * ATT_KD), att_idx),
        _resident(qn.shape),
        _resident(kn.shape),
        _resident((ATT_QD, D_MODEL), att_idx),
    ]
    args = [x, mods, ng, win, qn, kn, wout]
    n_keys = seq
    if latent:
        cos, sin, cache_k, cache_v = latent_inputs
        past = cache_k.shape[2]
        n_keys = seq + past
        in_specs += [
            _resident((seq, HEAD_DIM)),
            _resident((seq, HEAD_DIM)),
            pl.BlockSpec((1, 1, past, ATT_KV_HEADS, HEAD_DIM), lambda b: (b, att_idx, 0, 0, 0)),
            pl.BlockSpec((1, 1, past, ATT_KV_HEADS, HEAD_DIM), lambda b: (b, att_idx, 0, 0, 0)),
        ]
        args += [cos, sin, cache_k, cache_v]
    aliases = {}
    cache_alias = (not latent) and caches_so_far is not None
    if cache_alias:
        aliases = {len(args): 1, len(args) + 1: 2}
        in_specs += [pl.BlockSpec(memory_space=pl.ANY)] * 2
        args += list(caches_so_far)
    out_specs = [pl.BlockSpec((1, seq, D_MODEL), lambda b: (b, 0, 0))]
    out_shape = [jax.ShapeDtypeStruct((batch, seq, D_MODEL), F32)]
    if not latent:
        out_specs += [pl.BlockSpec((1, 1, seq, ATT_KV_HEADS, HEAD_DIM), lambda b: (b, att_idx, 0, 0, 0))] * 2
        out_shape += [jax.ShapeDtypeStruct((batch, n_att, seq, ATT_KV_HEADS, HEAD_DIM), F32)] * 2
    scratch = [
        pltpu.VMEM((seq, ATT_QD), BF16),
        pltpu.VMEM((n_keys, ATT_KD), BF16),
        pltpu.VMEM((n_keys, 2 * ATT_KD), BF16),
        pltpu.VMEM((seq, ATT_QD), F32),
        pltpu.VMEM((seq, ATT_QD), F32),
    ]
    outs = pl.pallas_call(
        functools.partial(_att_kernel, seq=seq, latent=latent, cache_alias=cache_alias, layer=layer, att_idx=att_idx),
        grid=(batch,),
        in_specs=in_specs,
        out_specs=out_specs,
        out_shape=out_shape,
        scratch_shapes=scratch,
        input_output_aliases=aliases,
        compiler_params=pltpu.CompilerParams(dimension_semantics=("arbitrary",),
                                             vmem_limit_bytes=V7X_VMEM_LIMIT_BYTES),
        name=f"att_layer_seq{seq}",
    )(*args)
    return outs


def _rope_tables(seq):
    half = HEAD_DIM // 2
    nf = half // 2
    pos = np.arange(seq)
    freqs = np.float32(ROPE_THETA) ** (-np.arange(nf, dtype=np.float32) / np.float32(nf))
    ang_row = (pos // GRID_W).astype(np.float32)[:, None] * freqs[None, :]
    ang_col = (pos % GRID_W).astype(np.float32)[:, None] * freqs[None, :]
    cos = np.concatenate([np.cos(ang_row)] * 2 + [np.cos(ang_col)] * 2, axis=-1)
    sin = np.concatenate([-np.sin(ang_row), np.sin(ang_row), -np.sin(ang_col), np.sin(ang_col)], axis=-1)
    return jnp.asarray(cos, F32), jnp.asarray(sin, F32)


def kernel(x_prompt, x_sample, state_gla, cache_k, cache_v, c, c_ctx, norm_g, w_ada, b_ada,
           gla_w_in, gla_wa1, gla_wa2, gla_ba, gla_onorm, gla_w_out,
           att_w_in, att_qnorm, att_knorm, att_w_out):
    n_dec = x_sample.shape[0]
    assert 1 + n_dec <= MOD_ROWS
    assert x_prompt.shape[1] % ROW_TILE == 0 and x_sample.shape[1] % ROW_TILE == 0

    cvec = jnp.concatenate([c_ctx[None], c, jnp.zeros((MOD_ROWS - 1 - n_dec, D_MODEL), F32)], axis=0)
    mods = _modulations(cvec, w_ada, b_ada)
    ctx_row = lambda b: 0
    dec_row = lambda b: b + 1

    cos, sin = _rope_tables(x_sample.shape[1])
    n_gla, n_att = gla_w_in.shape[0], att_w_in.shape[0]

    gla_win, gla_wout = gla_w_in.astype(BF16), gla_w_out.astype(BF16)
    att_win, att_wout = att_w_in.astype(BF16), att_w_out.astype(BF16)
    wa1 = jnp.concatenate([gla_wa1[:, 0], gla_wa1[:, 1],
                           jnp.zeros((n_gla, D_MODEL, RANK_PAD - 2 * GLA_RANK), F32)], axis=-1).astype(BF16)
    zeros_qd = jnp.zeros((n_gla, GLA_RANK, GLA_QD), F32)
    wa2 = jnp.concatenate([jnp.concatenate([gla_wa2[:, 0], zeros_qd], axis=-1),
                           jnp.concatenate([zeros_qd, gla_wa2[:, 1]], axis=-1),
                           jnp.zeros((n_gla, RANK_PAD - 2 * GLA_RANK, 2 * GLA_QD), F32)], axis=1).astype(BF16)

    xp, xs = x_prompt, x_sample
    states, caches = None, None
    for l in range(DEPTH):
        i = l // 2
        if l % 2 == 0:
            common = (norm_g, gla_win, wa1, wa2, gla_ba, gla_onorm, gla_wout)
            xp, states = _gla_layer(xp, mods, l, ctx_row, *common, None, i, n_gla, states)
            xs, _ = _gla_layer(xs, mods, l, dec_row, *common, state_gla, i, n_gla, None)
        else:
            xp, *caches = _att_layer(xp, mods, l, ctx_row, norm_g, att_win, att_qnorm, att_knorm, att_wout, None, i, n_att,
                                     caches)
            (xs,) = _att_layer(xs, mods, l, dec_row, norm_g, att_win, att_qnorm, att_knorm, att_wout,
                               (cos, sin, cache_k, cache_v), i, n_att, None)
    return (xp, xs, states, caches[0], caches[1])
```

```python
import functools
import math

import jax
import jax.numpy as jnp
import numpy as np
from jax import lax
from jax.experimental import pallas as pl
from jax.experimental.pallas import tpu as pltpu

D_MODEL = 1024
DEPTH = 4
GRID_W = 64
GLA_HEADS = 4
GLA_DK = 128
GLA_DV = 256
GLA_RANK = 16
GLA_TAU = 16.0
GLA_CHUNK = 64
GLA_QD = GLA_HEADS * GLA_DK
GLA_VD = GLA_HEADS * GLA_DV
HEAD_DIM = 128
ATT_HEADS = 8
ATT_KV_HEADS = 2
ATT_GROUP = ATT_HEADS // ATT_KV_HEADS
ATT_QD = ATT_HEADS * HEAD_DIM
ATT_KD = ATT_KV_HEADS * HEAD_DIM
ROPE_THETA = 10000.0
EPS = 1e-6

ROW_TILE = 256
GROUP_ROWS = 1024
MOD_ROWS = 8
RANK_PAD = 128
V7X_VMEM_LIMIT_BYTES = 60 * 1024 * 1024

F32 = jnp.float32
BF16 = jnp.bfloat16
_NT = (((1,), (1,)), ((), ()))


def _dot(a, b):
    return jnp.dot(a, b, preferred_element_type=F32)


def _dot_nt(a, b):
    return lax.dot_general(a, b, _NT, preferred_element_type=F32)


def _silu(x):
    return x * (1.0 / (1.0 + jnp.exp(-x)))


def _log_sigmoid(z):
    return jnp.minimum(z, 0.0) - jnp.log(1.0 + jnp.exp(-jnp.abs(z)))


def _split_top(x):
    top = pltpu.bitcast(pltpu.bitcast(x, jnp.uint32) & jnp.uint32(0xFFFF0000), F32)
    return top, x - top


def _modulated_norm(x, ng, mod_ref):
    shift = mod_ref[0, 0, :, 0:D_MODEL]
    scale = mod_ref[0, 0, :, D_MODEL:2 * D_MODEL]
    ms = jnp.mean(x * x, axis=-1, keepdims=True)
    return ((x * lax.rsqrt(ms + EPS)) * (ng * (1.0 + scale)) + shift).astype(BF16)


def _head_rms(x, g, width):
    outs = []
    for h in range(x.shape[-1] // width):
        xh = x[:, h * width:(h + 1) * width]
        ms = jnp.mean(xh * xh, axis=-1, keepdims=True)
        outs.append((xh * lax.rsqrt(ms + EPS)) * g)
    return jnp.concatenate(outs, axis=-1)


def _mod_kernel(c_ref, w_ref, b_ref, o_ref):
    acc = _dot(jnp.concatenate(_split_top(_silu(c_ref[...])), axis=0).astype(BF16), w_ref[0].astype(BF16))
    o_ref[0, :, 0, :] = acc[0:MOD_ROWS] + acc[MOD_ROWS:] + b_ref[pl.ds(pl.program_id(0), 1), :]


def _modulations(cvec, w_ada, b_ada):
    tn = D_MODEL
    n_tiles = 3 * D_MODEL // tn
    return pl.pallas_call(
        _mod_kernel,
        grid=(DEPTH, n_tiles),
        in_specs=[
            pl.BlockSpec((MOD_ROWS, D_MODEL), lambda l, j: (0, 0)),
            pl.BlockSpec((1, D_MODEL, tn), lambda l, j: (l, 0, j)),
            pl.BlockSpec((DEPTH, tn), lambda l, j: (0, j)),
        ],
        out_specs=pl.BlockSpec((1, MOD_ROWS, 1, tn), lambda l, j: (l, 0, 0, j)),
        out_shape=jax.ShapeDtypeStruct((DEPTH, MOD_ROWS, 1, 3 * D_MODEL), F32),
        compiler_params=pltpu.CompilerParams(dimension_semantics=("arbitrary", "arbitrary")),
        name="adaln_modulation",
    )(cvec, w_ada, b_ada)


def _chunk_sum_matrices():
    r = lax.broadcasted_iota(jnp.int32, (ROW_TILE, ROW_TILE), 0)
    c = lax.broadcasted_iota(jnp.int32, (ROW_TILE, ROW_TILE), 1)
    same = (r // GLA_CHUNK) == (c // GLA_CHUNK)
    prefix = jnp.where(same & (c <= r), 1.0, 0.0).astype(BF16)
    suffix = jnp.where(same & (c >= r), 1.0, 0.0).astype(BF16)
    return prefix, suffix


def _gla_kernel(*refs, seq, has_s0, emit_state, state_alias, layer, gla_idx):
    n_seq = GROUP_ROWS // seq
    n_tiles = GROUP_ROWS // ROW_TILE
    n_pos = seq // GLA_CHUNK
    it = iter(refs)
    x_ref, mod_ref, ng_ref, win_ref, wa1_ref, wa2_ref, ba_ref, on_ref, wout_ref = (next(it) for _ in range(9))
    s0_ref = next(it) if has_s0 else None
    if state_alias:
        next(it)
    y_ref = next(it)
    sout_ref = next(it) if emit_state else None
    q_s, k_s, b_s, v_s, gate_s, o_s = (next(it) for _ in range(6))
    st_s = None if emit_state else next(it)

    def state_at(s, d, h):
        return sout_ref.at[s, 0, d, h] if emit_state else st_s.at[d, h]

    ng = ng_ref[layer:layer + 1, :]
    ba = jnp.concatenate([ba_ref[gla_idx, 0:1, :], ba_ref[gla_idx, 1:2, :]], axis=1)
    prefix_m, suffix_m = _chunk_sum_matrices()

    def project_wide(t):
        rows = slice(t * ROW_TILE, (t + 1) * ROW_TILE)
        hb = _modulated_norm(x_ref[0, rows, :], ng, mod_ref)
        q_s[rows, :] = _dot(hb, win_ref[0, :, 0:GLA_QD]) * (GLA_DK ** -0.5)
        k_s[rows, :] = _dot(hb, win_ref[0, :, GLA_QD:2 * GLA_QD])
        v_s[rows, :] = _dot(hb, win_ref[0, :, 2 * GLA_QD:2 * GLA_QD + GLA_VD]).astype(BF16)
        gate_s[rows, :] = _dot(hb, win_ref[0, :, 2 * GLA_QD + GLA_VD:])
        low = _dot(hb, wa1_ref[0]).astype(BF16)
        return _dot(low, wa2_ref[0]) + ba

    def decay_sums(t, z):
        rows = slice(t * ROW_TILE, (t + 1) * ROW_TILE)
        logg2 = _log_sigmoid(z) * (math.log2(math.e) / GLA_TAU)
        for d, csum_m in enumerate((prefix_m, suffix_m)):
            parts = jnp.concatenate(_split_top(logg2[:, d * GLA_QD:(d + 1) * GLA_QD]), axis=0).astype(BF16)
            b_s[d, rows, :] = _dot(jnp.concatenate([csum_m, csum_m], axis=1), parts)

    z = project_wide(0)
    for t in range(n_tiles):
        z_next = project_wide(t + 1) if t + 1 < n_tiles else None
        decay_sums(t, z)
        z = z_next

    for s in range(n_seq):
        for d in range(2):
            for h in range(GLA_HEADS):
                state_at(s, d, h)[...] = s0_ref[0, 0, d, h] if has_s0 else jnp.zeros((GLA_DK, GLA_DV), F32)

    ri = lax.broadcasted_iota(jnp.int32, (GLA_CHUNK, GLA_CHUNK), 0)
    ci = lax.broadcasted_iota(jnp.int32, (GLA_CHUNK, GLA_CHUNK), 1)
    masks = (ci <= ri, ci >= ri)

    def scan_step(i, first_visit):
        chains = []
        for s in range(n_seq):
            for d in range(2):
                c = s * n_pos + (i if d == 0 else n_pos - 1 - i)
                rows = pl.ds(pl.multiple_of(c * GLA_CHUNK, GLA_CHUNK), GLA_CHUNK)
                b = b_s[d, rows, :]
                edge = GLA_CHUNK - 1 if d == 0 else 0
                total = b[edge:edge + 1, :]
                k = k_s[rows, :]
                qe = (q_s[rows, :] * jnp.exp2(b)).astype(BF16)
                ke = (k * jnp.exp2(-b)).astype(BF16)
                kdt = (k * jnp.exp2(total - b)).T.astype(BF16)
                dec = jnp.exp2(total)
                for h in range(GLA_HEADS):
                    kc = slice(h * GLA_DK, (h + 1) * GLA_DK)
                    chains.append((s, d, h, rows, slice(h * GLA_DV, (h + 1) * GLA_DV),
                                   qe[:, kc], ke[:, kc], kdt[kc, :], dec[:, kc]))
        scores = [_dot_nt(qe, ke) for (s, d, h, rows, vc, qe, ke, kdt, dec) in chains]
        no_state = jnp.zeros((GLA_DK, GLA_DK), BF16)
        for (s, d, h, rows, vc, qe, ke, kdt, dec), sc in zip(chains, scores):
            a = jnp.where(masks[d], sc, 0.0).astype(BF16)
            st_ref = state_at(s, d, h)
            st = st_ref[...]
            both = _dot(jnp.concatenate([jnp.concatenate([qe, a], axis=1),
                                         jnp.concatenate([no_state, kdt], axis=1)], axis=0),
                        jnp.concatenate([st.astype(BF16), v_s[rows, vc]], axis=0))
            if first_visit:
                o_s[rows, vc] = both[0:GLA_CHUNK]
            else:
                o_s[rows, vc] += both[0:GLA_CHUNK]
            dec_col = jnp.broadcast_to(dec, (GLA_DK, GLA_DK)).T
            st_ref[...] = jnp.concatenate([dec_col] * (GLA_DV // GLA_DK), axis=1) * st + both[GLA_CHUNK:]

    def first_half(i, carry):
        scan_step(i, True)
        return carry

    def second_half(i, carry):
        scan_step(i, False)
        return carry

    lax.fori_loop(0, n_pos // 2, first_half, 0, unroll=True)
    lax.fori_loop(n_pos // 2, n_pos, second_half, 0, unroll=True)

    on = on_ref[gla_idx:gla_idx + 1, :]
    res_gate = mod_ref[0, 0, :, 2 * D_MODEL:]

    def gated(t):
        rows = slice(t * ROW_TILE, (t + 1) * ROW_TILE)
        return (_head_rms(o_s[rows, :], on, GLA_DV) * _silu(gate_s[rows, :])).astype(BF16)

    og = gated(0)
    for t in range(n_tiles):
        rows = slice(t * ROW_TILE, (t + 1) * ROW_TILE)
        og_next = gated(t + 1) if t + 1 < n_tiles else None
        y_ref[0, rows, :] = x_ref[0, rows, :] + res_gate * _dot(og, wout_ref[0])
        og = og_next


def _resident(shape, layer=None):
    if layer is None:
        return pl.BlockSpec(shape, lambda b: (0,) * len(shape), pipeline_mode=pl.Buffered(1))
    return pl.BlockSpec((1,) + shape, lambda b: (layer,) + (0,) * len(shape), pipeline_mode=pl.Buffered(1))


def _gla_layer(x, mods, layer, mod_row, ng, win, wa1, wa2, ba, on, wout, state_in, gla_idx, n_gla, states_so_far):
    batch, seq, _ = x.shape
    n_seq = GROUP_ROWS // seq
    n_groups = batch // n_seq
    has_s0 = state_in is not None
    emit_state = not has_s0
    assert not has_s0 or n_seq == 1
    in_specs = [
        pl.BlockSpec((1, GROUP_ROWS, D_MODEL), lambda b: (b, 0, 0)),
        pl.BlockSpec((1, 1, 1, 3 * D_MODEL), lambda b: (layer, mod_row(b), 0, 0)),
        _resident(ng.shape),
        _resident((D_MODEL, 2 * GLA_QD + 2 * GLA_VD), gla_idx),
        _resident((D_MODEL, RANK_PAD), gla_idx),
        _resident((RANK_PAD, 2 * GLA_QD), gla_idx),
        _resident(ba.shape),
        _resident(on.shape),
        _resident((GLA_VD, D_MODEL), gla_idx),
    ]
    args = [x.reshape(n_groups, GROUP_ROWS, D_MODEL), mods, ng, win, wa1, wa2, ba, on, wout]
    if has_s0:
        in_specs.append(pl.BlockSpec((1, 1, 2, GLA_HEADS, GLA_DK, GLA_DV),
                                     lambda b: (b, gla_idx, 0, 0, 0, 0)))
        args.append(state_in)
    aliases = {}
    state_alias = emit_state and states_so_far is not None
    if state_alias:
        aliases[len(args)] = 1
        in_specs.append(pl.BlockSpec(memory_space=pl.ANY))
        args.append(states_so_far)
    out_specs = [pl.BlockSpec((1, GROUP_ROWS, D_MODEL), lambda b: (b, 0, 0))]
    out_shape = [jax.ShapeDtypeStruct((n_groups, GROUP_ROWS, D_MODEL), F32)]
    if emit_state:
        out_specs.append(pl.BlockSpec((n_seq, 1, 2, GLA_HEADS, GLA_DK, GLA_DV), lambda b: (b, gla_idx, 0, 0, 0, 0)))
        out_shape.append(jax.ShapeDtypeStruct((batch, n_gla, 2, GLA_HEADS, GLA_DK, GLA_DV), F32))
    scratch = [
        pltpu.VMEM((GROUP_ROWS, GLA_QD), F32),
        pltpu.VMEM((GROUP_ROWS, GLA_QD), F32),
        pltpu.VMEM((2, GROUP_ROWS, GLA_QD), F32),
        pltpu.VMEM((GROUP_ROWS, GLA_VD), BF16),
        pltpu.VMEM((GROUP_ROWS, GLA_VD), F32),
        pltpu.VMEM((GROUP_ROWS, GLA_VD), F32),
    ]
    if not emit_state:
        scratch.append(pltpu.VMEM((2, GLA_HEADS, GLA_DK, GLA_DV), F32))
    outs = pl.pallas_call(
        functools.partial(_gla_kernel, seq=seq, has_s0=has_s0, emit_state=emit_state, state_alias=state_alias,
                          layer=layer, gla_idx=gla_idx),
        grid=(n_groups,),
        in_specs=in_specs,
        out_specs=out_specs,
        out_shape=out_shape,
        scratch_shapes=scratch,
        input_output_aliases=aliases,
        compiler_params=pltpu.CompilerParams(dimension_semantics=("arbitrary",),
                                             vmem_limit_bytes=V7X_VMEM_LIMIT_BYTES),
        name=f"gla_layer_seq{seq}",
    )(*args)
    y = outs[0].reshape(batch, seq, D_MODEL)
    return (y, outs[1]) if emit_state else (y, None)


def _rope_swap(x):
    lane = lax.broadcasted_iota(jnp.int32, x.shape, 1)
    quarter = HEAD_DIM // 4
    first = (lane % (2 * quarter)) < quarter
    return jnp.where(first, pltpu.roll(x, HEAD_DIM - quarter, 1), pltpu.roll(x, quarter, 1))


def _att_kernel(*refs, seq, latent, cache_alias, layer, att_idx):
    n_tiles = seq // ROW_TILE
    n_keys = seq + (refs[9].shape[2] if latent else 0)
    it = iter(refs)
    x_ref, mod_ref, ng_ref, win_ref, qn_ref, kn_ref, wout_ref = (next(it) for _ in range(7))
    if latent:
        cos_ref, sin_ref, ck_ref, cv_ref = (next(it) for _ in range(4))
    if cache_alias:
        next(it), next(it)
    y_ref = next(it)
    if not latent:
        kout_ref, vout_ref = next(it), next(it)
    q_s, k_s, v_s, gate_s, ao_s = (next(it) for _ in range(5))

    ng = ng_ref[layer:layer + 1, :]
    qn = qn_ref[att_idx:att_idx + 1, :]
    kn = kn_ref[att_idx:att_idx + 1, :]
    exp2_scale = (HEAD_DIM ** -0.5) * math.log2(math.e)

    def project(t, carry):
        rows = pl.ds(pl.multiple_of(t * ROW_TILE, ROW_TILE), ROW_TILE)
        hb = _modulated_norm(x_ref[0, rows, :], ng, mod_ref)
        q = _head_rms(_dot(hb, win_ref[0, :, 0:ATT_QD]), qn, HEAD_DIM)
        k = _head_rms(_dot(hb, win_ref[0, :, ATT_QD:ATT_QD + ATT_KD]), kn, HEAD_DIM)
        v = _dot(hb, win_ref[0, :, ATT_QD + ATT_KD:ATT_QD + 2 * ATT_KD])
        gate_s[rows, :] = _dot(hb, win_ref[0, :, ATT_QD + 2 * ATT_KD:])
        if latent:
            cos = cos_ref[rows, :]
            sin = sin_ref[rows, :]
            q = jnp.concatenate(
                [q[:, h * HEAD_DIM:(h + 1) * HEAD_DIM] * cos + _rope_swap(q[:, h * HEAD_DIM:(h + 1) * HEAD_DIM]) * sin
                 for h in range(ATT_HEADS)], axis=-1)
            k = jnp.concatenate(
                [k[:, h * HEAD_DIM:(h + 1) * HEAD_DIM] * cos + _rope_swap(k[:, h * HEAD_DIM:(h + 1) * HEAD_DIM]) * sin
                 for h in range(ATT_KV_HEADS)], axis=-1)
        else:
            for h in range(ATT_KV_HEADS):
                kout_ref[0, 0, rows, h, :] = k[:, h * HEAD_DIM:(h + 1) * HEAD_DIM]
                vout_ref[0, 0, rows, h, :] = v[:, h * HEAD_DIM:(h + 1) * HEAD_DIM]
        q_s[rows, :] = (q * exp2_scale).astype(BF16)
        k_s[rows, :] = k.astype(BF16)
        for h in range(ATT_KV_HEADS):
            v_s[rows, 2 * h * HEAD_DIM:(2 * h + 1) * HEAD_DIM] = v[:, h * HEAD_DIM:(h + 1) * HEAD_DIM].astype(BF16)
        return carry

    for h in range(ATT_KV_HEADS):
        v_s[:, (2 * h + 1) * HEAD_DIM:(2 * h + 2) * HEAD_DIM] = jnp.ones((n_keys, HEAD_DIM), BF16)
    lax.fori_loop(0, n_tiles, project, 0, unroll=True)
    if latent:
        for h in range(ATT_KV_HEADS):
            k_s[seq:n_keys, h * HEAD_DIM:(h + 1) * HEAD_DIM] = ck_ref[0, 0, :, h, :].astype(BF16)
            v_s[seq:n_keys, 2 * h * HEAD_DIM:(2 * h + 1) * HEAD_DIM] = cv_ref[0, 0, :, h, :].astype(BF16)

    q_tile = min(seq, 4 * ROW_TILE)

    def attend(t, carry):
        rows = pl.ds(pl.multiple_of(t * q_tile, q_tile), q_tile)

        def scores(h):
            kc = slice((h // ATT_GROUP) * HEAD_DIM, (h // ATT_GROUP + 1) * HEAD_DIM)
            return _dot_nt(q_s[rows, h * HEAD_DIM:(h + 1) * HEAD_DIM], k_s[:, kc])

        s = scores(0)
        for h in range(ATT_HEADS):
            s_next = scores(h + 1) if h + 1 < ATT_HEADS else None
            vc = slice((h // ATT_GROUP) * 2 * HEAD_DIM, (h // ATT_GROUP + 1) * 2 * HEAD_DIM)
            p = jnp.exp2(s - jnp.max(s, axis=-1, keepdims=True))
            o = _dot(p.astype(BF16), v_s[:, vc])
            ao_s[rows, h * HEAD_DIM:(h + 1) * HEAD_DIM] = o[:, 0:HEAD_DIM] / o[:, HEAD_DIM:]
            s = s_next
        return carry

    lax.fori_loop(0, seq // q_tile, attend, 0)

    res_gate = mod_ref[0, 0, :, 2 * D_MODEL:]

    def finish(t, carry):
        rows = pl.ds(pl.multiple_of(t * ROW_TILE, ROW_TILE), ROW_TILE)
        y = _dot((ao_s[rows, :] * _silu(gate_s[rows, :])).astype(BF16), wout_ref[0])
        y_ref[0, rows, :] = x_ref[0, rows, :] + res_gate * y
        return carry

    lax.fori_loop(0, n_tiles, finish, 0, unroll=True)


def _att_layer(x, mods, layer, mod_row, ng, win, qn, kn, wout, latent_inputs, att_idx, n_att, caches_so_far):
    batch, seq, _ = x.shape
    latent = latent_inputs is not None
    in_specs = [
        pl.BlockSpec((1, seq, D_MODEL), lambda b: (b, 0, 0)),
        pl.BlockSpec((1, 1, 1, 3 * D_MODEL), lambda b: (layer, mod_row(b), 0, 0)),
        _resident(ng.shape),
        _resident((D_MODEL, 2 * ATT_QD + 2 * ATT_KD), att_idx),
        _resident(qn.shape),
        _resident(kn.shape),
        _resident((ATT_QD, D_MODEL), att_idx),
    ]
    args = [x, mods, ng, win, qn, kn, wout]
    n_keys = seq
    if latent:
        cos, sin, cache_k, cache_v = latent_inputs
        past = cache_k.shape[2]
        n_keys = seq + past
        in_specs += [
            _resident((seq, HEAD_DIM)),
            _resident((seq, HEAD_DIM)),
            pl.BlockSpec((1, 1, past, ATT_KV_HEADS, HEAD_DIM), lambda b: (b, att_idx, 0, 0, 0)),
            pl.BlockSpec((1, 1, past, ATT_KV_HEADS, HEAD_DIM), lambda b: (b, att_idx, 0, 0, 0)),
        ]
        args += [cos, sin, cache_k, cache_v]
    aliases = {}
    cache_alias = (not latent) and caches_so_far is not None
    if cache_alias:
        aliases = {len(args): 1, len(args) + 1: 2}
        in_specs += [pl.BlockSpec(memory_space=pl.ANY)] * 2
        args += list(caches_so_far)
    out_specs = [pl.BlockSpec((1, seq, D_MODEL), lambda b: (b, 0, 0))]
    out_shape = [jax.ShapeDtypeStruct((batch, seq, D_MODEL), F32)]
    if not latent:
        out_specs += [pl.BlockSpec((1, 1, seq, ATT_KV_HEADS, HEAD_DIM), lambda b: (b, att_idx, 0, 0, 0))] * 2
        out_shape += [jax.ShapeDtypeStruct((batch, n_att, seq, ATT_KV_HEADS, HEAD_DIM), F32)] * 2
    scratch = [
        pltpu.VMEM((seq, ATT_QD), BF16),
        pltpu.VMEM((n_keys, ATT_KD), BF16),
        pltpu.VMEM((n_keys, 2 * ATT_KD), BF16),
        pltpu.VMEM((seq, ATT_QD), F32),
        pltpu.VMEM((seq, ATT_QD), F32),
    ]
    outs = pl.pallas_call(
        functools.partial(_att_kernel, seq=seq, latent=latent, cache_alias=cache_alias, layer=layer, att_idx=att_idx),
        grid=(batch,),
        in_specs=in_specs,
        out_specs=out_specs,
        out_shape=out_shape,
        scratch_shapes=scratch,
        input_output_aliases=aliases,
        compiler_params=pltpu.CompilerParams(dimension_semantics=("arbitrary",),
                                             vmem_limit_bytes=V7X_VMEM_LIMIT_BYTES),
        name=f"att_layer_seq{seq}",
    )(*args)
    return outs


def _rope_tables(seq):
    half = HEAD_DIM // 2
    nf = half // 2
    pos = np.arange(seq)
    freqs = np.float32(ROPE_THETA) ** (-np.arange(nf, dtype=np.float32) / np.float32(nf))
    ang_row = (pos // GRID_W).astype(np.float32)[:, None] * freqs[None, :]
    ang_col = (pos % GRID_W).astype(np.float32)[:, None] * freqs[None, :]
    cos = np.concatenate([np.cos(ang_row)] * 2 + [np.cos(ang_col)] * 2, axis=-1)
    sin = np.concatenate([-np.sin(ang_row), np.sin(ang_row), -np.sin(ang_col), np.sin(ang_col)], axis=-1)
    return jnp.asarray(cos, F32), jnp.asarray(sin, F32)


def kernel(x_prompt, x_sample, state_gla, cache_k, cache_v, c, c_ctx, norm_g, w_ada, b_ada,
           gla_w_in, gla_wa1, gla_wa2, gla_ba, gla_onorm, gla_w_out,
           att_w_in, att_qnorm, att_knorm, att_w_out):
    n_dec = x_sample.shape[0]
    assert 1 + n_dec <= MOD_ROWS
    assert x_prompt.shape[1] % ROW_TILE == 0 and x_sample.shape[1] % ROW_TILE == 0

    cvec = jnp.concatenate([c_ctx[None], c, jnp.zeros((MOD_ROWS - 1 - n_dec, D_MODEL), F32)], axis=0)
    mods = _modulations(cvec, w_ada, b_ada)
    ctx_row = lambda b: 0
    dec_row = lambda b: b + 1

    cos, sin = _rope_tables(x_sample.shape[1])
    n_gla, n_att = gla_w_in.shape[0], att_w_in.shape[0]

    gla_win, gla_wout = gla_w_in.astype(BF16), gla_w_out.astype(BF16)
    att_win, att_wout = att_w_in.astype(BF16), att_w_out.astype(BF16)
    wa1 = jnp.concatenate([gla_wa1[:, 0], gla_wa1[:, 1],
                           jnp.zeros((n_gla, D_MODEL, RANK_PAD - 2 * GLA_RANK), F32)], axis=-1).astype(BF16)
    zeros_qd = jnp.zeros((n_gla, GLA_RANK, GLA_QD), F32)
    wa2 = jnp.concatenate([jnp.concatenate([gla_wa2[:, 0], zeros_qd], axis=-1),
                           jnp.concatenate([zeros_qd, gla_wa2[:, 1]], axis=-1),
                           jnp.zeros((n_gla, RANK_PAD - 2 * GLA_RANK, 2 * GLA_QD), F32)], axis=1).astype(BF16)

    xp, xs = x_prompt, x_sample
    states, caches = None, None
    for l in range(DEPTH):
        i = l // 2
        if l % 2 == 0:
            common = (norm_g, gla_win, wa1, wa2, gla_ba, gla_onorm, gla_wout)
            xp, states = _gla_layer(xp, mods, l, ctx_row, *common, None, i, n_gla, states)
            xs, _ = _gla_layer(xs, mods, l, dec_row, *common, state_gla, i, n_gla, None)
        else:
            xp, *caches = _att_layer(xp, mods, l, ctx_row, norm_g, att_win, att_qnorm, att_knorm, att_wout, None, i, n_att,
                                     caches)
            (xs,) = _att_layer(xs, mods, l, dec_row, norm_g, att_win, att_qnorm, att_knorm, att_wout,
                               (cos, sin, cache_k, cache_v), i, n_att, None)
    return (xp, xs, states, caches[0], caches[1])
```
